```python
import jax, jax.numpy as jnp
from jax import lax
import numpy as np


D_MODEL = 1024
BATCH = 4
SEQ = 4096
DEPTH = 4

GLA_HEADS = 4
GLA_DK = 64
GLA_DV = 128
GLA_RANK = 16
GLA_TAU = 16.0
GLA_CHUNK = 64
ML_HEADS = 4
ML_DK = 64
ML_DV = 128
ML_CONV = 4
ML_CHUNK = 64
MOBA_HEADS = 8
MOBA_HD = D_MODEL // MOBA_HEADS
MOBA_BLOCK = 256
MOBA_TOPK = 3
MOBA_QCHUNK = 16
D_FF = -(-8 * D_MODEL // (3 * 256)) * 256

N_EVEN = (DEPTH + 1) // 2
N_ODD = DEPTH // 2
GLA_QK = GLA_HEADS * GLA_DK
GLA_V = GLA_HEADS * GLA_DV
ML_QK = ML_HEADS * ML_DK
ML_V = ML_HEADS * ML_DV
HYB_SPLITS = (GLA_QK, GLA_QK, GLA_V, GLA_V, GLA_RANK, ML_QK, ML_QK, ML_V, ML_V, ML_HEADS, ML_HEADS)
HYB_IN = sum(HYB_SPLITS)
EPS = 1e-6
NEG = -1e30

kernel_name = 'hybrid_gla_mlstm_moba_swiglu'


def rms_norm(x, g):
    xf = x.astype(jnp.float32)
    y = xf * lax.rsqrt(jnp.mean(xf * xf, axis=-1, keepdims=True) + EPS)
    return (y * g.astype(jnp.float32)).astype(x.dtype)


def split_heads(t, h):
    b, s, _ = t.shape
    return t.reshape(b, s, h, -1).transpose(0, 2, 1, 3)


def merge_heads(t):
    b, h, s, d = t.shape
    return t.transpose(0, 2, 1, 3).reshape(b, s, h * d)


def to_chunks(t, c):
    b, h, s = t.shape[:3]
    return jnp.moveaxis(t.reshape(b, h, s // c, c, *t.shape[3:]), 2, 0)


def from_chunks(t):
    n, b, h, c = t.shape[:4]
    return jnp.moveaxis(t, 0, 2).reshape(b, h, n * c, *t.shape[4:])


def causal_conv(x, w):
    c = x.shape[-1]
    return lax.conv_general_dilated(x, w[:, None, :].astype(x.dtype), (1,), [(w.shape[0] - 1, 0)],
                                    dimension_numbers=('NWC', 'WIO', 'NWC'), feature_group_count=c)


def alibi_slopes(n):
    return jnp.asarray([2.0 ** (-8.0 * (i + 1) / n) for i in range(n)], jnp.float32)


def gla_chunked(q, k, v, log_a):
    b, h, s, dk = q.shape
    dv = v.shape[-1]
    c = GLA_CHUNK
    mask = jnp.tril(jnp.ones((c, c), bool))[:, :, None]

    def step(state, inp):
        qc, kc, vc, gc = inp
        bc = jnp.cumsum(gc, axis=2)
        diff = bc[:, :, :, None, :] - bc[:, :, None, :, :]
        decay = jnp.where(mask, jnp.exp(jnp.where(mask, diff, 0.0)), 0.0)
        scores = jnp.einsum('bhtk,bhsk,bhtsk->bhts', qc, kc, decay)
        out = (jnp.einsum('bhts,bhsv->bhtv', scores, vc)
               + jnp.einsum('bhtk,bhkv->bhtv', qc * jnp.exp(bc), state))
        last = bc[:, :, -1:, :]
        state = (jnp.exp(last[:, :, 0, :])[..., None] * state
                 + jnp.einsum('bhsk,bhsv->bhkv', kc * jnp.exp(last - bc), vc))
        return state, out

    state0 = jnp.zeros((b, h, dk, dv), jnp.float32)
    _, out = lax.scan(step, state0, tuple(to_chunks(t, c) for t in (q, k, v, log_a)))
    return from_chunks(out)


def mlstm_chunked(q, k, v, i_pre, log_f):
    b, h, s, dk = q.shape
    dv = v.shape[-1]
    c = ML_CHUNK
    mask = jnp.tril(jnp.ones((c, c), bool))

    def step(carry, inp):
        cs, ns, m = carry
        qc, kc, vc, ic, fc = inp
        bc = jnp.cumsum(fc, axis=-1)
        dmat = jnp.where(mask, bc[..., :, None] - bc[..., None, :] + ic[..., None, :], -jnp.inf)
        inter_log = bc + m[..., None]
        m_t = jnp.maximum(inter_log, jnp.max(dmat, axis=-1))
        w = jnp.einsum('bhtk,bhsk->bhts', qc, kc) * jnp.exp(dmat - m_t[..., None])
        inter = jnp.exp(inter_log - m_t)
        num = inter[..., None] * jnp.einsum('bhtk,bhkv->bhtv', qc, cs) + jnp.einsum('bhts,bhsv->bhtv', w, vc)
        den = inter * jnp.einsum('bhtk,bhk->bht', qc, ns) + jnp.sum(w, axis=-1)
        hout = num / jnp.maximum(jnp.abs(den), jnp.exp(-m_t))[..., None]
        g = bc[..., -1:] - bc + ic
        m_new = jnp.maximum(bc[..., -1] + m, jnp.max(g, axis=-1))
        carry_scale = jnp.exp(bc[..., -1] + m - m_new)
        wk = kc * jnp.exp(g - m_new[..., None])[..., None]
        cs = carry_scale[..., None, None] * cs + jnp.einsum('bhsk,bhsv->bhkv', wk, vc)
        ns = carry_scale[..., None] * ns + jnp.sum(wk, axis=2)
        return (cs, ns, m_new), hout

    carry0 = (jnp.zeros((b, h, dk, dv), jnp.float32), jnp.zeros((b, h, dk), jnp.float32),
              jnp.zeros((b, h), jnp.float32))
    _, out = lax.scan(step, carry0, tuple(to_chunks(t, c) for t in (q, k, v, i_pre, log_f)))
    return from_chunks(out)


def hybrid_mixer(xn, w_in, w_gate2, b_gate, gla_gain, conv_w, b_if, ml_gain, w_out):
    b, s, _ = xn.shape
    f32 = jnp.float32
    proj = xn @ w_in
    cuts = [int(i) for i in np.cumsum(HYB_SPLITS)[:-1]]
    gq, gk, gv, gr, glr, mq, mk, mv, mo, mi, mf = jnp.split(proj, cuts, axis=-1)
    log_a = jax.nn.log_sigmoid((glr @ w_gate2 + b_gate).astype(f32)) / GLA_TAU
    o_gla = gla_chunked(split_heads(gq, GLA_HEADS).astype(f32) * GLA_DK ** -0.5,
                        split_heads(gk, GLA_HEADS).astype(f32),
                        split_heads(gv, GLA_HEADS).astype(f32),
                        split_heads(log_a, GLA_HEADS))
    y_gla = rms_norm(o_gla.astype(xn.dtype).transpose(0, 2, 1, 3), gla_gain.reshape(GLA_HEADS, GLA_DV))
    y_gla = y_gla.reshape(b, s, GLA_V) * jax.nn.silu(gr)
    qk = jax.nn.silu(causal_conv(jnp.concatenate([mq, mk], axis=-1), conv_w))
    mq, mk = jnp.split(qk, 2, axis=-1)
    gates = (jnp.concatenate([mi, mf], axis=-1) + b_if).astype(f32).transpose(0, 2, 1)
    i_pre, f_pre = gates[:, :ML_HEADS], gates[:, ML_HEADS:]
    h_ml = mlstm_chunked(split_heads(mq, ML_HEADS).astype(f32),
                         split_heads(mk, ML_HEADS).astype(f32) * ML_DK ** -0.5,
                         split_heads(mv, ML_HEADS).astype(f32),
                         i_pre, jax.nn.log_sigmoid(f_pre))
    h_ml = h_ml.astype(xn.dtype).transpose(0, 2, 1, 3) * jax.nn.sigmoid(mo).reshape(b, s, ML_HEADS, ML_DV)
    y_ml = rms_norm(h_ml, ml_gain.reshape(ML_HEADS, ML_DV)).reshape(b, s, ML_V)
    return jnp.concatenate([y_gla, y_ml], axis=-1) @ w_out


def moba_attention(q, k, v):
    b, h, s, d = q.shape
    f32 = jnp.float32
    s_pad = -(-s // MOBA_BLOCK) * MOBA_BLOCK
    pad = ((0, 0), (0, 0), (0, s_pad - s), (0, 0))
    q, k, v = jnp.pad(q, pad), jnp.pad(k, pad), jnp.pad(v, pad)
    nb = s_pad // MOBA_BLOCK
    kb = k.reshape(b, h, nb, MOBA_BLOCK, d)
    vb = v.reshape(b, h, nb, MOBA_BLOCK, d)
    kmean = jnp.mean(kb.astype(f32), axis=3)
    pos = jnp.arange(s_pad, dtype=jnp.int32)
    qblk = pos // MOBA_BLOCK
    gate = jnp.einsum('bhtd,bhnd->bhtn', q.astype(f32), kmean)
    fully_past = jnp.arange(nb)[None, :] < qblk[:, None]
    gate = jnp.where(fully_past, gate, -jnp.inf)
    n_sel = min(MOBA_TOPK, nb)
    _, top = lax.top_k(gate, n_sel)
    idx = jnp.concatenate([top.astype(jnp.int32),
                           jnp.broadcast_to(qblk[:, None], (b, h, s_pad, 1))], axis=-1)
    valid = jnp.concatenate([jnp.arange(n_sel)[None, :] < qblk[:, None],
                             jnp.ones((s_pad, 1), bool)], axis=-1)
    slopes = alibi_slopes(h)[:, None, None, None]
    nq = s_pad // MOBA_QCHUNK
    q_c = jnp.moveaxis(q.reshape(b, h, nq, MOBA_QCHUNK, d), 2, 0)
    i_c = jnp.moveaxis(idx.reshape(b, h, nq, MOBA_QCHUNK, -1), 2, 0)
    t_c = pos.reshape(nq, MOBA_QCHUNK)
    v_c = valid.reshape(nq, MOBA_QCHUNK, -1)
    gather = jax.vmap(jax.vmap(lambda blocks, i: blocks[i]))

    def attend(inp):
        qq, ii, tt, vv = inp
        kg = gather(kb, ii)
        vg = gather(vb, ii)
        sc = jnp.einsum('bhqd,bhqrkd->bhqrk', qq, kg).astype(f32)
        kpos = ii[..., None] * MOBA_BLOCK + jnp.arange(MOBA_BLOCK, dtype=jnp.int32)
        dist = tt[:, None, None] - kpos
        allowed = vv[:, :, None] & (dist >= 0)
        sc = jnp.where(allowed, sc - slopes * dist.astype(f32), NEG)
        p = jax.nn.softmax(sc.reshape(*sc.shape[:3], -1), axis=-1).reshape(sc.shape)
        return jnp.einsum('bhqrk,bhqrkd->bhqd', p.astype(vg.dtype), vg)

    out = lax.map(attend, (q_c, i_c, t_c, v_c))
    return jnp.moveaxis(out, 0, 2).reshape(b, h, s_pad, d)[:, :, :s]


def moba_mixer(xn, w_qkv, q_gain, k_gain, w_o):
    q, k, v = jnp.split(xn @ w_qkv, 3, axis=-1)
    q = rms_norm(split_heads(q, MOBA_HEADS), q_gain) * MOBA_HD ** -0.5
    k = rms_norm(split_heads(k, MOBA_HEADS), k_gain)
    o = moba_attention(q, k, split_heads(v, MOBA_HEADS))
    return merge_heads(o) @ w_o


def swiglu(xn, wg, wu, wd):
    return (jax.nn.silu(xn @ wg) * (xn @ wu)) @ wd


def setup_inputs(seed: int = 0) -> dict:
    key = jax.random.key(seed)
    ks = jax.random.split(key, 24)

    def nrm(k, shape, scale):
        return jax.random.normal(k, shape, jnp.float32) * scale

    b_i = nrm(ks[8], (N_EVEN, ML_HEADS), 0.1)
    b_f = jnp.linspace(3.0, 6.0, ML_HEADS, dtype=jnp.float32)[None, :] + nrm(ks[9], (N_EVEN, ML_HEADS), 0.1)
    return {
        'x': nrm(ks[0], (BATCH, SEQ, D_MODEL), 1.0),
        'norm_mix': 1.0 + nrm(ks[1], (DEPTH, D_MODEL), 0.02),
        'norm_ffn': 1.0 + nrm(ks[2], (DEPTH, D_MODEL), 0.02),
        'hyb_w_in': nrm(ks[3], (N_EVEN, D_MODEL, HYB_IN), D_MODEL ** -0.5),
        'gla_w_gate2': nrm(ks[4], (N_EVEN, GLA_RANK, GLA_QK), GLA_RANK ** -0.5),
        'gla_b_gate': nrm(ks[5], (N_EVEN, GLA_QK), 0.1),
        'gla_norm': 1.0 + nrm(ks[6], (N_EVEN, GLA_V), 0.02),
        'ml_conv': nrm(ks[7], (N_EVEN, ML_CONV, 2 * ML_QK), ML_CONV ** -0.5),
        'ml_b_if': jnp.concatenate([b_i, b_f], axis=-1),
        'ml_norm': 1.0 + nrm(ks[10], (N_EVEN, ML_V), 0.02),
        'hyb_w_out': nrm(ks[11], (N_EVEN, GLA_V + ML_V, D_MODEL), (GLA_V + ML_V) ** -0.5),
        'moba_w_qkv': nrm(ks[12], (N_ODD, D_MODEL, 3 * D_MODEL), D_MODEL ** -0.5),
        'moba_q_norm': 1.0 + nrm(ks[13], (N_ODD, MOBA_HD), 0.02),
        'moba_k_norm': 1.0 + nrm(ks[14], (N_ODD, MOBA_HD), 0.02),
        'moba_w_o': nrm(ks[15], (N_ODD, D_MODEL, D_MODEL), D_MODEL ** -0.5),
        'ffn_w_gate': nrm(ks[16], (DEPTH, D_MODEL, D_FF), D_MODEL ** -0.5),
        'ffn_w_up': nrm(ks[17], (DEPTH, D_MODEL, D_FF), D_MODEL ** -0.5),
        'ffn_w_down': nrm(ks[18], (DEPTH, D_FF, D_MODEL), D_FF ** -0.5),
    }


def reference(x, norm_mix, norm_ffn, hyb_w_in, gla_w_gate2, gla_b_gate, gla_norm, ml_conv, ml_b_if,
              ml_norm, hyb_w_out, moba_w_qkv, moba_q_norm, moba_k_norm, moba_w_o,
              ffn_w_gate, ffn_w_up, ffn_w_down):
    for l in range(DEPTH):
        xn = rms_norm(x, norm_mix[l])
        j = l // 2
        if l % 2 == 0:
            x = x + hybrid_mixer(xn, hyb_w_in[j], gla_w_gate2[j], gla_b_gate[j], gla_norm[j],
                                 ml_conv[j], ml_b_if[j], ml_norm[j], hyb_w_out[j])
        else:
            x = x + moba_mixer(xn, moba_w_qkv[j], moba_q_norm[j], moba_k_norm[j], moba_w_o[j])
        xn = rms_norm(x, norm_ffn[l])
        x = x + swiglu(xn, ffn_w_gate[l], ffn_w_up[l], ffn_w_down[l])
    return x
```

```python
import functools

import jax
import jax.numpy as jnp
import numpy as np
from jax import lax
from jax.experimental import pallas as pl
from jax.experimental.pallas import tpu as pltpu

F32 = jnp.float32
BF16 = jnp.bfloat16
HIGHEST = lax.Precision.HIGHEST

LANES = 128
SUBLANES = 8
VMEM_LIMIT_BYTES = 56 * 1024 * 1024

EPS = 1e-6
NEG = -1e30

GLA_HEADS = 4
GLA_DK = 64
GLA_DV = 128
GLA_RANK = 16
GLA_TAU = 16.0
ML_HEADS = 4
ML_DK = 64
ML_DV = 128
ML_CONV = 4
CHUNK = 64
SUB = 16
MOBA_HEADS = 8
MOBA_HD = 128
MOBA_BLOCK = 256
MOBA_TOPK = 3

GLA_QK = GLA_HEADS * GLA_DK
GLA_V = GLA_HEADS * GLA_DV
ML_QK = ML_HEADS * ML_DK
ML_V = ML_HEADS * ML_DV

C_GQ = 0
C_GK = C_GQ + GLA_QK
C_GV = C_GK + GLA_QK
C_GR = C_GV + GLA_V
C_MQ = C_GR + GLA_V
C_MK = C_MQ + ML_QK
C_MV = C_MK + ML_QK
C_MO = C_MV + ML_V
C_SMALL = C_MO + ML_V
HYB_COLS = C_SMALL + LANES
S_GLR = 0
S_MI = GLA_RANK
S_MF = GLA_RANK + ML_HEADS

ROW_TILE = 512
MIX_TILE = 256
FF_TILE = 256


def _rms(x, gain):
    return x * lax.rsqrt(jnp.mean(x * x, axis=-1, keepdims=True) + EPS) * gain


def _log_sigmoid(z):
    return jnp.minimum(z, 0.0) - jnp.log1p(jnp.exp(-jnp.abs(z)))


def _sigmoid(z):
    return 1.0 / (1.0 + jnp.exp(-z))


def _dot(a, b):
    return jnp.dot(a.astype(BF16), b.astype(BF16), preferred_element_type=F32)


def _dot_nt(a, b):
    return lax.dot_general(a.astype(BF16), b.astype(BF16), (((1,), (1,)), ((), ())),
                           preferred_element_type=F32)


def _dot_tn(a, b):
    return lax.dot_general(a.astype(BF16), b.astype(BF16), (((0,), (0,)), ((), ())),
                           preferred_element_type=F32)


def _dot_f32(a, b):
    return jnp.dot(a, b, preferred_element_type=F32, precision=HIGHEST)


def _const_spec(shape):
    return pl.BlockSpec(shape, lambda *_: (0,) * len(shape), pipeline_mode=pl.Buffered(1))


def _params(n_grid):
    return pltpu.CompilerParams(dimension_semantics=("arbitrary",) * n_grid,
                                vmem_limit_bytes=VMEM_LIMIT_BYTES)


def _norm_proj_kernel(x_ref, g_ref, w_ref, o_ref, *, col_tile):
    xn = _rms(x_ref[...], g_ref[...]).astype(BF16)
    n = o_ref.shape[1]
    for c0 in range(0, n, col_tile):
        c1 = min(c0 + col_tile, n)
        o_ref[:, c0:c1] = jnp.dot(xn, w_ref[:, c0:c1], preferred_element_type=F32)


def norm_proj(x, gain, w):
    m, d = x.shape
    n = w.shape[1]
    tm = min(ROW_TILE, m)
    return pl.pallas_call(
        functools.partial(_norm_proj_kernel, col_tile=512),
        grid=(m // tm,),
        in_specs=[pl.BlockSpec((tm, d), lambda i: (i, 0)),
                  _const_spec((1, d)),
                  _const_spec((d, n))],
        out_specs=pl.BlockSpec((tm, n), lambda i: (i, 0)),
        out_shape=jax.ShapeDtypeStruct((m, n), F32),
        compiler_params=_params(1),
        name="norm_proj",
    )(x, gain.reshape(1, d), w)


def _out_ffn_kernel(x_ref, y_ref, wo_ref, g_ref, wg_ref, wu_ref, wd_ref, o_ref, h_ref, *, ff_tile):
    x1 = x_ref[...] + jnp.dot(y_ref[...], wo_ref[...], preferred_element_type=F32)
    xn = _rms(x1, g_ref[...]).astype(BF16)
    d_ff = wg_ref.shape[1]
    for c0 in range(0, d_ff, ff_tile):
        g = jnp.dot(xn, wg_ref[:, c0:c0 + ff_tile], preferred_element_type=F32)
        u = jnp.dot(xn, wu_ref[:, c0:c0 + ff_tile], preferred_element_type=F32)
        h_ref[:, c0:c0 + ff_tile] = (g * _sigmoid(g) * u).astype(BF16)
    o_ref[...] = x1 + jnp.dot(h_ref[...], wd_ref[...], preferred_element_type=F32)


def out_ffn(x, y, w_o, gain, w_gate, w_up, w_down):
    m, d = x.shape
    dy = y.shape[1]
    d_ff = w_gate.shape[1]
    tm = min(ROW_TILE, m)
    return pl.pallas_call(
        functools.partial(_out_ffn_kernel, ff_tile=FF_TILE),
        grid=(m // tm,),
        in_specs=[pl.BlockSpec((tm, d), lambda i: (i, 0)),
                  pl.BlockSpec((tm, dy), lambda i: (i, 0)),
                  _const_spec((dy, d)),
                  _const_spec((1, d)),
                  _const_spec((d, d_ff)),
                  _const_spec((d, d_ff)),
                  _const_spec((d_ff, d))],
        out_specs=pl.BlockSpec((tm, d), lambda i: (i, 0)),
        out_shape=jax.ShapeDtypeStruct((m, d), F32),
        scratch_shapes=[pltpu.VMEM((tm, d_ff), BF16)],
        compiler_params=_params(1),
        name="out_ffn",
    )(x, y, w_o, gain.reshape(1, d), w_gate, w_up, w_down)


def _hybrid_kernel(proj_ref, gates_ref, wg2_ref, bg_ref, gn_ref, conv_ref, bsmall_ref, bcol_ref, mn_ref,
                   y_ref, gla_st, ml_cs, ml_ns, ml_m, conv_carry, qk_ref):
    t_tile = proj_ref.shape[0]
    n_chunks = t_tile // CHUNK

    @pl.when(pl.program_id(1) == 0)
    def _():
        gla_st[...] = jnp.zeros_like(gla_st)
        ml_cs[...] = jnp.zeros_like(ml_cs)
        ml_ns[...] = jnp.zeros_like(ml_ns)
        ml_m[...] = jnp.zeros_like(ml_m)
        conv_carry[...] = jnp.zeros_like(conv_carry)

    xcat = proj_ref[:, C_MQ:C_MQ + 2 * ML_QK]
    ext = jnp.concatenate([conv_carry[...], xcat], axis=0)
    cw = conv_ref[...]
    acc = ext * cw[ML_CONV - 1:ML_CONV]
    for j in range(1, ML_CONV):
        acc = acc + pltpu.roll(ext, j, axis=0) * cw[ML_CONV - 1 - j:ML_CONV - j]
    conv = acc[SUBLANES:]
    qk_ref[...] = conv * _sigmoid(conv)
    conv_carry[...] = xcat[t_tile - SUBLANES:]

    row = lax.broadcasted_iota(jnp.int32, (CHUNK, CHUNK), 0)
    col = lax.broadcasted_iota(jnp.int32, (CHUNK, CHUNK), 1)
    causal = row >= col
    tril = causal.astype(F32)
    triu = (row <= col).astype(F32)
    lane = lax.broadcasted_iota(jnp.int32, (1, LANES), 1)
    head_mask = (lane < GLA_DK, lane >= GLA_DK)

    def chunk_body(c, carry):
        r0 = pl.multiple_of(c * CHUNK, CHUNK)
        rows = pl.ds(r0, CHUNK)
        small = proj_ref[rows, C_SMALL:C_SMALL + LANES]

        z = _dot_f32(small[:, S_GLR:S_GLR + GLA_RANK], wg2_ref[...]) + bg_ref[...]
        log_a = _log_sigmoid(z) * (1.0 / GLA_TAU)
        bc_all = _dot_f32(tril, log_a)
        for p in range(GLA_HEADS // 2):
            lanes = slice(p * LANES, (p + 1) * LANES)
            q2 = proj_ref[rows, C_GQ + p * LANES:C_GQ + (p + 1) * LANES] * (GLA_DK ** -0.5)
            k2 = proj_ref[rows, C_GK + p * LANES:C_GK + (p + 1) * LANES]
            bc = bc_all[:, lanes]
            st = gla_st[p]
            q_in = q2 * jnp.exp(bc)
            q_sub, k_sub = [], []
            for i in range(CHUNK // SUB):
                rs = slice(i * SUB, (i + 1) * SUB)
                ref_pt = bc[i * SUB - 1:i * SUB] if i else jnp.zeros((1, LANES), F32)
                q_sub.append(q2[rs] * jnp.exp(bc[rs] - ref_pt))
                k_sub.append(k2 * jnp.exp(ref_pt - bc))
            last = bc[CHUNK - 1:CHUNK]
            k_dec = k2 * jnp.exp(last - bc)
            upd = st * jnp.exp(last)
            for hl in range(2):
                h = 2 * p + hl
                hm = head_mask[hl]
                sc = jnp.concatenate(
                    [_dot_nt(jnp.where(hm, q_sub[i], 0.0), k_sub[i]) for i in range(CHUNK // SUB)], axis=0)
                sc = jnp.where(causal, sc, 0.0)
                v = proj_ref[rows, C_GV + h * GLA_DV:C_GV + (h + 1) * GLA_DV]
                o = _dot(sc, v) + _dot_nt(jnp.where(hm, q_in, 0.0), st)
                gate = proj_ref[rows, C_GR + h * GLA_DV:C_GR + (h + 1) * GLA_DV]
                yg = _rms(o, gn_ref[:, h * GLA_DV:(h + 1) * GLA_DV]) * (gate * _sigmoid(gate))
                y_ref[rows, h * GLA_DV:(h + 1) * GLA_DV] = yg.astype(y_ref.dtype)
                upd = upd + _dot_tn(v, jnp.where(hm, k_dec, 0.0))
            gla_st[p] = upd

        g_rows = gates_ref[0, c] + bcol_ref[...]
        ls_rows = _log_sigmoid(g_rows)
        bc_rows = _dot_f32(ls_rows, triu)
        smallb = small + bsmall_ref[...]
        bc_cols = _dot_f32(tril, _log_sigmoid(smallb))
        for p in range(ML_HEADS // 2):
            mq2 = qk_ref[rows, p * LANES:(p + 1) * LANES]
            mk2 = qk_ref[rows, ML_QK + p * LANES:ML_QK + (p + 1) * LANES] * (ML_DK ** -0.5)
            cs = ml_cs[p]
            ns = ml_ns[p]
            cs_new = jnp.zeros_like(cs)
            ns_new = jnp.zeros_like(ns)
            carry_row = jnp.zeros((1, LANES), F32)
            for hl in range(2):
                h = 2 * p + hl
                hm = head_mask[hl]
                i_row = g_rows[h:h + 1]
                bc_row = bc_rows[ML_HEADS + h:ML_HEADS + h + 1]
                b_row = i_row - bc_row
                bc_last = bc_row[:, CHUNK - 1:CHUNK]
                i_col = smallb[:, S_MI + h:S_MI + h + 1]
                bc_col = bc_cols[:, S_MF + h:S_MF + h + 1]
                m_prev = ml_m[h:h + 1, 0:1]

                b_mat = jnp.where(causal, b_row, -jnp.inf)
                m_col = jnp.maximum(m_prev, jnp.max(b_mat, axis=1, keepdims=True))
                w_dec = jnp.exp(b_mat - m_col)
                qm = jnp.where(hm, mq2, 0.0)
                w = _dot_nt(qm, mk2) * w_dec
                inter = jnp.exp(m_prev - m_col)
                v = proj_ref[rows, C_MV + h * ML_DV:C_MV + (h + 1) * ML_DV]
                num = inter * _dot_nt(qm, cs) + _dot(w, v)
                den = inter * jnp.sum(qm * ns, axis=1, keepdims=True) + jnp.sum(w, axis=1, keepdims=True)
                hout = num / jnp.maximum(jnp.abs(den), jnp.exp(-(bc_col + m_col)))
                og = proj_ref[rows, C_MO + h * ML_DV:C_MO + (h + 1) * ML_DV]
                ym = _rms(hout * _sigmoid(og), mn_ref[:, h * ML_DV:(h + 1) * ML_DV])
                y_ref[rows, GLA_V + h * ML_DV:GLA_V + (h + 1) * ML_DV] = ym.astype(y_ref.dtype)

                m_new = bc_last + jnp.maximum(m_prev, jnp.max(b_row, axis=1, keepdims=True))
                scale = jnp.exp(bc_last + m_prev - m_new)
                wk = jnp.where(hm, mk2 * jnp.exp(bc_last - bc_col + i_col - m_new), 0.0)
                cs_new = cs_new + _dot_tn(v, wk)
                ns_new = ns_new + jnp.sum(wk, axis=0, keepdims=True)
                carry_row = jnp.where(hm, scale, carry_row)
                ml_m[h:h + 1, :] = jnp.broadcast_to(m_new, (1, LANES))
            ml_cs[p] = cs * carry_row + cs_new
            ml_ns[p] = ns * carry_row + ns_new
        return carry

    lax.fori_loop(0, n_chunks, chunk_body, 0)


def hybrid_core(proj, gates_t, w_gate2, b_gate, gla_norm, conv_w, b_small, b_col, ml_norm, batch):
    m = proj.shape[0]
    seq = m // batch
    t = min(MIX_TILE, seq)
    nt = seq // t
    return pl.pallas_call(
        _hybrid_kernel,
        grid=(batch, nt),
        in_specs=[pl.BlockSpec((t, HYB_COLS), lambda b, i: (b * nt + i, 0)),
                  pl.BlockSpec((1, t // CHUNK, 2 * ML_HEADS, CHUNK), lambda b, i: (b, i, 0, 0)),
                  _const_spec((GLA_RANK, GLA_QK)),
                  _const_spec((1, GLA_QK)),
                  _const_spec((1, GLA_V)),
                  _const_spec((ML_CONV, 2 * ML_QK)),
                  _const_spec((1, LANES)),
                  _const_spec((2 * ML_HEADS, 1)),
                  _const_spec((1, ML_V))],
        out_specs=pl.BlockSpec((t, GLA_V + ML_V), lambda b, i: (b * nt + i, 0)),
        out_shape=jax.ShapeDtypeStruct((m, GLA_V + ML_V), BF16),
        scratch_shapes=[pltpu.VMEM((GLA_HEADS // 2, GLA_DV, LANES), F32),
                        pltpu.VMEM((ML_HEADS // 2, ML_DV, LANES), F32),
                        pltpu.VMEM((ML_HEADS // 2, 1, LANES), F32),
                        pltpu.VMEM((2 * ML_HEADS, LANES), F32),
                        pltpu.VMEM((SUBLANES, 2 * ML_QK), F32),
                        pltpu.VMEM((t, 2 * ML_QK), F32)],
        compiler_params=_params(2),
        name="hybrid_core",
    )(proj, gates_t, w_gate2, b_gate, gla_norm, conv_w, b_small, b_col, ml_norm)


def hybrid_layer(x, gain, w_in, w_gate2, b_gate, gla_norm, conv_w, b_if, ml_norm, batch):
    m, d = x.shape
    seq = m // batch
    segs = np.cumsum([0, GLA_QK, GLA_QK, GLA_V, GLA_V, GLA_RANK, ML_QK, ML_QK, ML_V, ML_V, ML_HEADS, ML_HEADS])
    seg = lambda i: w_in[:, int(segs[i]):int(segs[i + 1])]
    pad = jnp.zeros((d, LANES - GLA_RANK - 2 * ML_HEADS), w_in.dtype)
    w_perm = jnp.concatenate([seg(0), seg(1), seg(2), seg(3), seg(5), seg(6), seg(7), seg(8),
                              seg(4), seg(9), seg(10), pad], axis=1).astype(BF16)
    proj = norm_proj(x, gain, w_perm)
    gates = proj[:, C_SMALL + S_MI:C_SMALL + S_MI + 2 * ML_HEADS]
    gates_t = gates.reshape(batch, seq // CHUNK, CHUNK, 2 * ML_HEADS).transpose(0, 1, 3, 2)
    b_small = jnp.zeros((1, LANES), F32).at[0, S_MI:S_MI + 2 * ML_HEADS].set(b_if)
    return hybrid_core(proj, gates_t, w_gate2, b_gate.reshape(1, -1), gla_norm.reshape(1, -1), conv_w,
                       b_small, b_if.reshape(-1, 1), ml_norm.reshape(1, -1), batch)


def _moba_prep_kernel(qkv_ref, qg_ref, kg_ref, qn_ref, kn_ref, vt_ref, sel_ref, kmean_ref):
    i = pl.program_id(1)
    nb = kmean_ref.shape[1]

    @pl.when(i == 0)
    def _():
        kmean_ref[...] = jnp.zeros_like(kmean_ref)

    blk = lax.broadcasted_iota(jnp.int32, (nb, MOBA_BLOCK), 0)
    past = blk < i
    for h in range(MOBA_HEADS):
        cols = slice(h * MOBA_HD, (h + 1) * MOBA_HD)
        q = _rms(qkv_ref[:, cols], qg_ref[...]) * (MOBA_HD ** -0.5)
        k = _rms(qkv_ref[:, MOBA_HEADS * MOBA_HD + h * MOBA_HD:MOBA_HEADS * MOBA_HD + (h + 1) * MOBA_HD],
                 kg_ref[...])
        v = qkv_ref[:, 2 * MOBA_HEADS * MOBA_HD + h * MOBA_HD:2 * MOBA_HEADS * MOBA_HD + (h + 1) * MOBA_HD]
        qn_ref[:, cols] = q.astype(qn_ref.dtype)
        kn_ref[:, cols] = k.astype(kn_ref.dtype)
        vt_ref[0, h] = v.T.astype(vt_ref.dtype)
        gate = lax.dot_general(kmean_ref[h], q, (((1,), (1,)), ((), ())), preferred_element_type=F32,
                               precision=HIGHEST)
        gate = jnp.where(past, gate, -jnp.inf)
        rank = jnp.zeros((nb, MOBA_BLOCK), F32)
        for j in range(nb):
            other = gate[j:j + 1]
            ahead = (other > gate) | ((other == gate) & (j < blk))
            rank = rank + ahead.astype(F32)
        sel_ref[0, h] = (past & (rank < MOBA_TOPK)).astype(sel_ref.dtype)
        kmean_ref[h, pl.ds(i, 1), :] = jnp.mean(k, axis=0, keepdims=True)


def moba_prep(qkv, q_gain, k_gain, batch):
    m = qkv.shape[0]
    seq = m // batch
    nb = seq // MOBA_BLOCK
    d = MOBA_HEADS * MOBA_HD
    return pl.pallas_call(
        _moba_prep_kernel,
        grid=(batch, nb),
        in_specs=[pl.BlockSpec((MOBA_BLOCK, 3 * d), lambda b, i: (b * nb + i, 0)),
                  _const_spec((1, MOBA_HD)),
                  _const_spec((1, MOBA_HD))],
        out_specs=[pl.BlockSpec((MOBA_BLOCK, d), lambda b, i: (b * nb + i, 0)),
                   pl.BlockSpec((MOBA_BLOCK, d), lambda b, i: (b * nb + i, 0)),
                   pl.BlockSpec((1, MOBA_HEADS, MOBA_HD, MOBA_BLOCK), lambda b, i: (b, 0, 0, i)),
                   pl.BlockSpec((1, MOBA_HEADS, nb, MOBA_BLOCK), lambda b, i: (b, 0, 0, i))],
        out_shape=[jax.ShapeDtypeStruct((m, d), BF16),
                   jax.ShapeDtypeStruct((m, d), BF16),
                   jax.ShapeDtypeStruct((batch, MOBA_HEADS, MOBA_HD, seq), BF16),
                   jax.ShapeDtypeStruct((batch, MOBA_HEADS, nb, seq), F32)],
        scratch_shapes=[pltpu.VMEM((MOBA_HEADS, nb, MOBA_HD), F32)],
        compiler_params=_params(2),
        name="moba_prep",
    )(qkv, q_gain.reshape(1, -1), k_gain.reshape(1, -1))


def _moba_attn_kernel(slopes_ref, q_ref, k_ref, vt_ref, sel_ref, o_ref):
    i = pl.program_id(2)
    slope = slopes_ref[pl.program_id(1)]
    q = q_ref[...]
    key_r = lax.broadcasted_iota(jnp.int32, (MOBA_BLOCK, MOBA_BLOCK), 0)
    qry_c = lax.broadcasted_iota(jnp.int32, (MOBA_BLOCK, MOBA_BLOCK), 1)
    delta = (qry_c - key_r).astype(F32)

    def block(j, carry, diagonal):
        m_run, l_run, acc = carry
        k0 = pl.multiple_of(j * MOBA_BLOCK, MOBA_BLOCK)
        kb = k_ref[pl.ds(k0, MOBA_BLOCK), :]
        s = lax.dot_general(kb, q, (((1,), (1,)), ((), ())), preferred_element_type=F32)
        dist = delta + jnp.full((1, MOBA_BLOCK), (i - j) * MOBA_BLOCK, jnp.int32).astype(F32)
        if diagonal:
            allowed = qry_c >= key_r
        else:
            allowed = sel_ref[pl.ds(j, 1), :] > 0.5
        s = jnp.where(allowed, s - slope * dist, NEG)
        m_new = jnp.maximum(m_run, jnp.max(s, axis=0, keepdims=True))
        alpha = jnp.exp(m_run - m_new)
        p = jnp.where(allowed, jnp.exp(s - m_new), 0.0)
        l_new = alpha * l_run + jnp.sum(p, axis=0, keepdims=True)
        vb = vt_ref[:, pl.ds(k0, MOBA_BLOCK)]
        acc = alpha * acc + jnp.dot(vb, p.astype(BF16), preferred_element_type=F32)
        return m_new, l_new, acc

    init = (jnp.full((1, MOBA_BLOCK), NEG, F32), jnp.zeros((1, MOBA_BLOCK), F32),
            jnp.zeros((MOBA_HD, MOBA_BLOCK), F32))
    carry = lax.fori_loop(0, i, lambda j, c: block(j, c, False), init)
    _, l_fin, acc = block(i, carry, True)
    o_ref[...] = (acc / l_fin).T.astype(o_ref.dtype)


def moba_attention(qn, kn, vt, sel, batch):
    m, d = qn.shape
    seq = m // batch
    nb = seq // MOBA_BLOCK
    slopes = jnp.asarray([2.0 ** (-8.0 * (h + 1) / MOBA_HEADS) for h in range(MOBA_HEADS)], F32)
    return pl.pallas_call(
        _moba_attn_kernel,
        grid=(batch, MOBA_HEADS, nb),
        in_specs=[pl.BlockSpec(memory_space=pltpu.SMEM),
                  pl.BlockSpec((MOBA_BLOCK, MOBA_HD), lambda b, h, i: (b * nb + i, h)),
                  pl.BlockSpec((seq, MOBA_HD), lambda b, h, i: (b, h)),
                  pl.BlockSpec((None, None, MOBA_HD, seq), lambda b, h, i: (b, h, 0, 0)),
                  pl.BlockSpec((None, None, nb, MOBA_BLOCK), lambda b, h, i: (b, h, 0, i))],
        out_specs=pl.BlockSpec((MOBA_BLOCK, MOBA_HD), lambda b, h, i: (b * nb + i, h)),
        out_shape=jax.ShapeDtypeStruct((m, d), BF16),
        compiler_params=_params(3),
        name="moba_attn",
    )(slopes, qn, kn, vt, sel)


def moba_layer(x, gain, w_qkv, q_gain, k_gain, batch):
    qkv = norm_proj(x, gain, w_qkv.astype(BF16))
    qn, kn, vt, sel = moba_prep(qkv, q_gain, k_gain, batch)
    return moba_attention(qn, kn, vt, sel, batch)


def kernel(x, norm_mix, norm_ffn, hyb_w_in, gla_w_gate2, gla_b_gate, gla_norm, ml_conv, ml_b_if, ml_norm,
           hyb_w_out, moba_w_qkv, moba_q_norm, moba_k_norm, moba_w_o, ffn_w_gate, ffn_w_up, ffn_w_down):
    batch, seq, d = x.shape
    depth = norm_mix.shape[0]
    xf = x.reshape(batch * seq, d)
    for l in range(depth):
        j = l // 2
        if l % 2 == 0:
            y = hybrid_layer(xf, norm_mix[l], hyb_w_in[j], gla_w_gate2[j], gla_b_gate[j], gla_norm[j],
                             ml_conv[j], ml_b_if[j], ml_norm[j], batch)
            w_o = hyb_w_out[j]
        else:
            y = moba_layer(xf, norm_mix[l], moba_w_qkv[j], moba_q_norm[j], moba_k_norm[j], batch)
            w_o = moba_w_o[j]
        xf = out_ffn(xf, y, w_o.astype(BF16), norm_ffn[l], ffn_w_gate[l].astype(BF16),
                     ffn_w_up[l].astype(BF16), ffn_w_down[l].astype(BF16))
    return xf.reshape(batch, seq, d)
```

```python
import functools

import jax
import jax.numpy as jnp
import numpy as np
from jax import lax
from jax.experimental import pallas as pl
from jax.experimental.pallas import tpu as pltpu

F32 = jnp.float32
BF16 = jnp.bfloat16
HIGHEST = lax.Precision.HIGHEST

LANES = 128
SUBLANES = 8
VMEM_LIMIT_BYTES = 56 * 1024 * 1024

EPS = 1e-6
NEG = -1e30

GLA_HEADS = 4
GLA_DK = 64
GLA_DV = 128
GLA_RANK = 16
GLA_TAU = 16.0
ML_HEADS = 4
ML_DK = 64
ML_DV = 128
ML_CONV = 4
CHUNK = 64
SUB = 16
MOBA_HEADS = 8
MOBA_HD = 128
MOBA_BLOCK = 256
MOBA_TOPK = 3

GLA_QK = GLA_HEADS * GLA_DK
GLA_V = GLA_HEADS * GLA_DV
ML_QK = ML_HEADS * ML_DK
ML_V = ML_HEADS * ML_DV

C_GQ = 0
C_GK = C_GQ + GLA_QK
C_GV = C_GK + GLA_QK
C_GR = C_GV + GLA_V
C_MQ = C_GR + GLA_V
C_MK = C_MQ + ML_QK
C_MV = C_MK + ML_QK
C_MO = C_MV + ML_V
C_SMALL = C_MO + ML_V
HYB_COLS = C_SMALL + LANES
S_GLR = 0
S_MI = GLA_RANK
S_MF = GLA_RANK + ML_HEADS

ROW_TILE = 512
MIX_TILE = 256
FF_TILE = 256


def _rms(x, gain):
    return x * lax.rsqrt(jnp.mean(x * x, axis=-1, keepdims=True) + EPS) * gain


def _log_sigmoid(z):
    return jnp.minimum(z, 0.0) - jnp.log1p(jnp.exp(-jnp.abs(z)))


def _sigmoid(z):
    return 1.0 / (1.0 + jnp.exp(-z))


def _dot(a, b):
    return jnp.dot(a.astype(BF16), b.astype(BF16), preferred_element_type=F32)


def _dot_nt(a, b):
    return lax.dot_general(a.astype(BF16), b.astype(BF16), (((1,), (1,)), ((), ())),
                           preferred_element_type=F32)


def _dot_tn(a, b):
    return lax.dot_general(a.astype(BF16), b.astype(BF16), (((0,), (0,)), ((), ())),
                           preferred_element_type=F32)


def _dot_f32(a, b):
    return jnp.dot(a, b, preferred_element_type=F32, precision=HIGHEST)


def _const_spec(shape):
    return pl.BlockSpec(shape, lambda *_: (0,) * len(shape), pipeline_mode=pl.Buffered(1))


def _params(n_grid):
    return pltpu.CompilerParams(dimension_semantics=("arbitrary",) * n_grid,
                                vmem_limit_bytes=VMEM_LIMIT_BYTES)


def _norm_proj_kernel(x_ref, g_ref, w_ref, o_ref, *, col_tile):
    xn = _rms(x_ref[...], g_ref[...]).astype(BF16)
    n = o_ref.shape[1]
    for c0 in range(0, n, col_tile):
        c1 = min(c0 + col_tile, n)
        o_ref[:, c0:c1] = jnp.dot(xn, w_ref[:, c0:c1], preferred_element_type=F32)


def norm_proj(x, gain, w):
    m, d = x.shape
    n = w.shape[1]
    tm = min(ROW_TILE, m)
    return pl.pallas_call(
        functools.partial(_norm_proj_kernel, col_tile=512),
        grid=(m // tm,),
        in_specs=[pl.BlockSpec((tm, d), lambda i: (i, 0)),
                  _const_spec((1, d)),
                  _const_spec((d, n))],
        out_specs=pl.BlockSpec((tm, n), lambda i: (i, 0)),
        out_shape=jax.ShapeDtypeStruct((m, n), F32),
        compiler_params=_params(1),
        name="norm_proj",
    )(x, gain.reshape(1, d), w)


def _out_ffn_kernel(x_ref, y_ref, wo_ref, g_ref, wg_ref, wu_ref, wd_ref, o_ref, h_ref, *, ff_tile):
    x1 = x_ref[...] + jnp.dot(y_ref[...], wo_ref[...], preferred_element_type=F32)
    xn = _rms(x1, g_ref[...]).astype(BF16)
    d_ff = wg_ref.shape[1]
    for c0 in range(0, d_ff, ff_tile):
        g = jnp.dot(xn, wg_ref[:, c0:c0 + ff_tile], preferred_element_type=F32)
        u = jnp.dot(xn, wu_ref[:, c0:c0 + ff_tile], preferred_element_type=F32)
        h_ref[:, c0:c0 + ff_tile] = (g * _sigmoid(g) * u).astype(BF16)
    o_ref[...] = x1 + jnp.dot(h_ref[...], wd_ref[...], preferred_element_type=F32)


def out_ffn(x, y, w_o, gain, w_gate, w_up, w_down):
    m, d = x.shape
    dy = y.shape[1]
    d_ff = w_gate.shape[1]
    tm = min(ROW_TILE, m)
    return pl.pallas_call(
        functools.partial(_out_ffn_kernel, ff_tile=FF_TILE),
        grid=(m // tm,),
        in_specs=[pl.BlockSpec((tm, d), lambda i: (i, 0)),
                  pl.BlockSpec((tm, dy), lambda i: (i, 0)),
                  _const_spec((dy, d)),
                  _const_spec((1, d)),
                  _const_spec((d, d_ff)),
                  _const_spec((d, d_ff)),
                  _const_spec((d_ff, d))],
        out_specs=pl.BlockSpec((tm, d), lambda i: (i, 0)),
        out_shape=jax.ShapeDtypeStruct((m, d), F32),
        scratch_shapes=[pltpu.VMEM((tm, d_ff), BF16)],
        compiler_params=_params(1),
        name="out_ffn",
    )(x, y, w_o, gain.reshape(1, d), w_gate, w_up, w_down)


def _hybrid_kernel(proj_ref, gates_ref, wg2_ref, bg_ref, gn_ref, conv_ref, bsmall_ref, bcol_ref, mn_ref,
                   y_ref, gla_st, ml_cs, ml_ns, ml_m, conv_carry, qk_ref):
    t_tile = proj_ref.shape[0]
    n_chunks = t_tile // CHUNK

    @pl.when(pl.program_id(1) == 0)
    def _():
        gla_st[...] = jnp.zeros_like(gla_st)
        ml_cs[...] = jnp.zeros_like(ml_cs)
        ml_ns[...] = jnp.zeros_like(ml_ns)
        ml_m[...] = jnp.zeros_like(ml_m)
        conv_carry[...] = jnp.zeros_like(conv_carry)

    xcat = proj_ref[:, C_MQ:C_MQ + 2 * ML_QK]
    ext = jnp.concatenate([conv_carry[...], xcat], axis=0)
    cw = conv_ref[...]
    acc = ext * cw[ML_CONV - 1:ML_CONV]
    for j in range(1, ML_CONV):
        acc = acc + pltpu.roll(ext, j, axis=0) * cw[ML_CONV - 1 - j:ML_CONV - j]
    conv = acc[SUBLANES:]
    qk_ref[...] = conv * _sigmoid(conv)
    conv_carry[...] = xcat[t_tile - SUBLANES:]

    row = lax.broadcasted_iota(jnp.int32, (CHUNK, CHUNK), 0)
    col = lax.broadcasted_iota(jnp.int32, (CHUNK, CHUNK), 1)
    causal = row >= col
    tril = causal.astype(F32)
    triu = (row <= col).astype(F32)
    lane = lax.broadcasted_iota(jnp.int32, (1, LANES), 1)
    head_mask = (lane < GLA_DK, lane >= GLA_DK)

    def chunk_body(c, carry):
        r0 = pl.multiple_of(c * CHUNK, CHUNK)
        rows = pl.ds(r0, CHUNK)
        small = proj_ref[rows, C_SMALL:C_SMALL + LANES]

        z = _dot_f32(small[:, S_GLR:S_GLR + GLA_RANK], wg2_ref[...]) + bg_ref[...]
        log_a = _log_sigmoid(z) * (1.0 / GLA_TAU)
        bc_all = _dot_f32(tril, log_a)
        for p in range(GLA_HEADS // 2):
            lanes = slice(p * LANES, (p + 1) * LANES)
            q2 = proj_ref[rows, C_GQ + p * LANES:C_GQ + (p + 1) * LANES] * (GLA_DK ** -0.5)
            k2 = proj_ref[rows, C_GK + p * LANES:C_GK + (p + 1) * LANES]
            bc = bc_all[:, lanes]
            st = gla_st[p]
            q_in = q2 * jnp.exp(bc)
            q_sub, k_sub = [], []
            for i in range(CHUNK // SUB):
                rs = slice(i * SUB, (i + 1) * SUB)
                ref_pt = bc[i * SUB - 1:i * SUB] if i else jnp.zeros((1, LANES), F32)
                q_sub.append(q2[rs] * jnp.exp(bc[rs] - ref_pt))
                k_sub.append(k2 * jnp.exp(ref_pt - bc))
            last = bc[CHUNK - 1:CHUNK]
            k_dec = k2 * jnp.exp(last - bc)
            upd = st * jnp.exp(last)
            for hl in range(2):
                h = 2 * p + hl
                hm = head_mask[hl]
                sc = jnp.concatenate(
                    [_dot_nt(jnp.where(hm, q_sub[i], 0.0), k_sub[i]) for i in range(CHUNK // SUB)], axis=0)
                sc = jnp.where(causal, sc, 0.0)
                v = proj_ref[rows, C_GV + h * GLA_DV:C_GV + (h + 1) * GLA_DV]
                o = _dot(sc, v) + _dot_nt(jnp.where(hm, q_in, 0.0), st)
                gate = proj_ref[rows, C_GR + h * GLA_DV:C_GR + (h + 1) * GLA_DV]
                yg = _rms(o, gn_ref[:, h * GLA_DV:(h + 1) * GLA_DV]) * (gate * _sigmoid(gate))
                y_ref[rows, h * GLA_DV:(h + 1) * GLA_DV] = yg.astype(y_ref.dtype)
                upd = upd + _dot_tn(v, jnp.where(hm, k_dec, 0.0))
            gla_st[p] = upd

        g_rows = gates_ref[0, c] + bcol_ref[...]
        ls_rows = _log_sigmoid(g_rows)
        bc_rows = _dot_f32(ls_rows, triu)
        smallb = small + bsmall_ref[...]
        bc_cols = _dot_f32(tril, _log_sigmoid(smallb))
        for p in range(ML_HEADS // 2):
            mq2 = qk_ref[rows, p * LANES:(p + 1) * LANES]
            mk2 = qk_ref[rows, ML_QK + p * LANES:ML_QK + (p + 1) * LANES] * (ML_DK ** -0.5)
            cs = ml_cs[p]
            ns = ml_ns[p]
            cs_new = jnp.zeros_like(cs)
            ns_new = jnp.zeros_like(ns)
            carry_row = jnp.zeros((1, LANES), F32)
            for hl in range(2):
                h = 2 * p + hl
                hm = head_mask[hl]
                i_row = g_rows[h:h + 1]
                bc_row = bc_rows[ML_HEADS + h:ML_HEADS + h + 1]
                b_row = i_row - bc_row
                bc_last = bc_row[:, CHUNK - 1:CHUNK]
                i_col = smallb[:, S_MI + h:S_MI + h + 1]
                bc_col = bc_cols[:, S_MF + h:S_MF + h + 1]
                m_prev = ml_m[h:h + 1, 0:1]

                b_mat = jnp.where(causal, b_row, -jnp.inf)
                m_col = jnp.maximum(m_prev, jnp.max(b_mat, axis=1, keepdims=True))
                w_dec = jnp.exp(b_mat - m_col)
                qm = jnp.where(hm, mq2, 0.0)
                w = _dot_nt(qm, mk2) * w_dec
                inter = jnp.exp(m_prev - m_col)
                v = proj_ref[rows, C_MV + h * ML_DV:C_MV + (h + 1) * ML_DV]
                num = inter * _dot_nt(qm, cs) + _dot(w, v)
                den = inter * jnp.sum(qm * ns, axis=1, keepdims=True) + jnp.sum(w, axis=1, keepdims=True)
                hout = num / jnp.maximum(jnp.abs(den), jnp.exp(-(bc_col + m_col)))
                og = proj_ref[rows, C_MO + h * ML_DV:C_MO + (h + 1) * ML_DV]
                ym = _rms(hout * _sigmoid(og), mn_ref[:, h * ML_DV:(h + 1) * ML_DV])
                y_ref[rows, GLA_V + h * ML_DV:GLA_V + (h + 1) * ML_DV] = ym.astype(y_ref.dtype)

                m_new = bc_last + jnp.maximum(m_prev, jnp.max(b_row, axis=1, keepdims=True))
                scale = jnp.exp(bc_last + m_prev - m_new)
                wk = jnp.where(hm, mk2 * jnp.exp(bc_last - bc_col + i_col - m_new), 0.0)
                cs_new = cs_new + _dot_tn(v, wk)
                ns_new = ns_new + jnp.sum(wk, axis=0, keepdims=True)
                carry_row = jnp.where(hm, scale, carry_row)
                ml_m[h:h + 1, :] = jnp.broadcast_to(m_new, (1, LANES))
            ml_cs[p] = cs * carry_row + cs_new
            ml_ns[p] = ns * carry_row + ns_new
        return carry

    lax.fori_loop(0, n_chunks, chunk_body, 0)


def hybrid_core(proj, gates_t, w_gate2, b_gate, gla_norm, conv_w, b_small, b_col, ml_norm, batch):
    m = proj.shape[0]
    seq = m // batch
    t = min(MIX_TILE, seq)
    nt = seq // t
    return pl.pallas_call(
        _hybrid_kernel,
        grid=(batch, nt),
        in_specs=[pl.BlockSpec((t, HYB_COLS), lambda b, i: (b * nt + i, 0)),
                  pl.BlockSpec((1, t // CHUNK, 2 * ML_HEADS, CHUNK), lambda b, i: (b, i, 0, 0)),
                  _const_spec((GLA_RANK, GLA_QK)),
                  _const_spec((1, GLA_QK)),
                  _const_spec((1, GLA_V)),
                  _const_spec((ML_CONV, 2 * ML_QK)),
                  _const_spec((1, LANES)),
                  _const_spec((2 * ML_HEADS, 1)),
                  _const_spec((1, ML_V))],
        out_specs=pl.BlockSpec((t, GLA_V + ML_V), lambda b, i: (b * nt + i, 0)),
        out_shape=jax.ShapeDtypeStruct((m, GLA_V + ML_V), BF16),
        scratch_shapes=[pltpu.VMEM((GLA_HEADS // 2, GLA_DV, LANES), F32),
                        pltpu.VMEM((ML_HEADS // 2, ML_DV, LANES), F32),
                        pltpu.VMEM((ML_HEADS // 2, 1, LANES), F32),
                        pltpu.VMEM((2 * ML_HEADS, LANES), F32),
                        pltpu.VMEM((SUBLANES, 2 * ML_QK), F32),
                        pltpu.VMEM((t, 2 * ML_QK), F32)],
        compiler_params=_params(2),
        name="hybrid_core",
    )(proj, gates_t, w_gate2, b_gate, gla_norm, conv_w, b_small, b_col, ml_norm)


def hybrid_layer(x, gain, w_in, w_gate2, b_gate, gla_norm, conv_w, b_if, ml_norm, batch):
    m, d = x.shape
    seq = m // batch
    segs = np.cumsum([0, GLA_QK, GLA_QK, GLA_V, GLA_V, GLA_RANK, ML_QK, ML_QK, ML_V, ML_V, ML_HEADS, ML_HEADS])
    seg = lambda i: w_in[:, int(segs[i]):int(segs[i + 1])]
    pad = jnp.zeros((d, LANES - GLA_RANK - 2 * ML_HEADS), w_in.dtype)
    w_perm = jnp.concatenate([seg(0), seg(1), seg(2), seg(3), seg(5), seg(6), seg(7), seg(8),
                              seg(4), seg(9), seg(10), pad], axis=1).astype(BF16)
    proj = norm_proj(x, gain, w_perm)
    gates = proj[:, C_SMALL + S_MI:C_SMALL + S_MI + 2 * ML_HEADS]
    gates_t = gates.reshape(batch, seq // CHUNK, CHUNK, 2 * ML_HEADS).transpose(0, 1, 3, 2)
    b_small = jnp.zeros((1, LANES), F32).at[0, S_MI:S_MI + 2 * ML_HEADS].set(b_if)
    return hybrid_core(proj, gates_t, w_gate2, b_gate.reshape(1, -1), gla_norm.reshape(1, -1), conv_w,
                       b_small, b_if.reshape(-1, 1), ml_norm.reshape(1, -1), batch)


LOG2E = 1.4426950408889634
AUG = 2 * MOBA_HD
LANES_PER_BLOCK = 4
VT_ROWS = MOBA_HD + 16


def _split3(x):
    hi = x.astype(BF16)
    r1 = x - hi.astype(F32)
    mid = r1.astype(BF16)
    lo = (r1 - mid.astype(F32)).astype(BF16)
    return hi.astype(F32), mid.astype(F32), lo.astype(F32)


def _moba_prep_kernel(slopes_ref, qkv_ref, qg_ref, kg_ref, qa_ref, ka_ref, vt_ref, kmean_ref):
    i = pl.program_id(1)
    nb = kmean_ref.shape[1]
    d = MOBA_HEADS * MOBA_HD

    @pl.when(i == 0)
    def _():
        kmean_ref[...] = jnp.zeros_like(kmean_ref)

    blk = lax.broadcasted_iota(jnp.int32, (nb, MOBA_BLOCK), 0)
    past = blk < i
    lane = lax.broadcasted_iota(jnp.int32, (1, LANES), 1)
    lane_blk = lane // LANES_PER_BLOCK
    lane_piece = lane % LANES_PER_BLOCK
    in_blocks = lane < LANES_PER_BLOCK * nb
    expand = ((lax.broadcasted_iota(jnp.int32, (nb, LANES), 0) == lane_blk) & in_blocks).astype(BF16)
    own = (lane_blk == i) & in_blocks
    t_loc = lax.broadcasted_iota(jnp.int32, (MOBA_BLOCK, LANES), 0).astype(F32)
    start_gap = ((i - lane_blk) * MOBA_BLOCK).astype(F32)
    ones_rows = (lax.broadcasted_iota(jnp.int32, (VT_ROWS - MOBA_HD, MOBA_BLOCK), 0) == 0).astype(F32)

    for h in range(MOBA_HEADS):
        c = slopes_ref[h]
        q = _rms(qkv_ref[:, h * MOBA_HD:(h + 1) * MOBA_HD], qg_ref[...]) * (MOBA_HD ** -0.5)
        k = _rms(qkv_ref[:, d + h * MOBA_HD:d + (h + 1) * MOBA_HD], kg_ref[...])
        v = qkv_ref[:, 2 * d + h * MOBA_HD:2 * d + (h + 1) * MOBA_HD]
        vt_ref[0, h] = jnp.concatenate([v.T, ones_rows], axis=0).astype(vt_ref.dtype)

        gate = lax.dot_general(kmean_ref[h], q, (((1,), (1,)), ((), ())), preferred_element_type=F32,
                               precision=HIGHEST)
        gate = jnp.where(past, gate, -jnp.inf)
        rank = jnp.zeros((nb, MOBA_BLOCK), F32)
        for j in range(nb):
            other = gate[j:j + 1]
            ahead = (other > gate) | ((other == gate) & (j < blk))
            rank = rank + ahead.astype(F32)
        sel = (past & (rank < MOBA_TOPK)).astype(BF16)
        allowed = (_dot_tn(sel, expand) > 0.5) | own

        hi, mid, lo = _split3(-c * (t_loc + start_gap))
        q_blocks = jnp.where(lane_piece == 0, jnp.where(allowed, 0.0, NEG),
                             jnp.where(lane_piece == 1, hi, jnp.where(lane_piece == 2, mid, lo)))
        q_extra = jnp.where(in_blocks, q_blocks,
                            jnp.where(lane < LANES_PER_BLOCK * nb + 3, 1.0, 0.0))
        qa_ref[:, h * AUG:h * AUG + MOBA_HD] = (q * LOG2E).astype(qa_ref.dtype)
        qa_ref[:, h * AUG + MOBA_HD:(h + 1) * AUG] = q_extra.astype(qa_ref.dtype)

        hi, mid, lo = _split3(c * t_loc)
        k_extra = jnp.where(lane == LANES_PER_BLOCK * nb, hi,
                            jnp.where(lane == LANES_PER_BLOCK * nb + 1, mid,
                                      jnp.where(lane == LANES_PER_BLOCK * nb + 2, lo, own.astype(F32))))
        ka_ref[:, h * AUG:h * AUG + MOBA_HD] = k.astype(ka_ref.dtype)
        ka_ref[:, h * AUG + MOBA_HD:(h + 1) * AUG] = k_extra.astype(ka_ref.dtype)

        kmean_ref[h, pl.ds(i, 1), :] = jnp.mean(k, axis=0, keepdims=True)


def moba_prep(qkv, q_gain, k_gain, batch):
    m = qkv.shape[0]
    seq = m // batch
    nb = seq // MOBA_BLOCK
    assert LANES_PER_BLOCK * nb + 3 <= LANES
    d = MOBA_HEADS * MOBA_HD
    slopes = jnp.asarray([LOG2E * 2.0 ** (-8.0 * (h + 1) / MOBA_HEADS) for h in range(MOBA_HEADS)], F32)
    return pl.pallas_call(
        _moba_prep_kernel,
        grid=(batch, nb),
        in_specs=[pl.BlockSpec(memory_space=pltpu.SMEM),
                  pl.BlockSpec((MOBA_BLOCK, 3 * d), lambda b, i: (b * nb + i, 0)),
                  _const_spec((1, MOBA_HD)),
                  _const_spec((1, MOBA_HD))],
        out_specs=[pl.BlockSpec((MOBA_BLOCK, MOBA_HEADS * AUG), lambda b, i: (b * nb + i, 0)),
                   pl.BlockSpec((MOBA_BLOCK, MOBA_HEADS * AUG), lambda b, i: (b * nb + i, 0)),
                   pl.BlockSpec((1, MOBA_HEADS, VT_ROWS, MOBA_BLOCK), lambda b, i: (b, 0, 0, i))],
        out_shape=[jax.ShapeDtypeStruct((m, MOBA_HEADS * AUG), BF16),
                   jax.ShapeDtypeStruct((m, MOBA_HEADS * AUG), BF16),
                   jax.ShapeDtypeStruct((batch, MOBA_HEADS, VT_ROWS, seq), BF16)],
        scratch_shapes=[pltpu.VMEM((MOBA_HEADS, nb, MOBA_HD), F32)],
        compiler_params=_params(2),
        name="moba_prep",
    )(slopes, qkv, q_gain.reshape(1, -1), k_gain.reshape(1, -1))


def _moba_attn_kernel(q_ref, k_ref, vt_ref, o_ref, s_ref, *, heads):
    i = pl.program_id(2)
    key_r = lax.broadcasted_iota(jnp.int32, (MOBA_BLOCK, MOBA_BLOCK), 0)
    qry_c = lax.broadcasted_iota(jnp.int32, (MOBA_BLOCK, MOBA_BLOCK), 1)
    causal = qry_c >= key_r
    q_aug = [q_ref[:, hh * AUG:(hh + 1) * AUG] for hh in range(heads)]

    def scores(hh, j, n):
        k0 = pl.multiple_of(j * MOBA_BLOCK, MOBA_BLOCK)
        return lax.dot_general(k_ref[pl.ds(k0, n * MOBA_BLOCK), hh * AUG:(hh + 1) * AUG], q_aug[hh],
                               (((1,), (1,)), ((), ())), preferred_element_type=F32)

    def sublane_max(s):
        return jnp.max(s.reshape(s.shape[0] // SUBLANES, SUBLANES, MOBA_BLOCK), axis=0)

    def pass1(n):
        def body(t, m_part):
            j = t * n if n > 1 else t
            out = []
            for hh in range(heads):
                s = scores(hh, j, n)
                s_ref[hh, pl.ds(j, n)] = s.reshape(n, MOBA_BLOCK, MOBA_BLOCK)
                out.append(jnp.maximum(m_part[hh], sublane_max(s)))
            return tuple(out)
        return body

    n_wide = i // ATTN_UNROLL
    m_part = tuple(jnp.full((SUBLANES, MOBA_BLOCK), NEG, F32) for _ in range(heads))
    m_part = lax.fori_loop(0, n_wide, pass1(ATTN_UNROLL), m_part)
    m_part = lax.fori_loop(n_wide * ATTN_UNROLL, i, pass1(1), m_part)
    m_row = []
    for hh in range(heads):
        s = jnp.where(causal, scores(hh, i, 1), NEG)
        s_ref[hh, i] = s
        m_row.append(jnp.max(jnp.maximum(m_part[hh], sublane_max(s)), axis=0, keepdims=True))

    def pass2(n):
        def body(t, acc):
            j = t * n if n > 1 else t
            k0 = pl.multiple_of(j * MOBA_BLOCK, MOBA_BLOCK)
            out = []
            for hh in range(heads):
                s = s_ref[hh, pl.ds(j, n)].reshape(n * MOBA_BLOCK, MOBA_BLOCK)
                p = jnp.exp2(s - m_row[hh]).astype(BF16)
                out.append(acc[hh] + jnp.dot(vt_ref[0, hh, :, pl.ds(k0, n * MOBA_BLOCK)], p,
                                             preferred_element_type=F32))
            return tuple(out)
        return body

    acc = tuple(jnp.zeros((VT_ROWS, MOBA_BLOCK), F32) for _ in range(heads))
    acc = lax.fori_loop(0, n_wide, pass2(ATTN_UNROLL), acc)
    acc = lax.fori_loop(n_wide * ATTN_UNROLL, i + 1, pass2(1), acc)
    for hh in range(heads):
        o = acc[hh][:MOBA_HD] * (1.0 / acc[hh][MOBA_HD:MOBA_HD + 1])
        o_ref[:, hh * MOBA_HD:(hh + 1) * MOBA_HD] = o.T.astype(o_ref.dtype)


ATTN_HEADS_PER_STEP = 2
ATTN_UNROLL = 4


def moba_attention(q_aug, k_aug, vt, batch):
    m = q_aug.shape[0]
    seq = m // batch
    nb = seq // MOBA_BLOCK
    hps = ATTN_HEADS_PER_STEP
    return pl.pallas_call(
        functools.partial(_moba_attn_kernel, heads=hps),
        grid=(batch, MOBA_HEADS // hps, nb),
        in_specs=[pl.BlockSpec((MOBA_BLOCK, hps * AUG), lambda b, g, i: (b * nb + i, g)),
                  pl.BlockSpec((seq, hps * AUG), lambda b, g, i: (b, g)),
                  pl.BlockSpec((1, hps, VT_ROWS, seq), lambda b, g, i: (b, g, 0, 0))],
        out_specs=pl.BlockSpec((MOBA_BLOCK, hps * MOBA_HD), lambda b, g, i: (b * nb + i, g)),
        out_shape=jax.ShapeDtypeStruct((m, MOBA_HEADS * MOBA_HD), BF16),
        scratch_shapes=[pltpu.VMEM((hps, nb, MOBA_BLOCK, MOBA_BLOCK), F32)],
        compiler_params=_params(3),
        name="moba_attn",
    )(q_aug, k_aug, vt)


def moba_layer(x, gain, w_qkv, q_gain, k_gain, batch):
    qkv = norm_proj(x, gain, w_qkv.astype(BF16))
    q_aug, k_aug, vt = moba_prep(qkv, q_gain, k_gain, batch)
    return moba_attention(q_aug, k_aug, vt, batch)


def kernel(x, norm_mix, norm_ffn, hyb_w_in, gla_w_gate2, gla_b_gate, gla_norm, ml_conv, ml_b_if, ml_norm,
           hyb_w_out, moba_w_qkv, moba_q_norm, moba_k_norm, moba_w_o, ffn_w_gate, ffn_w_up, ffn_w_down):
    batch, seq, d = x.shape
    depth = norm_mix.shape[0]
    xf = x.reshape(batch * seq, d)
    for l in range(depth):
        j = l // 2
        if l % 2 == 0:
            y = hybrid_layer(xf, norm_mix[l], hyb_w_in[j], gla_w_gate2[j], gla_b_gate[j], gla_norm[j],
                             ml_conv[j], ml_b_if[j], ml_norm[j], batch)
            w_o = hyb_w_out[j]
        else:
            y = moba_layer(xf, norm_mix[l], moba_w_qkv[j], moba_q_norm[j], moba_k_norm[j], batch)
            w_o = moba_w_o[j]
        xf = out_ffn(xf, y, w_o.astype(BF16), norm_ffn[l], ffn_w_gate[l].astype(BF16),
                     ffn_w_up[l].astype(BF16), ffn_w_down[l].astype(BF16))
    return xf.reshape(batch, seq, d)
```

```python
import functools

import jax
import jax.numpy as jnp
import numpy as np
from jax import lax
from jax.experimental import pallas as pl
from jax.experimental.pallas import tpu as pltpu

F32 = jnp.float32
BF16 = jnp.bfloat16
HIGHEST = lax.Precision.HIGHEST

LANES = 128
SUBLANES = 8
VMEM_LIMIT_BYTES = 56 * 1024 * 1024

EPS = 1e-6
NEG = -1e30

GLA_HEADS = 4
GLA_DK = 64
GLA_DV = 128
GLA_RANK = 16
GLA_TAU = 16.0
ML_HEADS = 4
ML_DK = 64
ML_DV = 128
ML_CONV = 4
CHUNK = 64
SUB = 16
MOBA_HEADS = 8
MOBA_HD = 128
MOBA_BLOCK = 256
MOBA_TOPK = 3

GLA_QK = GLA_HEADS * GLA_DK
GLA_V = GLA_HEADS * GLA_DV
ML_QK = ML_HEADS * ML_DK
ML_V = ML_HEADS * ML_DV

C_GQ = 0
C_GK = C_GQ + GLA_QK
C_GV = C_GK + GLA_QK
C_GR = C_GV + GLA_V
C_MQ = C_GR + GLA_V
C_MK = C_MQ + ML_QK
C_MV = C_MK + ML_QK
C_MO = C_MV + ML_V
C_SMALL = C_MO + ML_V
HYB_COLS = C_SMALL + LANES
S_GLR = 0
S_MI = GLA_RANK
S_MF = GLA_RANK + ML_HEADS

ROW_TILE = 512
MIX_TILE = 256
FF_TILE = 256


def _rms(x, gain):
    return x * lax.rsqrt(jnp.mean(x * x, axis=-1, keepdims=True) + EPS) * gain


def _log_sigmoid(z):
    return jnp.minimum(z, 0.0) - jnp.log1p(jnp.exp(-jnp.abs(z)))


def _sigmoid(z):
    return 1.0 / (1.0 + jnp.exp(-z))


def _dot(a, b):
    return jnp.dot(a.astype(BF16), b.astype(BF16), preferred_element_type=F32)


def _dot_nt(a, b):
    return lax.dot_general(a.astype(BF16), b.astype(BF16), (((1,), (1,)), ((), ())),
                           preferred_element_type=F32)


def _dot_tn(a, b):
    return lax.dot_general(a.astype(BF16), b.astype(BF16), (((0,), (0,)), ((), ())),
                           preferred_element_type=F32)


def _dot_f32(a, b):
    return jnp.dot(a, b, preferred_element_type=F32, precision=HIGHEST)


def _const_spec(shape):
    return pl.BlockSpec(shape, lambda *_: (0,) * len(shape), pipeline_mode=pl.Buffered(1))


def _params(n_grid):
    return pltpu.CompilerParams(dimension_semantics=("arbitrary",) * n_grid,
                                vmem_limit_bytes=VMEM_LIMIT_BYTES)


def _norm_proj_kernel(x_ref, g_ref, w_ref, o_ref, *, col_tile):
    xn = _rms(x_ref[...], g_ref[...]).astype(BF16)
    n = o_ref.shape[1]
    for c0 in range(0, n, col_tile):
        c1 = min(c0 + col_tile, n)
        o_ref[:, c0:c1] = jnp.dot(xn, w_ref[:, c0:c1], preferred_element_type=F32)


def norm_proj(x, gain, w):
    m, d = x.shape
    n = w.shape[1]
    tm = min(ROW_TILE, m)
    return pl.pallas_call(
        functools.partial(_norm_proj_kernel, col_tile=512),
        grid=(m // tm,),
        in_specs=[pl.BlockSpec((tm, d), lambda i: (i, 0)),
                  _const_spec((1, d)),
                  _const_spec((d, n))],
        out_specs=pl.BlockSpec((tm, n), lambda i: (i, 0)),
        out_shape=jax.ShapeDtypeStruct((m, n), F32),
        compiler_params=_params(1),
        name="norm_proj",
    )(x, gain.reshape(1, d), w)


def _out_ffn_kernel(x_ref, y_ref, wo_ref, g_ref, wg_ref, wu_ref, wd_ref, o_ref, h_ref, *, ff_tile):
    x1 = x_ref[...] + jnp.dot(y_ref[...], wo_ref[...], preferred_element_type=F32)
    xn = _rms(x1, g_ref[...]).astype(BF16)
    d_ff = wg_ref.shape[1]
    for c0 in range(0, d_ff, ff_tile):
        g = jnp.dot(xn, wg_ref[:, c0:c0 + ff_tile], preferred_element_type=F32)
        u = jnp.dot(xn, wu_ref[:, c0:c0 + ff_tile], preferred_element_type=F32)
        h_ref[:, c0:c0 + ff_tile] = (g * _sigmoid(g) * u).astype(BF16)
    o_ref[...] = x1 + jnp.dot(h_ref[...], wd_ref[...], preferred_element_type=F32)


def out_ffn(x, y, w_o, gain, w_gate, w_up, w_down):
    m, d = x.shape
    dy = y.shape[1]
    d_ff = w_gate.shape[1]
    tm = min(ROW_TILE, m)
    return pl.pallas_call(
        functools.partial(_out_ffn_kernel, ff_tile=FF_TILE),
        grid=(m // tm,),
        in_specs=[pl.BlockSpec((tm, d), lambda i: (i, 0)),
                  pl.BlockSpec((tm, dy), lambda i: (i, 0)),
                  _const_spec((dy, d)),
                  _const_spec((1, d)),
                  _const_spec((d, d_ff)),
                  _const_spec((d, d_ff)),
                  _const_spec((d_ff, d))],
        out_specs=pl.BlockSpec((tm, d), lambda i: (i, 0)),
        out_shape=jax.ShapeDtypeStruct((m, d), F32),
        scratch_shapes=[pltpu.VMEM((tm, d_ff), BF16)],
        compiler_params=_params(1),
        name="out_ffn",
    )(x, y, w_o, gain.reshape(1, d), w_gate, w_up, w_down)


def _hybrid_kernel(proj_ref, gates_ref, wg2_ref, bg_ref, gn_ref, conv_ref, bsmall_ref, bcol_ref, mn_ref,
                   y_ref, gla_st, ml_cs, ml_ns, ml_m, conv_carry, qk_ref):
    t_tile = proj_ref.shape[0]
    n_chunks = t_tile // CHUNK

    @pl.when(pl.program_id(1) == 0)
    def _():
        gla_st[...] = jnp.zeros_like(gla_st)
        ml_cs[...] = jnp.zeros_like(ml_cs)
        ml_ns[...] = jnp.zeros_like(ml_ns)
        ml_m[...] = jnp.zeros_like(ml_m)
        conv_carry[...] = jnp.zeros_like(conv_carry)

    xcat = proj_ref[:, C_MQ:C_MQ + 2 * ML_QK]
    ext = jnp.concatenate([conv_carry[...], xcat], axis=0)
    cw = conv_ref[...]
    acc = ext * cw[ML_CONV - 1:ML_CONV]
    for j in range(1, ML_CONV):
        acc = acc + pltpu.roll(ext, j, axis=0) * cw[ML_CONV - 1 - j:ML_CONV - j]
    conv = acc[SUBLANES:]
    qk_ref[...] = conv * _sigmoid(conv)
    conv_carry[...] = xcat[t_tile - SUBLANES:]

    row = lax.broadcasted_iota(jnp.int32, (CHUNK, CHUNK), 0)
    col = lax.broadcasted_iota(jnp.int32, (CHUNK, CHUNK), 1)
    causal = row >= col
    tril = causal.astype(F32)
    triu = (row <= col).astype(F32)
    lane = lax.broadcasted_iota(jnp.int32, (1, LANES), 1)
    head_mask = (lane < GLA_DK, lane >= GLA_DK)

    def chunk_body(c, carry):
        r0 = pl.multiple_of(c * CHUNK, CHUNK)
        rows = pl.ds(r0, CHUNK)
        small = proj_ref[rows, C_SMALL:C_SMALL + LANES]

        z = _dot_f32(small[:, S_GLR:S_GLR + GLA_RANK], wg2_ref[...]) + bg_ref[...]
        log_a = _log_sigmoid(z) * (1.0 / GLA_TAU)
        bc_all = _dot_f32(tril, log_a)
        for p in range(GLA_HEADS // 2):
            lanes = slice(p * LANES, (p + 1) * LANES)
            q2 = proj_ref[rows, C_GQ + p * LANES:C_GQ + (p + 1) * LANES] * (GLA_DK ** -0.5)
            k2 = proj_ref[rows, C_GK + p * LANES:C_GK + (p + 1) * LANES]
            bc = bc_all[:, lanes]
            st = gla_st[p]
            q_in = q2 * jnp.exp(bc)
            q_sub, k_sub = [], []
            for i in range(CHUNK // SUB):
                rs = slice(i * SUB, (i + 1) * SUB)
                ref_pt = bc[i * SUB - 1:i * SUB] if i else jnp.zeros((1, LANES), F32)
                q_sub.append(q2[rs] * jnp.exp(bc[rs] - ref_pt))
                k_sub.append(k2 * jnp.exp(ref_pt - bc))
            last = bc[CHUNK - 1:CHUNK]
            k_dec = k2 * jnp.exp(last - bc)
            upd = st * jnp.exp(last)
            for hl in range(2):
                h = 2 * p + hl
                hm = head_mask[hl]
                sc = jnp.concatenate(
                    [_dot_nt(jnp.where(hm, q_sub[i], 0.0), k_sub[i]) for i in range(CHUNK // SUB)], axis=0)
                sc = jnp.where(causal, sc, 0.0)
                v = proj_ref[rows, C_GV + h * GLA_DV:C_GV + (h + 1) * GLA_DV]
                o = _dot(sc, v) + _dot_nt(jnp.where(hm, q_in, 0.0), st)
                gate = proj_ref[rows, C_GR + h * GLA_DV:C_GR + (h + 1) * GLA_DV]
                yg = _rms(o, gn_ref[:, h * GLA_DV:(h + 1) * GLA_DV]) * (gate * _sigmoid(gate))
                y_ref[rows, h * GLA_DV:(h + 1) * GLA_DV] = yg.astype(y_ref.dtype)
                upd = upd + _dot_tn(v, jnp.where(hm, k_dec, 0.0))
            gla_st[p] = upd

        g_rows = gates_ref[0, c] + bcol_ref[...]
        ls_rows = _log_sigmoid(g_rows)
        bc_rows = _dot_f32(ls_rows, triu)
        smallb = small + bsmall_ref[...]
        bc_cols = _dot_f32(tril, _log_sigmoid(smallb))
        for p in range(ML_HEADS // 2):
            mq2 = qk_ref[rows, p * LANES:(p + 1) * LANES]
            mk2 = qk_ref[rows, ML_QK + p * LANES:ML_QK + (p + 1) * LANES] * (ML_DK ** -0.5)
            cs = ml_cs[p]
            ns = ml_ns[p]
            cs_new = jnp.zeros_like(cs)
            ns_new = jnp.zeros_like(ns)
            carry_row = jnp.zeros((1, LANES), F32)
            for hl in range(2):
                h = 2 * p + hl
                hm = head_mask[hl]
                i_row = g_rows[h:h + 1]
                bc_row = bc_rows[ML_HEADS + h:ML_HEADS + h + 1]
                b_row = i_row - bc_row
                bc_last = bc_row[:, CHUNK - 1:CHUNK]
                i_col = smallb[:, S_MI + h:S_MI + h + 1]
                bc_col = bc_cols[:, S_MF + h:S_MF + h + 1]
                m_prev = ml_m[h:h + 1, 0:1]

                b_mat = jnp.where(causal, b_row, -jnp.inf)
                m_col = jnp.maximum(m_prev, jnp.max(b_mat, axis=1, keepdims=True))
                w_dec = jnp.exp(b_mat - m_col)
                qm = jnp.where(hm, mq2, 0.0)
                w = _dot_nt(qm, mk2) * w_dec
                inter = jnp.exp(m_prev - m_col)
                v = proj_ref[rows, C_MV + h * ML_DV:C_MV + (h + 1) * ML_DV]
                num = inter * _dot_nt(qm, cs) + _dot(w, v)
                den = inter * jnp.sum(qm * ns, axis=1, keepdims=True) + jnp.sum(w, axis=1, keepdims=True)
                hout = num / jnp.maximum(jnp.abs(den), jnp.exp(-(bc_col + m_col)))
                og = proj_ref[rows, C_MO + h * ML_DV:C_MO + (h + 1) * ML_DV]
                ym = _rms(hout * _sigmoid(og), mn_ref[:, h * ML_DV:(h + 1) * ML_DV])
                y_ref[rows, GLA_V + h * ML_DV:GLA_V + (h + 1) * ML_DV] = ym.astype(y_ref.dtype)

                m_new = bc_last + jnp.maximum(m_prev, jnp.max(b_row, axis=1, keepdims=True))
                scale = jnp.exp(bc_last + m_prev - m_new)
                wk = jnp.where(hm, mk2 * jnp.exp(bc_last - bc_col + i_col - m_new), 0.0)
                cs_new = cs_new + _dot_tn(v, wk)
                ns_new = ns_new + jnp.sum(wk, axis=0, keepdims=True)
                carry_row = jnp.where(hm, scale, carry_row)
                ml_m[h:h + 1, :] = jnp.broadcast_to(m_new, (1, LANES))
            ml_cs[p] = cs * carry_row + cs_new
            ml_ns[p] = ns * carry_row + ns_new
        return carry

    lax.fori_loop(0, n_chunks, chunk_body, 0)


def hybrid_core(proj, gates_t, w_gate2, b_gate, gla_norm, conv_w, b_small, b_col, ml_norm, batch):
    m = proj.shape[0]
    seq = m // batch
    t = min(MIX_TILE, seq)
    nt = seq // t
    return pl.pallas_call(
        _hybrid_kernel,
        grid=(batch, nt),
        in_specs=[pl.BlockSpec((t, HYB_COLS), lambda b, i: (b * nt + i, 0)),
                  pl.BlockSpec((1, t // CHUNK, 2 * ML_HEADS, CHUNK), lambda b, i: (b, i, 0, 0)),
                  _const_spec((GLA_RANK, GLA_QK)),
                  _const_spec((1, GLA_QK)),
                  _const_spec((1, GLA_V)),
                  _const_spec((ML_CONV, 2 * ML_QK)),
                  _const_spec((1, LANES)),
                  _const_spec((2 * ML_HEADS, 1)),
                  _const_spec((1, ML_V))],
        out_specs=pl.BlockSpec((t, GLA_V + ML_V), lambda b, i: (b * nt + i, 0)),
        out_shape=jax.ShapeDtypeStruct((m, GLA_V + ML_V), BF16),
        scratch_shapes=[pltpu.VMEM((GLA_HEADS // 2, GLA_DV, LANES), F32),
                        pltpu.VMEM((ML_HEADS // 2, ML_DV, LANES), F32),
                        pltpu.VMEM((ML_HEADS // 2, 1, LANES), F32),
                        pltpu.VMEM((2 * ML_HEADS, LANES), F32),
                        pltpu.VMEM((SUBLANES, 2 * ML_QK), F32),
                        pltpu.VMEM((t, 2 * ML_QK), F32)],
        compiler_params=_params(2),
        name="hybrid_core",
    )(proj, gates_t, w_gate2, b_gate, gla_norm, conv_w, b_small, b_col, ml_norm)


def hybrid_layer(x, gain, w_in, w_gate2, b_gate, gla_norm, conv_w, b_if, ml_norm, batch):
    m, d = x.shape
    seq = m // batch
    segs = np.cumsum([0, GLA_QK, GLA_QK, GLA_V, GLA_V, GLA_RANK, ML_QK, ML_QK, ML_V, ML_V, ML_HEADS, ML_HEADS])
    seg = lambda i: w_in[:, int(segs[i]):int(segs[i + 1])]
    pad = jnp.zeros((d, LANES - GLA_RANK - 2 * ML_HEADS), w_in.dtype)
    w_perm = jnp.concatenate([seg(0), seg(1), seg(2), seg(3), seg(5), seg(6), seg(7), seg(8),
                              seg(4), seg(9), seg(10), pad], axis=1).astype(BF16)
    proj = norm_proj(x, gain, w_perm)
    gates = proj[:, C_SMALL + S_MI:C_SMALL + S_MI + 2 * ML_HEADS]
    gates_t = gates.reshape(batch, seq // CHUNK, CHUNK, 2 * ML_HEADS).transpose(0, 1, 3, 2)
    b_small = jnp.zeros((1, LANES), F32).at[0, S_MI:S_MI + 2 * ML_HEADS].set(b_if)
    return hybrid_core(proj, gates_t, w_gate2, b_gate.reshape(1, -1), gla_norm.reshape(1, -1), conv_w,
                       b_small, b_if.reshape(-1, 1), ml_norm.reshape(1, -1), batch)


LOG2E = 1.4426950408889634
AUG = 2 * MOBA_HD
LANES_PER_BLOCK = 4
VT_ROWS = MOBA_HD + 16


def _split3(x):
    hi = x.astype(BF16)
    r1 = x - hi.astype(F32)
    mid = r1.astype(BF16)
    lo = (r1 - mid.astype(F32)).astype(BF16)
    return hi.astype(F32), mid.astype(F32), lo.astype(F32)


def _moba_prep_kernel(slopes_ref, qkv_ref, qg_ref, kg_ref, qa_ref, ka_ref, vt_ref, kmean_ref):
    i = pl.program_id(1)
    nb = kmean_ref.shape[1]
    d = MOBA_HEADS * MOBA_HD

    @pl.when(i == 0)
    def _():
        kmean_ref[...] = jnp.zeros_like(kmean_ref)

    blk = lax.broadcasted_iota(jnp.int32, (nb, MOBA_BLOCK), 0)
    past = blk < i
    lane = lax.broadcasted_iota(jnp.int32, (1, LANES), 1)
    lane_blk = lane // LANES_PER_BLOCK
    lane_piece = lane % LANES_PER_BLOCK
    in_blocks = lane < LANES_PER_BLOCK * nb
    expand = ((lax.broadcasted_iota(jnp.int32, (nb, LANES), 0) == lane_blk) & in_blocks).astype(BF16)
    own = (lane_blk == i) & in_blocks
    t_loc = lax.broadcasted_iota(jnp.int32, (MOBA_BLOCK, LANES), 0).astype(F32)
    start_gap = ((i - lane_blk) * MOBA_BLOCK).astype(F32)
    ones_rows = (lax.broadcasted_iota(jnp.int32, (VT_ROWS - MOBA_HD, MOBA_BLOCK), 0) == 0).astype(F32)

    for h in range(MOBA_HEADS):
        c = slopes_ref[h]
        q = _rms(qkv_ref[:, h * MOBA_HD:(h + 1) * MOBA_HD], qg_ref[...]) * (MOBA_HD ** -0.5)
        k = _rms(qkv_ref[:, d + h * MOBA_HD:d + (h + 1) * MOBA_HD], kg_ref[...])
        v = qkv_ref[:, 2 * d + h * MOBA_HD:2 * d + (h + 1) * MOBA_HD]
        vt_ref[0, h] = jnp.concatenate([v.T, ones_rows], axis=0).astype(vt_ref.dtype)

        gate = lax.dot_general(kmean_ref[h], q, (((1,), (1,)), ((), ())), preferred_element_type=F32,
                               precision=HIGHEST)
        gate = jnp.where(past, gate, -jnp.inf)
        rank = jnp.zeros((nb, MOBA_BLOCK), F32)
        for j in range(nb):
            other = gate[j:j + 1]
            ahead = (other > gate) | ((other == gate) & (j < blk))
            rank = rank + ahead.astype(F32)
        sel = (past & (rank < MOBA_TOPK)).astype(BF16)
        allowed = (_dot_tn(sel, expand) > 0.5) | own

        hi, mid, lo = _split3(-c * (t_loc + start_gap))
        q_blocks = jnp.where(lane_piece == 0, jnp.where(allowed, 0.0, NEG),
                             jnp.where(lane_piece == 1, hi, jnp.where(lane_piece == 2, mid, lo)))
        q_extra = jnp.where(in_blocks, q_blocks,
                            jnp.where(lane < LANES_PER_BLOCK * nb + 3, 1.0, 0.0))
        qa_ref[:, h * AUG:h * AUG + MOBA_HD] = (q * LOG2E).astype(qa_ref.dtype)
        qa_ref[:, h * AUG + MOBA_HD:(h + 1) * AUG] = q_extra.astype(qa_ref.dtype)

        hi, mid, lo = _split3(c * t_loc)
        k_extra = jnp.where(lane == LANES_PER_BLOCK * nb, hi,
                            jnp.where(lane == LANES_PER_BLOCK * nb + 1, mid,
                                      jnp.where(lane == LANES_PER_BLOCK * nb + 2, lo, own.astype(F32))))
        ka_ref[:, h * AUG:h * AUG + MOBA_HD] = k.astype(ka_ref.dtype)
        ka_ref[:, h * AUG + MOBA_HD:(h + 1) * AUG] = k_extra.astype(ka_ref.dtype)

        kmean_ref[h, pl.ds(i, 1), :] = jnp.mean(k, axis=0, keepdims=True)


def moba_prep(qkv, q_gain, k_gain, batch):
    m = qkv.shape[0]
    seq = m // batch
    nb = seq // MOBA_BLOCK
    assert LANES_PER_BLOCK * nb + 3 <= LANES
    d = MOBA_HEADS * MOBA_HD
    slopes = jnp.asarray([LOG2E * 2.0 ** (-8.0 * (h + 1) / MOBA_HEADS) for h in range(MOBA_HEADS)], F32)
    return pl.pallas_call(
        _moba_prep_kernel,
        grid=(batch, nb),
        in_specs=[pl.BlockSpec(memory_space=pltpu.SMEM),
                  pl.BlockSpec((MOBA_BLOCK, 3 * d), lambda b, i: (b * nb + i, 0)),
                  _const_spec((1, MOBA_HD)),
                  _const_spec((1, MOBA_HD))],
        out_specs=[pl.BlockSpec((MOBA_BLOCK, MOBA_HEADS * AUG), lambda b, i: (b * nb + i, 0)),
                   pl.BlockSpec((MOBA_BLOCK, MOBA_HEADS * AUG), lambda b, i: (b * nb + i, 0)),
                   pl.BlockSpec((1, MOBA_HEADS, VT_ROWS, MOBA_BLOCK), lambda b, i: (b, 0, 0, i))],
        out_shape=[jax.ShapeDtypeStruct((m, MOBA_HEADS * AUG), BF16),
                   jax.ShapeDtypeStruct((m, MOBA_HEADS * AUG), BF16),
                   jax.ShapeDtypeStruct((batch, MOBA_HEADS, VT_ROWS, seq), BF16)],
        scratch_shapes=[pltpu.VMEM((MOBA_HEADS, nb, MOBA_HD), F32)],
        compiler_params=_params(2),
        name="moba_prep",
    )(slopes, qkv, q_gain.reshape(1, -1), k_gain.reshape(1, -1))


ATTN_HEADS_PER_STEP = 4
ATTN_CHUNK_BLOCKS = 4


def _moba_attn_kernel(q_ref, k_ref, vt_ref, o_ref, s_ref, *, heads, chunk_blocks):
    i = pl.program_id(2)
    chunk_keys = chunk_blocks * MOBA_BLOCK
    n_chunks = i // chunk_blocks + 1
    key_r = lax.broadcasted_iota(jnp.int32, (MOBA_BLOCK, MOBA_BLOCK), 0)
    qry_c = lax.broadcasted_iota(jnp.int32, (MOBA_BLOCK, MOBA_BLOCK), 1)
    causal = qry_c >= key_r
    q_aug = [q_ref[:, hh * AUG:(hh + 1) * AUG] for hh in range(heads)]

    def pass1(c, m_part, last):
        k0 = pl.multiple_of(c * chunk_keys, chunk_keys)
        out = []
        for hh in range(heads):
            s = lax.dot_general(k_ref[pl.ds(k0, chunk_keys), hh * AUG:(hh + 1) * AUG], q_aug[hh],
                                (((1,), (1,)), ((), ())), preferred_element_type=F32)
            if last:
                parts = []
                for u in range(chunk_blocks):
                    other_block = jnp.full((1, MOBA_BLOCK), c * chunk_blocks + u, jnp.int32) != i
                    parts.append(jnp.where(causal | other_block, s[u * MOBA_BLOCK:(u + 1) * MOBA_BLOCK], NEG))
                s = jnp.concatenate(parts, axis=0)
            s_ref[hh, pl.ds(k0, chunk_keys), :] = s
            out.append(jnp.maximum(m_part[hh],
                                   jnp.max(s.reshape(chunk_keys // SUBLANES, SUBLANES, MOBA_BLOCK), axis=0)))
        return tuple(out)

    m_part = tuple(jnp.full((SUBLANES, MOBA_BLOCK), NEG, F32) for _ in range(heads))
    m_part = lax.fori_loop(0, n_chunks - 1, lambda c, m: pass1(c, m, False), m_part)
    m_part = pass1(n_chunks - 1, m_part, True)
    m_row = [jnp.max(m_part[hh], axis=0, keepdims=True) for hh in range(heads)]

    def pass2(c, acc):
        k0 = pl.multiple_of(c * chunk_keys, chunk_keys)
        out = []
        for hh in range(heads):
            p = jnp.exp2(s_ref[hh, pl.ds(k0, chunk_keys), :] - m_row[hh]).astype(BF16)
            out.append(acc[hh] + jnp.dot(vt_ref[0, hh, :, pl.ds(k0, chunk_keys)], p, preferred_element_type=F32))
        return tuple(out)

    acc = tuple(jnp.zeros((VT_ROWS, MOBA_BLOCK), F32) for _ in range(heads))
    acc = lax.fori_loop(0, n_chunks, pass2, acc)
    for hh in range(heads):
        o = acc[hh][:MOBA_HD] * (1.0 / acc[hh][MOBA_HD:MOBA_HD + 1])
        o_ref[:, hh * MOBA_HD:(hh + 1) * MOBA_HD] = o.T.astype(o_ref.dtype)


def moba_attention(q_aug, k_aug, vt, batch):
    m = q_aug.shape[0]
    seq = m // batch
    nb = seq // MOBA_BLOCK
    hps = ATTN_HEADS_PER_STEP
    assert nb % ATTN_CHUNK_BLOCKS == 0
    return pl.pallas_call(
        functools.partial(_moba_attn_kernel, heads=hps, chunk_blocks=ATTN_CHUNK_BLOCKS),
        grid=(batch, MOBA_HEADS // hps, nb),
        in_specs=[pl.BlockSpec((MOBA_BLOCK, hps * AUG), lambda b, g, i: (b * nb + i, g)),
                  pl.BlockSpec((seq, hps * AUG), lambda b, g, i: (b, g)),
                  pl.BlockSpec((1, hps, VT_ROWS, seq), lambda b, g, i: (b, g, 0, 0))],
        out_specs=pl.BlockSpec((MOBA_BLOCK, hps * MOBA_HD), lambda b, g, i: (b * nb + i, g)),
        out_shape=jax.ShapeDtypeStruct((m, MOBA_HEADS * MOBA_HD), BF16),
        scratch_shapes=[pltpu.VMEM((hps, seq, MOBA_BLOCK), F32)],
        compiler_params=_params(3),
        name="moba_attn",
    )(q_aug, k_aug, vt)


def moba_layer(x, gain, w_qkv, q_gain, k_gain, batch):
    qkv = norm_proj(x, gain, w_qkv.astype(BF16))
    q_aug, k_aug, vt = moba_prep(qkv, q_gain, k_gain, batch)
    return moba_attention(q_aug, k_aug, vt, batch)


def kernel(x, norm_mix, norm_ffn, hyb_w_in, gla_w_gate2, gla_b_gate, gla_norm, ml_conv, ml_b_if, ml_norm,
           hyb_w_out, moba_w_qkv, moba_q_norm, moba_k_norm, moba_w_o, ffn_w_gate, ffn_w_up, ffn_w_down):
    batch, seq, d = x.shape
    depth = norm_mix.shape[0]
    xf = x.reshape(batch * seq, d)
    for l in range(depth):
        j = l // 2
        if l % 2 == 0:
            y = hybrid_layer(xf, norm_mix[l], hyb_w_in[j], gla_w_gate2[j], gla_b_gate[j], gla_norm[j],
                             ml_conv[j], ml_b_if[j], ml_norm[j], batch)
            w_o = hyb_w_out[j]
        else:
            y = moba_layer(xf, norm_mix[l], moba_w_qkv[j], moba_q_norm[j], moba_k_norm[j], batch)
            w_o = moba_w_o[j]
        xf = out_ffn(xf, y, w_o.astype(BF16), norm_ffn[l], ffn_w_gate[l].astype(BF16),
                     ffn_w_up[l].astype(BF16), ffn_w_down[l].astype(BF16))
    return xf.reshape(batch, seq, d)
```

```python
import functools

import jax
import jax.numpy as jnp
import numpy as np
from jax import lax
from jax.experimental import pallas as pl
from jax.experimental.pallas import tpu as pltpu

F32 = jnp.float32
BF16 = jnp.bfloat16
HIGHEST = lax.Precision.HIGHEST

LANES = 128
SUBLANES = 8
VMEM_LIMIT_BYTES = 56 * 1024 * 1024

EPS = 1e-6
NEG = -1e30

GLA_HEADS = 4
GLA_DK = 64
GLA_DV = 128
GLA_RANK = 16
GLA_TAU = 16.0
ML_HEADS = 4
ML_DK = 64
ML_DV = 128
ML_CONV = 4
CHUNK = 64
SUB = 16
MOBA_HEADS = 8
MOBA_HD = 128
MOBA_BLOCK = 256
MOBA_TOPK = 3

GLA_QK = GLA_HEADS * GLA_DK
GLA_V = GLA_HEADS * GLA_DV
ML_QK = ML_HEADS * ML_DK
ML_V = ML_HEADS * ML_DV

C_GQ = 0
C_GK = C_GQ + GLA_QK
C_GV = C_GK + GLA_QK
C_GR = C_GV + GLA_V
C_MQ = C_GR + GLA_V
C_MK = C_MQ + ML_QK
C_MV = C_MK + ML_QK
C_MO = C_MV + ML_V
C_SMALL = C_MO + ML_V
HYB_COLS = C_SMALL + LANES
S_GLR = 0
S_MI = GLA_RANK
S_MF = GLA_RANK + ML_HEADS

ROW_TILE = 512
MIX_TILE = 256
FF_TILE = 256


def _rms(x, gain):
    return x * lax.rsqrt(jnp.mean(x * x, axis=-1, keepdims=True) + EPS) * gain


def _log_sigmoid(z):
    return jnp.minimum(z, 0.0) - jnp.log1p(jnp.exp(-jnp.abs(z)))


def _sigmoid(z):
    return 1.0 / (1.0 + jnp.exp(-z))


def _dot(a, b):
    return jnp.dot(a.astype(BF16), b.astype(BF16), preferred_element_type=F32)


def _dot_nt(a, b):
    return lax.dot_general(a.astype(BF16), b.astype(BF16), (((1,), (1,)), ((), ())),
                           preferred_element_type=F32)


def _dot_tn(a, b):
    return lax.dot_general(a.astype(BF16), b.astype(BF16), (((0,), (0,)), ((), ())),
                           preferred_element_type=F32)


def _dot_f32(a, b):
    return jnp.dot(a, b, preferred_element_type=F32, precision=HIGHEST)


def _const_spec(shape):
    return pl.BlockSpec(shape, lambda *_: (0,) * len(shape), pipeline_mode=pl.Buffered(1))


def _params(n_grid):
    return pltpu.CompilerParams(dimension_semantics=("arbitrary",) * n_grid,
                                vmem_limit_bytes=VMEM_LIMIT_BYTES)


def _norm_proj_kernel(x_ref, g_ref, w_ref, o_ref, *, col_tile):
    xn = _rms(x_ref[...], g_ref[...]).astype(BF16)
    n = o_ref.shape[1]
    for c0 in range(0, n, col_tile):
        c1 = min(c0 + col_tile, n)
        o_ref[:, c0:c1] = jnp.dot(xn, w_ref[:, c0:c1], preferred_element_type=F32)


def norm_proj(x, gain, w):
    m, d = x.shape
    n = w.shape[1]
    tm = min(ROW_TILE, m)
    return pl.pallas_call(
        functools.partial(_norm_proj_kernel, col_tile=512),
        grid=(m // tm,),
        in_specs=[pl.BlockSpec((tm, d), lambda i: (i, 0)),
                  _const_spec((1, d)),
                  _const_spec((d, n))],
        out_specs=pl.BlockSpec((tm, n), lambda i: (i, 0)),
        out_shape=jax.ShapeDtypeStruct((m, n), F32),
        compiler_params=_params(1),
        name="norm_proj",
    )(x, gain.reshape(1, d), w)


def _out_ffn_kernel(x_ref, y_ref, wo_ref, g_ref, wg_ref, wu_ref, wd_ref, o_ref, h_ref, *, ff_tile):
    x1 = x_ref[...] + jnp.dot(y_ref[...], wo_ref[...], preferred_element_type=F32)
    xn = _rms(x1, g_ref[...]).astype(BF16)
    d_ff = wg_ref.shape[1]
    for c0 in range(0, d_ff, ff_tile):
        g = jnp.dot(xn, wg_ref[:, c0:c0 + ff_tile], preferred_element_type=F32)
        u = jnp.dot(xn, wu_ref[:, c0:c0 + ff_tile], preferred_element_type=F32)
        h_ref[:, c0:c0 + ff_tile] = (g * _sigmoid(g) * u).astype(BF16)
    o_ref[...] = x1 + jnp.dot(h_ref[...], wd_ref[...], preferred_element_type=F32)


def out_ffn(x, y, w_o, gain, w_gate, w_up, w_down):
    m, d = x.shape
    dy = y.shape[1]
    d_ff = w_gate.shape[1]
    tm = min(ROW_TILE, m)
    return pl.pallas_call(
        functools.partial(_out_ffn_kernel, ff_tile=FF_TILE),
        grid=(m // tm,),
        in_specs=[pl.BlockSpec((tm, d), lambda i: (i, 0)),
                  pl.BlockSpec((tm, dy), lambda i: (i, 0)),
                  _const_spec((dy, d)),
                  _const_spec((1, d)),
                  _const_spec((d, d_ff)),
                  _const_spec((d, d_ff)),
                  _const_spec((d_ff, d))],
        out_specs=pl.BlockSpec((tm, d), lambda i: (i, 0)),
        out_shape=jax.ShapeDtypeStruct((m, d), F32),
        scratch_shapes=[pltpu.VMEM((tm, d_ff), BF16)],
        compiler_params=_params(1),
        name="out_ffn",
    )(x, y, w_o, gain.reshape(1, d), w_gate, w_up, w_down)


def _hybrid_kernel(proj_ref, gates_ref, wg2_ref, bg_ref, gn_ref, conv_ref, bsmall_ref, bcol_ref, mn_ref,
                   y_ref, gla_st, ml_cs, ml_ns, ml_m, conv_carry, qk_ref):
    t_tile = proj_ref.shape[0]
    n_chunks = t_tile // CHUNK

    @pl.when(pl.program_id(1) == 0)
    def _():
        gla_st[...] = jnp.zeros_like(gla_st)
        ml_cs[...] = jnp.zeros_like(ml_cs)
        ml_ns[...] = jnp.zeros_like(ml_ns)
        ml_m[...] = jnp.zeros_like(ml_m)
        conv_carry[...] = jnp.zeros_like(conv_carry)

    xcat = proj_ref[:, C_MQ:C_MQ + 2 * ML_QK]
    ext = jnp.concatenate([conv_carry[...], xcat], axis=0)
    cw = conv_ref[...]
    acc = ext * cw[ML_CONV - 1:ML_CONV]
    for j in range(1, ML_CONV):
        acc = acc + pltpu.roll(ext, j, axis=0) * cw[ML_CONV - 1 - j:ML_CONV - j]
    conv = acc[SUBLANES:]
    qk_ref[...] = conv * _sigmoid(conv)
    conv_carry[...] = xcat[t_tile - SUBLANES:]

    row = lax.broadcasted_iota(jnp.int32, (CHUNK, CHUNK), 0)
    col = lax.broadcasted_iota(jnp.int32, (CHUNK, CHUNK), 1)
    causal = row >= col
    tril = causal.astype(F32)
    triu = (row <= col).astype(F32)
    lane = lax.broadcasted_iota(jnp.int32, (1, LANES), 1)
    head_mask = (lane < GLA_DK, lane >= GLA_DK)

    def chunk_body(c, carry):
        r0 = pl.multiple_of(c * CHUNK, CHUNK)
        rows = pl.ds(r0, CHUNK)
        small = proj_ref[rows, C_SMALL:C_SMALL + LANES]

        z = _dot_f32(small[:, S_GLR:S_GLR + GLA_RANK], wg2_ref[...]) + bg_ref[...]
        log_a = _log_sigmoid(z) * (1.0 / GLA_TAU)
        bc_all = _dot_f32(tril, log_a)
        for p in range(GLA_HEADS // 2):
            lanes = slice(p * LANES, (p + 1) * LANES)
            q2 = proj_ref[rows, C_GQ + p * LANES:C_GQ + (p + 1) * LANES] * (GLA_DK ** -0.5)
            k2 = proj_ref[rows, C_GK + p * LANES:C_GK + (p + 1) * LANES]
            bc = bc_all[:, lanes]
            st = gla_st[p]
            q_in = q2 * jnp.exp(bc)
            q_sub, k_sub = [], []
            for i in range(CHUNK // SUB):
                rs = slice(i * SUB, (i + 1) * SUB)
                ref_pt = bc[i * SUB - 1:i * SUB] if i else jnp.zeros((1, LANES), F32)
                q_sub.append(q2[rs] * jnp.exp(bc[rs] - ref_pt))
                k_sub.append(k2 * jnp.exp(ref_pt - bc))
            last = bc[CHUNK - 1:CHUNK]
            k_dec = k2 * jnp.exp(last - bc)
            upd = st * jnp.exp(last)
            for hl in range(2):
                h = 2 * p + hl
                hm = head_mask[hl]
                sc = jnp.concatenate(
                    [_dot_nt(jnp.where(hm, q_sub[i], 0.0), k_sub[i]) for i in range(CHUNK // SUB)], axis=0)
                sc = jnp.where(causal, sc, 0.0)
                v = proj_ref[rows, C_GV + h * GLA_DV:C_GV + (h + 1) * GLA_DV]
                o = _dot(sc, v) + _dot_nt(jnp.where(hm, q_in, 0.0), st)
                gate = proj_ref[rows, C_GR + h * GLA_DV:C_GR + (h + 1) * GLA_DV]
                yg = _rms(o, gn_ref[:, h * GLA_DV:(h + 1) * GLA_DV]) * (gate * _sigmoid(gate))
                y_ref[rows, h * GLA_DV:(h + 1) * GLA_DV] = yg.astype(y_ref.dtype)
                upd = upd + _dot_tn(v, jnp.where(hm, k_dec, 0.0))
            gla_st[p] = upd

        g_rows = gates_ref[0, c] + bcol_ref[...]
        ls_rows = _log_sigmoid(g_rows)
        bc_rows = _dot_f32(ls_rows, triu)
        smallb = small + bsmall_ref[...]
        bc_cols = _dot_f32(tril, _log_sigmoid(smallb))
        for p in range(ML_HEADS // 2):
            mq2 = qk_ref[rows, p * LANES:(p + 1) * LANES]
            mk2 = qk_ref[rows, ML_QK + p * LANES:ML_QK + (p + 1) * LANES] * (ML_DK ** -0.5)
            cs = ml_cs[p]
            ns = ml_ns[p]
            cs_new = jnp.zeros_like(cs)
            ns_new = jnp.zeros_like(ns)
            carry_row = jnp.zeros((1, LANES), F32)
            for hl in range(2):
                h = 2 * p + hl
                hm = head_mask[hl]
                i_row = g_rows[h:h + 1]
                bc_row = bc_rows[ML_HEADS + h:ML_HEADS + h + 1]
                b_row = i_row - bc_row
                bc_last = bc_row[:, CHUNK - 1:CHUNK]
                i_col = smallb[:, S_MI + h:S_MI + h + 1]
                bc_col = bc_cols[:, S_MF + h:S_MF + h + 1]
                m_prev = ml_m[h:h + 1, 0:1]

                b_mat = jnp.where(causal, b_row, -jnp.inf)
                m_col = jnp.maximum(m_prev, jnp.max(b_mat, axis=1, keepdims=True))
                w_dec = jnp.exp(b_mat - m_col)
                qm = jnp.where(hm, mq2, 0.0)
                w = _dot_nt(qm, mk2) * w_dec
                inter = jnp.exp(m_prev - m_col)
                v = proj_ref[rows, C_MV + h * ML_DV:C_MV + (h + 1) * ML_DV]
                num = inter * _dot_nt(qm, cs) + _dot(w, v)
                den = inter * jnp.sum(qm * ns, axis=1, keepdims=True) + jnp.sum(w, axis=1, keepdims=True)
                hout = num / jnp.maximum(jnp.abs(den), jnp.exp(-(bc_col + m_col)))
                og = proj_ref[rows, C_MO + h * ML_DV:C_MO + (h + 1) * ML_DV]
                ym = _rms(hout * _sigmoid(og), mn_ref[:, h * ML_DV:(h + 1) * ML_DV])
                y_ref[rows, GLA_V + h * ML_DV:GLA_V + (h + 1) * ML_DV] = ym.astype(y_ref.dtype)

                m_new = bc_last + jnp.maximum(m_prev, jnp.max(b_row, axis=1, keepdims=True))
                scale = jnp.exp(bc_last + m_prev - m_new)
                wk = jnp.where(hm, mk2 * jnp.exp(bc_last - bc_col + i_col - m_new), 0.0)
                cs_new = cs_new + _dot_tn(v, wk)
                ns_new = ns_new + jnp.sum(wk, axis=0, keepdims=True)
                carry_row = jnp.where(hm, scale, carry_row)
                ml_m[h:h + 1, :] = jnp.broadcast_to(m_new, (1, LANES))
            ml_cs[p] = cs * carry_row + cs_new
            ml_ns[p] = ns * carry_row + ns_new
        return carry

    lax.fori_loop(0, n_chunks, chunk_body, 0)


def hybrid_core(proj, gates_t, w_gate2, b_gate, gla_norm, conv_w, b_small, b_col, ml_norm, batch):
    m = proj.shape[0]
    seq = m // batch
    t = min(MIX_TILE, seq)
    nt = seq // t
    return pl.pallas_call(
        _hybrid_kernel,
        grid=(batch, nt),
        in_specs=[pl.BlockSpec((t, HYB_COLS), lambda b, i: (b * nt + i, 0)),
                  pl.BlockSpec((1, t // CHUNK, 2 * ML_HEADS, CHUNK), lambda b, i: (b, i, 0, 0)),
                  _const_spec((GLA_RANK, GLA_QK)),
                  _const_spec((1, GLA_QK)),
                  _const_spec((1, GLA_V)),
                  _const_spec((ML_CONV, 2 * ML_QK)),
                  _const_spec((1, LANES)),
                  _const_spec((2 * ML_HEADS, 1)),
                  _const_spec((1, ML_V))],
        out_specs=pl.BlockSpec((t, GLA_V + ML_V), lambda b, i: (b * nt + i, 0)),
        out_shape=jax.ShapeDtypeStruct((m, GLA_V + ML_V), BF16),
        scratch_shapes=[pltpu.VMEM((GLA_HEADS // 2, GLA_DV, LANES), F32),
                        pltpu.VMEM((ML_HEADS // 2, ML_DV, LANES), F32),
                        pltpu.VMEM((ML_HEADS // 2, 1, LANES), F32),
                        pltpu.VMEM((2 * ML_HEADS, LANES), F32),
                        pltpu.VMEM((SUBLANES, 2 * ML_QK), F32),
                        pltpu.VMEM((t, 2 * ML_QK), F32)],
        compiler_params=_params(2),
        name="hybrid_core",
    )(proj, gates_t, w_gate2, b_gate, gla_norm, conv_w, b_small, b_col, ml_norm)


def hybrid_layer(x, gain, w_in, w_gate2, b_gate, gla_norm, conv_w, b_if, ml_norm, batch):
    m, d = x.shape
    seq = m // batch
    segs = np.cumsum([0, GLA_QK, GLA_QK, GLA_V, GLA_V, GLA_RANK, ML_QK, ML_QK, ML_V, ML_V, ML_HEADS, ML_HEADS])
    seg = lambda i: w_in[:, int(segs[i]):int(segs[i + 1])]
    pad = jnp.zeros((d, LANES - GLA_RANK - 2 * ML_HEADS), w_in.dtype)
    w_perm = jnp.concatenate([seg(0), seg(1), seg(2), seg(3), seg(5), seg(6), seg(7), seg(8),
                              seg(4), seg(9), seg(10), pad], axis=1).astype(BF16)
    proj = norm_proj(x, gain, w_perm)
    gates = proj[:, C_SMALL + S_MI:C_SMALL + S_MI + 2 * ML_HEADS]
    gates_t = gates.reshape(batch, seq // CHUNK, CHUNK, 2 * ML_HEADS).transpose(0, 1, 3, 2)
    b_small = jnp.zeros((1, LANES), F32).at[0, S_MI:S_MI + 2 * ML_HEADS].set(b_if)
    return hybrid_core(proj, gates_t, w_gate2, b_gate.reshape(1, -1), gla_norm.reshape(1, -1), conv_w,
                       b_small, b_if.reshape(-1, 1), ml_norm.reshape(1, -1), batch)


LOG2E = 1.4426950408889634
AUG = 2 * MOBA_HD
LANES_PER_BLOCK = 4
MASK_BIG = 2.0 ** 100
PROJ_HEADS = 4
VT_ROWS = MOBA_HD + 16


def _split3(x):
    x = np.asarray(x, np.float32)
    hi = x.astype(BF16).astype(np.float32)
    mid = (x - hi).astype(BF16).astype(np.float32)
    lo = (x - hi - mid).astype(BF16).astype(np.float32)
    return np.stack([hi, mid, lo], axis=-1)


def _aug_tables(nb):
    c = np.asarray([LOG2E * 2.0 ** (-8.0 * (h + 1) / MOBA_HEADS) for h in range(MOBA_HEADS)], np.float32)
    pos = np.arange(MOBA_BLOCK, dtype=np.float32)
    base = LANES_PER_BLOCK * nb
    q_rows = np.zeros((MOBA_HEADS, MOBA_BLOCK, LANES), np.float32)
    k_rows = np.zeros((MOBA_HEADS, MOBA_BLOCK, LANES), np.float32)
    q_rows[:, :, base:base + 3] = _split3(-c[:, None] * pos[None, :])
    q_rows[:, :, base + 3:base + 6] = 1.0
    k_rows[:, :, base:base + 3] = 1.0
    k_rows[:, :, base + 3:base + 6] = _split3(c[:, None] * pos[None, :])
    i_idx = np.arange(nb)[:, None]
    j_idx = np.arange(nb)[None, :]
    steps = np.zeros((MOBA_HEADS, nb, nb, LANES_PER_BLOCK), np.float32)
    steps[..., 0] = np.where(j_idx == i_idx, 0.0, -MASK_BIG)
    gap = _split3(-c[:, None, None] * (MOBA_BLOCK * (i_idx - j_idx)).astype(np.float32)[None])
    steps[..., 1:] = np.where((j_idx <= i_idx)[None, :, :, None], gap, 0.0)
    q_steps = np.zeros((MOBA_HEADS, nb, 1, LANES), np.float32)
    q_steps[:, :, 0, :base] = steps.reshape(MOBA_HEADS, nb, base)
    k_steps = np.zeros((nb, 1, LANES), np.float32)
    k_steps[:, 0, :base] = np.repeat(np.eye(nb, dtype=np.float32), LANES_PER_BLOCK, axis=1)
    expand = np.zeros((nb, LANES), np.float32)
    expand[np.arange(nb), LANES_PER_BLOCK * np.arange(nb)] = MASK_BIG
    return (jnp.asarray(q_rows), jnp.asarray(q_steps), jnp.asarray(k_rows), jnp.asarray(k_steps),
            jnp.asarray(expand, BF16))


def _moba_prep_kernel(x_ref, g_ref, w_ref, qg_ref, kg_ref, qrow_ref, qstep_ref, krow_ref, kstep_ref, exp_ref,
                      qa_ref, ka_ref, vt_ref, kmean_ref):
    i = pl.program_id(1)
    nb = kmean_ref.shape[1]
    d = MOBA_HEADS * MOBA_HD

    @pl.when(i == 0)
    def _():
        kmean_ref[...] = jnp.zeros_like(kmean_ref)

    xn = _rms(x_ref[...], g_ref[...]).astype(BF16)
    blk = lax.broadcasted_iota(jnp.int32, (nb, MOBA_BLOCK), 0)
    past = blk < i
    ones_rows = (lax.broadcasted_iota(jnp.int32, (VT_ROWS - MOBA_HD, MOBA_BLOCK), 0) == 0).astype(F32)
    k_step = kstep_ref[i]

    for h in range(MOBA_HEADS):
        hg = h % PROJ_HEADS
        if hg == 0:
            c0 = h * MOBA_HD
            q_grp, k_grp, v_grp = (jnp.dot(xn, w_ref[:, off + c0:off + c0 + PROJ_HEADS * MOBA_HD],
                                           preferred_element_type=F32) for off in (0, d, 2 * d))
        part = slice(hg * MOBA_HD, (hg + 1) * MOBA_HD)
        q = _rms(q_grp[:, part], qg_ref[...]) * (MOBA_HD ** -0.5)
        k = _rms(k_grp[:, part], kg_ref[...])
        v = v_grp[:, part]
        vt_ref[0, h] = jnp.concatenate([v.T, ones_rows], axis=0).astype(vt_ref.dtype)

        gate = _dot_nt(kmean_ref[h], q)
        gate = jnp.where(past, gate, -jnp.inf)
        rank = jnp.zeros((nb, MOBA_BLOCK), F32)
        for j in range(nb):
            other = gate[j:j + 1]
            ahead = (other > gate) | ((other == gate) & (j < blk))
            rank = rank + ahead.astype(F32)
        sel = (past & (rank < MOBA_TOPK)).astype(BF16)
        q_extra = (qrow_ref[h] + qstep_ref[h, i]) + _dot_tn(sel, exp_ref[...])
        qa_ref[:, h * AUG:h * AUG + MOBA_HD] = (q * LOG2E).astype(qa_ref.dtype)
        qa_ref[:, h * AUG + MOBA_HD:(h + 1) * AUG] = q_extra.astype(qa_ref.dtype)
        ka_ref[:, h * AUG:h * AUG + MOBA_HD] = k.astype(ka_ref.dtype)
        ka_ref[:, h * AUG + MOBA_HD:(h + 1) * AUG] = (krow_ref[h] + k_step).astype(ka_ref.dtype)

        kmean_ref[h, pl.ds(i, 1), :] = jnp.mean(k, axis=0, keepdims=True)


def moba_prep(x, gain, w_qkv, q_gain, k_gain, batch):
    m, dm = x.shape
    seq = m // batch
    nb = seq // MOBA_BLOCK
    assert LANES_PER_BLOCK * nb + 6 <= LANES
    d = MOBA_HEADS * MOBA_HD
    q_rows, q_steps, k_rows, k_steps, expand = _aug_tables(nb)
    return pl.pallas_call(
        _moba_prep_kernel,
        grid=(batch, nb),
        in_specs=[pl.BlockSpec((MOBA_BLOCK, dm), lambda b, i: (b * nb + i, 0)),
                  _const_spec((1, dm)),
                  _const_spec((dm, 3 * d)),
                  _const_spec((1, MOBA_HD)),
                  _const_spec((1, MOBA_HD)),
                  _const_spec((MOBA_HEADS, MOBA_BLOCK, LANES)),
                  _const_spec((MOBA_HEADS, nb, 1, LANES)),
                  _const_spec((MOBA_HEADS, MOBA_BLOCK, LANES)),
                  _const_spec((nb, 1, LANES)),
                  _const_spec((nb, LANES))],
        out_specs=[pl.BlockSpec((MOBA_BLOCK, MOBA_HEADS * AUG), lambda b, i: (b * nb + i, 0)),
                   pl.BlockSpec((MOBA_BLOCK, MOBA_HEADS * AUG), lambda b, i: (b * nb + i, 0)),
                   pl.BlockSpec((1, MOBA_HEADS, VT_ROWS, MOBA_BLOCK), lambda b, i: (b, 0, 0, i))],
        out_shape=[jax.ShapeDtypeStruct((m, MOBA_HEADS * AUG), BF16),
                   jax.ShapeDtypeStruct((m, MOBA_HEADS * AUG), BF16),
                   jax.ShapeDtypeStruct((batch, MOBA_HEADS, VT_ROWS, seq), BF16)],
        scratch_shapes=[pltpu.VMEM((MOBA_HEADS, nb, MOBA_HD), F32)],
        compiler_params=_params(2),
        name="moba_prep",
    )(x, gain.reshape(1, dm), w_qkv, q_gain.reshape(1, -1), k_gain.reshape(1, -1),
      q_rows, q_steps, k_rows, k_steps, expand)


ATTN_HEADS_PER_STEP = 4
ATTN_CHUNK_BLOCKS = 4
PV_KEYS = 256


def _moba_attn_kernel(q_ref, k_ref, vt_ref, o_ref, s_ref, *, heads, chunk_blocks):
    i = pl.program_id(2)
    chunk_keys = chunk_blocks * MOBA_BLOCK
    n_chunks = i // chunk_blocks + 1
    key_r = lax.broadcasted_iota(jnp.int32, (MOBA_BLOCK, MOBA_BLOCK), 0)
    qry_c = lax.broadcasted_iota(jnp.int32, (MOBA_BLOCK, MOBA_BLOCK), 1)
    causal = qry_c >= key_r
    q_aug = [q_ref[:, hh * AUG:(hh + 1) * AUG] for hh in range(heads)]

    def pass1(c, m_part, last):
        k0 = pl.multiple_of(c * chunk_keys, chunk_keys)
        out = []
        for hh in range(heads):
            s = lax.dot_general(k_ref[pl.ds(k0, chunk_keys), hh * AUG:(hh + 1) * AUG], q_aug[hh],
                                (((1,), (1,)), ((), ())), preferred_element_type=F32)
            if last:
                parts = []
                for u in range(chunk_blocks):
                    other_block = jnp.full((1, MOBA_BLOCK), c * chunk_blocks + u, jnp.int32) != i
                    parts.append(jnp.where(causal | other_block, s[u * MOBA_BLOCK:(u + 1) * MOBA_BLOCK], NEG))
                s = jnp.concatenate(parts, axis=0)
            s_ref[hh, pl.ds(k0, chunk_keys), :] = s
            out.append(jnp.maximum(m_part[hh],
                                   jnp.max(s.reshape(chunk_keys // SUBLANES, SUBLANES, MOBA_BLOCK), axis=0)))
        return tuple(out)

    m_part = tuple(jnp.full((SUBLANES, MOBA_BLOCK), NEG, F32) for _ in range(heads))
    m_part = lax.fori_loop(0, n_chunks - 1, lambda c, m: pass1(c, m, False), m_part)
    m_part = pass1(n_chunks - 1, m_part, True)
    m_row = [jnp.max(m_part[hh], axis=0, keepdims=True) for hh in range(heads)]

    def pass2(c, acc):
        k0 = pl.multiple_of(c * chunk_keys, chunk_keys)
        out = []
        for hh in range(heads):
            a = acc[hh]
            for u in range(0, chunk_keys, PV_KEYS):
                ku = pl.multiple_of(k0 + u, PV_KEYS)
                p = jnp.exp2(s_ref[hh, pl.ds(ku, PV_KEYS), :] - m_row[hh]).astype(BF16)
                a = a + jnp.dot(vt_ref[0, hh, :, pl.ds(ku, PV_KEYS)], p, preferred_element_type=F32)
            out.append(a)
        return tuple(out)

    acc = tuple(jnp.zeros((VT_ROWS, MOBA_BLOCK), F32) for _ in range(heads))
    acc = lax.fori_loop(0, n_chunks, pass2, acc)
    for hh in range(heads):
        o = acc[hh][:MOBA_HD] * (1.0 / acc[hh][MOBA_HD:MOBA_HD + 1])
        o_ref[:, hh * MOBA_HD:(hh + 1) * MOBA_HD] = o.T.astype(o_ref.dtype)


def moba_attention(q_aug, k_aug, vt, batch):
    m = q_aug.shape[0]
    seq = m // batch
    nb = seq // MOBA_BLOCK
    hps = ATTN_HEADS_PER_STEP
    assert nb % ATTN_CHUNK_BLOCKS == 0
    return pl.pallas_call(
        functools.partial(_moba_attn_kernel, heads=hps, chunk_blocks=ATTN_CHUNK_BLOCKS),
        grid=(batch, MOBA_HEADS // hps, nb),
        in_specs=[pl.BlockSpec((MOBA_BLOCK, hps * AUG), lambda b, g, i: (b * nb + i, g)),
                  pl.BlockSpec((seq, hps * AUG), lambda b, g, i: (b, g)),
                  pl.BlockSpec((1, hps, VT_ROWS, seq), lambda b, g, i: (b, g, 0, 0))],
        out_specs=pl.BlockSpec((MOBA_BLOCK, hps * MOBA_HD), lambda b, g, i: (b * nb + i, g)),
        out_shape=jax.ShapeDtypeStruct((m, MOBA_HEADS * MOBA_HD), BF16),
        scratch_shapes=[pltpu.VMEM((hps, seq, MOBA_BLOCK), F32)],
        compiler_params=_params(3),
        name="moba_attn",
    )(q_aug, k_aug, vt)


def moba_layer(x, gain, w_qkv, q_gain, k_gain, batch):
    q_aug, k_aug, vt = moba_prep(x, gain, w_qkv.astype(BF16), q_gain, k_gain, batch)
    return moba_attention(q_aug, k_aug, vt, batch)


def kernel(x, norm_mix, norm_ffn, hyb_w_in, gla_w_gate2, gla_b_gate, gla_norm, ml_conv, ml_b_if, ml_norm,
           hyb_w_out, moba_w_qkv, moba_q_norm, moba_k_norm, moba_w_o, ffn_w_gate, ffn_w_up, ffn_w_down):
    batch, seq, d = x.shape
    depth = norm_mix.shape[0]
    xf = x.reshape(batch * seq, d)
    for l in range(depth):
        j = l // 2
        if l % 2 == 0:
            y = hybrid_layer(xf, norm_mix[l], hyb_w_in[j], gla_w_gate2[j], gla_b_gate[j], gla_norm[j],
                             ml_conv[j], ml_b_if[j], ml_norm[j], batch)
            w_o = hyb_w_out[j]
        else:
            y = moba_layer(xf, norm_mix[l], moba_w_qkv[j], moba_q_norm[j], moba_k_norm[j], batch)
            w_o = moba_w_o[j]
        xf = out_ffn(xf, y, w_o.astype(BF16), norm_ffn[l], ffn_w_gate[l].astype(BF16),
                     ffn_w_up[l].astype(BF16), ffn_w_down[l].astype(BF16))
    return xf.reshape(batch, seq, d)
```

```python
import functools

import jax
import jax.numpy as jnp
import numpy as np
from jax import lax
from jax.experimental import pallas as pl
from jax.experimental.pallas import tpu as pltpu

F32 = jnp.float32
BF16 = jnp.bfloat16
HIGHEST = lax.Precision.HIGHEST

LANES = 128
SUBLANES = 8
VMEM_LIMIT_BYTES = 56 * 1024 * 1024

EPS = 1e-6
NEG = -1e30

GLA_HEADS = 4
GLA_DK = 64
GLA_DV = 128
GLA_RANK = 16
GLA_TAU = 16.0
ML_HEADS = 4
ML_DK = 64
ML_DV = 128
ML_CONV = 4
CHUNK = 64
SUB = 16
MOBA_HEADS = 8
MOBA_HD = 128
MOBA_BLOCK = 256
MOBA_TOPK = 3

GLA_QK = GLA_HEADS * GLA_DK
GLA_V = GLA_HEADS * GLA_DV
ML_QK = ML_HEADS * ML_DK
ML_V = ML_HEADS * ML_DV

C_GQ = 0
C_GK = C_GQ + GLA_QK
C_GV = C_GK + GLA_QK
C_GR = C_GV + GLA_V
C_MQ = C_GR + GLA_V
C_MK = C_MQ + ML_QK
C_MV = C_MK + ML_QK
C_MO = C_MV + ML_V
C_SMALL = C_MO + ML_V
HYB_COLS = C_SMALL + LANES
S_GLR = 0
S_MI = GLA_RANK
S_MF = GLA_RANK + ML_HEADS

ROW_TILE = 512
MIX_TILE = 256
FF_TILE = 256


def _rms(x, gain):
    return x * lax.rsqrt(jnp.mean(x * x, axis=-1, keepdims=True) + EPS) * gain


def _log_sigmoid(z):
    return jnp.minimum(z, 0.0) - jnp.log1p(jnp.exp(-jnp.abs(z)))


def _sigmoid(z):
    return 1.0 / (1.0 + jnp.exp(-z))


def _dot(a, b):
    return jnp.dot(a.astype(BF16), b.astype(BF16), preferred_element_type=F32)


def _dot_nt(a, b):
    return lax.dot_general(a.astype(BF16), b.astype(BF16), (((1,), (1,)), ((), ())),
                           preferred_element_type=F32)


def _dot_tn(a, b):
    return lax.dot_general(a.astype(BF16), b.astype(BF16), (((0,), (0,)), ((), ())),
                           preferred_element_type=F32)


def _dot_f32(a, b):
    return jnp.dot(a, b, preferred_element_type=F32, precision=HIGHEST)


def _const_spec(shape):
    return pl.BlockSpec(shape, lambda *_: (0,) * len(shape), pipeline_mode=pl.Buffered(1))


def _params(n_grid):
    return pltpu.CompilerParams(dimension_semantics=("arbitrary",) * n_grid,
                                vmem_limit_bytes=VMEM_LIMIT_BYTES)


def _norm_proj_kernel(x_ref, g_ref, w_ref, o_ref, *, col_tile):
    xn = _rms(x_ref[...], g_ref[...]).astype(BF16)
    n = o_ref.shape[1]
    for c0 in range(0, n, col_tile):
        c1 = min(c0 + col_tile, n)
        o_ref[:, c0:c1] = jnp.dot(xn, w_ref[:, c0:c1], preferred_element_type=F32)


def norm_proj(x, gain, w):
    m, d = x.shape
    n = w.shape[1]
    tm = min(ROW_TILE, m)
    return pl.pallas_call(
        functools.partial(_norm_proj_kernel, col_tile=512),
        grid=(m // tm,),
        in_specs=[pl.BlockSpec((tm, d), lambda i: (i, 0)),
                  _const_spec((1, d)),
                  _const_spec((d, n))],
        out_specs=pl.BlockSpec((tm, n), lambda i: (i, 0)),
        out_shape=jax.ShapeDtypeStruct((m, n), F32),
        compiler_params=_params(1),
        name="norm_proj",
    )(x, gain.reshape(1, d), w)


def _out_ffn_kernel(x_ref, y_ref, wo_ref, g_ref, wg_ref, wu_ref, wd_ref, o_ref, h_ref, *, ff_tile):
    x1 = x_ref[...] + jnp.dot(y_ref[...], wo_ref[...], preferred_element_type=F32)
    xn = _rms(x1, g_ref[...]).astype(BF16)
    d_ff = wg_ref.shape[1]
    for c0 in range(0, d_ff, ff_tile):
        g = jnp.dot(xn, wg_ref[:, c0:c0 + ff_tile], preferred_element_type=F32)
        u = jnp.dot(xn, wu_ref[:, c0:c0 + ff_tile], preferred_element_type=F32)
        h_ref[:, c0:c0 + ff_tile] = (g * _sigmoid(g) * u).astype(BF16)
    o_ref[...] = x1 + jnp.dot(h_ref[...], wd_ref[...], preferred_element_type=F32)


def out_ffn(x, y, w_o, gain, w_gate, w_up, w_down):
    m, d = x.shape
    dy = y.shape[1]
    d_ff = w_gate.shape[1]
    tm = min(ROW_TILE, m)
    return pl.pallas_call(
        functools.partial(_out_ffn_kernel, ff_tile=FF_TILE),
        grid=(m // tm,),
        in_specs=[pl.BlockSpec((tm, d), lambda i: (i, 0)),
                  pl.BlockSpec((tm, dy), lambda i: (i, 0)),
                  _const_spec((dy, d)),
                  _const_spec((1, d)),
                  _const_spec((d, d_ff)),
                  _const_spec((d, d_ff)),
                  _const_spec((d_ff, d))],
        out_specs=pl.BlockSpec((tm, d), lambda i: (i, 0)),
        out_shape=jax.ShapeDtypeStruct((m, d), F32),
        scratch_shapes=[pltpu.VMEM((tm, d_ff), BF16)],
        compiler_params=_params(1),
        name="out_ffn",
    )(x, y, w_o, gain.reshape(1, d), w_gate, w_up, w_down)


def _hybrid_kernel(proj_ref, gates_ref, wg2_ref, bg_ref, gn_ref, conv_ref, bsmall_ref, bcol_ref, mn_ref,
                   y_ref, gla_st, ml_cs, ml_ns, ml_m, conv_carry, qk_ref):
    t_tile = proj_ref.shape[0]
    n_chunks = t_tile // CHUNK

    @pl.when(pl.program_id(1) == 0)
    def _():
        gla_st[...] = jnp.zeros_like(gla_st)
        ml_cs[...] = jnp.zeros_like(ml_cs)
        ml_ns[...] = jnp.zeros_like(ml_ns)
        ml_m[...] = jnp.zeros_like(ml_m)
        conv_carry[...] = jnp.zeros_like(conv_carry)

    xcat = proj_ref[:, C_MQ:C_MQ + 2 * ML_QK]
    ext = jnp.concatenate([conv_carry[...], xcat], axis=0)
    cw = conv_ref[...]
    acc = ext * cw[ML_CONV - 1:ML_CONV]
    for j in range(1, ML_CONV):
        acc = acc + pltpu.roll(ext, j, axis=0) * cw[ML_CONV - 1 - j:ML_CONV - j]
    conv = acc[SUBLANES:]
    qk_ref[...] = conv * _sigmoid(conv)
    conv_carry[...] = xcat[t_tile - SUBLANES:]

    row = lax.broadcasted_iota(jnp.int32, (CHUNK, CHUNK), 0)
    col = lax.broadcasted_iota(jnp.int32, (CHUNK, CHUNK), 1)
    causal = row >= col
    tril = causal.astype(F32)
    triu = (row <= col).astype(F32)
    lane = lax.broadcasted_iota(jnp.int32, (1, LANES), 1)
    head_mask = (lane < GLA_DK, lane >= GLA_DK)

    def chunk_body(c, carry):
        r0 = pl.multiple_of(c * CHUNK, CHUNK)
        rows = pl.ds(r0, CHUNK)
        small = proj_ref[rows, C_SMALL:C_SMALL + LANES]

        z = _dot_f32(small[:, S_GLR:S_GLR + GLA_RANK], wg2_ref[...]) + bg_ref[...]
        log_a = _log_sigmoid(z) * (1.0 / GLA_TAU)
        bc_all = _dot_f32(tril, log_a)
        for p in range(GLA_HEADS // 2):
            lanes = slice(p * LANES, (p + 1) * LANES)
            q2 = proj_ref[rows, C_GQ + p * LANES:C_GQ + (p + 1) * LANES] * (GLA_DK ** -0.5)
            k2 = proj_ref[rows, C_GK + p * LANES:C_GK + (p + 1) * LANES]
            bc = bc_all[:, lanes]
            st = gla_st[p]
            q_in = q2 * jnp.exp(bc)
            q_sub, k_sub = [], []
            for i in range(CHUNK // SUB):
                rs = slice(i * SUB, (i + 1) * SUB)
                ref_pt = bc[i * SUB - 1:i * SUB] if i else jnp.zeros((1, LANES), F32)
                q_sub.append(q2[rs] * jnp.exp(bc[rs] - ref_pt))
                k_sub.append(k2 * jnp.exp(ref_pt - bc))
            last = bc[CHUNK - 1:CHUNK]
            k_dec = k2 * jnp.exp(last - bc)
            upd = st * jnp.exp(last)
            for hl in range(2):
                h = 2 * p + hl
                hm = head_mask[hl]
                sc = jnp.concatenate(
                    [_dot_nt(jnp.where(hm, q_sub[i], 0.0), k_sub[i]) for i in range(CHUNK // SUB)], axis=0)
                sc = jnp.where(causal, sc, 0.0)
                v = proj_ref[rows, C_GV + h * GLA_DV:C_GV + (h + 1) * GLA_DV]
                o = _dot(sc, v) + _dot_nt(jnp.where(hm, q_in, 0.0), st)
                gate = proj_ref[rows, C_GR + h * GLA_DV:C_GR + (h + 1) * GLA_DV]
                yg = _rms(o, gn_ref[:, h * GLA_DV:(h + 1) * GLA_DV]) * (gate * _sigmoid(gate))
                y_ref[rows, h * GLA_DV:(h + 1) * GLA_DV] = yg.astype(y_ref.dtype)
                upd = upd + _dot_tn(v, jnp.where(hm, k_dec, 0.0))
            gla_st[p] = upd

        g_rows = gates_ref[0, c] + bcol_ref[...]
        ls_rows = _log_sigmoid(g_rows)
        bc_rows = _dot_f32(ls_rows, triu)
        smallb = small + bsmall_ref[...]
        bc_cols = _dot_f32(tril, _log_sigmoid(smallb))
        for p in range(ML_HEADS // 2):
            mq2 = qk_ref[rows, p * LANES:(p + 1) * LANES]
            mk2 = qk_ref[rows, ML_QK + p * LANES:ML_QK + (p + 1) * LANES] * (ML_DK ** -0.5)
            cs = ml_cs[p]
            ns = ml_ns[p]
            cs_new = jnp.zeros_like(cs)
            ns_new = jnp.zeros_like(ns)
            carry_row = jnp.zeros((1, LANES), F32)
            for hl in range(2):
                h = 2 * p + hl
                hm = head_mask[hl]
                i_row = g_rows[h:h + 1]
                bc_row = bc_rows[ML_HEADS + h:ML_HEADS + h + 1]
                b_row = i_row - bc_row
                bc_last = bc_row[:, CHUNK - 1:CHUNK]
                i_col = smallb[:, S_MI + h:S_MI + h + 1]
                bc_col = bc_cols[:, S_MF + h:S_MF + h + 1]
                m_prev = ml_m[h:h + 1, 0:1]

                b_mat = jnp.where(causal, b_row, -jnp.inf)
                m_col = jnp.maximum(m_prev, jnp.max(b_mat, axis=1, keepdims=True))
                w_dec = jnp.exp(b_mat - m_col)
                qm = jnp.where(hm, mq2, 0.0)
                w = _dot_nt(qm, mk2) * w_dec
                inter = jnp.exp(m_prev - m_col)
                v = proj_ref[rows, C_MV + h * ML_DV:C_MV + (h + 1) * ML_DV]
                num = inter * _dot_nt(qm, cs) + _dot(w, v)
                den = inter * jnp.sum(qm * ns, axis=1, keepdims=True) + jnp.sum(w, axis=1, keepdims=True)
                hout = num / jnp.maximum(jnp.abs(den), jnp.exp(-(bc_col + m_col)))
                og = proj_ref[rows, C_MO + h * ML_DV:C_MO + (h + 1) * ML_DV]
                ym = _rms(hout * _sigmoid(og), mn_ref[:, h * ML_DV:(h + 1) * ML_DV])
                y_ref[rows, GLA_V + h * ML_DV:GLA_V + (h + 1) * ML_DV] = ym.astype(y_ref.dtype)

                m_new = bc_last + jnp.maximum(m_prev, jnp.max(b_row, axis=1, keepdims=True))
                scale = jnp.exp(bc_last + m_prev - m_new)
                wk = jnp.where(hm, mk2 * jnp.exp(bc_last - bc_col + i_col - m_new), 0.0)
                cs_new = cs_new + _dot_tn(v, wk)
                ns_new = ns_new + jnp.sum(wk, axis=0, keepdims=True)
                carry_row = jnp.where(hm, scale, carry_row)
                ml_m[h:h + 1, :] = jnp.broadcast_to(m_new, (1, LANES))
            ml_cs[p] = cs * carry_row + cs_new
            ml_ns[p] = ns * carry_row + ns_new
        return carry

    lax.fori_loop(0, n_chunks, chunk_body, 0)


def hybrid_core(proj, gates_t, w_gate2, b_gate, gla_norm, conv_w, b_small, b_col, ml_norm, batch):
    m = proj.shape[0]
    seq = m // batch
    t = min(MIX_TILE, seq)
    nt = seq // t
    return pl.pallas_call(
        _hybrid_kernel,
        grid=(batch, nt),
        in_specs=[pl.BlockSpec((t, HYB_COLS), lambda b, i: (b * nt + i, 0)),
                  pl.BlockSpec((1, t // CHUNK, 2 * ML_HEADS, CHUNK), lambda b, i: (b, i, 0, 0)),
                  _const_spec((GLA_RANK, GLA_QK)),
                  _const_spec((1, GLA_QK)),
                  _const_spec((1, GLA_V)),
                  _const_spec((ML_CONV, 2 * ML_QK)),
                  _const_spec((1, LANES)),
                  _const_spec((2 * ML_HEADS, 1)),
                  _const_spec((1, ML_V))],
        out_specs=pl.BlockSpec((t, GLA_V + ML_V), lambda b, i: (b * nt + i, 0)),
        out_shape=jax.ShapeDtypeStruct((m, GLA_V + ML_V), BF16),
        scratch_shapes=[pltpu.VMEM((GLA_HEADS // 2, GLA_DV, LANES), F32),
                        pltpu.VMEM((ML_HEADS // 2, ML_DV, LANES), F32),
                        pltpu.VMEM((ML_HEADS // 2, 1, LANES), F32),
                        pltpu.VMEM((2 * ML_HEADS, LANES), F32),
                        pltpu.VMEM((SUBLANES, 2 * ML_QK), F32),
                        pltpu.VMEM((t, 2 * ML_QK), F32)],
        compiler_params=_params(2),
        name="hybrid_core",
    )(proj, gates_t, w_gate2, b_gate, gla_norm, conv_w, b_small, b_col, ml_norm)


def hybrid_layer(x, gain, w_in, w_gate2, b_gate, gla_norm, conv_w, b_if, ml_norm, batch):
    m, d = x.shape
    seq = m // batch
    segs = np.cumsum([0, GLA_QK, GLA_QK, GLA_V, GLA_V, GLA_RANK, ML_QK, ML_QK, ML_V, ML_V, ML_HEADS, ML_HEADS])
    seg = lambda i: w_in[:, int(segs[i]):int(segs[i + 1])]
    pad = jnp.zeros((d, LANES - GLA_RANK - 2 * ML_HEADS), w_in.dtype)
    w_perm = jnp.concatenate([seg(0), seg(1), seg(2), seg(3), seg(5), seg(6), seg(7), seg(8),
                              seg(4), seg(9), seg(10), pad], axis=1).astype(BF16)
    proj = norm_proj(x, gain, w_perm)
    gates = proj[:, C_SMALL + S_MI:C_SMALL + S_MI + 2 * ML_HEADS]
    gates_t = gates.reshape(batch, seq // CHUNK, CHUNK, 2 * ML_HEADS).transpose(0, 1, 3, 2)
    b_small = jnp.zeros((1, LANES), F32).at[0, S_MI:S_MI + 2 * ML_HEADS].set(b_if)
    return hybrid_core(proj, gates_t, w_gate2, b_gate.reshape(1, -1), gla_norm.reshape(1, -1), conv_w,
                       b_small, b_if.reshape(-1, 1), ml_norm.reshape(1, -1), batch)


LOG2E = 1.4426950408889634
AUG = 2 * MOBA_HD
LANES_PER_BLOCK = 4
MASK_BIG = 2.0 ** 100
PROJ_HEADS = 4
VT_ROWS = MOBA_HD + 16


def _split3(x):
    x = np.asarray(x, np.float32)
    hi = x.astype(BF16).astype(np.float32)
    mid = (x - hi).astype(BF16).astype(np.float32)
    lo = (x - hi - mid).astype(BF16).astype(np.float32)
    return np.stack([hi, mid, lo], axis=-1)


def _aug_tables(nb):
    c = np.asarray([LOG2E * 2.0 ** (-8.0 * (h + 1) / MOBA_HEADS) for h in range(MOBA_HEADS)], np.float32)
    pos = np.arange(MOBA_BLOCK, dtype=np.float32)
    base = LANES_PER_BLOCK * nb
    q_rows = np.zeros((MOBA_HEADS, MOBA_BLOCK, LANES), np.float32)
    k_rows = np.zeros((MOBA_HEADS, MOBA_BLOCK, LANES), np.float32)
    q_rows[:, :, base:base + 3] = _split3(-c[:, None] * pos[None, :])
    q_rows[:, :, base + 3:base + 6] = 1.0
    k_rows[:, :, base:base + 3] = 1.0
    k_rows[:, :, base + 3:base + 6] = _split3(c[:, None] * pos[None, :])
    i_idx = np.arange(nb)[:, None]
    j_idx = np.arange(nb)[None, :]
    steps = np.zeros((MOBA_HEADS, nb, nb, LANES_PER_BLOCK), np.float32)
    steps[..., 0] = np.where(j_idx == i_idx, 0.0, -MASK_BIG)
    gap = _split3(-c[:, None, None] * (MOBA_BLOCK * (i_idx - j_idx)).astype(np.float32)[None])
    steps[..., 1:] = np.where((j_idx <= i_idx)[None, :, :, None], gap, 0.0)
    q_steps = np.zeros((MOBA_HEADS, nb, 1, LANES), np.float32)
    q_steps[:, :, 0, :base] = steps.reshape(MOBA_HEADS, nb, base)
    k_steps = np.zeros((nb, 1, LANES), np.float32)
    k_steps[:, 0, :base] = np.repeat(np.eye(nb, dtype=np.float32), LANES_PER_BLOCK, axis=1)
    expand = np.zeros((nb, LANES), np.float32)
    expand[np.arange(nb), LANES_PER_BLOCK * np.arange(nb)] = MASK_BIG
    return (jnp.asarray(q_rows), jnp.asarray(q_steps), jnp.asarray(k_rows), jnp.asarray(k_steps),
            jnp.asarray(expand, BF16))


def _moba_prep_kernel(x_ref, g_ref, w_ref, qg_ref, kg_ref, qrow_ref, qstep_ref, krow_ref, kstep_ref, exp_ref,
                      qa_ref, ka_ref, vt_ref, kmean_ref):
    i = pl.program_id(1)
    nb = kmean_ref.shape[1]
    d = MOBA_HEADS * MOBA_HD

    @pl.when(i == 0)
    def _():
        kmean_ref[...] = jnp.zeros_like(kmean_ref)

    xn = _rms(x_ref[...], g_ref[...]).astype(BF16)
    blk = lax.broadcasted_iota(jnp.int32, (nb, MOBA_BLOCK), 0)
    past = blk < i
    ones_rows = (lax.broadcasted_iota(jnp.int32, (VT_ROWS - MOBA_HD, MOBA_BLOCK), 0) == 0).astype(F32)
    k_step = kstep_ref[i]

    for h in range(MOBA_HEADS):
        hg = h % PROJ_HEADS
        if hg == 0:
            c0 = h * MOBA_HD
            q_grp, k_grp, v_grp = (jnp.dot(xn, w_ref[:, off + c0:off + c0 + PROJ_HEADS * MOBA_HD],
                                           preferred_element_type=F32) for off in (0, d, 2 * d))
        part = slice(hg * MOBA_HD, (hg + 1) * MOBA_HD)
        q = _rms(q_grp[:, part], qg_ref[...]) * (MOBA_HD ** -0.5)
        k = _rms(k_grp[:, part], kg_ref[...])
        v = v_grp[:, part]
        vt_ref[0, h] = jnp.concatenate([v.T, ones_rows], axis=0).astype(vt_ref.dtype)

        gate = _dot_nt(kmean_ref[h], q)
        gate = jnp.where(past, gate, -jnp.inf)
        rank = jnp.zeros((nb, MOBA_BLOCK), F32)
        for j in range(nb):
            other = gate[j:j + 1]
            ahead = (other > gate) | ((other == gate) & (j < blk))
            rank = rank + ahead.astype(F32)
        sel = (past & (rank < MOBA_TOPK)).astype(BF16)
        q_extra = (qrow_ref[h] + qstep_ref[h, i]) + _dot_tn(sel, exp_ref[...])
        qa_ref[:, h * AUG:h * AUG + MOBA_HD] = (q * LOG2E).astype(qa_ref.dtype)
        qa_ref[:, h * AUG + MOBA_HD:(h + 1) * AUG] = q_extra.astype(qa_ref.dtype)
        ka_ref[:, h * AUG:h * AUG + MOBA_HD] = k.astype(ka_ref.dtype)
        ka_ref[:, h * AUG + MOBA_HD:(h + 1) * AUG] = (krow_ref[h] + k_step).astype(ka_ref.dtype)

        kmean_ref[h, pl.ds(i, 1), :] = jnp.mean(k, axis=0, keepdims=True)


def moba_prep(x, gain, w_qkv, q_gain, k_gain, batch):
    m, dm = x.shape
    seq = m // batch
    nb = seq // MOBA_BLOCK
    assert LANES_PER_BLOCK * nb + 6 <= LANES
    d = MOBA_HEADS * MOBA_HD
    q_rows, q_steps, k_rows, k_steps, expand = _aug_tables(nb)
    return pl.pallas_call(
        _moba_prep_kernel,
        grid=(batch, nb),
        in_specs=[pl.BlockSpec((MOBA_BLOCK, dm), lambda b, i: (b * nb + i, 0)),
                  _const_spec((1, dm)),
                  _const_spec((dm, 3 * d)),
                  _const_spec((1, MOBA_HD)),
                  _const_spec((1, MOBA_HD)),
                  _const_spec((MOBA_HEADS, MOBA_BLOCK, LANES)),
                  _const_spec((MOBA_HEADS, nb, 1, LANES)),
                  _const_spec((MOBA_HEADS, MOBA_BLOCK, LANES)),
                  _const_spec((nb, 1, LANES)),
                  _const_spec((nb, LANES))],
        out_specs=[pl.BlockSpec((MOBA_BLOCK, MOBA_HEADS * AUG), lambda b, i: (b * nb + i, 0)),
                   pl.BlockSpec((MOBA_BLOCK, MOBA_HEADS * AUG), lambda b, i: (b * nb + i, 0)),
                   pl.BlockSpec((1, MOBA_HEADS, VT_ROWS, MOBA_BLOCK), lambda b, i: (b, 0, 0, i))],
        out_shape=[jax.ShapeDtypeStruct((m, MOBA_HEADS * AUG), BF16),
                   jax.ShapeDtypeStruct((m, MOBA_HEADS * AUG), BF16),
                   jax.ShapeDtypeStruct((batch, MOBA_HEADS, VT_ROWS, seq), BF16)],
        scratch_shapes=[pltpu.VMEM((MOBA_HEADS, nb, MOBA_HD), F32)],
        compiler_params=_params(2),
        name="moba_prep",
    )(x, gain.reshape(1, dm), w_qkv, q_gain.reshape(1, -1), k_gain.reshape(1, -1),
      q_rows, q_steps, k_rows, k_steps, expand)


ATTN_HEADS_PER_STEP = 2
ATTN_CHUNK_BLOCKS = 4
PV_KEYS = 256


def _moba_attn_kernel(q_ref, k_ref, vt_ref, o_ref, s_even, s_odd, m_even, m_odd, *, heads, chunk_blocks, nb):
    t = pl.program_id(2)
    i = jnp.minimum(t, nb - 1)
    chunk_keys = chunk_blocks * MOBA_BLOCK
    n_chunks = i // chunk_blocks + 1
    n_valid2 = jnp.where(t > 0, (t - 1) // chunk_blocks + 1, 0)

    @pl.when((pl.program_id(0) == 0) & (pl.program_id(1) == 0) & (t == 0))
    def _():
        s_even[...] = jnp.zeros_like(s_even)
        s_odd[...] = jnp.zeros_like(s_odd)
        m_odd[...] = jnp.zeros_like(m_odd)

    key_r = lax.broadcasted_iota(jnp.int32, (MOBA_BLOCK, MOBA_BLOCK), 0)
    qry_c = lax.broadcasted_iota(jnp.int32, (MOBA_BLOCK, MOBA_BLOCK), 1)
    causal = qry_c >= key_r
    q_aug = [q_ref[:, hh * AUG:(hh + 1) * AUG] for hh in range(heads)]

    def both(c, carry, last, s_w, s_r, m_row2):
        m_part, acc = carry
        k0 = pl.multiple_of(c * chunk_keys, chunk_keys)
        live2 = jnp.full((1, MOBA_BLOCK), c, jnp.int32) < n_valid2
        new_m, new_acc = [], []
        for hh in range(heads):
            s = lax.dot_general(k_ref[pl.ds(k0, chunk_keys), hh * AUG:(hh + 1) * AUG], q_aug[hh],
                                (((1,), (1,)), ((), ())), preferred_element_type=F32)
            if last:
                parts = []
                for u in range(chunk_blocks):
                    other_block = jnp.full((1, MOBA_BLOCK), c * chunk_blocks + u, jnp.int32) != i
                    parts.append(jnp.where(causal | other_block, s[u * MOBA_BLOCK:(u + 1) * MOBA_BLOCK], NEG))
                s = jnp.concatenate(parts, axis=0)
            s_w[hh, pl.ds(k0, chunk_keys), :] = s
            new_m.append(jnp.maximum(m_part[hh],
                                     jnp.max(s.reshape(chunk_keys // SUBLANES, SUBLANES, MOBA_BLOCK), axis=0)))
            a = acc[hh]
            for u in range(0, chunk_keys, PV_KEYS):
                ku = pl.multiple_of(k0 + u, PV_KEYS)
                p = jnp.exp2(s_r[hh, pl.ds(ku, PV_KEYS), :] - m_row2[hh])
                p = jnp.where(live2, p, 0.0).astype(BF16)
                a = a + jnp.dot(vt_ref[0, hh, :, pl.ds(ku, PV_KEYS)], p, preferred_element_type=F32)
            new_acc.append(a)
        return tuple(new_m), tuple(new_acc)

    def step(s_w, s_r, m_w, m_r):
        m_row2 = [m_r[hh] for hh in range(heads)]
        carry = (tuple(jnp.full((SUBLANES, MOBA_BLOCK), NEG, F32) for _ in range(heads)),
                 tuple(jnp.zeros((VT_ROWS, MOBA_BLOCK), F32) for _ in range(heads)))
        carry = lax.fori_loop(0, n_chunks - 1, lambda c, cr: both(c, cr, False, s_w, s_r, m_row2), carry)
        m_part, acc = both(n_chunks - 1, carry, True, s_w, s_r, m_row2)
        for hh in range(heads):
            m_w[hh] = jnp.max(m_part[hh], axis=0, keepdims=True)
            o = acc[hh][:MOBA_HD] * (1.0 / acc[hh][MOBA_HD:MOBA_HD + 1])
            o_ref[:, hh * MOBA_HD:(hh + 1) * MOBA_HD] = o.T.astype(o_ref.dtype)

    @pl.when(t % 2 == 0)
    def _():
        step(s_even, s_odd, m_even, m_odd)

    @pl.when(t % 2 == 1)
    def _():
        step(s_odd, s_even, m_odd, m_even)


def moba_attention(q_aug, k_aug, vt, batch):
    m = q_aug.shape[0]
    seq = m // batch
    nb = seq // MOBA_BLOCK
    hps = ATTN_HEADS_PER_STEP
    assert nb % ATTN_CHUNK_BLOCKS == 0
    return pl.pallas_call(
        functools.partial(_moba_attn_kernel, heads=hps, chunk_blocks=ATTN_CHUNK_BLOCKS, nb=nb),
        grid=(batch, MOBA_HEADS // hps, nb + 1),
        in_specs=[pl.BlockSpec((MOBA_BLOCK, hps * AUG), lambda b, g, t: (b * nb + jnp.minimum(t, nb - 1), g)),
                  pl.BlockSpec((seq, hps * AUG), lambda b, g, t: (b, g)),
                  pl.BlockSpec((1, hps, VT_ROWS, seq), lambda b, g, t: (b, g, 0, 0))],
        out_specs=pl.BlockSpec((MOBA_BLOCK, hps * MOBA_HD), lambda b, g, t: (b * nb + jnp.maximum(t - 1, 0), g)),
        out_shape=jax.ShapeDtypeStruct((m, MOBA_HEADS * MOBA_HD), BF16),
        scratch_shapes=[pltpu.VMEM((hps, seq, MOBA_BLOCK), F32), pltpu.VMEM((hps, seq, MOBA_BLOCK), F32),
                        pltpu.VMEM((hps, 1, MOBA_BLOCK), F32), pltpu.VMEM((hps, 1, MOBA_BLOCK), F32)],
        compiler_params=_params(3),
        name="moba_attn",
    )(q_aug, k_aug, vt)


def moba_layer(x, gain, w_qkv, q_gain, k_gain, batch):
    q_aug, k_aug, vt = moba_prep(x, gain, w_qkv.astype(BF16), q_gain, k_gain, batch)
    return moba_attention(q_aug, k_aug, vt, batch)


def kernel(x, norm_mix, norm_ffn, hyb_w_in, gla_w_gate2, gla_b_gate, gla_norm, ml_conv, ml_b_if, ml_norm,
           hyb_w_out, moba_w_qkv, moba_q_norm, moba_k_norm, moba_w_o, ffn_w_gate, ffn_w_up, ffn_w_down):
    batch, seq, d = x.shape
    depth = norm_mix.shape[0]
    xf = x.reshape(batch * seq, d)
    for l in range(depth):
        j = l // 2
        if l % 2 == 0:
            y = hybrid_layer(xf, norm_mix[l], hyb_w_in[j], gla_w_gate2[j], gla_b_gate[j], gla_norm[j],
                             ml_conv[j], ml_b_if[j], ml_norm[j], batch)
            w_o = hyb_w_out[j]
        else:
            y = moba_layer(xf, norm_mix[l], moba_w_qkv[j], moba_q_norm[j], moba_k_norm[j], batch)
            w_o = moba_w_o[j]
        xf = out_ffn(xf, y, w_o.astype(BF16), norm_ffn[l], ffn_w_gate[l].astype(BF16),
                     ffn_w_up[l].astype(BF16), ffn_w_down[l].astype(BF16))
    return xf.reshape(batch, seq, d)
```

```python
import functools

import jax
import jax.numpy as jnp
import numpy as np
from jax import lax
from jax.experimental import pallas as pl
from jax.experimental.pallas import tpu as pltpu

F32 = jnp.float32
BF16 = jnp.bfloat16

LANES = 128
SUBLANES = 8
VMEM_LIMIT_BYTES = 56 * 1024 * 1024

EPS = 1e-6
NEG = -1e30

GLA_HEADS = 4
GLA_DK = 64
GLA_DV = 128
GLA_RANK = 16
GLA_TAU = 16.0
ML_HEADS = 4
ML_DK = 64
ML_DV = 128
ML_CONV = 4
CHUNK = 64
SUB = 16
MOBA_HEADS = 8
MOBA_HD = 128
MOBA_BLOCK = 256
MOBA_TOPK = 3

GLA_QK = GLA_HEADS * GLA_DK
GLA_V = GLA_HEADS * GLA_DV
ML_QK = ML_HEADS * ML_DK
ML_V = ML_HEADS * ML_DV

C_GQ = 0
C_GK = C_GQ + GLA_QK
C_GV = C_GK + GLA_QK
C_GR = C_GV + GLA_V
C_MQ = C_GR + GLA_V
C_MK = C_MQ + ML_QK
C_MV = C_MK + ML_QK
C_MO = C_MV + ML_V
C_SMALL = C_MO + ML_V
HYB_COLS = C_SMALL + LANES
S_GLR = 0
S_MI = GLA_RANK
S_MF = GLA_RANK + ML_HEADS

ROW_TILE = 512
MIX_TILE = 256
FF_TILE = 256


def _rms(x, gain):
    return x * lax.rsqrt(jnp.mean(x * x, axis=-1, keepdims=True) + EPS) * gain


def _log_sigmoid(z):
    return jnp.minimum(z, 0.0) - jnp.log1p(jnp.exp(-jnp.abs(z)))


def _sigmoid(z):
    return 1.0 / (1.0 + jnp.exp(-z))


def _dot(a, b):
    return jnp.dot(a.astype(BF16), b.astype(BF16), preferred_element_type=F32)


def _dot_nt(a, b):
    return lax.dot_general(a.astype(BF16), b.astype(BF16), (((1,), (1,)), ((), ())),
                           preferred_element_type=F32)


def _dot_tn(a, b):
    return lax.dot_general(a.astype(BF16), b.astype(BF16), (((0,), (0,)), ((), ())),
                           preferred_element_type=F32)


def _const_spec(shape):
    return pl.BlockSpec(shape, lambda *_: (0,) * len(shape), pipeline_mode=pl.Buffered(1))


def _params(n_grid):
    return pltpu.CompilerParams(dimension_semantics=("arbitrary",) * n_grid,
                                vmem_limit_bytes=VMEM_LIMIT_BYTES)


def _norm_proj_kernel(x_ref, g_ref, w_ref, o_ref, *, col_tile):
    xn = _rms(x_ref[...], g_ref[...]).astype(BF16)
    n = o_ref.shape[1]
    for c0 in range(0, n, col_tile):
        c1 = min(c0 + col_tile, n)
        o_ref[:, c0:c1] = jnp.dot(xn, w_ref[:, c0:c1], preferred_element_type=F32)


def norm_proj(x, gain, w):
    m, d = x.shape
    n = w.shape[1]
    tm = min(ROW_TILE, m)
    return pl.pallas_call(
        functools.partial(_norm_proj_kernel, col_tile=512),
        grid=(m // tm,),
        in_specs=[pl.BlockSpec((tm, d), lambda i: (i, 0)),
                  _const_spec((1, d)),
                  _const_spec((d, n))],
        out_specs=pl.BlockSpec((tm, n), lambda i: (i, 0)),
        out_shape=jax.ShapeDtypeStruct((m, n), F32),
        compiler_params=_params(1),
        name="norm_proj",
    )(x, gain.reshape(1, d), w)


def _out_ffn_kernel(x_ref, y_ref, wo_ref, g_ref, wg_ref, wu_ref, wd_ref, o_ref, h_ref, *, ff_tile):
    x1 = x_ref[...] + jnp.dot(y_ref[...], wo_ref[...], preferred_element_type=F32)
    xn = _rms(x1, g_ref[...]).astype(BF16)
    d_ff = wg_ref.shape[1]
    for c0 in range(0, d_ff, ff_tile):
        g = jnp.dot(xn, wg_ref[:, c0:c0 + ff_tile], preferred_element_type=F32)
        u = jnp.dot(xn, wu_ref[:, c0:c0 + ff_tile], preferred_element_type=F32)
        h_ref[:, c0:c0 + ff_tile] = (g * _sigmoid(g) * u).astype(BF16)
    o_ref[...] = x1 + jnp.dot(h_ref[...], wd_ref[...], preferred_element_type=F32)


def out_ffn(x, y, w_o, gain, w_gate, w_up, w_down):
    m, d = x.shape
    dy = y.shape[1]
    d_ff = w_gate.shape[1]
    tm = min(ROW_TILE, m)
    return pl.pallas_call(
        functools.partial(_out_ffn_kernel, ff_tile=FF_TILE),
        grid=(m // tm,),
        in_specs=[pl.BlockSpec((tm, d), lambda i: (i, 0)),
                  pl.BlockSpec((tm, dy), lambda i: (i, 0)),
                  _const_spec((dy, d)),
                  _const_spec((1, d)),
                  _const_spec((d, d_ff)),
                  _const_spec((d, d_ff)),
                  _const_spec((d_ff, d))],
        out_specs=pl.BlockSpec((tm, d), lambda i: (i, 0)),
        out_shape=jax.ShapeDtypeStruct((m, d), F32),
        scratch_shapes=[pltpu.VMEM((tm, d_ff), BF16)],
        compiler_params=_params(1),
        name="out_ffn",
    )(x, y, w_o, gain.reshape(1, d), w_gate, w_up, w_down)


def _split3_rows(x):
    hi = x.astype(BF16)
    r1 = x - hi.astype(F32)
    mid = r1.astype(BF16)
    lo = (r1 - mid.astype(F32)).astype(BF16)
    return jnp.concatenate([hi, mid, lo], axis=0)


def _split3_lanes(x):
    hi = x.astype(BF16)
    r1 = x - hi.astype(F32)
    mid = r1.astype(BF16)
    lo = (r1 - mid.astype(F32)).astype(BF16)
    return jnp.concatenate([hi, mid, lo], axis=1)


def _blocks_diag(tiles, zero):
    n = len(tiles)
    return jnp.concatenate(
        [jnp.concatenate([tiles[a] if a == b else zero for b in range(n)], axis=1) for a in range(n)], axis=0)


def _hybrid_kernel(proj_ref, gates_ref, wg2_ref, bg_ref, gn_ref, conv_ref, bsmall_ref, brow_ref, mn_ref,
                   y_ref, gla_st, ml_cs, ml_ns, ml_m, conv_carry):
    t_tile = proj_ref.shape[0]
    nc = t_tile // CHUNK
    n_sub = CHUNK // SUB

    @pl.when(pl.program_id(1) == 0)
    def _():
        gla_st[...] = jnp.zeros_like(gla_st)
        ml_cs[...] = jnp.zeros_like(ml_cs)
        ml_ns[...] = jnp.zeros_like(ml_ns)
        ml_m[...] = jnp.zeros_like(ml_m)
        conv_carry[...] = jnp.zeros_like(conv_carry)

    xcat = proj_ref[:, C_MQ:C_MQ + 2 * ML_QK]
    ext = jnp.concatenate([conv_carry[...], xcat], axis=0)
    cw = conv_ref[...]
    acc = ext * cw[ML_CONV - 1:ML_CONV]
    for j in range(1, ML_CONV):
        acc = acc + pltpu.roll(ext, j, axis=0) * cw[ML_CONV - 1 - j:ML_CONV - j]
    conv = acc[SUBLANES:]
    qk = conv * _sigmoid(conv)
    conv_carry[...] = xcat[t_tile - SUBLANES:]

    lane = lax.broadcasted_iota(jnp.int32, (1, LANES), 1)
    head_mask = (lane < GLA_DK, lane >= GLA_DK)
    hm0 = head_mask[0]
    r_t = lax.broadcasted_iota(jnp.int32, (t_tile, t_tile), 0)
    c_t = lax.broadcasted_iota(jnp.int32, (t_tile, t_tile), 1)
    tril_bd = ((r_t >= c_t) & (r_t // CHUNK == c_t // CHUNK)).astype(BF16)
    tril3 = jnp.concatenate([tril_bd, tril_bd, tril_bd], axis=1)
    r_l = lax.broadcasted_iota(jnp.int32, (LANES, LANES), 0)
    c_l = lax.broadcasted_iota(jnp.int32, (LANES, LANES), 1)
    triu_bd = ((r_l <= c_l) & (r_l // CHUNK == c_l // CHUNK)).astype(BF16)
    triu3 = jnp.concatenate([triu_bd, triu_bd, triu_bd], axis=0)
    row_c = lax.broadcasted_iota(jnp.int32, (CHUNK, LANES), 0)
    causal2 = (lax.broadcasted_iota(jnp.int32, (CHUNK, LANES), 1) % CHUNK) <= row_c
    sub_id = (lax.broadcasted_iota(jnp.int32, (t_tile, LANES), 0) % CHUNK) // SUB
    row_sub = lax.broadcasted_iota(jnp.int32, (2 * SUBLANES, LANES), 0)
    zero_tile = jnp.zeros((CHUNK, LANES), F32)

    def chunk_rows(x, c):
        return x[c * CHUNK:(c + 1) * CHUNK]

    def diag_blocks(r):
        return [r[c * CHUNK:(c + 1) * CHUNK, c * LANES:(c + 1) * LANES] for c in range(nc)]

    small = proj_ref[:, C_SMALL:C_SMALL + LANES]
    smallb = small + bsmall_ref[...]

    z = _dot(small, wg2_ref[...]) + bg_ref[...]
    log_a = _log_sigmoid(z) * (1.0 / GLA_TAU)
    cum = jnp.dot(tril3, _split3_rows(jnp.concatenate([log_a, _log_sigmoid(smallb)], axis=1)),
                  preferred_element_type=F32)
    g_all = gates_ref[0, 0] + brow_ref[...]
    bc_rows = jnp.dot(_split3_lanes(_log_sigmoid(g_all)), triu3, preferred_element_type=F32)

    for p in range(GLA_HEADS // 2):
        q2 = proj_ref[:, C_GQ + p * LANES:C_GQ + (p + 1) * LANES] * (GLA_DK ** -0.5)
        k2 = proj_ref[:, C_GK + p * LANES:C_GK + (p + 1) * LANES]
        bc = cum[:, p * LANES:(p + 1) * LANES]
        v = [proj_ref[:, C_GV + (2 * p + hl) * GLA_DV:C_GV + (2 * p + hl + 1) * GLA_DV] for hl in range(2)]

        ref_pt = [[chunk_rows(bc, c)[i * SUB - 1:i * SUB] if i else jnp.zeros((1, LANES), F32)
                   for i in range(n_sub)] for c in range(nc)]
        r_own = jnp.concatenate([jnp.broadcast_to(ref_pt[c][i], (SUB, LANES))
                                 for c in range(nc) for i in range(n_sub)], axis=0)
        qs = q2 * jnp.exp(bc - r_own)
        q_hat = jnp.concatenate([jnp.where(sub_id == i, qs, 0.0) for i in range(n_sub)], axis=1)
        k_hat, k_dec, last = [], [], []
        for c in range(nc):
            kc, bcc = chunk_rows(k2, c), chunk_rows(bc, c)
            ke = [kc * jnp.exp(ref_pt[c][i] - bcc) for i in range(n_sub)]
            for hl in range(2):
                k_hat.append(jnp.concatenate([jnp.where(head_mask[hl], ke[i], 0.0) for i in range(n_sub)], axis=1))
            last.append(bcc[CHUNK - 1:CHUNK])
            k_dec.append(kc * jnp.exp(last[c] - bcc))
        k_hat = jnp.concatenate(k_hat, axis=0)
        sc = diag_blocks(_dot_nt(q_hat, k_hat))
        sc = [jnp.where(causal2, s, 0.0) for s in sc]
        v_bd = jnp.concatenate([jnp.concatenate([chunk_rows(v[hl], c) if hl == b else zero_tile for b in range(2)],
                                                axis=1) for c in range(nc) for hl in range(2)], axis=0)
        o_intra = _dot(_blocks_diag(sc, zero_tile), v_bd)
        u_all = _dot_tn(jnp.concatenate(v, axis=1), _blocks_diag(k_dec, zero_tile))

        st = gla_st[p]
        before = []
        for c in range(nc):
            before.append(st)
            u = jnp.where(hm0, u_all[:GLA_DV, c * LANES:(c + 1) * LANES], u_all[GLA_DV:, c * LANES:(c + 1) * LANES])
            st = st * jnp.exp(last[c]) + u
        gla_st[p] = st
        st_stack = jnp.concatenate(before, axis=0)
        q_in = q2 * jnp.exp(bc)
        for hl in range(2):
            h = 2 * p + hl
            o_inter = jnp.concatenate(diag_blocks(_dot_nt(jnp.where(head_mask[hl], q_in, 0.0), st_stack)), axis=0)
            o = o_intra[:, hl * GLA_DV:(hl + 1) * GLA_DV] + o_inter
            gate = proj_ref[:, C_GR + h * GLA_DV:C_GR + (h + 1) * GLA_DV]
            yg = _rms(o, gn_ref[:, h * GLA_DV:(h + 1) * GLA_DV]) * (gate * _sigmoid(gate))
            y_ref[:, h * GLA_DV:(h + 1) * GLA_DV] = yg.astype(y_ref.dtype)

    ones_tile = jnp.ones((t_tile, LANES), F32)
    for p in range(ML_HEADS // 2):
        mq2 = qk[:, p * LANES:(p + 1) * LANES]
        mk2 = qk[:, ML_QK + p * LANES:ML_QK + (p + 1) * LANES] * (ML_DK ** -0.5)
        v = [proj_ref[:, C_MV + (2 * p + hl) * ML_DV:C_MV + (2 * p + hl + 1) * ML_DV] for hl in range(2)]
        k_stack = jnp.concatenate([jnp.where(head_mask[hl], chunk_rows(mk2, c), 0.0)
                                   for c in range(nc) for hl in range(2)], axis=0)
        qk_blocks = diag_blocks(_dot_nt(mq2, k_stack))

        w_blocks, wk_blocks, cm_cols, b_max, bc_last = [], [], [], [], []
        for c in range(nc):
            bcr = bc_rows[ML_HEADS * c + 2 + p:ML_HEADS * c + 3 + p]
            b_pair = g_all[ML_HEADS * c + p:ML_HEADS * c + p + 1] - bcr
            bm = jnp.where(causal2, b_pair, -jnp.inf)
            cm = [jnp.max(jnp.where(head_mask[hl], bm, -jnp.inf), axis=1, keepdims=True) for hl in range(2)]
            w_blocks.append(qk_blocks[c] * jnp.exp(bm - jnp.where(hm0, cm[0], cm[1])))
            cm_cols.append(cm)
            bmx = [jnp.max(jnp.where(head_mask[hl], b_pair, -jnp.inf), axis=1, keepdims=True) for hl in range(2)]
            b_max.append(jnp.where(hm0, bmx[0], bmx[1]))
            bc_last.append(jnp.where(hm0, bcr[:, CHUNK - 1:CHUNK], bcr[:, LANES - 1:LANES]))
            e_col = []
            for hl in range(2):
                h = 2 * p + hl
                b_col = (chunk_rows(smallb, c)[:, S_MI + h:S_MI + h + 1]
                         - chunk_rows(cum, c)[:, GLA_QK + S_MF + h:GLA_QK + S_MF + h + 1])
                e_col.append(jnp.exp(b_col - bmx[hl]))
            wk_blocks.append(chunk_rows(mk2, c) * jnp.where(hm0, e_col[0], e_col[1]))

        ones_col = [(lane == hl).astype(F32) * jnp.ones((CHUNK, 1), F32) for hl in range(2)]
        v_bd = jnp.concatenate([jnp.concatenate([chunk_rows(v[hl], c) if hl == b else zero_tile for b in range(2)]
                                                + [ones_col[hl]], axis=1)
                                for c in range(nc) for hl in range(2)], axis=0)
        nv = _dot(_blocks_diag(w_blocks, zero_tile), v_bd)
        u_all = _dot_tn(jnp.concatenate(v + [ones_tile], axis=1), _blocks_diag(wk_blocks, zero_tile))

        cs, ns, m_run = ml_cs[p], ml_ns[p], ml_m[p]
        before = []
        for c in range(nc):
            before.append((cs, ns, m_run))
            m_top = jnp.maximum(m_run, b_max[c])
            keep = jnp.exp(m_run - m_top)
            gain = jnp.exp(b_max[c] - m_top)
            blk = slice(c * LANES, (c + 1) * LANES)
            cs = keep * cs + gain * jnp.where(hm0, u_all[:ML_DV, blk], u_all[ML_DV:2 * ML_DV, blk])
            ns = keep * ns + gain * u_all[2 * ML_DV:2 * ML_DV + 1, blk]
            m_run = bc_last[c] + m_top
        ml_cs[p], ml_ns[p], ml_m[p] = cs, ns, m_run
        ns_rows = jnp.zeros((2 * SUBLANES, LANES), F32)
        for c in range(nc):
            ns_rows = jnp.where(row_sub == c, before[c][1], ns_rows)
        state_stack = jnp.concatenate([b[0] for b in before] + [ns_rows], axis=0)

        for hl in range(2):
            h = 2 * p + hl
            rr = _dot_nt(jnp.where(head_mask[hl], mq2, 0.0), state_stack)
            q_cs = diag_blocks(rr)
            h_chunks = []
            for c in range(nc):
                rows = slice(c * CHUNK, (c + 1) * CHUNK)
                q_ns = rr[rows, nc * ML_DV + c:nc * ML_DV + c + 1]
                m_prev = before[c][2][:, hl * ML_DK:hl * ML_DK + 1]
                cm = cm_cols[c][hl]
                m_top = jnp.maximum(m_prev, cm)
                local = jnp.exp(cm - m_top)
                carried = jnp.exp(m_prev - m_top)
                num = local * nv[rows, hl * ML_DV:(hl + 1) * ML_DV] + carried * q_cs[c]
                den = local * nv[rows, 2 * ML_DV + hl:2 * ML_DV + hl + 1] + carried * q_ns
                bc_col = chunk_rows(cum, c)[:, GLA_QK + S_MF + h:GLA_QK + S_MF + h + 1]
                h_chunks.append(num / jnp.maximum(jnp.abs(den), jnp.exp(-(bc_col + m_top))))
            og = proj_ref[:, C_MO + h * ML_DV:C_MO + (h + 1) * ML_DV]
            ym = _rms(jnp.concatenate(h_chunks, axis=0) * _sigmoid(og), mn_ref[:, h * ML_DV:(h + 1) * ML_DV])
            y_ref[:, GLA_V + h * ML_DV:GLA_V + (h + 1) * ML_DV] = ym.astype(y_ref.dtype)


def hybrid_core(proj, gates_t, w_gate2, b_gate, gla_norm, conv_w, b_small, b_rows, ml_norm, batch):
    m = proj.shape[0]
    seq = m // batch
    t = min(MIX_TILE, seq)
    nt = seq // t
    assert 2 * SUBLANES >= t // CHUNK
    return pl.pallas_call(
        _hybrid_kernel,
        grid=(batch, nt),
        in_specs=[pl.BlockSpec((t, HYB_COLS), lambda b, i: (b * nt + i, 0)),
                  pl.BlockSpec((1, 1, ML_HEADS * (t // CHUNK), LANES), lambda b, i: (b, i, 0, 0)),
                  _const_spec((LANES, GLA_QK)),
                  _const_spec((1, GLA_QK)),
                  _const_spec((1, GLA_V)),
                  _const_spec((ML_CONV, 2 * ML_QK)),
                  _const_spec((1, LANES)),
                  _const_spec((ML_HEADS * (t // CHUNK), LANES)),
                  _const_spec((1, ML_V))],
        out_specs=pl.BlockSpec((t, GLA_V + ML_V), lambda b, i: (b * nt + i, 0)),
        out_shape=jax.ShapeDtypeStruct((m, GLA_V + ML_V), BF16),
        scratch_shapes=[pltpu.VMEM((GLA_HEADS // 2, GLA_DV, LANES), F32),
                        pltpu.VMEM((ML_HEADS // 2, ML_DV, LANES), F32),
                        pltpu.VMEM((ML_HEADS // 2, 1, LANES), F32),
                        pltpu.VMEM((ML_HEADS // 2, 1, LANES), F32),
                        pltpu.VMEM((SUBLANES, 2 * ML_QK), F32)],
        compiler_params=_params(2),
        name="hybrid_core",
    )(proj, gates_t, w_gate2, b_gate, gla_norm, conv_w, b_small, b_rows, ml_norm)


def hybrid_layer(x, gain, w_in, w_gate2, b_gate, gla_norm, conv_w, b_if, ml_norm, batch):
    m, d = x.shape
    seq = m // batch
    segs = np.cumsum([0, GLA_QK, GLA_QK, GLA_V, GLA_V, GLA_RANK, ML_QK, ML_QK, ML_V, ML_V, ML_HEADS, ML_HEADS])
    seg = lambda i: w_in[:, int(segs[i]):int(segs[i + 1])]
    pad = jnp.zeros((d, LANES - GLA_RANK - 2 * ML_HEADS), w_in.dtype)
    w_perm = jnp.concatenate([seg(0), seg(1), seg(2), seg(3), seg(5), seg(6), seg(7), seg(8),
                              seg(4), seg(9), seg(10), pad], axis=1).astype(BF16)
    proj = norm_proj(x, gain, w_perm)
    gates = proj[:, C_SMALL + S_MI:C_SMALL + S_MI + 2 * ML_HEADS]
    gates_t = gates.reshape(batch, seq // CHUNK, CHUNK, 2 * ML_HEADS).transpose(0, 1, 3, 2)
    t = min(MIX_TILE, seq)
    gates_t = gates_t.reshape(batch, seq // t, ML_HEADS * (t // CHUNK), LANES)
    b_small = jnp.zeros((1, LANES), F32).at[0, S_MI:S_MI + 2 * ML_HEADS].set(b_if)
    b_rows = jnp.tile(jnp.repeat(b_if, CHUNK).reshape(ML_HEADS, LANES), (t // CHUNK, 1))
    w_gate2_rows = jnp.zeros((LANES, GLA_QK), F32).at[S_GLR:S_GLR + GLA_RANK].set(w_gate2)
    return hybrid_core(proj, gates_t, w_gate2_rows, b_gate.reshape(1, -1), gla_norm.reshape(1, -1), conv_w,
                       b_small, b_rows, ml_norm.reshape(1, -1), batch)


LOG2E = 1.4426950408889634
AUG = 2 * MOBA_HD
LANES_PER_BLOCK = 4
MASK_BIG = 2.0 ** 100
PROJ_HEADS = 4
VT_ROWS = MOBA_HD + 16


def _split3(x):
    x = np.asarray(x, np.float32)
    hi = x.astype(BF16).astype(np.float32)
    mid = (x - hi).astype(BF16).astype(np.float32)
    lo = (x - hi - mid).astype(BF16).astype(np.float32)
    return np.stack([hi, mid, lo], axis=-1)


def _aug_tables(nb):
    c = np.asarray([LOG2E * 2.0 ** (-8.0 * (h + 1) / MOBA_HEADS) for h in range(MOBA_HEADS)], np.float32)
    pos = np.arange(MOBA_BLOCK, dtype=np.float32)
    base = LANES_PER_BLOCK * nb
    q_rows = np.zeros((MOBA_HEADS, MOBA_BLOCK, LANES), np.float32)
    k_rows = np.zeros((MOBA_HEADS, MOBA_BLOCK, LANES), np.float32)
    q_rows[:, :, base:base + 3] = _split3(-c[:, None] * pos[None, :])
    q_rows[:, :, base + 3:base + 6] = 1.0
    k_rows[:, :, base:base + 3] = 1.0
    k_rows[:, :, base + 3:base + 6] = _split3(c[:, None] * pos[None, :])
    i_idx = np.arange(nb)[:, None]
    j_idx = np.arange(nb)[None, :]
    steps = np.zeros((MOBA_HEADS, nb, nb, LANES_PER_BLOCK), np.float32)
    steps[..., 0] = np.where(j_idx == i_idx, 0.0, -MASK_BIG)
    gap = _split3(-c[:, None, None] * (MOBA_BLOCK * (i_idx - j_idx)).astype(np.float32)[None])
    steps[..., 1:] = np.where((j_idx <= i_idx)[None, :, :, None], gap, 0.0)
    q_steps = np.zeros((MOBA_HEADS, nb, 1, LANES), np.float32)
    q_steps[:, :, 0, :base] = steps.reshape(MOBA_HEADS, nb, base)
    k_steps = np.zeros((nb, 1, LANES), np.float32)
    k_steps[:, 0, :base] = np.repeat(np.eye(nb, dtype=np.float32), LANES_PER_BLOCK, axis=1)
    expand = np.zeros((nb, LANES), np.float32)
    expand[np.arange(nb), LANES_PER_BLOCK * np.arange(nb)] = MASK_BIG
    return (jnp.asarray(q_rows), jnp.asarray(q_steps), jnp.asarray(k_rows), jnp.asarray(k_steps),
            jnp.asarray(expand, BF16))


def _moba_prep_kernel(x_ref, g_ref, w_ref, qg_ref, kg_ref, qrow_ref, qstep_ref, krow_ref, kstep_ref, exp_ref,
                      qa_ref, ka_ref, vt_ref, kmean_ref):
    i = pl.program_id(1)
    nb = kmean_ref.shape[1]
    d = MOBA_HEADS * MOBA_HD

    @pl.when(i == 0)
    def _():
        kmean_ref[...] = jnp.zeros_like(kmean_ref)

    xn = _rms(x_ref[...], g_ref[...]).astype(BF16)
    blk = lax.broadcasted_iota(jnp.int32, (nb, MOBA_BLOCK), 0)
    past = blk < i
    ones_rows = (lax.broadcasted_iota(jnp.int32, (VT_ROWS - MOBA_HD, MOBA_BLOCK), 0) == 0).astype(F32)
    k_step = kstep_ref[i]

    for h in range(MOBA_HEADS):
        hg = h % PROJ_HEADS
        if hg == 0:
            c0 = h * MOBA_HD
            q_grp, k_grp, v_grp = (jnp.dot(xn, w_ref[:, off + c0:off + c0 + PROJ_HEADS * MOBA_HD],
                                           preferred_element_type=F32) for off in (0, d, 2 * d))
        part = slice(hg * MOBA_HD, (hg + 1) * MOBA_HD)
        q = _rms(q_grp[:, part], qg_ref[...]) * (MOBA_HD ** -0.5)
        k = _rms(k_grp[:, part], kg_ref[...])
        v = v_grp[:, part]
        vt_ref[0, h] = jnp.concatenate([v.T, ones_rows], axis=0).astype(vt_ref.dtype)

        gate = _dot_nt(kmean_ref[h], q)
        gate = jnp.where(past, gate, -jnp.inf)
        rank = jnp.zeros((nb, MOBA_BLOCK), F32)
        for j in range(nb):
            other = gate[j:j + 1]
            ahead = (other > gate) | ((other == gate) & (j < blk))
            rank = rank + ahead.astype(F32)
        sel = (past & (rank < MOBA_TOPK)).astype(BF16)
        q_extra = (qrow_ref[h] + qstep_ref[h, i]) + _dot_tn(sel, exp_ref[...])
        qa_ref[:, h * AUG:h * AUG + MOBA_HD] = (q * LOG2E).astype(qa_ref.dtype)
        qa_ref[:, h * AUG + MOBA_HD:(h + 1) * AUG] = q_extra.astype(qa_ref.dtype)
        ka_ref[:, h * AUG:h * AUG + MOBA_HD] = k.astype(ka_ref.dtype)
        ka_ref[:, h * AUG + MOBA_HD:(h + 1) * AUG] = (krow_ref[h] + k_step).astype(ka_ref.dtype)

        kmean_ref[h, pl.ds(i, 1), :] = jnp.mean(k, axis=0, keepdims=True)


def moba_prep(x, gain, w_qkv, q_gain, k_gain, batch):
    m, dm = x.shape
    seq = m // batch
    nb = seq // MOBA_BLOCK
    assert LANES_PER_BLOCK * nb + 6 <= LANES
    d = MOBA_HEADS * MOBA_HD
    q_rows, q_steps, k_rows, k_steps, expand = _aug_tables(nb)
    return pl.pallas_call(
        _moba_prep_kernel,
        grid=(batch, nb),
        in_specs=[pl.BlockSpec((MOBA_BLOCK, dm), lambda b, i: (b * nb + i, 0)),
                  _const_spec((1, dm)),
                  _const_spec((dm, 3 * d)),
                  _const_spec((1, MOBA_HD)),
                  _const_spec((1, MOBA_HD)),
                  _const_spec((MOBA_HEADS, MOBA_BLOCK, LANES)),
                  _const_spec((MOBA_HEADS, nb, 1, LANES)),
                  _const_spec((MOBA_HEADS, MOBA_BLOCK, LANES)),
                  _const_spec((nb, 1, LANES)),
                  _const_spec((nb, LANES))],
        out_specs=[pl.BlockSpec((MOBA_BLOCK, MOBA_HEADS * AUG), lambda b, i: (b * nb + i, 0)),
                   pl.BlockSpec((MOBA_BLOCK, MOBA_HEADS * AUG), lambda b, i: (b * nb + i, 0)),
                   pl.BlockSpec((1, MOBA_HEADS, VT_ROWS, MOBA_BLOCK), lambda b, i: (b, 0, 0, i))],
        out_shape=[jax.ShapeDtypeStruct((m, MOBA_HEADS * AUG), BF16),
                   jax.ShapeDtypeStruct((m, MOBA_HEADS * AUG), BF16),
                   jax.ShapeDtypeStruct((batch, MOBA_HEADS, VT_ROWS, seq), BF16)],
        scratch_shapes=[pltpu.VMEM((MOBA_HEADS, nb, MOBA_HD), F32)],
        compiler_params=_params(2),
        name="moba_prep",
    )(x, gain.reshape(1, dm), w_qkv, q_gain.reshape(1, -1), k_gain.reshape(1, -1),
      q_rows, q_steps, k_rows, k_steps, expand)


ATTN_HEADS_PER_STEP = 2
ATTN_CHUNK_BLOCKS = 4
PV_KEYS = 256


def _moba_attn_kernel(q_ref, k_ref, vt_ref, o_ref, s_even, s_odd, m_even, m_odd, *, heads, chunk_blocks, nb):
    t = pl.program_id(2)
    i = jnp.minimum(t, nb - 1)
    chunk_keys = chunk_blocks * MOBA_BLOCK
    n_chunks = i // chunk_blocks + 1
    n_valid2 = jnp.where(t > 0, (t - 1) // chunk_blocks + 1, 0)

    @pl.when((pl.program_id(0) == 0) & (pl.program_id(1) == 0) & (t == 0))
    def _():
        s_even[...] = jnp.zeros_like(s_even)
        s_odd[...] = jnp.zeros_like(s_odd)
        m_odd[...] = jnp.zeros_like(m_odd)

    key_r = lax.broadcasted_iota(jnp.int32, (MOBA_BLOCK, MOBA_BLOCK), 0)
    qry_c = lax.broadcasted_iota(jnp.int32, (MOBA_BLOCK, MOBA_BLOCK), 1)
    causal = qry_c >= key_r
    q_aug = [q_ref[:, hh * AUG:(hh + 1) * AUG] for hh in range(heads)]

    def both(c, carry, last, s_w, s_r, m_row2):
        m_part, acc = carry
        k0 = pl.multiple_of(c * chunk_keys, chunk_keys)
        live2 = jnp.full((1, MOBA_BLOCK), c, jnp.int32) < n_valid2
        new_m, new_acc = [], []
        for hh in range(heads):
            s = lax.dot_general(k_ref[pl.ds(k0, chunk_keys), hh * AUG:(hh + 1) * AUG], q_aug[hh],
                                (((1,), (1,)), ((), ())), preferred_element_type=F32)
            if last:
                parts = []
                for u in range(chunk_blocks):
                    other_block = jnp.full((1, MOBA_BLOCK), c * chunk_blocks + u, jnp.int32) != i
                    parts.append(jnp.where(causal | other_block, s[u * MOBA_BLOCK:(u + 1) * MOBA_BLOCK], NEG))
                s = jnp.concatenate(parts, axis=0)
            s_w[hh, pl.ds(k0, chunk_keys), :] = s
            new_m.append(jnp.maximum(m_part[hh],
                                     jnp.max(s.reshape(chunk_keys // SUBLANES, SUBLANES, MOBA_BLOCK), axis=0)))
            a = acc[hh]
            for u in range(0, chunk_keys, PV_KEYS):
                ku = pl.multiple_of(k0 + u, PV_KEYS)
                p = jnp.exp2(s_r[hh, pl.ds(ku, PV_KEYS), :] - m_row2[hh])
                p = jnp.where(live2, p, 0.0).astype(BF16)
                a = a + jnp.dot(vt_ref[0, hh, :, pl.ds(ku, PV_KEYS)], p, preferred_element_type=F32)
            new_acc.append(a)
        return tuple(new_m), tuple(new_acc)

    def step(s_w, s_r, m_w, m_r):
        m_row2 = [m_r[hh] for hh in range(heads)]
        carry = (tuple(jnp.full((SUBLANES, MOBA_BLOCK), NEG, F32) for _ in range(heads)),
                 tuple(jnp.zeros((VT_ROWS, MOBA_BLOCK), F32) for _ in range(heads)))
        carry = lax.fori_loop(0, n_chunks - 1, lambda c, cr: both(c, cr, False, s_w, s_r, m_row2), carry)
        m_part, acc = both(n_chunks - 1, carry, True, s_w, s_r, m_row2)
        for hh in range(heads):
            m_w[hh] = jnp.max(m_part[hh], axis=0, keepdims=True)
            o = acc[hh][:MOBA_HD] * (1.0 / acc[hh][MOBA_HD:MOBA_HD + 1])
            o_ref[:, hh * MOBA_HD:(hh + 1) * MOBA_HD] = o.T.astype(o_ref.dtype)

    @pl.when(t % 2 == 0)
    def _():
        step(s_even, s_odd, m_even, m_odd)

    @pl.when(t % 2 == 1)
    def _():
        step(s_odd, s_even, m_odd, m_even)


def moba_attention(q_aug, k_aug, vt, batch):
    m = q_aug.shape[0]
    seq = m // batch
    nb = seq // MOBA_BLOCK
    hps = ATTN_HEADS_PER_STEP
    assert nb % ATTN_CHUNK_BLOCKS == 0
    return pl.pallas_call(
        functools.partial(_moba_attn_kernel, heads=hps, chunk_blocks=ATTN_CHUNK_BLOCKS, nb=nb),
        grid=(batch, MOBA_HEADS // hps, nb + 1),
        in_specs=[pl.BlockSpec((MOBA_BLOCK, hps * AUG), lambda b, g, t: (b * nb + jnp.minimum(t, nb - 1), g)),
                  pl.BlockSpec((seq, hps * AUG), lambda b, g, t: (b, g)),
                  pl.BlockSpec((1, hps, VT_ROWS, seq), lambda b, g, t: (b, g, 0, 0))],
        out_specs=pl.BlockSpec((MOBA_BLOCK, hps * MOBA_HD), lambda b, g, t: (b * nb + jnp.maximum(t - 1, 0), g)),
        out_shape=jax.ShapeDtypeStruct((m, MOBA_HEADS * MOBA_HD), BF16),
        scratch_shapes=[pltpu.VMEM((hps, seq, MOBA_BLOCK), F32), pltpu.VMEM((hps, seq, MOBA_BLOCK), F32),
                        pltpu.VMEM((hps, 1, MOBA_BLOCK), F32), pltpu.VMEM((hps, 1, MOBA_BLOCK), F32)],
        compiler_params=_params(3),
        name="moba_attn",
    )(q_aug, k_aug, vt)


def moba_layer(x, gain, w_qkv, q_gain, k_gain, batch):
    q_aug, k_aug, vt = moba_prep(x, gain, w_qkv.astype(BF16), q_gain, k_gain, batch)
    return moba_attention(q_aug, k_aug, vt, batch)


def kernel(x, norm_mix, norm_ffn, hyb_w_in, gla_w_gate2, gla_b_gate, gla_norm, ml_conv, ml_b_if, ml_norm,
           hyb_w_out, moba_w_qkv, moba_q_norm, moba_k_norm, moba_w_o, ffn_w_gate, ffn_w_up, ffn_w_down):
    batch, seq, d = x.shape
    depth = norm_mix.shape[0]
    xf = x.reshape(batch * seq, d)
    for l in range(depth):
        j = l // 2
        if l % 2 == 0:
            y = hybrid_layer(xf, norm_mix[l], hyb_w_in[j], gla_w_gate2[j], gla_b_gate[j], gla_norm[j],
                             ml_conv[j], ml_b_if[j], ml_norm[j], batch)
            w_o = hyb_w_out[j]
        else:
            y = moba_layer(xf, norm_mix[l], moba_w_qkv[j], moba_q_norm[j], moba_k_norm[j], batch)
            w_o = moba_w_o[j]
        xf = out_ffn(xf, y, w_o.astype(BF16), norm_ffn[l], ffn_w_gate[l].astype(BF16),
                     ffn_w_up[l].astype(BF16), ffn_w_down[l].astype(BF16))
    return xf.reshape(batch, seq, d)
```

```python
import functools

import jax
import jax.numpy as jnp
import numpy as np
from jax import lax
from jax.experimental import pallas as pl
from jax.experimental.pallas import tpu as pltpu

F32 = jnp.float32
BF16 = jnp.bfloat16

LANES = 128
SUBLANES = 8
VMEM_LIMIT_BYTES = 56 * 1024 * 1024

EPS = 1e-6
NEG = -1e30

GLA_HEADS = 4
GLA_DK = 64
GLA_DV = 128
GLA_RANK = 16
GLA_TAU = 16.0
ML_HEADS = 4
ML_DK = 64
ML_DV = 128
ML_CONV = 4
CHUNK = 64
SUB = 16
MOBA_HEADS = 8
MOBA_HD = 128
MOBA_BLOCK = 256
MOBA_TOPK = 3

GLA_QK = GLA_HEADS * GLA_DK
GLA_V = GLA_HEADS * GLA_DV
ML_QK = ML_HEADS * ML_DK
ML_V = ML_HEADS * ML_DV

C_GQ = 0
C_GK = C_GQ + GLA_QK
C_GV = C_GK + GLA_QK
C_GR = C_GV + GLA_V
C_MQ = C_GR + GLA_V
C_MK = C_MQ + ML_QK
C_MV = C_MK + ML_QK
C_MO = C_MV + ML_V
C_SMALL = C_MO + ML_V
HYB_COLS = C_SMALL + LANES
S_GLR = 0
S_MI = GLA_RANK
S_MF = GLA_RANK + ML_HEADS

ROW_TILE = 512
MIX_TILE = 256
FF_TILE = 256
PROJ_COL_TILE = 512


def _rms(x, gain):
    return x * lax.rsqrt(jnp.mean(x * x, axis=-1, keepdims=True) + EPS) * gain


def _log_sigmoid(z):
    return jnp.minimum(z, 0.0) - jnp.log1p(jnp.exp(-jnp.abs(z)))


def _sigmoid(z):
    return 1.0 / (1.0 + jnp.exp(-z))


def _dot(a, b):
    return jnp.dot(a.astype(BF16), b.astype(BF16), preferred_element_type=F32)


def _dot_nt(a, b):
    return lax.dot_general(a.astype(BF16), b.astype(BF16), (((1,), (1,)), ((), ())),
                           preferred_element_type=F32)


def _dot_tn(a, b):
    return lax.dot_general(a.astype(BF16), b.astype(BF16), (((0,), (0,)), ((), ())),
                           preferred_element_type=F32)


def _const_spec(shape):
    return pl.BlockSpec(shape, lambda *_: (0,) * len(shape), pipeline_mode=pl.Buffered(1))


def _params(n_grid):
    return pltpu.CompilerParams(dimension_semantics=("arbitrary",) * n_grid,
                                vmem_limit_bytes=VMEM_LIMIT_BYTES)


def _norm_proj_kernel(x_ref, g_ref, w_ref, main_ref, tail_ref, *, col_tile):
    xn = _rms(x_ref[...], g_ref[...]).astype(BF16)
    n_main = main_ref.shape[1]
    for c0 in range(0, n_main, col_tile):
        main_ref[:, c0:c0 + col_tile] = jnp.dot(xn, w_ref[:, c0:c0 + col_tile],
                                                 preferred_element_type=F32).astype(main_ref.dtype)
    tail_ref[...] = jnp.dot(xn, w_ref[:, n_main:], preferred_element_type=F32)


def norm_proj(x, gain, w, n_tail):
    m, d = x.shape
    n = w.shape[1]
    n_main = n - n_tail
    tm = min(ROW_TILE, m)
    assert n_main % PROJ_COL_TILE == 0
    return pl.pallas_call(
        functools.partial(_norm_proj_kernel, col_tile=PROJ_COL_TILE),
        grid=(m // tm,),
        in_specs=[pl.BlockSpec((tm, d), lambda i: (i, 0)),
                  _const_spec((1, d)),
                  _const_spec((d, n))],
        out_specs=[pl.BlockSpec((tm, n_main), lambda i: (i, 0)),
                   pl.BlockSpec((tm, n_tail), lambda i: (i, 0))],
        out_shape=[jax.ShapeDtypeStruct((m, n_main), BF16),
                   jax.ShapeDtypeStruct((m, n_tail), F32)],
        compiler_params=_params(1),
        name="norm_proj",
    )(x, gain.reshape(1, d), w)


def _out_ffn_kernel(x_ref, y_ref, wo_ref, g_ref, wg_ref, wu_ref, wd_ref, o_ref, h_ref, *, ff_tile):
    x1 = x_ref[...] + jnp.dot(y_ref[...], wo_ref[...], preferred_element_type=F32)
    xn = _rms(x1, g_ref[...]).astype(BF16)
    d_ff = wg_ref.shape[1]
    for c0 in range(0, d_ff, ff_tile):
        g = jnp.dot(xn, wg_ref[:, c0:c0 + ff_tile], preferred_element_type=F32)
        u = jnp.dot(xn, wu_ref[:, c0:c0 + ff_tile], preferred_element_type=F32)
        h_ref[:, c0:c0 + ff_tile] = (g * _sigmoid(g) * u).astype(BF16)
    o_ref[...] = x1 + jnp.dot(h_ref[...], wd_ref[...], preferred_element_type=F32)


def out_ffn(x, y, w_o, gain, w_gate, w_up, w_down):
    m, d = x.shape
    dy = y.shape[1]
    d_ff = w_gate.shape[1]
    tm = min(ROW_TILE, m)
    return pl.pallas_call(
        functools.partial(_out_ffn_kernel, ff_tile=FF_TILE),
        grid=(m // tm,),
        in_specs=[pl.BlockSpec((tm, d), lambda i: (i, 0)),
                  pl.BlockSpec((tm, dy), lambda i: (i, 0)),
                  _const_spec((dy, d)),
                  _const_spec((1, d)),
                  _const_spec((d, d_ff)),
                  _const_spec((d, d_ff)),
                  _const_spec((d_ff, d))],
        out_specs=pl.BlockSpec((tm, d), lambda i: (i, 0)),
        out_shape=jax.ShapeDtypeStruct((m, d), F32),
        scratch_shapes=[pltpu.VMEM((tm, d_ff), BF16)],
        compiler_params=_params(1),
        name="out_ffn",
    )(x, y, w_o, gain.reshape(1, d), w_gate, w_up, w_down)


def _split3_rows(x):
    hi = x.astype(BF16)
    r1 = x - hi.astype(F32)
    mid = r1.astype(BF16)
    lo = (r1 - mid.astype(F32)).astype(BF16)
    return jnp.concatenate([hi, mid, lo], axis=0)


def _split3_lanes(x):
    hi = x.astype(BF16)
    r1 = x - hi.astype(F32)
    mid = r1.astype(BF16)
    lo = (r1 - mid.astype(F32)).astype(BF16)
    return jnp.concatenate([hi, mid, lo], axis=1)


def _blocks_diag(tiles, zero):
    n = len(tiles)
    return jnp.concatenate(
        [jnp.concatenate([tiles[a] if a == b else zero for b in range(n)], axis=1) for a in range(n)], axis=0)


def _hybrid_kernel(proj_ref, small_ref, gates_ref, wg2_ref, bg_ref, gn_ref, conv_ref, bsmall_ref, brow_ref, mn_ref,
                   y_ref, gla_st, ml_cs, ml_ns, ml_m, conv_carry):
    t_tile = proj_ref.shape[0]
    nc = t_tile // CHUNK
    n_sub = CHUNK // SUB

    @pl.when(pl.program_id(1) == 0)
    def _():
        gla_st[...] = jnp.zeros_like(gla_st)
        ml_cs[...] = jnp.zeros_like(ml_cs)
        ml_ns[...] = jnp.zeros_like(ml_ns)
        ml_m[...] = jnp.zeros_like(ml_m)
        conv_carry[...] = jnp.zeros_like(conv_carry)

    xcat = proj_ref[:, C_MQ:C_MQ + 2 * ML_QK].astype(F32)
    ext = jnp.concatenate([conv_carry[...], xcat], axis=0)
    cw = conv_ref[...]
    acc = ext * cw[ML_CONV - 1:ML_CONV]
    for j in range(1, ML_CONV):
        acc = acc + pltpu.roll(ext, j, axis=0) * cw[ML_CONV - 1 - j:ML_CONV - j]
    conv = acc[SUBLANES:]
    qk = conv * _sigmoid(conv)
    conv_carry[...] = xcat[t_tile - SUBLANES:]

    lane = lax.broadcasted_iota(jnp.int32, (1, LANES), 1)
    head_mask = (lane < GLA_DK, lane >= GLA_DK)
    hm0 = head_mask[0]
    r_t = lax.broadcasted_iota(jnp.int32, (t_tile, t_tile), 0)
    c_t = lax.broadcasted_iota(jnp.int32, (t_tile, t_tile), 1)
    tril_bd = ((r_t >= c_t) & (r_t // CHUNK == c_t // CHUNK)).astype(BF16)
    tril3 = jnp.concatenate([tril_bd, tril_bd, tril_bd], axis=1)
    r_l = lax.broadcasted_iota(jnp.int32, (LANES, LANES), 0)
    c_l = lax.broadcasted_iota(jnp.int32, (LANES, LANES), 1)
    triu_bd = ((r_l <= c_l) & (r_l // CHUNK == c_l // CHUNK)).astype(BF16)
    triu3 = jnp.concatenate([triu_bd, triu_bd, triu_bd], axis=0)
    row_c = lax.broadcasted_iota(jnp.int32, (CHUNK, LANES), 0)
    causal2 = (lax.broadcasted_iota(jnp.int32, (CHUNK, LANES), 1) % CHUNK) <= row_c
    sub_id = (lax.broadcasted_iota(jnp.int32, (t_tile, LANES), 0) % CHUNK) // SUB
    row_sub = lax.broadcasted_iota(jnp.int32, (2 * SUBLANES, LANES), 0)
    zero_tile = jnp.zeros((CHUNK, LANES), F32)
    zero_v = jnp.zeros((CHUNK, LANES), BF16)

    def chunk_rows(x, c):
        return x[c * CHUNK:(c + 1) * CHUNK]

    def diag_blocks(r):
        return [r[c * CHUNK:(c + 1) * CHUNK, c * LANES:(c + 1) * LANES] for c in range(nc)]

    small = small_ref[...]
    smallb = small + bsmall_ref[...]

    z = _dot(small, wg2_ref[...]) + bg_ref[...]
    log_a = _log_sigmoid(z) * (1.0 / GLA_TAU)
    cum = jnp.dot(tril3, _split3_rows(jnp.concatenate([log_a, _log_sigmoid(smallb)], axis=1)),
                  preferred_element_type=F32)
    g_all = gates_ref[0, 0] + brow_ref[...]
    bc_rows = jnp.dot(_split3_lanes(_log_sigmoid(g_all)), triu3, preferred_element_type=F32)

    for p in range(GLA_HEADS // 2):
        q2 = proj_ref[:, C_GQ + p * LANES:C_GQ + (p + 1) * LANES].astype(F32) * (GLA_DK ** -0.5)
        k2 = proj_ref[:, C_GK + p * LANES:C_GK + (p + 1) * LANES].astype(F32)
        bc = cum[:, p * LANES:(p + 1) * LANES]
        v = [proj_ref[:, C_GV + (2 * p + hl) * GLA_DV:C_GV + (2 * p + hl + 1) * GLA_DV] for hl in range(2)]

        ref_pt = [[chunk_rows(bc, c)[i * SUB - 1:i * SUB] if i else jnp.zeros((1, LANES), F32)
                   for i in range(n_sub)] for c in range(nc)]
        r_own = jnp.concatenate([jnp.broadcast_to(ref_pt[c][i], (SUB, LANES))
                                 for c in range(nc) for i in range(n_sub)], axis=0)
        qs = q2 * jnp.exp(bc - r_own)
        q_hat = jnp.concatenate([jnp.where(sub_id == i, qs, 0.0) for i in range(n_sub)], axis=1)
        k_hat, k_dec, last = [], [], []
        for c in range(nc):
            kc, bcc = chunk_rows(k2, c), chunk_rows(bc, c)
            ke = [kc * jnp.exp(ref_pt[c][i] - bcc) for i in range(n_sub)]
            for hl in range(2):
                k_hat.append(jnp.concatenate([jnp.where(head_mask[hl], ke[i], 0.0) for i in range(n_sub)], axis=1))
            last.append(bcc[CHUNK - 1:CHUNK])
            k_dec.append(kc * jnp.exp(last[c] - bcc))
        k_hat = jnp.concatenate(k_hat, axis=0)
        sc = diag_blocks(_dot_nt(q_hat, k_hat))
        sc = [jnp.where(causal2, s, 0.0) for s in sc]
        v_bd = jnp.concatenate([jnp.concatenate([chunk_rows(v[hl], c) if hl == b else zero_v for b in range(2)],
                                                axis=1) for c in range(nc) for hl in range(2)], axis=0)
        o_intra = _dot(_blocks_diag(sc, zero_tile), v_bd)
        u_all = _dot_tn(jnp.concatenate(v, axis=1), _blocks_diag(k_dec, zero_tile))

        st = gla_st[p]
        before = []
        for c in range(nc):
            before.append(st)
            u = jnp.where(hm0, u_all[:GLA_DV, c * LANES:(c + 1) * LANES], u_all[GLA_DV:, c * LANES:(c + 1) * LANES])
            st = st * jnp.exp(last[c]) + u
        gla_st[p] = st
        st_stack = jnp.concatenate(before, axis=0)
        q_in = q2 * jnp.exp(bc)
        for hl in range(2):
            h = 2 * p + hl
            o_inter = jnp.concatenate(diag_blocks(_dot_nt(jnp.where(head_mask[hl], q_in, 0.0), st_stack)), axis=0)
            o = o_intra[:, hl * GLA_DV:(hl + 1) * GLA_DV] + o_inter
            gate = proj_ref[:, C_GR + h * GLA_DV:C_GR + (h + 1) * GLA_DV].astype(F32)
            yg = _rms(o, gn_ref[:, h * GLA_DV:(h + 1) * GLA_DV]) * (gate * _sigmoid(gate))
            y_ref[:, h * GLA_DV:(h + 1) * GLA_DV] = yg.astype(y_ref.dtype)

    ones_tile = jnp.ones((t_tile, LANES), BF16)
    for p in range(ML_HEADS // 2):
        mq2 = qk[:, p * LANES:(p + 1) * LANES]
        mk2 = qk[:, ML_QK + p * LANES:ML_QK + (p + 1) * LANES] * (ML_DK ** -0.5)
        v = [proj_ref[:, C_MV + (2 * p + hl) * ML_DV:C_MV + (2 * p + hl + 1) * ML_DV] for hl in range(2)]
        k_stack = jnp.concatenate([jnp.where(head_mask[hl], chunk_rows(mk2, c), 0.0)
                                   for c in range(nc) for hl in range(2)], axis=0)
        qk_blocks = diag_blocks(_dot_nt(mq2, k_stack))

        w_blocks, wk_blocks, cm_cols, b_max, bc_last = [], [], [], [], []
        for c in range(nc):
            bcr = bc_rows[ML_HEADS * c + 2 + p:ML_HEADS * c + 3 + p]
            b_pair = g_all[ML_HEADS * c + p:ML_HEADS * c + p + 1] - bcr
            bm = jnp.where(causal2, b_pair, -jnp.inf)
            cm = [jnp.max(jnp.where(head_mask[hl], bm, -jnp.inf), axis=1, keepdims=True) for hl in range(2)]
            w_blocks.append(qk_blocks[c] * jnp.exp(bm - jnp.where(hm0, cm[0], cm[1])))
            cm_cols.append(cm)
            bmx = [jnp.max(jnp.where(head_mask[hl], b_pair, -jnp.inf), axis=1, keepdims=True) for hl in range(2)]
            b_max.append(jnp.where(hm0, bmx[0], bmx[1]))
            bc_last.append(jnp.where(hm0, bcr[:, CHUNK - 1:CHUNK], bcr[:, LANES - 1:LANES]))
            e_col = []
            for hl in range(2):
                h = 2 * p + hl
                b_col = (chunk_rows(smallb, c)[:, S_MI + h:S_MI + h + 1]
                         - chunk_rows(cum, c)[:, GLA_QK + S_MF + h:GLA_QK + S_MF + h + 1])
                e_col.append(jnp.exp(b_col - bmx[hl]))
            wk_blocks.append(chunk_rows(mk2, c) * jnp.where(hm0, e_col[0], e_col[1]))

        ones_col = [jnp.broadcast_to((lane == hl).astype(BF16), (CHUNK, LANES)) for hl in range(2)]
        v_bd = jnp.concatenate([jnp.concatenate([chunk_rows(v[hl], c) if hl == b else zero_v for b in range(2)]
                                                + [ones_col[hl]], axis=1)
                                for c in range(nc) for hl in range(2)], axis=0)
        nv = _dot(_blocks_diag(w_blocks, zero_tile), v_bd)
        u_all = _dot_tn(jnp.concatenate(v + [ones_tile], axis=1), _blocks_diag(wk_blocks, zero_tile))

        cs, ns, m_run = ml_cs[p], ml_ns[p], ml_m[p]
        before = []
        for c in range(nc):
            before.append((cs, ns, m_run))
            m_top = jnp.maximum(m_run, b_max[c])
            keep = jnp.exp(m_run - m_top)
            gain = jnp.exp(b_max[c] - m_top)
            blk = slice(c * LANES, (c + 1) * LANES)
            cs = keep * cs + gain * jnp.where(hm0, u_all[:ML_DV, blk], u_all[ML_DV:2 * ML_DV, blk])
            ns = keep * ns + gain * u_all[2 * ML_DV:2 * ML_DV + 1, blk]
            m_run = bc_last[c] + m_top
        ml_cs[p], ml_ns[p], ml_m[p] = cs, ns, m_run
        ns_rows = jnp.zeros((2 * SUBLANES, LANES), F32)
        for c in range(nc):
            ns_rows = jnp.where(row_sub == c, before[c][1], ns_rows)
        state_stack = jnp.concatenate([b[0] for b in before] + [ns_rows], axis=0)

        for hl in range(2):
            h = 2 * p + hl
            rr = _dot_nt(jnp.where(head_mask[hl], mq2, 0.0), state_stack)
            q_cs = diag_blocks(rr)
            h_chunks = []
            for c in range(nc):
                rows = slice(c * CHUNK, (c + 1) * CHUNK)
                q_ns = rr[rows, nc * ML_DV + c:nc * ML_DV + c + 1]
                m_prev = before[c][2][:, hl * ML_DK:hl * ML_DK + 1]
                cm = cm_cols[c][hl]
                m_top = jnp.maximum(m_prev, cm)
                local = jnp.exp(cm - m_top)
                carried = jnp.exp(m_prev - m_top)
                num = local * nv[rows, hl * ML_DV:(hl + 1) * ML_DV] + carried * q_cs[c]
                den = local * nv[rows, 2 * ML_DV + hl:2 * ML_DV + hl + 1] + carried * q_ns
                bc_col = chunk_rows(cum, c)[:, GLA_QK + S_MF + h:GLA_QK + S_MF + h + 1]
                h_chunks.append(num / jnp.maximum(jnp.abs(den), jnp.exp(-(bc_col + m_top))))
            og = proj_ref[:, C_MO + h * ML_DV:C_MO + (h + 1) * ML_DV].astype(F32)
            ym = _rms(jnp.concatenate(h_chunks, axis=0) * _sigmoid(og), mn_ref[:, h * ML_DV:(h + 1) * ML_DV])
            y_ref[:, GLA_V + h * ML_DV:GLA_V + (h + 1) * ML_DV] = ym.astype(y_ref.dtype)


def hybrid_core(proj, small, gates_t, w_gate2, b_gate, gla_norm, conv_w, b_small, b_rows, ml_norm, batch):
    m = proj.shape[0]
    seq = m // batch
    t = min(MIX_TILE, seq)
    nt = seq // t
    assert 2 * SUBLANES >= t // CHUNK
    return pl.pallas_call(
        _hybrid_kernel,
        grid=(batch, nt),
        in_specs=[pl.BlockSpec((t, C_SMALL), lambda b, i: (b * nt + i, 0)),
                  pl.BlockSpec((t, LANES), lambda b, i: (b * nt + i, 0)),
                  pl.BlockSpec((1, 1, ML_HEADS * (t // CHUNK), LANES), lambda b, i: (b, i, 0, 0)),
                  _const_spec((LANES, GLA_QK)),
                  _const_spec((1, GLA_QK)),
                  _const_spec((1, GLA_V)),
                  _const_spec((ML_CONV, 2 * ML_QK)),
                  _const_spec((1, LANES)),
                  _const_spec((ML_HEADS * (t // CHUNK), LANES)),
                  _const_spec((1, ML_V))],
        out_specs=pl.BlockSpec((t, GLA_V + ML_V), lambda b, i: (b * nt + i, 0)),
        out_shape=jax.ShapeDtypeStruct((m, GLA_V + ML_V), BF16),
        scratch_shapes=[pltpu.VMEM((GLA_HEADS // 2, GLA_DV, LANES), F32),
                        pltpu.VMEM((ML_HEADS // 2, ML_DV, LANES), F32),
                        pltpu.VMEM((ML_HEADS // 2, 1, LANES), F32),
                        pltpu.VMEM((ML_HEADS // 2, 1, LANES), F32),
                        pltpu.VMEM((SUBLANES, 2 * ML_QK), F32)],
        compiler_params=_params(2),
        name="hybrid_core",
    )(proj, small, gates_t, w_gate2, b_gate, gla_norm, conv_w, b_small, b_rows, ml_norm)


def hybrid_layer(x, gain, w_in, w_gate2, b_gate, gla_norm, conv_w, b_if, ml_norm, batch):
    m, d = x.shape
    seq = m // batch
    segs = np.cumsum([0, GLA_QK, GLA_QK, GLA_V, GLA_V, GLA_RANK, ML_QK, ML_QK, ML_V, ML_V, ML_HEADS, ML_HEADS])
    seg = lambda i: w_in[:, int(segs[i]):int(segs[i + 1])]
    pad = jnp.zeros((d, LANES - GLA_RANK - 2 * ML_HEADS), w_in.dtype)
    w_perm = jnp.concatenate([seg(0), seg(1), seg(2), seg(3), seg(5), seg(6), seg(7), seg(8),
                              seg(4), seg(9), seg(10), pad], axis=1).astype(BF16)
    proj, small = norm_proj(x, gain, w_perm, LANES)
    gates = small[:, S_MI:S_MI + 2 * ML_HEADS]
    gates_t = gates.reshape(batch, seq // CHUNK, CHUNK, 2 * ML_HEADS).transpose(0, 1, 3, 2)
    t = min(MIX_TILE, seq)
    gates_t = gates_t.reshape(batch, seq // t, ML_HEADS * (t // CHUNK), LANES)
    b_small = jnp.zeros((1, LANES), F32).at[0, S_MI:S_MI + 2 * ML_HEADS].set(b_if)
    b_rows = jnp.tile(jnp.repeat(b_if, CHUNK).reshape(ML_HEADS, LANES), (t // CHUNK, 1))
    w_gate2_rows = jnp.zeros((LANES, GLA_QK), F32).at[S_GLR:S_GLR + GLA_RANK].set(w_gate2)
    return hybrid_core(proj, small, gates_t, w_gate2_rows, b_gate.reshape(1, -1), gla_norm.reshape(1, -1), conv_w,
                       b_small, b_rows, ml_norm.reshape(1, -1), batch)


LOG2E = 1.4426950408889634
AUG = 2 * MOBA_HD
LANES_PER_BLOCK = 4
MASK_BIG = 2.0 ** 100
PROJ_HEADS = 4
VT_ROWS = MOBA_HD + 16


def _split3(x):
    x = np.asarray(x, np.float32)
    hi = x.astype(BF16).astype(np.float32)
    mid = (x - hi).astype(BF16).astype(np.float32)
    lo = (x - hi - mid).astype(BF16).astype(np.float32)
    return np.stack([hi, mid, lo], axis=-1)


def _aug_tables(nb):
    c = np.asarray([LOG2E * 2.0 ** (-8.0 * (h + 1) / MOBA_HEADS) for h in range(MOBA_HEADS)], np.float32)
    pos = np.arange(MOBA_BLOCK, dtype=np.float32)
    base = LANES_PER_BLOCK * nb
    q_rows = np.zeros((MOBA_HEADS, MOBA_BLOCK, LANES), np.float32)
    k_rows = np.zeros((MOBA_HEADS, MOBA_BLOCK, LANES), np.float32)
    q_rows[:, :, base:base + 3] = _split3(-c[:, None] * pos[None, :])
    q_rows[:, :, base + 3:base + 6] = 1.0
    k_rows[:, :, base:base + 3] = 1.0
    k_rows[:, :, base + 3:base + 6] = _split3(c[:, None] * pos[None, :])
    i_idx = np.arange(nb)[:, None]
    j_idx = np.arange(nb)[None, :]
    steps = np.zeros((MOBA_HEADS, nb, nb, LANES_PER_BLOCK), np.float32)
    steps[..., 0] = np.where(j_idx == i_idx, 0.0, -MASK_BIG)
    gap = _split3(-c[:, None, None] * (MOBA_BLOCK * (i_idx - j_idx)).astype(np.float32)[None])
    steps[..., 1:] = np.where((j_idx <= i_idx)[None, :, :, None], gap, 0.0)
    q_steps = np.zeros((MOBA_HEADS, nb, 1, LANES), np.float32)
    q_steps[:, :, 0, :base] = steps.reshape(MOBA_HEADS, nb, base)
    k_steps = np.zeros((nb, 1, LANES), np.float32)
    k_steps[:, 0, :base] = np.repeat(np.eye(nb, dtype=np.float32), LANES_PER_BLOCK, axis=1)
    expand = np.zeros((nb, LANES), np.float32)
    expand[np.arange(nb), LANES_PER_BLOCK * np.arange(nb)] = MASK_BIG
    return (jnp.asarray(q_rows), jnp.asarray(q_steps), jnp.asarray(k_rows), jnp.asarray(k_steps),
            jnp.asarray(expand, BF16))


def _moba_prep_kernel(x_ref, g_ref, w_ref, qg_ref, kg_ref, qrow_ref, qstep_ref, krow_ref, kstep_ref, exp_ref,
                      qa_ref, ka_ref, vt_ref, kmean_ref):
    i = pl.program_id(1)
    nb = kmean_ref.shape[1]
    d = MOBA_HEADS * MOBA_HD

    @pl.when(i == 0)
    def _():
        kmean_ref[...] = jnp.zeros_like(kmean_ref)

    xn = _rms(x_ref[...], g_ref[...]).astype(BF16)
    blk = lax.broadcasted_iota(jnp.int32, (nb, MOBA_BLOCK), 0)
    past = blk < i
    ones_rows = (lax.broadcasted_iota(jnp.int32, (VT_ROWS - MOBA_HD, MOBA_BLOCK), 0) == 0).astype(F32)
    k_step = kstep_ref[i]

    for h in range(MOBA_HEADS):
        hg = h % PROJ_HEADS
        if hg == 0:
            c0 = h * MOBA_HD
            q_grp, k_grp, v_grp = (jnp.dot(xn, w_ref[:, off + c0:off + c0 + PROJ_HEADS * MOBA_HD],
                                           preferred_element_type=F32) for off in (0, d, 2 * d))
        part = slice(hg * MOBA_HD, (hg + 1) * MOBA_HD)
        q = _rms(q_grp[:, part], qg_ref[...]) * (MOBA_HD ** -0.5)
        k = _rms(k_grp[:, part], kg_ref[...])
        v = v_grp[:, part]
        vt_ref[0, h] = jnp.concatenate([v.T, ones_rows], axis=0).astype(vt_ref.dtype)

        gate = _dot_nt(kmean_ref[h], q)
        gate = jnp.where(past, gate, -jnp.inf)
        rank = jnp.zeros((nb, MOBA_BLOCK), F32)
        for j in range(nb):
            other = gate[j:j + 1]
            ahead = (other > gate) | ((other == gate) & (j < blk))
            rank = rank + ahead.astype(F32)
        sel = (past & (rank < MOBA_TOPK)).astype(BF16)
        q_extra = (qrow_ref[h] + qstep_ref[h, i]) + _dot_tn(sel, exp_ref[...])
        qa_ref[:, h * AUG:h * AUG + MOBA_HD] = (q * LOG2E).astype(qa_ref.dtype)
        qa_ref[:, h * AUG + MOBA_HD:(h + 1) * AUG] = q_extra.astype(qa_ref.dtype)
        ka_ref[:, h * AUG:h * AUG + MOBA_HD] = k.astype(ka_ref.dtype)
        ka_ref[:, h * AUG + MOBA_HD:(h + 1) * AUG] = (krow_ref[h] + k_step).astype(ka_ref.dtype)

        kmean_ref[h, pl.ds(i, 1), :] = jnp.mean(k, axis=0, keepdims=True)


def moba_prep(x, gain, w_qkv, q_gain, k_gain, batch):
    m, dm = x.shape
    seq = m // batch
    nb = seq // MOBA_BLOCK
    assert LANES_PER_BLOCK * nb + 6 <= LANES
    d = MOBA_HEADS * MOBA_HD
    q_rows, q_steps, k_rows, k_steps, expand = _aug_tables(nb)
    return pl.pallas_call(
        _moba_prep_kernel,
        grid=(batch, nb),
        in_specs=[pl.BlockSpec((MOBA_BLOCK, dm), lambda b, i: (b * nb + i, 0)),
                  _const_spec((1, dm)),
                  _const_spec((dm, 3 * d)),
                  _const_spec((1, MOBA_HD)),
                  _const_spec((1, MOBA_HD)),
                  _const_spec((MOBA_HEADS, MOBA_BLOCK, LANES)),
                  _const_spec((MOBA_HEADS, nb, 1, LANES)),
                  _const_spec((MOBA_HEADS, MOBA_BLOCK, LANES)),
                  _const_spec((nb, 1, LANES)),
                  _const_spec((nb, LANES))],
        out_specs=[pl.BlockSpec((MOBA_BLOCK, MOBA_HEADS * AUG), lambda b, i: (b * nb + i, 0)),
                   pl.BlockSpec((MOBA_BLOCK, MOBA_HEADS * AUG), lambda b, i: (b * nb + i, 0)),
                   pl.BlockSpec((1, MOBA_HEADS, VT_ROWS, MOBA_BLOCK), lambda b, i: (b, 0, 0, i))],
        out_shape=[jax.ShapeDtypeStruct((m, MOBA_HEADS * AUG), BF16),
                   jax.ShapeDtypeStruct((m, MOBA_HEADS * AUG), BF16),
                   jax.ShapeDtypeStruct((batch, MOBA_HEADS, VT_ROWS, seq), BF16)],
        scratch_shapes=[pltpu.VMEM((MOBA_HEADS, nb, MOBA_HD), F32)],
        compiler_params=_params(2),
        name="moba_prep",
    )(x, gain.reshape(1, dm), w_qkv, q_gain.reshape(1, -1), k_gain.reshape(1, -1),
      q_rows, q_steps, k_rows, k_steps, expand)


ATTN_HEADS_PER_STEP = 2
ATTN_CHUNK_BLOCKS = 4
PV_KEYS = 256


def _moba_attn_kernel(q_ref, k_ref, vt_ref, o_ref, s_even, s_odd, m_even, m_odd, *, heads, chunk_blocks, nb):
    t = pl.program_id(2)
    i = jnp.minimum(t, nb - 1)
    chunk_keys = chunk_blocks * MOBA_BLOCK
    n_chunks = i // chunk_blocks + 1
    n_valid2 = jnp.where(t > 0, (t - 1) // chunk_blocks + 1, 0)

    @pl.when((pl.program_id(0) == 0) & (pl.program_id(1) == 0) & (t == 0))
    def _():
        s_even[...] = jnp.zeros_like(s_even)
        s_odd[...] = jnp.zeros_like(s_odd)
        m_odd[...] = jnp.zeros_like(m_odd)

    key_r = lax.broadcasted_iota(jnp.int32, (MOBA_BLOCK, MOBA_BLOCK), 0)
    qry_c = lax.broadcasted_iota(jnp.int32, (MOBA_BLOCK, MOBA_BLOCK), 1)
    causal = qry_c >= key_r
    q_aug = [q_ref[:, hh * AUG:(hh + 1) * AUG] for hh in range(heads)]

    def both(c, carry, last, s_w, s_r, m_row2):
        m_part, acc = carry
        k0 = pl.multiple_of(c * chunk_keys, chunk_keys)
        live2 = jnp.full((1, MOBA_BLOCK), c, jnp.int32) < n_valid2
        new_m, new_acc = [], []
        for hh in range(heads):
            s = lax.dot_general(k_ref[pl.ds(k0, chunk_keys), hh * AUG:(hh + 1) * AUG], q_aug[hh],
                                (((1,), (1,)), ((), ())), preferred_element_type=F32)
            if last:
                parts = []
                for u in range(chunk_blocks):
                    other_block = jnp.full((1, MOBA_BLOCK), c * chunk_blocks + u, jnp.int32) != i
                    parts.append(jnp.where(causal | other_block, s[u * MOBA_BLOCK:(u + 1) * MOBA_BLOCK], NEG))
                s = jnp.concatenate(parts, axis=0)
            s_w[hh, pl.ds(k0, chunk_keys), :] = s
            new_m.append(jnp.maximum(m_part[hh],
                                     jnp.max(s.reshape(chunk_keys // SUBLANES, SUBLANES, MOBA_BLOCK), axis=0)))
            a = acc[hh]
            for u in range(0, chunk_keys, PV_KEYS):
                ku = pl.multiple_of(k0 + u, PV_KEYS)
                p = jnp.exp2(s_r[hh, pl.ds(ku, PV_KEYS), :] - m_row2[hh])
                p = jnp.where(live2, p, 0.0).astype(BF16)
                a = a + jnp.dot(vt_ref[0, hh, :, pl.ds(ku, PV_KEYS)], p, preferred_element_type=F32)
            new_acc.append(a)
        return tuple(new_m), tuple(new_acc)

    def step(s_w, s_r, m_w, m_r):
        m_row2 = [m_r[hh] for hh in range(heads)]
        carry = (tuple(jnp.full((SUBLANES, MOBA_BLOCK), NEG, F32) for _ in range(heads)),
                 tuple(jnp.zeros((VT_ROWS, MOBA_BLOCK), F32) for _ in range(heads)))
        carry = lax.fori_loop(0, n_chunks - 1, lambda c, cr: both(c, cr, False, s_w, s_r, m_row2), carry)
        m_part, acc = both(n_chunks - 1, carry, True, s_w, s_r, m_row2)
        for hh in range(heads):
            m_w[hh] = jnp.max(m_part[hh], axis=0, keepdims=True)
            o = acc[hh][:MOBA_HD] * (1.0 / acc[hh][MOBA_HD:MOBA_HD + 1])
            o_ref[:, hh * MOBA_HD:(hh + 1) * MOBA_HD] = o.T.astype(o_ref.dtype)

    @pl.when(t % 2 == 0)
    def _():
        step(s_even, s_odd, m_even, m_odd)

    @pl.when(t % 2 == 1)
    def _():
        step(s_odd, s_even, m_odd, m_even)


def moba_attention(q_aug, k_aug, vt, batch):
    m = q_aug.shape[0]
    seq = m // batch
    nb = seq // MOBA_BLOCK
    hps = ATTN_HEADS_PER_STEP
    assert nb % ATTN_CHUNK_BLOCKS == 0
    return pl.pallas_call(
        functools.partial(_moba_attn_kernel, heads=hps, chunk_blocks=ATTN_CHUNK_BLOCKS, nb=nb),
        grid=(batch, MOBA_HEADS // hps, nb + 1),
        in_specs=[pl.BlockSpec((MOBA_BLOCK, hps * AUG), lambda b, g, t: (b * nb + jnp.minimum(t, nb - 1), g)),
                  pl.BlockSpec((seq, hps * AUG), lambda b, g, t: (b, g)),
                  pl.BlockSpec((1, hps, VT_ROWS, seq), lambda b, g, t: (b, g, 0, 0))],
        out_specs=pl.BlockSpec((MOBA_BLOCK, hps * MOBA_HD), lambda b, g, t: (b * nb + jnp.maximum(t - 1, 0), g)),
        out_shape=jax.ShapeDtypeStruct((m, MOBA_HEADS * MOBA_HD), BF16),
        scratch_shapes=[pltpu.VMEM((hps, seq, MOBA_BLOCK), F32), pltpu.VMEM((hps, seq, MOBA_BLOCK), F32),
                        pltpu.VMEM((hps, 1, MOBA_BLOCK), F32), pltpu.VMEM((hps, 1, MOBA_BLOCK), F32)],
        compiler_params=_params(3),
        name="moba_attn",
    )(q_aug, k_aug, vt)


def moba_layer(x, gain, w_qkv, q_gain, k_gain, batch):
    q_aug, k_aug, vt = moba_prep(x, gain, w_qkv.astype(BF16), q_gain, k_gain, batch)
    return moba_attention(q_aug, k_aug, vt, batch)


def kernel(x, norm_mix, norm_ffn, hyb_w_in, gla_w_gate2, gla_b_gate, gla_norm, ml_conv, ml_b_if, ml_norm,
           hyb_w_out, moba_w_qkv, moba_q_norm, moba_k_norm, moba_w_o, ffn_w_gate, ffn_w_up, ffn_w_down):
    batch, seq, d = x.shape
    depth = norm_mix.shape[0]
    xf = x.reshape(batch * seq, d)
    for l in range(depth):
        j = l // 2
        if l % 2 == 0:
            y = hybrid_layer(xf, norm_mix[l], hyb_w_in[j], gla_w_gate2[j], gla_b_gate[j], gla_norm[j],
                             ml_conv[j], ml_b_if[j], ml_norm[j], batch)
            w_o = hyb_w_out[j]
        else:
            y = moba_layer(xf, norm_mix[l], moba_w_qkv[j], moba_q_norm[j], moba_k_norm[j], batch)
            w_o = moba_w_o[j]
        xf = out_ffn(xf, y, w_o.astype(BF16), norm_ffn[l], ffn_w_gate[l].astype(BF16),
                     ffn_w_up[l].astype(BF16), ffn_w_down[l].astype(BF16))
    return xf.reshape(batch, seq, d)
```

```python
import functools

import jax
import jax.numpy as jnp
import numpy as np
from jax import lax
from jax.experimental import pallas as pl
from jax.experimental.pallas import tpu as pltpu

F32 = jnp.float32
BF16 = jnp.bfloat16

LANES = 128
SUBLANES = 8
VMEM_LIMIT_BYTES = 56 * 1024 * 1024

EPS = 1e-6
NEG = -1e30

GLA_HEADS = 4
GLA_DK = 64
GLA_DV = 128
GLA_RANK = 16
GLA_TAU = 16.0
ML_HEADS = 4
ML_DK = 64
ML_DV = 128
ML_CONV = 4
CHUNK = 64
SUB = 16
MOBA_HEADS = 8
MOBA_HD = 128
MOBA_BLOCK = 256
MOBA_TOPK = 3

GLA_QK = GLA_HEADS * GLA_DK
GLA_V = GLA_HEADS * GLA_DV
ML_QK = ML_HEADS * ML_DK
ML_V = ML_HEADS * ML_DV

C_GQ = 0
C_GK = C_GQ + GLA_QK
C_GV = C_GK + GLA_QK
C_GR = C_GV + GLA_V
C_MQ = C_GR + GLA_V
C_MK = C_MQ + ML_QK
C_MV = C_MK + ML_QK
C_MO = C_MV + ML_V
C_SMALL = C_MO + ML_V
HYB_COLS = C_SMALL + LANES
S_GLR = 0
S_MI = GLA_RANK
S_MF = GLA_RANK + ML_HEADS

ROW_TILE = 512
MIX_TILE = 256
FF_TILE = 256
PROJ_COL_TILE = 512


def _rms(x, gain):
    return x * lax.rsqrt(jnp.mean(x * x, axis=-1, keepdims=True) + EPS) * gain


def _log_sigmoid(z):
    return jnp.minimum(z, 0.0) - jnp.log1p(jnp.exp(-jnp.abs(z)))


def _sigmoid(z):
    return 1.0 / (1.0 + jnp.exp(-z))


def _dot(a, b):
    return jnp.dot(a.astype(BF16), b.astype(BF16), preferred_element_type=F32)


def _dot_nt(a, b):
    return lax.dot_general(a.astype(BF16), b.astype(BF16), (((1,), (1,)), ((), ())),
                           preferred_element_type=F32)


def _dot_tn(a, b):
    return lax.dot_general(a.astype(BF16), b.astype(BF16), (((0,), (0,)), ((), ())),
                           preferred_element_type=F32)


def _const_spec(shape):
    return pl.BlockSpec(shape, lambda *_: (0,) * len(shape), pipeline_mode=pl.Buffered(1))


def _params(n_grid):
    return pltpu.CompilerParams(dimension_semantics=("arbitrary",) * n_grid,
                                vmem_limit_bytes=VMEM_LIMIT_BYTES)


def _norm_proj_kernel(x_ref, g_ref, w_ref, main_ref, tail_ref, *, col_tile):
    xn = _rms(x_ref[...], g_ref[...]).astype(BF16)
    n_main = main_ref.shape[1]
    for c0 in range(0, n_main, col_tile):
        main_ref[:, c0:c0 + col_tile] = jnp.dot(xn, w_ref[:, c0:c0 + col_tile],
                                                 preferred_element_type=F32).astype(main_ref.dtype)
    tail_ref[...] = jnp.dot(xn, w_ref[:, n_main:], preferred_element_type=F32)


def norm_proj(x, gain, w, n_tail):
    m, d = x.shape
    n = w.shape[1]
    n_main = n - n_tail
    tm = min(ROW_TILE, m)
    assert n_main % PROJ_COL_TILE == 0
    return pl.pallas_call(
        functools.partial(_norm_proj_kernel, col_tile=PROJ_COL_TILE),
        grid=(m // tm,),
        in_specs=[pl.BlockSpec((tm, d), lambda i: (i, 0)),
                  _const_spec((1, d)),
                  _const_spec((d, n))],
        out_specs=[pl.BlockSpec((tm, n_main), lambda i: (i, 0)),
                   pl.BlockSpec((tm, n_tail), lambda i: (i, 0))],
        out_shape=[jax.ShapeDtypeStruct((m, n_main), BF16),
                   jax.ShapeDtypeStruct((m, n_tail), F32)],
        compiler_params=_params(1),
        name="norm_proj",
    )(x, gain.reshape(1, d), w)


def _out_ffn_kernel(x_ref, y_ref, wo_ref, g_ref, wg_ref, wu_ref, wd_ref, o_ref, h_ref, *, ff_tile):
    x1 = x_ref[...] + jnp.dot(y_ref[...], wo_ref[...], preferred_element_type=F32)
    xn = _rms(x1, g_ref[...]).astype(BF16)
    d_ff = wg_ref.shape[1]
    for c0 in range(0, d_ff, ff_tile):
        g = jnp.dot(xn, wg_ref[:, c0:c0 + ff_tile], preferred_element_type=F32)
        u = jnp.dot(xn, wu_ref[:, c0:c0 + ff_tile], preferred_element_type=F32)
        h_ref[:, c0:c0 + ff_tile] = (g * _sigmoid(g) * u).astype(BF16)
    o_ref[...] = x1 + jnp.dot(h_ref[...], wd_ref[...], preferred_element_type=F32)


def out_ffn(x, y, w_o, gain, w_gate, w_up, w_down):
    m, d = x.shape
    dy = y.shape[1]
    d_ff = w_gate.shape[1]
    tm = min(ROW_TILE, m)
    return pl.pallas_call(
        functools.partial(_out_ffn_kernel, ff_tile=FF_TILE),
        grid=(m // tm,),
        in_specs=[pl.BlockSpec((tm, d), lambda i: (i, 0)),
                  pl.BlockSpec((tm, dy), lambda i: (i, 0)),
                  _const_spec((dy, d)),
                  _const_spec((1, d)),
                  _const_spec((d, d_ff)),
                  _const_spec((d, d_ff)),
                  _const_spec((d_ff, d))],
        out_specs=pl.BlockSpec((tm, d), lambda i: (i, 0)),
        out_shape=jax.ShapeDtypeStruct((m, d), F32),
        scratch_shapes=[pltpu.VMEM((tm, d_ff), BF16)],
        compiler_params=_params(1),
        name="out_ffn",
    )(x, y, w_o, gain.reshape(1, d), w_gate, w_up, w_down)


def _split3_rows(x):
    hi = x.astype(BF16)
    r1 = x - hi.astype(F32)
    mid = r1.astype(BF16)
    lo = (r1 - mid.astype(F32)).astype(BF16)
    return jnp.concatenate([hi, mid, lo], axis=0)


def _split3_lanes(x):
    hi = x.astype(BF16)
    r1 = x - hi.astype(F32)
    mid = r1.astype(BF16)
    lo = (r1 - mid.astype(F32)).astype(BF16)
    return jnp.concatenate([hi, mid, lo], axis=1)


def _blocks_diag(tiles, zero):
    n = len(tiles)
    return jnp.concatenate(
        [jnp.concatenate([tiles[a] if a == b else zero for b in range(n)], axis=1) for a in range(n)], axis=0)


def _hybrid_kernel(proj_ref, small_ref, gates_ref, wg2_ref, bg_ref, gn_ref, conv_ref, bsmall_ref, brow_ref, mn_ref,
                   y_ref, gla_st, ml_cs, ml_ns, ml_m, conv_carry):
    t_tile = proj_ref.shape[0]
    nc = t_tile // CHUNK
    n_sub = CHUNK // SUB

    @pl.when(pl.program_id(1) == 0)
    def _():
        gla_st[...] = jnp.zeros_like(gla_st)
        ml_cs[...] = jnp.zeros_like(ml_cs)
        ml_ns[...] = jnp.zeros_like(ml_ns)
        ml_m[...] = jnp.zeros_like(ml_m)
        conv_carry[...] = jnp.zeros_like(conv_carry)

    xcat = proj_ref[:, C_MQ:C_MQ + 2 * ML_QK].astype(F32)
    ext = jnp.concatenate([conv_carry[...], xcat], axis=0)
    cw = conv_ref[...]
    acc = ext * cw[ML_CONV - 1:ML_CONV]
    for j in range(1, ML_CONV):
        acc = acc + pltpu.roll(ext, j, axis=0) * cw[ML_CONV - 1 - j:ML_CONV - j]
    conv = acc[SUBLANES:]
    qk = conv * _sigmoid(conv)
    conv_carry[...] = xcat[t_tile - SUBLANES:]

    lane = lax.broadcasted_iota(jnp.int32, (1, LANES), 1)
    head_mask = (lane < GLA_DK, lane >= GLA_DK)
    hm0 = head_mask[0]
    r_t = lax.broadcasted_iota(jnp.int32, (t_tile, t_tile), 0)
    c_t = lax.broadcasted_iota(jnp.int32, (t_tile, t_tile), 1)
    tril_bd = ((r_t >= c_t) & (r_t // CHUNK == c_t // CHUNK)).astype(BF16)
    tril3 = jnp.concatenate([tril_bd, tril_bd, tril_bd], axis=1)
    r_l = lax.broadcasted_iota(jnp.int32, (LANES, LANES), 0)
    c_l = lax.broadcasted_iota(jnp.int32, (LANES, LANES), 1)
    triu_bd = ((r_l <= c_l) & (r_l // CHUNK == c_l // CHUNK)).astype(BF16)
    triu3 = jnp.concatenate([triu_bd, triu_bd, triu_bd], axis=0)
    row_c = lax.broadcasted_iota(jnp.int32, (CHUNK, LANES), 0)
    causal2 = (lax.broadcasted_iota(jnp.int32, (CHUNK, LANES), 1) % CHUNK) <= row_c
    sub_id = (lax.broadcasted_iota(jnp.int32, (t_tile, LANES), 0) % CHUNK) // SUB
    row_sub = lax.broadcasted_iota(jnp.int32, (2 * SUBLANES, LANES), 0)
    zero_tile = jnp.zeros((CHUNK, LANES), F32)
    zero_v = jnp.zeros((CHUNK, LANES), BF16)

    def chunk_rows(x, c):
        return x[c * CHUNK:(c + 1) * CHUNK]

    def diag_blocks(r):
        return [r[c * CHUNK:(c + 1) * CHUNK, c * LANES:(c + 1) * LANES] for c in range(nc)]

    small = small_ref[...]
    smallb = small + bsmall_ref[...]

    z = _dot(small, wg2_ref[...]) + bg_ref[...]
    log_a = _log_sigmoid(z) * (1.0 / GLA_TAU)
    cum = jnp.dot(tril3, _split3_rows(jnp.concatenate([log_a, _log_sigmoid(smallb)], axis=1)),
                  preferred_element_type=F32)
    g_all = gates_ref[0, 0] + brow_ref[...]
    bc_rows = jnp.dot(_split3_lanes(_log_sigmoid(g_all)), triu3, preferred_element_type=F32)

    for p in range(GLA_HEADS // 2):
        q2 = proj_ref[:, C_GQ + p * LANES:C_GQ + (p + 1) * LANES].astype(F32) * (GLA_DK ** -0.5)
        k2 = proj_ref[:, C_GK + p * LANES:C_GK + (p + 1) * LANES].astype(F32)
        bc = cum[:, p * LANES:(p + 1) * LANES]
        v = [proj_ref[:, C_GV + (2 * p + hl) * GLA_DV:C_GV + (2 * p + hl + 1) * GLA_DV] for hl in range(2)]

        ref_pt = [[chunk_rows(bc, c)[i * SUB - 1:i * SUB] if i else jnp.zeros((1, LANES), F32)
                   for i in range(n_sub)] for c in range(nc)]
        r_own = jnp.concatenate([jnp.broadcast_to(ref_pt[c][i], (SUB, LANES))
                                 for c in range(nc) for i in range(n_sub)], axis=0)
        qs = q2 * jnp.exp(bc - r_own)
        q_hat = jnp.concatenate([jnp.where(sub_id == i, qs, 0.0) for i in range(n_sub)], axis=1)
        k_hat, k_dec, last = [], [], []
        for c in range(nc):
            kc, bcc = chunk_rows(k2, c), chunk_rows(bc, c)
            ke = [kc * jnp.exp(ref_pt[c][i] - bcc) for i in range(n_sub)]
            for hl in range(2):
                k_hat.append(jnp.concatenate([jnp.where(head_mask[hl], ke[i], 0.0) for i in range(n_sub)], axis=1))
            last.append(bcc[CHUNK - 1:CHUNK])
            k_dec.append(kc * jnp.exp(last[c] - bcc))
        k_hat = jnp.concatenate(k_hat, axis=0)
        sc = diag_blocks(_dot_nt(q_hat, k_hat))
        sc = [jnp.where(causal2, s, 0.0) for s in sc]
        v_bd = jnp.concatenate([jnp.concatenate([chunk_rows(v[hl], c) if hl == b else zero_v for b in range(2)],
                                                axis=1) for c in range(nc) for hl in range(2)], axis=0)
        o_intra = _dot(_blocks_diag(sc, zero_tile), v_bd)
        u_all = _dot_tn(jnp.concatenate(v, axis=1), _blocks_diag(k_dec, zero_tile))

        st = gla_st[p]
        before = []
        for c in range(nc):
            before.append(st)
            u = jnp.where(hm0, u_all[:GLA_DV, c * LANES:(c + 1) * LANES], u_all[GLA_DV:, c * LANES:(c + 1) * LANES])
            st = st * jnp.exp(last[c]) + u
        gla_st[p] = st
        st_stack = jnp.concatenate(before, axis=0)
        q_in = q2 * jnp.exp(bc)
        for hl in range(2):
            h = 2 * p + hl
            o_inter = jnp.concatenate(diag_blocks(_dot_nt(jnp.where(head_mask[hl], q_in, 0.0), st_stack)), axis=0)
            o = o_intra[:, hl * GLA_DV:(hl + 1) * GLA_DV] + o_inter
            gate = proj_ref[:, C_GR + h * GLA_DV:C_GR + (h + 1) * GLA_DV].astype(F32)
            yg = _rms(o, gn_ref[:, h * GLA_DV:(h + 1) * GLA_DV]) * (gate * _sigmoid(gate))
            y_ref[:, h * GLA_DV:(h + 1) * GLA_DV] = yg.astype(y_ref.dtype)

    ones_tile = jnp.ones((t_tile, LANES), BF16)
    for p in range(ML_HEADS // 2):
        mq2 = qk[:, p * LANES:(p + 1) * LANES]
        mk2 = qk[:, ML_QK + p * LANES:ML_QK + (p + 1) * LANES] * (ML_DK ** -0.5)
        v = [proj_ref[:, C_MV + (2 * p + hl) * ML_DV:C_MV + (2 * p + hl + 1) * ML_DV] for hl in range(2)]
        k_stack = jnp.concatenate([jnp.where(head_mask[hl], chunk_rows(mk2, c), 0.0)
                                   for c in range(nc) for hl in range(2)], axis=0)
        qk_blocks = diag_blocks(_dot_nt(mq2, k_stack))

        w_blocks, wk_blocks, cm_cols, b_max, bc_last = [], [], [], [], []
        for c in range(nc):
            bcr = bc_rows[ML_HEADS * c + 2 + p:ML_HEADS * c + 3 + p]
            b_pair = g_all[ML_HEADS * c + p:ML_HEADS * c + p + 1] - bcr
            bm = jnp.where(causal2, b_pair, -jnp.inf)
            cm = [jnp.max(jnp.where(head_mask[hl], bm, -jnp.inf), axis=1, keepdims=True) for hl in range(2)]
            w_blocks.append(qk_blocks[c] * jnp.exp(bm - jnp.where(hm0, cm[0], cm[1])))
            cm_cols.append(cm)
            bmx = [jnp.max(jnp.where(head_mask[hl], b_pair, -jnp.inf), axis=1, keepdims=True) for hl in range(2)]
            b_max.append(jnp.where(hm0, bmx[0], bmx[1]))
            bc_last.append(jnp.where(hm0, bcr[:, CHUNK - 1:CHUNK], bcr[:, LANES - 1:LANES]))
            e_col = []
            for hl in range(2):
                h = 2 * p + hl
                b_col = (chunk_rows(smallb, c)[:, S_MI + h:S_MI + h + 1]
                         - chunk_rows(cum, c)[:, GLA_QK + S_MF + h:GLA_QK + S_MF + h + 1])
                e_col.append(jnp.exp(b_col - bmx[hl]))
            wk_blocks.append(chunk_rows(mk2, c) * jnp.where(hm0, e_col[0], e_col[1]))

        ones_col = [jnp.broadcast_to((lane == hl).astype(BF16), (CHUNK, LANES)) for hl in range(2)]
        v_bd = jnp.concatenate([jnp.concatenate([chunk_rows(v[hl], c) if hl == b else zero_v for b in range(2)]
                                                + [ones_col[hl]], axis=1)
                                for c in range(nc) for hl in range(2)], axis=0)
        nv = _dot(_blocks_diag(w_blocks, zero_tile), v_bd)
        u_all = _dot_tn(jnp.concatenate(v + [ones_tile], axis=1), _blocks_diag(wk_blocks, zero_tile))

        cs, ns, m_run = ml_cs[p], ml_ns[p], ml_m[p]
        before = []
        for c in range(nc):
            before.append((cs, ns, m_run))
            m_top = jnp.maximum(m_run, b_max[c])
            keep = jnp.exp(m_run - m_top)
            gain = jnp.exp(b_max[c] - m_top)
            blk = slice(c * LANES, (c + 1) * LANES)
            cs = keep * cs + gain * jnp.where(hm0, u_all[:ML_DV, blk], u_all[ML_DV:2 * ML_DV, blk])
            ns = keep * ns + gain * u_all[2 * ML_DV:2 * ML_DV + 1, blk]
            m_run = bc_last[c] + m_top
        ml_cs[p], ml_ns[p], ml_m[p] = cs, ns, m_run
        ns_rows = jnp.zeros((2 * SUBLANES, LANES), F32)
        for c in range(nc):
            ns_rows = jnp.where(row_sub == c, before[c][1], ns_rows)
        state_stack = jnp.concatenate([b[0] for b in before] + [ns_rows], axis=0)

        for hl in range(2):
            h = 2 * p + hl
            rr = _dot_nt(jnp.where(head_mask[hl], mq2, 0.0), state_stack)
            q_cs = diag_blocks(rr)
            h_chunks = []
            for c in range(nc):
                rows = slice(c * CHUNK, (c + 1) * CHUNK)
                q_ns = rr[rows, nc * ML_DV + c:nc * ML_DV + c + 1]
                m_prev = before[c][2][:, hl * ML_DK:hl * ML_DK + 1]
                cm = cm_cols[c][hl]
                m_top = jnp.maximum(m_prev, cm)
                local = jnp.exp(cm - m_top)
                carried = jnp.exp(m_prev - m_top)
                num = local * nv[rows, hl * ML_DV:(hl + 1) * ML_DV] + carried * q_cs[c]
                den = local * nv[rows, 2 * ML_DV + hl:2 * ML_DV + hl + 1] + carried * q_ns
                bc_col = chunk_rows(cum, c)[:, GLA_QK + S_MF + h:GLA_QK + S_MF + h + 1]
                h_chunks.append(num / jnp.maximum(jnp.abs(den), jnp.exp(-(bc_col + m_top))))
            og = proj_ref[:, C_MO + h * ML_DV:C_MO + (h + 1) * ML_DV].astype(F32)
            ym = _rms(jnp.concatenate(h_chunks, axis=0) * _sigmoid(og), mn_ref[:, h * ML_DV:(h + 1) * ML_DV])
            y_ref[:, GLA_V + h * ML_DV:GLA_V + (h + 1) * ML_DV] = ym.astype(y_ref.dtype)


def hybrid_core(proj, small, gates_t, w_gate2, b_gate, gla_norm, conv_w, b_small, b_rows, ml_norm, batch):
    m = proj.shape[0]
    seq = m // batch
    t = min(MIX_TILE, seq)
    nt = seq // t
    assert 2 * SUBLANES >= t // CHUNK
    return pl.pallas_call(
        _hybrid_kernel,
        grid=(batch, nt),
        in_specs=[pl.BlockSpec((t, C_SMALL), lambda b, i: (b * nt + i, 0)),
                  pl.BlockSpec((t, LANES), lambda b, i: (b * nt + i, 0)),
                  pl.BlockSpec((1, 1, ML_HEADS * (t // CHUNK), LANES), lambda b, i: (b, i, 0, 0)),
                  _const_spec((LANES, GLA_QK)),
                  _const_spec((1, GLA_QK)),
                  _const_spec((1, GLA_V)),
                  _const_spec((ML_CONV, 2 * ML_QK)),
                  _const_spec((1, LANES)),
                  _const_spec((ML_HEADS * (t // CHUNK), LANES)),
                  _const_spec((1, ML_V))],
        out_specs=pl.BlockSpec((t, GLA_V + ML_V), lambda b, i: (b * nt + i, 0)),
        out_shape=jax.ShapeDtypeStruct((m, GLA_V + ML_V), BF16),
        scratch_shapes=[pltpu.VMEM((GLA_HEADS // 2, GLA_DV, LANES), F32),
                        pltpu.VMEM((ML_HEADS // 2, ML_DV, LANES), F32),
                        pltpu.VMEM((ML_HEADS // 2, 1, LANES), F32),
                        pltpu.VMEM((ML_HEADS // 2, 1, LANES), F32),
                        pltpu.VMEM((SUBLANES, 2 * ML_QK), F32)],
        compiler_params=_params(2),
        name="hybrid_core",
    )(proj, small, gates_t, w_gate2, b_gate, gla_norm, conv_w, b_small, b_rows, ml_norm)


def hybrid_layer(x, gain, w_in, w_gate2, b_gate, gla_norm, conv_w, b_if, ml_norm, batch):
    m, d = x.shape
    seq = m // batch
    segs = np.cumsum([0, GLA_QK, GLA_QK, GLA_V, GLA_V, GLA_RANK, ML_QK, ML_QK, ML_V, ML_V, ML_HEADS, ML_HEADS])
    seg = lambda i: w_in[:, int(segs[i]):int(segs[i + 1])]
    pad = jnp.zeros((d, LANES - GLA_RANK - 2 * ML_HEADS), w_in.dtype)
    w_perm = jnp.concatenate([seg(0), seg(1), seg(2), seg(3), seg(5), seg(6), seg(7), seg(8),
                              seg(4), seg(9), seg(10), pad], axis=1).astype(BF16)
    proj, small = norm_proj(x, gain, w_perm, LANES)
    gates = small[:, S_MI:S_MI + 2 * ML_HEADS]
    gates_t = gates.reshape(batch, seq // CHUNK, CHUNK, 2 * ML_HEADS).transpose(0, 1, 3, 2)
    t = min(MIX_TILE, seq)
    gates_t = gates_t.reshape(batch, seq // t, ML_HEADS * (t // CHUNK), LANES)
    b_small = jnp.zeros((1, LANES), F32).at[0, S_MI:S_MI + 2 * ML_HEADS].set(b_if)
    b_rows = jnp.tile(jnp.repeat(b_if, CHUNK).reshape(ML_HEADS, LANES), (t // CHUNK, 1))
    w_gate2_rows = jnp.zeros((LANES, GLA_QK), F32).at[S_GLR:S_GLR + GLA_RANK].set(w_gate2)
    return hybrid_core(proj, small, gates_t, w_gate2_rows, b_gate.reshape(1, -1), gla_norm.reshape(1, -1), conv_w,
                       b_small, b_rows, ml_norm.reshape(1, -1), batch)


LOG2E = 1.4426950408889634
AUG = 2 * MOBA_HD
LANES_PER_BLOCK = 4
MASK_BIG = 2.0 ** 100
PROJ_HEADS = 4
VT_ROWS = MOBA_HD + 16


def _split3(x):
    x = np.asarray(x, np.float32)
    hi = x.astype(BF16).astype(np.float32)
    mid = (x - hi).astype(BF16).astype(np.float32)
    lo = (x - hi - mid).astype(BF16).astype(np.float32)
    return np.stack([hi, mid, lo], axis=-1)


def _aug_tables(nb):
    c = np.asarray([LOG2E * 2.0 ** (-8.0 * (h + 1) / MOBA_HEADS) for h in range(MOBA_HEADS)], np.float32)
    pos = np.arange(MOBA_BLOCK, dtype=np.float32)
    base = LANES_PER_BLOCK * nb
    q_rows = np.zeros((MOBA_HEADS, MOBA_BLOCK, LANES), np.float32)
    k_rows = np.zeros((MOBA_HEADS, MOBA_BLOCK, LANES), np.float32)
    q_rows[:, :, base:base + 3] = _split3(-c[:, None] * pos[None, :])
    q_rows[:, :, base + 3:base + 6] = 1.0
    k_rows[:, :, base:base + 3] = 1.0
    k_rows[:, :, base + 3:base + 6] = _split3(c[:, None] * pos[None, :])
    i_idx = np.arange(nb)[:, None]
    j_idx = np.arange(nb)[None, :]
    steps = np.zeros((MOBA_HEADS, nb, nb, LANES_PER_BLOCK), np.float32)
    steps[..., 0] = np.where(j_idx == i_idx, 0.0, -MASK_BIG)
    gap = _split3(-c[:, None, None] * (MOBA_BLOCK * (i_idx - j_idx)).astype(np.float32)[None])
    steps[..., 1:] = np.where((j_idx <= i_idx)[None, :, :, None], gap, 0.0)
    q_steps = np.zeros((MOBA_HEADS, nb, 1, LANES), np.float32)
    q_steps[:, :, 0, :base] = steps.reshape(MOBA_HEADS, nb, base)
    k_steps = np.zeros((nb, 1, LANES), np.float32)
    k_steps[:, 0, :base] = np.repeat(np.eye(nb, dtype=np.float32), LANES_PER_BLOCK, axis=1)
    expand = np.zeros((nb, LANES), np.float32)
    expand[np.arange(nb), LANES_PER_BLOCK * np.arange(nb)] = MASK_BIG
    return (jnp.asarray(q_rows), jnp.asarray(q_steps), jnp.asarray(k_rows), jnp.asarray(k_steps),
            jnp.asarray(expand, BF16))


def _moba_prep_kernel(x_ref, g_ref, w_ref, qg_ref, kg_ref, qrow_ref, qstep_ref, krow_ref, kstep_ref, exp_ref,
                      qa_ref, ka_ref, vt_ref, kmean_ref):
    i = pl.program_id(1)
    nb = kmean_ref.shape[1]
    d = MOBA_HEADS * MOBA_HD

    @pl.when(i == 0)
    def _():
        kmean_ref[...] = jnp.zeros_like(kmean_ref)

    xn = _rms(x_ref[...], g_ref[...]).astype(BF16)
    blk = lax.broadcasted_iota(jnp.int32, (nb, MOBA_BLOCK), 0)
    past = blk < i
    ones_rows = (lax.broadcasted_iota(jnp.int32, (VT_ROWS - MOBA_HD, MOBA_BLOCK), 0) == 0).astype(F32)
    k_step = kstep_ref[i]

    for h in range(MOBA_HEADS):
        hg = h % PROJ_HEADS
        if hg == 0:
            c0 = h * MOBA_HD
            q_grp, k_grp, v_grp = (jnp.dot(xn, w_ref[:, off + c0:off + c0 + PROJ_HEADS * MOBA_HD],
                                           preferred_element_type=F32) for off in (0, d, 2 * d))
        part = slice(hg * MOBA_HD, (hg + 1) * MOBA_HD)
        q = _rms(q_grp[:, part], qg_ref[...]) * (MOBA_HD ** -0.5)
        k = _rms(k_grp[:, part], kg_ref[...])
        v = v_grp[:, part]
        vt_ref[0, h] = jnp.concatenate([v.T, ones_rows], axis=0).astype(vt_ref.dtype)

        gate = _dot_nt(kmean_ref[h], q)
        gate = jnp.where(past, gate, -jnp.inf)
        rank = jnp.zeros((nb, MOBA_BLOCK), F32)
        for j in range(nb):
            other = gate[j:j + 1]
            ahead = (other > gate) | ((other == gate) & (j < blk))
            rank = rank + ahead.astype(F32)
        sel = (past & (rank < MOBA_TOPK)).astype(BF16)
        q_extra = (qrow_ref[h] + qstep_ref[h, i]) + _dot_tn(sel, exp_ref[...])
        qa_ref[:, h * AUG:h * AUG + MOBA_HD] = (q * LOG2E).astype(qa_ref.dtype)
        qa_ref[:, h * AUG + MOBA_HD:(h + 1) * AUG] = q_extra.astype(qa_ref.dtype)
        ka_ref[:, h * AUG:h * AUG + MOBA_HD] = k.astype(ka_ref.dtype)
        ka_ref[:, h * AUG + MOBA_HD:(h + 1) * AUG] = (krow_ref[h] + k_step).astype(ka_ref.dtype)

        kmean_ref[h, pl.ds(i, 1), :] = jnp.mean(k, axis=0, keepdims=True)


def moba_prep(x, gain, w_qkv, q_gain, k_gain, batch):
    m, dm = x.shape
    seq = m // batch
    nb = seq // MOBA_BLOCK
    assert LANES_PER_BLOCK * nb + 6 <= LANES
    d = MOBA_HEADS * MOBA_HD
    q_rows, q_steps, k_rows, k_steps, expand = _aug_tables(nb)
    return pl.pallas_call(
        _moba_prep_kernel,
        grid=(batch, nb),
        in_specs=[pl.BlockSpec((MOBA_BLOCK, dm), lambda b, i: (b * nb + i, 0)),
                  _const_spec((1, dm)),
                  _const_spec((dm, 3 * d)),
                  _const_spec((1, MOBA_HD)),
                  _const_spec((1, MOBA_HD)),
                  _const_spec((MOBA_HEADS, MOBA_BLOCK, LANES)),
                  _const_spec((MOBA_HEADS, nb, 1, LANES)),
                  _const_spec((MOBA_HEADS, MOBA_BLOCK, LANES)),
                  _const_spec((nb, 1, LANES)),
                  _const_spec((nb, LANES))],
        out_specs=[pl.BlockSpec((MOBA_BLOCK, MOBA_HEADS * AUG), lambda b, i: (b * nb + i, 0)),
                   pl.BlockSpec((MOBA_BLOCK, MOBA_HEADS * AUG), lambda b, i: (b * nb + i, 0)),
                   pl.BlockSpec((1, MOBA_HEADS, VT_ROWS, MOBA_BLOCK), lambda b, i: (b, 0, 0, i))],
        out_shape=[jax.ShapeDtypeStruct((m, MOBA_HEADS * AUG), BF16),
                   jax.ShapeDtypeStruct((m, MOBA_HEADS * AUG), BF16),
                   jax.ShapeDtypeStruct((batch, MOBA_HEADS, VT_ROWS, seq), BF16)],
        scratch_shapes=[pltpu.VMEM((MOBA_HEADS, nb, MOBA_HD), F32)],
        compiler_params=_params(2),
        name="moba_prep",
    )(x, gain.reshape(1, dm), w_qkv, q_gain.reshape(1, -1), k_gain.reshape(1, -1),
      q_rows, q_steps, k_rows, k_steps, expand)


ATTN_HEADS_PER_STEP = 2
ATTN_CHUNK_BLOCKS = 4
PV_KEYS = 256


def _moba_attn_kernel(q_ref, k_ref, vt_ref, o_ref, s_even, s_odd, m_even, m_odd, *, heads, chunk_blocks, nb):
    t = pl.program_id(2)
    i = jnp.minimum(t, nb - 1)
    chunk_keys = chunk_blocks * MOBA_BLOCK
    n_chunks = i // chunk_blocks + 1
    n_valid2 = jnp.where(t > 0, (t - 1) // chunk_blocks + 1, 0)

    @pl.when((pl.program_id(0) == 0) & (pl.program_id(1) == 0) & (t == 0))
    def _():
        s_even[...] = jnp.zeros_like(s_even)
        s_odd[...] = jnp.zeros_like(s_odd)
        m_odd[...] = jnp.zeros_like(m_odd)

    key_r = lax.broadcasted_iota(jnp.int32, (MOBA_BLOCK, MOBA_BLOCK), 0)
    qry_c = lax.broadcasted_iota(jnp.int32, (MOBA_BLOCK, MOBA_BLOCK), 1)
    causal = qry_c >= key_r
    q_aug = [q_ref[:, hh * AUG:(hh + 1) * AUG] for hh in range(heads)]

    def both(c, carry, last, s_w, s_r, m_row2):
        m_part, acc = carry
        k0 = c * chunk_keys
        live2 = jnp.full((1, MOBA_BLOCK), c, jnp.int32) < n_valid2
        new_m, new_acc = [], []
        for hh in range(heads):
            s = lax.dot_general(k_ref[pl.ds(k0, chunk_keys), hh * AUG:(hh + 1) * AUG], q_aug[hh],
                                (((1,), (1,)), ((), ())), preferred_element_type=F32)
            if last:
                parts = []
                for u in range(chunk_blocks):
                    other_block = jnp.full((1, MOBA_BLOCK), c * chunk_blocks + u, jnp.int32) != i
                    parts.append(jnp.where(causal | other_block, s[u * MOBA_BLOCK:(u + 1) * MOBA_BLOCK], NEG))
                s = jnp.concatenate(parts, axis=0)
            s_w[hh, pl.ds(k0, chunk_keys), :] = s
            new_m.append(jnp.maximum(m_part[hh],
                                     jnp.max(s.reshape(chunk_keys // SUBLANES, SUBLANES, MOBA_BLOCK), axis=0)))
            a = acc[hh]
            for u in range(0, chunk_keys, PV_KEYS):
                ku = k0 + u
                p = jnp.exp2(s_r[hh, pl.ds(ku, PV_KEYS), :] - m_row2[hh])
                p = jnp.where(live2, p, 0.0).astype(BF16)
                a = a + jnp.dot(vt_ref[0, hh, :, pl.ds(ku, PV_KEYS)], p, preferred_element_type=F32)
            new_acc.append(a)
        return tuple(new_m), tuple(new_acc)

    def step(n, s_w, s_r, m_w, m_r):
        m_row2 = [m_r[hh] for hh in range(heads)]
        carry = (tuple(jnp.full((SUBLANES, MOBA_BLOCK), NEG, F32) for _ in range(heads)),
                 tuple(jnp.zeros((VT_ROWS, MOBA_BLOCK), F32) for _ in range(heads)))
        for c in range(n):
            carry = both(c, carry, c == n - 1, s_w, s_r, m_row2)
        m_part, acc = carry
        for hh in range(heads):
            m_w[hh] = jnp.max(m_part[hh], axis=0, keepdims=True)
            o = acc[hh][:MOBA_HD] * (1.0 / acc[hh][MOBA_HD:MOBA_HD + 1])
            o_ref[:, hh * MOBA_HD:(hh + 1) * MOBA_HD] = o.T.astype(o_ref.dtype)

    for n in range(1, nb // chunk_blocks + 1):
        pl.when((n_chunks == n) & (t % 2 == 0))(functools.partial(step, n, s_even, s_odd, m_even, m_odd))
        pl.when((n_chunks == n) & (t % 2 == 1))(functools.partial(step, n, s_odd, s_even, m_odd, m_even))


def moba_attention(q_aug, k_aug, vt, batch):
    m = q_aug.shape[0]
    seq = m // batch
    nb = seq // MOBA_BLOCK
    hps = ATTN_HEADS_PER_STEP
    assert nb % ATTN_CHUNK_BLOCKS == 0
    return pl.pallas_call(
        functools.partial(_moba_attn_kernel, heads=hps, chunk_blocks=ATTN_CHUNK_BLOCKS, nb=nb),
        grid=(batch, MOBA_HEADS // hps, nb + 1),
        in_specs=[pl.BlockSpec((MOBA_BLOCK, hps * AUG), lambda b, g, t: (b * nb + jnp.minimum(t, nb - 1), g)),
                  pl.BlockSpec((seq, hps * AUG), lambda b, g, t: (b, g)),
                  pl.BlockSpec((1, hps, VT_ROWS, seq), lambda b, g, t: (b, g, 0, 0))],
        out_specs=pl.BlockSpec((MOBA_BLOCK, hps * MOBA_HD), lambda b, g, t: (b * nb + jnp.maximum(t - 1, 0), g)),
        out_shape=jax.ShapeDtypeStruct((m, MOBA_HEADS * MOBA_HD), BF16),
        scratch_shapes=[pltpu.VMEM((hps, seq, MOBA_BLOCK), F32), pltpu.VMEM((hps, seq, MOBA_BLOCK), F32),
                        pltpu.VMEM((hps, 1, MOBA_BLOCK), F32), pltpu.VMEM((hps, 1, MOBA_BLOCK), F32)],
        compiler_params=_params(3),
        name="moba_attn",
    )(q_aug, k_aug, vt)


def moba_layer(x, gain, w_qkv, q_gain, k_gain, batch):
    q_aug, k_aug, vt = moba_prep(x, gain, w_qkv.astype(BF16), q_gain, k_gain, batch)
    return moba_attention(q_aug, k_aug, vt, batch)


def kernel(x, norm_mix, norm_ffn, hyb_w_in, gla_w_gate2, gla_b_gate, gla_norm, ml_conv, ml_b_if, ml_norm,
           hyb_w_out, moba_w_qkv, moba_q_norm, moba_k_norm, moba_w_o, ffn_w_gate, ffn_w_up, ffn_w_down):
    batch, seq, d = x.shape
    depth = norm_mix.shape[0]
    xf = x.reshape(batch * seq, d)
    for l in range(depth):
        j = l // 2
        if l % 2 == 0:
            y = hybrid_layer(xf, norm_mix[l], hyb_w_in[j], gla_w_gate2[j], gla_b_gate[j], gla_norm[j],
                             ml_conv[j], ml_b_if[j], ml_norm[j], batch)
            w_o = hyb_w_out[j]
        else:
            y = moba_layer(xf, norm_mix[l], moba_w_qkv[j], moba_q_norm[j], moba_k_norm[j], batch)
            w_o = moba_w_o[j]
        xf = out_ffn(xf, y, w_o.astype(BF16), norm_ffn[l], ffn_w_gate[l].astype(BF16),
                     ffn_w_up[l].astype(BF16), ffn_w_down[l].astype(BF16))
    return xf.reshape(batch, seq, d)
```

```python
import functools

import jax
import jax.numpy as jnp
import numpy as np
from jax import lax
from jax.experimental import pallas as pl
from jax.experimental.pallas import tpu as pltpu

F32 = jnp.float32
BF16 = jnp.bfloat16

LANES = 128
SUBLANES = 8
VMEM_LIMIT_BYTES = 56 * 1024 * 1024

EPS = 1e-6
NEG = -1e30

GLA_HEADS = 4
GLA_DK = 64
GLA_DV = 128
GLA_RANK = 16
GLA_TAU = 16.0
ML_HEADS = 4
ML_DK = 64
ML_DV = 128
ML_CONV = 4
CHUNK = 64
SUB = 16
MOBA_HEADS = 8
MOBA_HD = 128
MOBA_BLOCK = 256
MOBA_TOPK = 3

GLA_QK = GLA_HEADS * GLA_DK
GLA_V = GLA_HEADS * GLA_DV
ML_QK = ML_HEADS * ML_DK
ML_V = ML_HEADS * ML_DV

C_GQ = 0
C_GK = C_GQ + GLA_QK
C_GV = C_GK + GLA_QK
C_GR = C_GV + GLA_V
C_MQ = C_GR + GLA_V
C_MK = C_MQ + ML_QK
C_MV = C_MK + ML_QK
C_MO = C_MV + ML_V
C_SMALL = C_MO + ML_V
HYB_COLS = C_SMALL + LANES
S_GLR = 0
S_MI = GLA_RANK
S_MF = GLA_RANK + ML_HEADS

ROW_TILE = 512
MIX_TILE = 256
FF_TILE = 256
PROJ_COL_TILE = 512


def _rms(x, gain):
    return x * lax.rsqrt(jnp.mean(x * x, axis=-1, keepdims=True) + EPS) * gain


def _log_sigmoid(z):
    return jnp.minimum(z, 0.0) - jnp.log1p(jnp.exp(-jnp.abs(z)))


def _sigmoid(z):
    return 1.0 / (1.0 + jnp.exp(-z))


def _dot(a, b):
    return jnp.dot(a.astype(BF16), b.astype(BF16), preferred_element_type=F32)


def _dot_nt(a, b):
    return lax.dot_general(a.astype(BF16), b.astype(BF16), (((1,), (1,)), ((), ())),
                           preferred_element_type=F32)


def _dot_tn(a, b):
    return lax.dot_general(a.astype(BF16), b.astype(BF16), (((0,), (0,)), ((), ())),
                           preferred_element_type=F32)


def _const_spec(shape):
    return pl.BlockSpec(shape, lambda *_: (0,) * len(shape), pipeline_mode=pl.Buffered(1))


def _params(n_grid):
    return pltpu.CompilerParams(dimension_semantics=("arbitrary",) * n_grid,
                                vmem_limit_bytes=VMEM_LIMIT_BYTES)


def _norm_proj_kernel(x_ref, g_ref, w_ref, main_ref, tail_ref, *, col_tile):
    xn = _rms(x_ref[...], g_ref[...]).astype(BF16)
    n_main = main_ref.shape[1]
    for c0 in range(0, n_main, col_tile):
        main_ref[:, c0:c0 + col_tile] = jnp.dot(xn, w_ref[:, c0:c0 + col_tile],
                                                 preferred_element_type=F32).astype(main_ref.dtype)
    tail_ref[...] = jnp.dot(xn, w_ref[:, n_main:], preferred_element_type=F32)


def norm_proj(x, gain, w, n_tail):
    m, d = x.shape
    n = w.shape[1]
    n_main = n - n_tail
    tm = min(ROW_TILE, m)
    assert n_main % PROJ_COL_TILE == 0
    return pl.pallas_call(
        functools.partial(_norm_proj_kernel, col_tile=PROJ_COL_TILE),
        grid=(m // tm,),
        in_specs=[pl.BlockSpec((tm, d), lambda i: (i, 0)),
                  _const_spec((1, d)),
                  _const_spec((d, n))],
        out_specs=[pl.BlockSpec((tm, n_main), lambda i: (i, 0)),
                   pl.BlockSpec((tm, n_tail), lambda i: (i, 0))],
        out_shape=[jax.ShapeDtypeStruct((m, n_main), BF16),
                   jax.ShapeDtypeStruct((m, n_tail), F32)],
        compiler_params=_params(1),
        name="norm_proj",
    )(x, gain.reshape(1, d), w)


def _out_ffn_kernel(x_ref, y_ref, wo_ref, g_ref, wg_ref, wu_ref, wd_ref, o_ref, h_ref, *, ff_tile):
    x1 = x_ref[...] + jnp.dot(y_ref[...], wo_ref[...], preferred_element_type=F32)
    xn = _rms(x1, g_ref[...]).astype(BF16)
    d_ff = wg_ref.shape[1]
    for c0 in range(0, d_ff, ff_tile):
        g = jnp.dot(xn, wg_ref[:, c0:c0 + ff_tile], preferred_element_type=F32)
        u = jnp.dot(xn, wu_ref[:, c0:c0 + ff_tile], preferred_element_type=F32)
        h_ref[:, c0:c0 + ff_tile] = (g * _sigmoid(g) * u).astype(BF16)
    o_ref[...] = x1 + jnp.dot(h_ref[...], wd_ref[...], preferred_element_type=F32)


def out_ffn(x, y, w_o, gain, w_gate, w_up, w_down):
    m, d = x.shape
    dy = y.shape[1]
    d_ff = w_gate.shape[1]
    tm = min(ROW_TILE, m)
    return pl.pallas_call(
        functools.partial(_out_ffn_kernel, ff_tile=FF_TILE),
        grid=(m // tm,),
        in_specs=[pl.BlockSpec((tm, d), lambda i: (i, 0)),
                  pl.BlockSpec((tm, dy), lambda i: (i, 0)),
                  _const_spec((dy, d)),
                  _const_spec((1, d)),
                  _const_spec((d, d_ff)),
                  _const_spec((d, d_ff)),
                  _const_spec((d_ff, d))],
        out_specs=pl.BlockSpec((tm, d), lambda i: (i, 0)),
        out_shape=jax.ShapeDtypeStruct((m, d), F32),
        scratch_shapes=[pltpu.VMEM((tm, d_ff), BF16)],
        compiler_params=_params(1),
        name="out_ffn",
    )(x, y, w_o, gain.reshape(1, d), w_gate, w_up, w_down)


def _split3_rows(x):
    hi = x.astype(BF16)
    r1 = x - hi.astype(F32)
    mid = r1.astype(BF16)
    lo = (r1 - mid.astype(F32)).astype(BF16)
    return jnp.concatenate([hi, mid, lo], axis=0)


def _split3_lanes(x):
    hi = x.astype(BF16)
    r1 = x - hi.astype(F32)
    mid = r1.astype(BF16)
    lo = (r1 - mid.astype(F32)).astype(BF16)
    return jnp.concatenate([hi, mid, lo], axis=1)


def _blocks_diag(tiles, zero):
    n = len(tiles)
    return jnp.concatenate(
        [jnp.concatenate([tiles[a] if a == b else zero for b in range(n)], axis=1) for a in range(n)], axis=0)


def _hybrid_kernel(proj_ref, small_ref, gates_ref, wg2_ref, bg_ref, gn_ref, conv_ref, bsmall_ref, brow_ref, mn_ref,
                   y_ref, gla_st, ml_cs, ml_ns, ml_m, conv_carry):
    t_tile = proj_ref.shape[0]
    nc = t_tile // CHUNK
    n_sub = CHUNK // SUB

    @pl.when(pl.program_id(1) == 0)
    def _():
        gla_st[...] = jnp.zeros_like(gla_st)
        ml_cs[...] = jnp.zeros_like(ml_cs)
        ml_ns[...] = jnp.zeros_like(ml_ns)
        ml_m[...] = jnp.zeros_like(ml_m)
        conv_carry[...] = jnp.zeros_like(conv_carry)

    xcat = proj_ref[:, C_MQ:C_MQ + 2 * ML_QK].astype(F32)
    ext = jnp.concatenate([conv_carry[...], xcat], axis=0)
    cw = conv_ref[...]
    acc = ext * cw[ML_CONV - 1:ML_CONV]
    for j in range(1, ML_CONV):
        acc = acc + pltpu.roll(ext, j, axis=0) * cw[ML_CONV - 1 - j:ML_CONV - j]
    conv = acc[SUBLANES:]
    qk = conv * _sigmoid(conv)
    conv_carry[...] = xcat[t_tile - SUBLANES:]

    lane = lax.broadcasted_iota(jnp.int32, (1, LANES), 1)
    head_mask = (lane < GLA_DK, lane >= GLA_DK)
    hm0 = head_mask[0]
    r_t = lax.broadcasted_iota(jnp.int32, (t_tile, t_tile), 0)
    c_t = lax.broadcasted_iota(jnp.int32, (t_tile, t_tile), 1)
    tril_bd = ((r_t >= c_t) & (r_t // CHUNK == c_t // CHUNK)).astype(BF16)
    tril3 = jnp.concatenate([tril_bd, tril_bd, tril_bd], axis=1)
    r_l = lax.broadcasted_iota(jnp.int32, (LANES, LANES), 0)
    c_l = lax.broadcasted_iota(jnp.int32, (LANES, LANES), 1)
    triu_bd = ((r_l <= c_l) & (r_l // CHUNK == c_l // CHUNK)).astype(BF16)
    triu3 = jnp.concatenate([triu_bd, triu_bd, triu_bd], axis=0)
    row_c = lax.broadcasted_iota(jnp.int32, (CHUNK, LANES), 0)
    causal2 = (lax.broadcasted_iota(jnp.int32, (CHUNK, LANES), 1) % CHUNK) <= row_c
    sub_id = (lax.broadcasted_iota(jnp.int32, (t_tile, LANES), 0) % CHUNK) // SUB
    row_sub = lax.broadcasted_iota(jnp.int32, (2 * SUBLANES, LANES), 0)
    zero_tile = jnp.zeros((CHUNK, LANES), F32)
    zero_v = jnp.zeros((CHUNK, LANES), BF16)

    def chunk_rows(x, c):
        return x[c * CHUNK:(c + 1) * CHUNK]

    def diag_blocks(r):
        return [r[c * CHUNK:(c + 1) * CHUNK, c * LANES:(c + 1) * LANES] for c in range(nc)]

    small = small_ref[...]
    smallb = small + bsmall_ref[...]

    z = _dot(small, wg2_ref[...]) + bg_ref[...]
    log_a = _log_sigmoid(z) * (1.0 / GLA_TAU)
    cum = jnp.dot(tril3, _split3_rows(jnp.concatenate([log_a, _log_sigmoid(smallb)], axis=1)),
                  preferred_element_type=F32)
    g_all = gates_ref[0, 0] + brow_ref[...]
    bc_rows = jnp.dot(_split3_lanes(_log_sigmoid(g_all)), triu3, preferred_element_type=F32)

    for p in range(GLA_HEADS // 2):
        q2 = proj_ref[:, C_GQ + p * LANES:C_GQ + (p + 1) * LANES].astype(F32) * (GLA_DK ** -0.5)
        k2 = proj_ref[:, C_GK + p * LANES:C_GK + (p + 1) * LANES].astype(F32)
        bc = cum[:, p * LANES:(p + 1) * LANES]
        v = [proj_ref[:, C_GV + (2 * p + hl) * GLA_DV:C_GV + (2 * p + hl + 1) * GLA_DV] for hl in range(2)]

        ref_pt = [[chunk_rows(bc, c)[i * SUB - 1:i * SUB] if i else jnp.zeros((1, LANES), F32)
                   for i in range(n_sub)] for c in range(nc)]
        r_own = jnp.concatenate([jnp.broadcast_to(ref_pt[c][i], (SUB, LANES))
                                 for c in range(nc) for i in range(n_sub)], axis=0)
        qs = q2 * jnp.exp(bc - r_own)
        q_hat = jnp.concatenate([jnp.where(sub_id == i, qs, 0.0) for i in range(n_sub)], axis=1)
        k_hat, k_dec, last = [], [], []
        for c in range(nc):
            kc, bcc = chunk_rows(k2, c), chunk_rows(bc, c)
            ke = [kc * jnp.exp(ref_pt[c][i] - bcc) for i in range(n_sub)]
            for hl in range(2):
                k_hat.append(jnp.concatenate([jnp.where(head_mask[hl], ke[i], 0.0) for i in range(n_sub)], axis=1))
            last.append(bcc[CHUNK - 1:CHUNK])
            k_dec.append(kc * jnp.exp(last[c] - bcc))
        k_hat = jnp.concatenate(k_hat, axis=0)
        sc = diag_blocks(_dot_nt(q_hat, k_hat))
        sc = [jnp.where(causal2, s, 0.0) for s in sc]
        v_bd = jnp.concatenate([jnp.concatenate([chunk_rows(v[hl], c) if hl == b else zero_v for b in range(2)],
                                                axis=1) for c in range(nc) for hl in range(2)], axis=0)
        o_intra = _dot(_blocks_diag(sc, zero_tile), v_bd)
        u_all = _dot_tn(jnp.concatenate(v, axis=1), _blocks_diag(k_dec, zero_tile))

        st = gla_st[p]
        before = []
        for c in range(nc):
            before.append(st)
            u = jnp.where(hm0, u_all[:GLA_DV, c * LANES:(c + 1) * LANES], u_all[GLA_DV:, c * LANES:(c + 1) * LANES])
            st = st * jnp.exp(last[c]) + u
        gla_st[p] = st
        st_stack = jnp.concatenate(before, axis=0)
        q_in = q2 * jnp.exp(bc)
        for hl in range(2):
            h = 2 * p + hl
            o_inter = jnp.concatenate(diag_blocks(_dot_nt(jnp.where(head_mask[hl], q_in, 0.0), st_stack)), axis=0)
            o = o_intra[:, hl * GLA_DV:(hl + 1) * GLA_DV] + o_inter
            gate = proj_ref[:, C_GR + h * GLA_DV:C_GR + (h + 1) * GLA_DV].astype(F32)
            yg = _rms(o, gn_ref[:, h * GLA_DV:(h + 1) * GLA_DV]) * (gate * _sigmoid(gate))
            y_ref[:, h * GLA_DV:(h + 1) * GLA_DV] = yg.astype(y_ref.dtype)

    ones_tile = jnp.ones((t_tile, LANES), BF16)
    for p in range(ML_HEADS // 2):
        mq2 = qk[:, p * LANES:(p + 1) * LANES]
        mk2 = qk[:, ML_QK + p * LANES:ML_QK + (p + 1) * LANES] * (ML_DK ** -0.5)
        v = [proj_ref[:, C_MV + (2 * p + hl) * ML_DV:C_MV + (2 * p + hl + 1) * ML_DV] for hl in range(2)]
        k_stack = jnp.concatenate([jnp.where(head_mask[hl], chunk_rows(mk2, c), 0.0)
                                   for c in range(nc) for hl in range(2)], axis=0)
        qk_blocks = diag_blocks(_dot_nt(mq2, k_stack))

        w_blocks, wk_blocks, cm_cols, b_max, bc_last = [], [], [], [], []
        for c in range(nc):
            bcr = bc_rows[ML_HEADS * c + 2 + p:ML_HEADS * c + 3 + p]
            b_pair = g_all[ML_HEADS * c + p:ML_HEADS * c + p + 1] - bcr
            bm = jnp.where(causal2, b_pair, -jnp.inf)
            cm = [jnp.max(jnp.where(head_mask[hl], bm, -jnp.inf), axis=1, keepdims=True) for hl in range(2)]
            w_blocks.append(qk_blocks[c] * jnp.exp(bm - jnp.where(hm0, cm[0], cm[1])))
            cm_cols.append(cm)
            bmx = [jnp.max(jnp.where(head_mask[hl], b_pair, -jnp.inf), axis=1, keepdims=True) for hl in range(2)]
            b_max.append(jnp.where(hm0, bmx[0], bmx[1]))
            bc_last.append(jnp.where(hm0, bcr[:, CHUNK - 1:CHUNK], bcr[:, LANES - 1:LANES]))
            e_col = []
            for hl in range(2):
                h = 2 * p + hl
                b_col = (chunk_rows(smallb, c)[:, S_MI + h:S_MI + h + 1]
                         - chunk_rows(cum, c)[:, GLA_QK + S_MF + h:GLA_QK + S_MF + h + 1])
                e_col.append(jnp.exp(b_col - bmx[hl]))
            wk_blocks.append(chunk_rows(mk2, c) * jnp.where(hm0, e_col[0], e_col[1]))

        ones_col = [jnp.broadcast_to((lane == hl).astype(BF16), (CHUNK, LANES)) for hl in range(2)]
        v_bd = jnp.concatenate([jnp.concatenate([chunk_rows(v[hl], c) if hl == b else zero_v for b in range(2)]
                                                + [ones_col[hl]], axis=1)
                                for c in range(nc) for hl in range(2)], axis=0)
        nv = _dot(_blocks_diag(w_blocks, zero_tile), v_bd)
        u_all = _dot_tn(jnp.concatenate(v + [ones_tile], axis=1), _blocks_diag(wk_blocks, zero_tile))

        cs, ns, m_run = ml_cs[p], ml_ns[p], ml_m[p]
        before = []
        for c in range(nc):
            before.append((cs, ns, m_run))
            m_top = jnp.maximum(m_run, b_max[c])
            keep = jnp.exp(m_run - m_top)
            gain = jnp.exp(b_max[c] - m_top)
            blk = slice(c * LANES, (c + 1) * LANES)
            cs = keep * cs + gain * jnp.where(hm0, u_all[:ML_DV, blk], u_all[ML_DV:2 * ML_DV, blk])
            ns = keep * ns + gain * u_all[2 * ML_DV:2 * ML_DV + 1, blk]
            m_run = bc_last[c] + m_top
        ml_cs[p], ml_ns[p], ml_m[p] = cs, ns, m_run
        ns_rows = jnp.zeros((2 * SUBLANES, LANES), F32)
        for c in range(nc):
            ns_rows = jnp.where(row_sub == c, before[c][1], ns_rows)
        state_stack = jnp.concatenate([b[0] for b in before] + [ns_rows], axis=0)

        for hl in range(2):
            h = 2 * p + hl
            rr = _dot_nt(jnp.where(head_mask[hl], mq2, 0.0), state_stack)
            q_cs = diag_blocks(rr)
            h_chunks = []
            for c in range(nc):
                rows = slice(c * CHUNK, (c + 1) * CHUNK)
                q_ns = rr[rows, nc * ML_DV + c:nc * ML_DV + c + 1]
                m_prev = before[c][2][:, hl * ML_DK:hl * ML_DK + 1]
                cm = cm_cols[c][hl]
                m_top = jnp.maximum(m_prev, cm)
                local = jnp.exp(cm - m_top)
                carried = jnp.exp(m_prev - m_top)
                num = local * nv[rows, hl * ML_DV:(hl + 1) * ML_DV] + carried * q_cs[c]
                den = local * nv[rows, 2 * ML_DV + hl:2 * ML_DV + hl + 1] + carried * q_ns
                bc_col = chunk_rows(cum, c)[:, GLA_QK + S_MF + h:GLA_QK + S_MF + h + 1]
                h_chunks.append(num / jnp.maximum(jnp.abs(den), jnp.exp(-(bc_col + m_top))))
            og = proj_ref[:, C_MO + h * ML_DV:C_MO + (h + 1) * ML_DV].astype(F32)
            ym = _rms(jnp.concatenate(h_chunks, axis=0) * _sigmoid(og), mn_ref[:, h * ML_DV:(h + 1) * ML_DV])
            y_ref[:, GLA_V + h * ML_DV:GLA_V + (h + 1) * ML_DV] = ym.astype(y_ref.dtype)


def hybrid_core(proj, small, gates_t, w_gate2, b_gate, gla_norm, conv_w, b_small, b_rows, ml_norm, batch):
    m = proj.shape[0]
    seq = m // batch
    t = min(MIX_TILE, seq)
    nt = seq // t
    assert 2 * SUBLANES >= t // CHUNK
    return pl.pallas_call(
        _hybrid_kernel,
        grid=(batch, nt),
        in_specs=[pl.BlockSpec((t, C_SMALL), lambda b, i: (b * nt + i, 0)),
                  pl.BlockSpec((t, LANES), lambda b, i: (b * nt + i, 0)),
                  pl.BlockSpec((1, 1, ML_HEADS * (t // CHUNK), LANES), lambda b, i: (b, i, 0, 0)),
                  _const_spec((LANES, GLA_QK)),
                  _const_spec((1, GLA_QK)),
                  _const_spec((1, GLA_V)),
                  _const_spec((ML_CONV, 2 * ML_QK)),
                  _const_spec((1, LANES)),
                  _const_spec((ML_HEADS * (t // CHUNK), LANES)),
                  _const_spec((1, ML_V))],
        out_specs=pl.BlockSpec((t, GLA_V + ML_V), lambda b, i: (b * nt + i, 0)),
        out_shape=jax.ShapeDtypeStruct((m, GLA_V + ML_V), BF16),
        scratch_shapes=[pltpu.VMEM((GLA_HEADS // 2, GLA_DV, LANES), F32),
                        pltpu.VMEM((ML_HEADS // 2, ML_DV, LANES), F32),
                        pltpu.VMEM((ML_HEADS // 2, 1, LANES), F32),
                        pltpu.VMEM((ML_HEADS // 2, 1, LANES), F32),
                        pltpu.VMEM((SUBLANES, 2 * ML_QK), F32)],
        compiler_params=_params(2),
        name="hybrid_core",
    )(proj, small, gates_t, w_gate2, b_gate, gla_norm, conv_w, b_small, b_rows, ml_norm)


def hybrid_layer(x, gain, w_in, w_gate2, b_gate, gla_norm, conv_w, b_if, ml_norm, batch):
    m, d = x.shape
    seq = m // batch
    segs = np.cumsum([0, GLA_QK, GLA_QK, GLA_V, GLA_V, GLA_RANK, ML_QK, ML_QK, ML_V, ML_V, ML_HEADS, ML_HEADS])
    seg = lambda i: w_in[:, int(segs[i]):int(segs[i + 1])]
    pad = jnp.zeros((d, LANES - GLA_RANK - 2 * ML_HEADS), w_in.dtype)
    w_perm = jnp.concatenate([seg(0), seg(1), seg(2), seg(3), seg(5), seg(6), seg(7), seg(8),
                              seg(4), seg(9), seg(10), pad], axis=1).astype(BF16)
    proj, small = norm_proj(x, gain, w_perm, LANES)
    gates = small[:, S_MI:S_MI + 2 * ML_HEADS]
    gates_t = gates.reshape(batch, seq // CHUNK, CHUNK, 2 * ML_HEADS).transpose(0, 1, 3, 2)
    t = min(MIX_TILE, seq)
    gates_t = gates_t.reshape(batch, seq // t, ML_HEADS * (t // CHUNK), LANES)
    b_small = jnp.zeros((1, LANES), F32).at[0, S_MI:S_MI + 2 * ML_HEADS].set(b_if)
    b_rows = jnp.tile(jnp.repeat(b_if, CHUNK).reshape(ML_HEADS, LANES), (t // CHUNK, 1))
    w_gate2_rows = jnp.zeros((LANES, GLA_QK), F32).at[S_GLR:S_GLR + GLA_RANK].set(w_gate2)
    return hybrid_core(proj, small, gates_t, w_gate2_rows, b_gate.reshape(1, -1), gla_norm.reshape(1, -1), conv_w,
                       b_small, b_rows, ml_norm.reshape(1, -1), batch)


LOG2E = 1.4426950408889634
AUG = 2 * MOBA_HD
LANES_PER_BLOCK = 4
MASK_BIG = 2.0 ** 100
PROJ_HEADS = 4
VT_ROWS = MOBA_HD + 16


def _split3(x):
    x = np.asarray(x, np.float32)
    hi = x.astype(BF16).astype(np.float32)
    mid = (x - hi).astype(BF16).astype(np.float32)
    lo = (x - hi - mid).astype(BF16).astype(np.float32)
    return np.stack([hi, mid, lo], axis=-1)


def _aug_tables(nb):
    c = np.asarray([LOG2E * 2.0 ** (-8.0 * (h + 1) / MOBA_HEADS) for h in range(MOBA_HEADS)], np.float32)
    pos = np.arange(MOBA_BLOCK, dtype=np.float32)
    base = LANES_PER_BLOCK * nb
    q_rows = np.zeros((MOBA_HEADS, MOBA_BLOCK, LANES), np.float32)
    k_rows = np.zeros((MOBA_HEADS, MOBA_BLOCK, LANES), np.float32)
    q_rows[:, :, base:base + 3] = _split3(-c[:, None] * pos[None, :])
    q_rows[:, :, base + 3:base + 6] = 1.0
    k_rows[:, :, base:base + 3] = 1.0
    k_rows[:, :, base + 3:base + 6] = _split3(c[:, None] * pos[None, :])
    i_idx = np.arange(nb)[:, None]
    j_idx = np.arange(nb)[None, :]
    steps = np.zeros((MOBA_HEADS, nb, nb, LANES_PER_BLOCK), np.float32)
    steps[..., 0] = np.where(j_idx == i_idx, 0.0, -MASK_BIG)
    gap = _split3(-c[:, None, None] * (MOBA_BLOCK * (i_idx - j_idx)).astype(np.float32)[None])
    steps[..., 1:] = np.where((j_idx <= i_idx)[None, :, :, None], gap, 0.0)
    q_steps = np.zeros((MOBA_HEADS, nb, 1, LANES), np.float32)
    q_steps[:, :, 0, :base] = steps.reshape(MOBA_HEADS, nb, base)
    k_steps = np.zeros((nb, 1, LANES), np.float32)
    k_steps[:, 0, :base] = np.repeat(np.eye(nb, dtype=np.float32), LANES_PER_BLOCK, axis=1)
    expand = np.zeros((nb, LANES), np.float32)
    expand[np.arange(nb), LANES_PER_BLOCK * np.arange(nb)] = MASK_BIG
    return (jnp.asarray(q_rows), jnp.asarray(q_steps), jnp.asarray(k_rows), jnp.asarray(k_steps),
            jnp.asarray(expand, BF16))


def _moba_prep_kernel(x_ref, g_ref, w_ref, qg_ref, kg_ref, qrow_ref, qstep_ref, krow_ref, kstep_ref, exp_ref,
                      qa_ref, ka_ref, vt_ref, kmean_ref):
    i = pl.program_id(1)
    nb = kmean_ref.shape[1]
    d = MOBA_HEADS * MOBA_HD

    @pl.when(i == 0)
    def _():
        kmean_ref[...] = jnp.zeros_like(kmean_ref)

    xn = _rms(x_ref[...], g_ref[...]).astype(BF16)
    blk = lax.broadcasted_iota(jnp.int32, (nb, MOBA_BLOCK), 0)
    past = blk < i
    ones_rows = (lax.broadcasted_iota(jnp.int32, (VT_ROWS - MOBA_HD, MOBA_BLOCK), 0) == 0).astype(F32)
    k_step = kstep_ref[i]

    for h in range(MOBA_HEADS):
        hg = h % PROJ_HEADS
        if hg == 0:
            c0 = h * MOBA_HD
            q_grp, k_grp, v_grp = (jnp.dot(xn, w_ref[:, off + c0:off + c0 + PROJ_HEADS * MOBA_HD],
                                           preferred_element_type=F32) for off in (0, d, 2 * d))
        part = slice(hg * MOBA_HD, (hg + 1) * MOBA_HD)
        q = _rms(q_grp[:, part], qg_ref[...]) * (MOBA_HD ** -0.5)
        k = _rms(k_grp[:, part], kg_ref[...])
        v = v_grp[:, part]
        vt_ref[0, h] = jnp.concatenate([v.T, ones_rows], axis=0).astype(vt_ref.dtype)

        gate = _dot_nt(kmean_ref[h], q)
        gate = jnp.where(past, gate, -jnp.inf)
        rank = jnp.zeros((nb, MOBA_BLOCK), F32)
        for j in range(nb):
            other = gate[j:j + 1]
            ahead = (other > gate) | ((other == gate) & (j < blk))
            rank = rank + ahead.astype(F32)
        sel = (past & (rank < MOBA_TOPK)).astype(BF16)
        q_extra = (qrow_ref[h] + qstep_ref[h, i]) + _dot_tn(sel, exp_ref[...])
        qa_ref[:, h * AUG:h * AUG + MOBA_HD] = (q * LOG2E).astype(qa_ref.dtype)
        qa_ref[:, h * AUG + MOBA_HD:(h + 1) * AUG] = q_extra.astype(qa_ref.dtype)
        ka_ref[:, h * AUG:h * AUG + MOBA_HD] = k.astype(ka_ref.dtype)
        ka_ref[:, h * AUG + MOBA_HD:(h + 1) * AUG] = (krow_ref[h] + k_step).astype(ka_ref.dtype)

        kmean_ref[h, pl.ds(i, 1), :] = jnp.mean(k, axis=0, keepdims=True)


def moba_prep(x, gain, w_qkv, q_gain, k_gain, batch):
    m, dm = x.shape
    seq = m // batch
    nb = seq // MOBA_BLOCK
    assert LANES_PER_BLOCK * nb + 6 <= LANES
    d = MOBA_HEADS * MOBA_HD
    q_rows, q_steps, k_rows, k_steps, expand = _aug_tables(nb)
    return pl.pallas_call(
        _moba_prep_kernel,
        grid=(batch, nb),
        in_specs=[pl.BlockSpec((MOBA_BLOCK, dm), lambda b, i: (b * nb + i, 0)),
                  _const_spec((1, dm)),
                  _const_spec((dm, 3 * d)),
                  _const_spec((1, MOBA_HD)),
                  _const_spec((1, MOBA_HD)),
                  _const_spec((MOBA_HEADS, MOBA_BLOCK, LANES)),
                  _const_spec((MOBA_HEADS, nb, 1, LANES)),
                  _const_spec((MOBA_HEADS, MOBA_BLOCK, LANES)),
                  _const_spec((nb, 1, LANES)),
                  _const_spec((nb, LANES))],
        out_specs=[pl.BlockSpec((MOBA_BLOCK, MOBA_HEADS * AUG), lambda b, i: (b * nb + i, 0)),
                   pl.BlockSpec((MOBA_BLOCK, MOBA_HEADS * AUG), lambda b, i: (b * nb + i, 0)),
                   pl.BlockSpec((1, MOBA_HEADS, VT_ROWS, MOBA_BLOCK), lambda b, i: (b, 0, 0, i))],
        out_shape=[jax.ShapeDtypeStruct((m, MOBA_HEADS * AUG), BF16),
                   jax.ShapeDtypeStruct((m, MOBA_HEADS * AUG), BF16),
                   jax.ShapeDtypeStruct((batch, MOBA_HEADS, VT_ROWS, seq), BF16)],
        scratch_shapes=[pltpu.VMEM((MOBA_HEADS, nb, MOBA_HD), F32)],
        compiler_params=_params(2),
        name="moba_prep",
    )(x, gain.reshape(1, dm), w_qkv, q_gain.reshape(1, -1), k_gain.reshape(1, -1),
      q_rows, q_steps, k_rows, k_steps, expand)


ATTN_HEADS_PER_STEP = 2
ATTN_CHUNK_BLOCKS = 2
PV_KEYS = 256


def _moba_attn_kernel(q_ref, k_ref, vt_ref, o_ref, s_even, s_odd, m_even, m_odd, *, heads, chunk_blocks, nb):
    t = pl.program_id(2)
    i = jnp.minimum(t, nb - 1)
    chunk_keys = chunk_blocks * MOBA_BLOCK
    n_chunks = i // chunk_blocks + 1
    n_valid2 = jnp.where(t > 0, (t - 1) // chunk_blocks + 1, 0)

    @pl.when((pl.program_id(0) == 0) & (pl.program_id(1) == 0) & (t == 0))
    def _():
        s_even[...] = jnp.zeros_like(s_even)
        s_odd[...] = jnp.zeros_like(s_odd)
        m_odd[...] = jnp.zeros_like(m_odd)

    key_r = lax.broadcasted_iota(jnp.int32, (MOBA_BLOCK, MOBA_BLOCK), 0)
    qry_c = lax.broadcasted_iota(jnp.int32, (MOBA_BLOCK, MOBA_BLOCK), 1)
    causal = qry_c >= key_r
    q_aug = [q_ref[:, hh * AUG:(hh + 1) * AUG] for hh in range(heads)]

    def both(c, carry, last, s_w, s_r, m_row2):
        m_part, acc = carry
        k0 = c * chunk_keys
        live2 = jnp.full((1, MOBA_BLOCK), c, jnp.int32) < n_valid2
        new_m, new_acc = [], []
        for hh in range(heads):
            s = lax.dot_general(k_ref[pl.ds(k0, chunk_keys), hh * AUG:(hh + 1) * AUG], q_aug[hh],
                                (((1,), (1,)), ((), ())), preferred_element_type=F32)
            if last:
                parts = []
                for u in range(chunk_blocks):
                    other_block = jnp.full((1, MOBA_BLOCK), c * chunk_blocks + u, jnp.int32) != i
                    parts.append(jnp.where(causal | other_block, s[u * MOBA_BLOCK:(u + 1) * MOBA_BLOCK], NEG))
                s = jnp.concatenate(parts, axis=0)
            s_w[hh, pl.ds(k0, chunk_keys), :] = s
            new_m.append(jnp.maximum(m_part[hh],
                                     jnp.max(s.reshape(chunk_keys // SUBLANES, SUBLANES, MOBA_BLOCK), axis=0)))
            a = acc[hh]
            for u in range(0, chunk_keys, PV_KEYS):
                ku = k0 + u
                p = jnp.exp2(s_r[hh, pl.ds(ku, PV_KEYS), :] - m_row2[hh])
                p = jnp.where(live2, p, 0.0).astype(BF16)
                a = a + jnp.dot(vt_ref[0, hh, :, pl.ds(ku, PV_KEYS)], p, preferred_element_type=F32)
            new_acc.append(a)
        return tuple(new_m), tuple(new_acc)

    def step(n, s_w, s_r, m_w, m_r):
        m_row2 = [m_r[hh] for hh in range(heads)]
        carry = (tuple(jnp.full((SUBLANES, MOBA_BLOCK), NEG, F32) for _ in range(heads)),
                 tuple(jnp.zeros((VT_ROWS, MOBA_BLOCK), F32) for _ in range(heads)))
        for c in range(n):
            carry = both(c, carry, c == n - 1, s_w, s_r, m_row2)
        m_part, acc = carry
        for hh in range(heads):
            m_w[hh] = jnp.max(m_part[hh], axis=0, keepdims=True)
            o = acc[hh][:MOBA_HD] * (1.0 / acc[hh][MOBA_HD:MOBA_HD + 1])
            o_ref[:, hh * MOBA_HD:(hh + 1) * MOBA_HD] = o.T.astype(o_ref.dtype)

    for n in range(1, nb // chunk_blocks + 1):
        pl.when((n_chunks == n) & (t % 2 == 0))(functools.partial(step, n, s_even, s_odd, m_even, m_odd))
        pl.when((n_chunks == n) & (t % 2 == 1))(functools.partial(step, n, s_odd, s_even, m_odd, m_even))


def moba_attention(q_aug, k_aug, vt, batch):
    m = q_aug.shape[0]
    seq = m // batch
    nb = seq // MOBA_BLOCK
    hps = ATTN_HEADS_PER_STEP
    assert nb % ATTN_CHUNK_BLOCKS == 0
    return pl.pallas_call(
        functools.partial(_moba_attn_kernel, heads=hps, chunk_blocks=ATTN_CHUNK_BLOCKS, nb=nb),
        grid=(batch, MOBA_HEADS // hps, nb + 1),
        in_specs=[pl.BlockSpec((MOBA_BLOCK, hps * AUG), lambda b, g, t: (b * nb + jnp.minimum(t, nb - 1), g)),
                  pl.BlockSpec((seq, hps * AUG), lambda b, g, t: (b, g)),
                  pl.BlockSpec((1, hps, VT_ROWS, seq), lambda b, g, t: (b, g, 0, 0))],
        out_specs=pl.BlockSpec((MOBA_BLOCK, hps * MOBA_HD), lambda b, g, t: (b * nb + jnp.maximum(t - 1, 0), g)),
        out_shape=jax.ShapeDtypeStruct((m, MOBA_HEADS * MOBA_HD), BF16),
        scratch_shapes=[pltpu.VMEM((hps, seq, MOBA_BLOCK), F32), pltpu.VMEM((hps, seq, MOBA_BLOCK), F32),
                        pltpu.VMEM((hps, 1, MOBA_BLOCK), F32), pltpu.VMEM((hps, 1, MOBA_BLOCK), F32)],
        compiler_params=_params(3),
        name="moba_attn",
    )(q_aug, k_aug, vt)


def moba_layer(x, gain, w_qkv, q_gain, k_gain, batch):
    q_aug, k_aug, vt = moba_prep(x, gain, w_qkv.astype(BF16), q_gain, k_gain, batch)
    return moba_attention(q_aug, k_aug, vt, batch)


def kernel(x, norm_mix, norm_ffn, hyb_w_in, gla_w_gate2, gla_b_gate, gla_norm, ml_conv, ml_b_if, ml_norm,
           hyb_w_out, moba_w_qkv, moba_q_norm, moba_k_norm, moba_w_o, ffn_w_gate, ffn_w_up, ffn_w_down):
    batch, seq, d = x.shape
    depth = norm_mix.shape[0]
    xf = x.reshape(batch * seq, d)
    for l in range(depth):
        j = l // 2
        if l % 2 == 0:
            y = hybrid_layer(xf, norm_mix[l], hyb_w_in[j], gla_w_gate2[j], gla_b_gate[j], gla_norm[j],
                             ml_conv[j], ml_b_if[j], ml_norm[j], batch)
            w_o = hyb_w_out[j]
        else:
            y = moba_layer(xf, norm_mix[l], moba_w_qkv[j], moba_q_norm[j], moba_k_norm[j], batch)
            w_o = moba_w_o[j]
        xf = out_ffn(xf, y, w_o.astype(BF16), norm_ffn[l], ffn_w_gate[l].astype(BF16),
                     ffn_w_up[l].astype(BF16), ffn_w_down[l].astype(BF16))
    return xf.reshape(batch, seq, d)
```

```python
import functools

import jax
import jax.numpy as jnp
import numpy as np
from jax import lax
from jax.experimental import pallas as pl
from jax.experimental.pallas import tpu as pltpu

F32 = jnp.float32
BF16 = jnp.bfloat16

LANES = 128
SUBLANES = 8
VMEM_LIMIT_BYTES = 56 * 1024 * 1024

EPS = 1e-6
NEG = -1e30

GLA_HEADS = 4
GLA_DK = 64
GLA_DV = 128
GLA_RANK = 16
GLA_TAU = 16.0
ML_HEADS = 4
ML_DK = 64
ML_DV = 128
ML_CONV = 4
CHUNK = 64
SUB = 16
MOBA_HEADS = 8
MOBA_HD = 128
MOBA_BLOCK = 256
MOBA_TOPK = 3

GLA_QK = GLA_HEADS * GLA_DK
GLA_V = GLA_HEADS * GLA_DV
ML_QK = ML_HEADS * ML_DK
ML_V = ML_HEADS * ML_DV

C_GQ = 0
C_GK = C_GQ + GLA_QK
C_GV = C_GK + GLA_QK
C_GR = C_GV + GLA_V
C_MQ = C_GR + GLA_V
C_MK = C_MQ + ML_QK
C_MV = C_MK + ML_QK
C_MO = C_MV + ML_V
C_SMALL = C_MO + ML_V
HYB_COLS = C_SMALL + LANES
S_GLR = 0
S_MI = GLA_RANK
S_MF = GLA_RANK + ML_HEADS

ROW_TILE = 512
MIX_TILE = 256
FF_TILE = 256
PROJ_COL_TILE = 512


def _rms(x, gain):
    return x * lax.rsqrt(jnp.mean(x * x, axis=-1, keepdims=True) + EPS) * gain


def _log_sigmoid(z):
    return jnp.minimum(z, 0.0) - jnp.log1p(jnp.exp(-jnp.abs(z)))


def _sigmoid(z):
    return 1.0 / (1.0 + jnp.exp(-z))


def _dot(a, b):
    return jnp.dot(a.astype(BF16), b.astype(BF16), preferred_element_type=F32)


def _dot_nt(a, b):
    return lax.dot_general(a.astype(BF16), b.astype(BF16), (((1,), (1,)), ((), ())),
                           preferred_element_type=F32)


def _dot_tn(a, b):
    return lax.dot_general(a.astype(BF16), b.astype(BF16), (((0,), (0,)), ((), ())),
                           preferred_element_type=F32)


def _const_spec(shape):
    return pl.BlockSpec(shape, lambda *_: (0,) * len(shape), pipeline_mode=pl.Buffered(1))


def _layer_spec(shape, layer):
    return pl.BlockSpec((None,) + tuple(shape), lambda *_: (layer,) + (0,) * len(shape),
                        pipeline_mode=pl.Buffered(1))


def _params(n_grid):
    return pltpu.CompilerParams(dimension_semantics=("arbitrary",) * n_grid,
                                vmem_limit_bytes=VMEM_LIMIT_BYTES)


def _norm_proj_kernel(x_ref, g_ref, w_ref, main_ref, tail_ref, *, col_tile):
    xn = _rms(x_ref[...], g_ref[...]).astype(BF16)
    n_main = main_ref.shape[1]
    for c0 in range(0, n_main, col_tile):
        main_ref[:, c0:c0 + col_tile] = jnp.dot(xn, w_ref[:, c0:c0 + col_tile],
                                                 preferred_element_type=F32).astype(main_ref.dtype)
    tail_ref[...] = jnp.dot(xn, w_ref[:, n_main:], preferred_element_type=F32)


def norm_proj(x, gain, w, layer, n_tail):
    m, d = x.shape
    n = w.shape[2]
    n_main = n - n_tail
    tm = min(ROW_TILE, m)
    assert n_main % PROJ_COL_TILE == 0
    return pl.pallas_call(
        functools.partial(_norm_proj_kernel, col_tile=PROJ_COL_TILE),
        grid=(m // tm,),
        in_specs=[pl.BlockSpec((tm, d), lambda i: (i, 0)),
                  _const_spec((1, d)),
                  _layer_spec((d, n), layer)],
        out_specs=[pl.BlockSpec((tm, n_main), lambda i: (i, 0)),
                   pl.BlockSpec((tm, n_tail), lambda i: (i, 0))],
        out_shape=[jax.ShapeDtypeStruct((m, n_main), BF16),
                   jax.ShapeDtypeStruct((m, n_tail), F32)],
        compiler_params=_params(1),
        name="norm_proj",
    )(x, gain.reshape(1, d), w)


def _out_ffn_kernel(x_ref, y_ref, wo_ref, g_ref, wg_ref, wu_ref, wd_ref, o_ref, h_ref, *, ff_tile):
    x1 = x_ref[...] + jnp.dot(y_ref[...], wo_ref[...], preferred_element_type=F32)
    xn = _rms(x1, g_ref[...]).astype(BF16)
    d_ff = wg_ref.shape[1]
    for c0 in range(0, d_ff, ff_tile):
        g = jnp.dot(xn, wg_ref[:, c0:c0 + ff_tile], preferred_element_type=F32)
        u = jnp.dot(xn, wu_ref[:, c0:c0 + ff_tile], preferred_element_type=F32)
        h_ref[:, c0:c0 + ff_tile] = (g * _sigmoid(g) * u).astype(BF16)
    o_ref[...] = x1 + jnp.dot(h_ref[...], wd_ref[...], preferred_element_type=F32)


def out_ffn(x, y, w_o, o_layer, gain, w_gate, w_up, w_down, layer):
    m, d = x.shape
    dy = y.shape[1]
    d_ff = w_gate.shape[2]
    tm = min(ROW_TILE, m)
    return pl.pallas_call(
        functools.partial(_out_ffn_kernel, ff_tile=FF_TILE),
        grid=(m // tm,),
        in_specs=[pl.BlockSpec((tm, d), lambda i: (i, 0)),
                  pl.BlockSpec((tm, dy), lambda i: (i, 0)),
                  _layer_spec((dy, d), o_layer),
                  _const_spec((1, d)),
                  _layer_spec((d, d_ff), layer),
                  _layer_spec((d, d_ff), layer),
                  _layer_spec((d_ff, d), layer)],
        out_specs=pl.BlockSpec((tm, d), lambda i: (i, 0)),
        out_shape=jax.ShapeDtypeStruct((m, d), F32),
        scratch_shapes=[pltpu.VMEM((tm, d_ff), BF16)],
        compiler_params=_params(1),
        name="out_ffn",
    )(x, y, w_o, gain.reshape(1, d), w_gate, w_up, w_down)


def _split3_rows(x):
    hi = x.astype(BF16)
    r1 = x - hi.astype(F32)
    mid = r1.astype(BF16)
    lo = (r1 - mid.astype(F32)).astype(BF16)
    return jnp.concatenate([hi, mid, lo], axis=0)


def _split3_lanes(x):
    hi = x.astype(BF16)
    r1 = x - hi.astype(F32)
    mid = r1.astype(BF16)
    lo = (r1 - mid.astype(F32)).astype(BF16)
    return jnp.concatenate([hi, mid, lo], axis=1)


def _blocks_diag(tiles, zero):
    n = len(tiles)
    return jnp.concatenate(
        [jnp.concatenate([tiles[a] if a == b else zero for b in range(n)], axis=1) for a in range(n)], axis=0)


def _hybrid_kernel(proj_ref, small_ref, gates_ref, wg2_ref, bg_ref, gn_ref, conv_ref, bsmall_ref, brow_ref, mn_ref,
                   y_ref, gla_st, ml_cs, ml_ns, ml_m, conv_carry):
    t_tile = proj_ref.shape[0]
    nc = t_tile // CHUNK
    n_sub = CHUNK // SUB

    @pl.when(pl.program_id(1) == 0)
    def _():
        gla_st[...] = jnp.zeros_like(gla_st)
        ml_cs[...] = jnp.zeros_like(ml_cs)
        ml_ns[...] = jnp.zeros_like(ml_ns)
        ml_m[...] = jnp.zeros_like(ml_m)
        conv_carry[...] = jnp.zeros_like(conv_carry)

    xcat = proj_ref[:, C_MQ:C_MQ + 2 * ML_QK].astype(F32)
    ext = jnp.concatenate([conv_carry[...], xcat], axis=0)
    cw = conv_ref[...]
    acc = ext * cw[ML_CONV - 1:ML_CONV]
    for j in range(1, ML_CONV):
        acc = acc + pltpu.roll(ext, j, axis=0) * cw[ML_CONV - 1 - j:ML_CONV - j]
    conv = acc[SUBLANES:]
    qk = conv * _sigmoid(conv)
    conv_carry[...] = xcat[t_tile - SUBLANES:]

    lane = lax.broadcasted_iota(jnp.int32, (1, LANES), 1)
    head_mask = (lane < GLA_DK, lane >= GLA_DK)
    hm0 = head_mask[0]
    r_t = lax.broadcasted_iota(jnp.int32, (t_tile, t_tile), 0)
    c_t = lax.broadcasted_iota(jnp.int32, (t_tile, t_tile), 1)
    tril_bd = ((r_t >= c_t) & (r_t // CHUNK == c_t // CHUNK)).astype(BF16)
    tril3 = jnp.concatenate([tril_bd, tril_bd, tril_bd], axis=1)
    r_l = lax.broadcasted_iota(jnp.int32, (LANES, LANES), 0)
    c_l = lax.broadcasted_iota(jnp.int32, (LANES, LANES), 1)
    triu_bd = ((r_l <= c_l) & (r_l // CHUNK == c_l // CHUNK)).astype(BF16)
    triu3 = jnp.concatenate([triu_bd, triu_bd, triu_bd], axis=0)
    row_c = lax.broadcasted_iota(jnp.int32, (CHUNK, LANES), 0)
    causal2 = (lax.broadcasted_iota(jnp.int32, (CHUNK, LANES), 1) % CHUNK) <= row_c
    sub_id = (lax.broadcasted_iota(jnp.int32, (t_tile, LANES), 0) % CHUNK) // SUB
    row_sub = lax.broadcasted_iota(jnp.int32, (2 * SUBLANES, LANES), 0)
    zero_tile = jnp.zeros((CHUNK, LANES), F32)
    zero_v = jnp.zeros((CHUNK, LANES), BF16)

    def chunk_rows(x, c):
        return x[c * CHUNK:(c + 1) * CHUNK]

    def diag_blocks(r):
        return [r[c * CHUNK:(c + 1) * CHUNK, c * LANES:(c + 1) * LANES] for c in range(nc)]

    small = small_ref[...]
    smallb = small + bsmall_ref[...]

    z = _dot(small, wg2_ref[...]) + bg_ref[...]
    log_a = _log_sigmoid(z) * (1.0 / GLA_TAU)
    cum = jnp.dot(tril3, _split3_rows(jnp.concatenate([log_a, _log_sigmoid(smallb)], axis=1)),
                  preferred_element_type=F32)
    g_all = gates_ref[0, 0] + brow_ref[...]
    bc_rows = jnp.dot(_split3_lanes(_log_sigmoid(g_all)), triu3, preferred_element_type=F32)

    for p in range(GLA_HEADS // 2):
        q2 = proj_ref[:, C_GQ + p * LANES:C_GQ + (p + 1) * LANES].astype(F32) * (GLA_DK ** -0.5)
        k2 = proj_ref[:, C_GK + p * LANES:C_GK + (p + 1) * LANES].astype(F32)
        bc = cum[:, p * LANES:(p + 1) * LANES]
        v = [proj_ref[:, C_GV + (2 * p + hl) * GLA_DV:C_GV + (2 * p + hl + 1) * GLA_DV] for hl in range(2)]

        ref_pt = [[chunk_rows(bc, c)[i * SUB - 1:i * SUB] if i else jnp.zeros((1, LANES), F32)
                   for i in range(n_sub)] for c in range(nc)]
        r_own = jnp.concatenate([jnp.broadcast_to(ref_pt[c][i], (SUB, LANES))
                                 for c in range(nc) for i in range(n_sub)], axis=0)
        qs = q2 * jnp.exp(bc - r_own)
        q_hat = jnp.concatenate([jnp.where(sub_id == i, qs, 0.0) for i in range(n_sub)], axis=1)
        k_hat, k_dec, last = [], [], []
        for c in range(nc):
            kc, bcc = chunk_rows(k2, c), chunk_rows(bc, c)
            ke = [kc * jnp.exp(ref_pt[c][i] - bcc) for i in range(n_sub)]
            for hl in range(2):
                k_hat.append(jnp.concatenate([jnp.where(head_mask[hl], ke[i], 0.0) for i in range(n_sub)], axis=1))
            last.append(bcc[CHUNK - 1:CHUNK])
            k_dec.append(kc * jnp.exp(last[c] - bcc))
        k_hat = jnp.concatenate(k_hat, axis=0)
        sc = diag_blocks(_dot_nt(q_hat, k_hat))
        sc = [jnp.where(causal2, s, 0.0) for s in sc]
        v_bd = jnp.concatenate([jnp.concatenate([chunk_rows(v[hl], c) if hl == b else zero_v for b in range(2)],
                                                axis=1) for c in range(nc) for hl in range(2)], axis=0)
        o_intra = _dot(_blocks_diag(sc, zero_tile), v_bd)
        u_all = _dot_tn(jnp.concatenate(v, axis=1), _blocks_diag(k_dec, zero_tile))

        st = gla_st[p]
        before = []
        for c in range(nc):
            before.append(st)
            u = jnp.where(hm0, u_all[:GLA_DV, c * LANES:(c + 1) * LANES], u_all[GLA_DV:, c * LANES:(c + 1) * LANES])
            st = st * jnp.exp(last[c]) + u
        gla_st[p] = st
        st_stack = jnp.concatenate(before, axis=0)
        q_in = q2 * jnp.exp(bc)
        for hl in range(2):
            h = 2 * p + hl
            o_inter = jnp.concatenate(diag_blocks(_dot_nt(jnp.where(head_mask[hl], q_in, 0.0), st_stack)), axis=0)
            o = o_intra[:, hl * GLA_DV:(hl + 1) * GLA_DV] + o_inter
            gate = proj_ref[:, C_GR + h * GLA_DV:C_GR + (h + 1) * GLA_DV].astype(F32)
            yg = _rms(o, gn_ref[:, h * GLA_DV:(h + 1) * GLA_DV]) * (gate * _sigmoid(gate))
            y_ref[:, h * GLA_DV:(h + 1) * GLA_DV] = yg.astype(y_ref.dtype)

    ones_tile = jnp.ones((t_tile, LANES), BF16)
    for p in range(ML_HEADS // 2):
        mq2 = qk[:, p * LANES:(p + 1) * LANES]
        mk2 = qk[:, ML_QK + p * LANES:ML_QK + (p + 1) * LANES] * (ML_DK ** -0.5)
        v = [proj_ref[:, C_MV + (2 * p + hl) * ML_DV:C_MV + (2 * p + hl + 1) * ML_DV] for hl in range(2)]
        k_stack = jnp.concatenate([jnp.where(head_mask[hl], chunk_rows(mk2, c), 0.0)
                                   for c in range(nc) for hl in range(2)], axis=0)
        qk_blocks = diag_blocks(_dot_nt(mq2, k_stack))

        w_blocks, wk_blocks, cm_cols, b_max, bc_last = [], [], [], [], []
        for c in range(nc):
            bcr = bc_rows[ML_HEADS * c + 2 + p:ML_HEADS * c + 3 + p]
            b_pair = g_all[ML_HEADS * c + p:ML_HEADS * c + p + 1] - bcr
            bm = jnp.where(causal2, b_pair, -jnp.inf)
            cm = [jnp.max(jnp.where(head_mask[hl], bm, -jnp.inf), axis=1, keepdims=True) for hl in range(2)]
            w_blocks.append(qk_blocks[c] * jnp.exp(bm - jnp.where(hm0, cm[0], cm[1])))
            cm_cols.append(cm)
            bmx = [jnp.max(jnp.where(head_mask[hl], b_pair, -jnp.inf), axis=1, keepdims=True) for hl in range(2)]
            b_max.append(jnp.where(hm0, bmx[0], bmx[1]))
            bc_last.append(jnp.where(hm0, bcr[:, CHUNK - 1:CHUNK], bcr[:, LANES - 1:LANES]))
            e_col = []
            for hl in range(2):
                h = 2 * p + hl
                b_col = (chunk_rows(smallb, c)[:, S_MI + h:S_MI + h + 1]
                         - chunk_rows(cum, c)[:, GLA_QK + S_MF + h:GLA_QK + S_MF + h + 1])
                e_col.append(jnp.exp(b_col - bmx[hl]))
            wk_blocks.append(chunk_rows(mk2, c) * jnp.where(hm0, e_col[0], e_col[1]))

        ones_col = [jnp.broadcast_to((lane == hl).astype(BF16), (CHUNK, LANES)) for hl in range(2)]
        v_bd = jnp.concatenate([jnp.concatenate([chunk_rows(v[hl], c) if hl == b else zero_v for b in range(2)]
                                                + [ones_col[hl]], axis=1)
                                for c in range(nc) for hl in range(2)], axis=0)
        nv = _dot(_blocks_diag(w_blocks, zero_tile), v_bd)
        u_all = _dot_tn(jnp.concatenate(v + [ones_tile], axis=1), _blocks_diag(wk_blocks, zero_tile))

        cs, ns, m_run = ml_cs[p], ml_ns[p], ml_m[p]
        before = []
        for c in range(nc):
            before.append((cs, ns, m_run))
            m_top = jnp.maximum(m_run, b_max[c])
            keep = jnp.exp(m_run - m_top)
            gain = jnp.exp(b_max[c] - m_top)
            blk = slice(c * LANES, (c + 1) * LANES)
            cs = keep * cs + gain * jnp.where(hm0, u_all[:ML_DV, blk], u_all[ML_DV:2 * ML_DV, blk])
            ns = keep * ns + gain * u_all[2 * ML_DV:2 * ML_DV + 1, blk]
            m_run = bc_last[c] + m_top
        ml_cs[p], ml_ns[p], ml_m[p] = cs, ns, m_run
        ns_rows = jnp.zeros((2 * SUBLANES, LANES), F32)
        for c in range(nc):
            ns_rows = jnp.where(row_sub == c, before[c][1], ns_rows)
        state_stack = jnp.concatenate([b[0] for b in before] + [ns_rows], axis=0)

        for hl in range(2):
            h = 2 * p + hl
            rr = _dot_nt(jnp.where(head_mask[hl], mq2, 0.0), state_stack)
            q_cs = diag_blocks(rr)
            h_chunks = []
            for c in range(nc):
                rows = slice(c * CHUNK, (c + 1) * CHUNK)
                q_ns = rr[rows, nc * ML_DV + c:nc * ML_DV + c + 1]
                m_prev = before[c][2][:, hl * ML_DK:hl * ML_DK + 1]
                cm = cm_cols[c][hl]
                m_top = jnp.maximum(m_prev, cm)
                local = jnp.exp(cm - m_top)
                carried = jnp.exp(m_prev - m_top)
                num = local * nv[rows, hl * ML_DV:(hl + 1) * ML_DV] + carried * q_cs[c]
                den = local * nv[rows, 2 * ML_DV + hl:2 * ML_DV + hl + 1] + carried * q_ns
                bc_col = chunk_rows(cum, c)[:, GLA_QK + S_MF + h:GLA_QK + S_MF + h + 1]
                h_chunks.append(num / jnp.maximum(jnp.abs(den), jnp.exp(-(bc_col + m_top))))
            og = proj_ref[:, C_MO + h * ML_DV:C_MO + (h + 1) * ML_DV].astype(F32)
            ym = _rms(jnp.concatenate(h_chunks, axis=0) * _sigmoid(og), mn_ref[:, h * ML_DV:(h + 1) * ML_DV])
            y_ref[:, GLA_V + h * ML_DV:GLA_V + (h + 1) * ML_DV] = ym.astype(y_ref.dtype)


def hybrid_core(proj, small, gates_t, w_gate2, b_gate, gla_norm, conv_w, b_small, b_rows, ml_norm, batch):
    m = proj.shape[0]
    seq = m // batch
    t = min(MIX_TILE, seq)
    nt = seq // t
    assert 2 * SUBLANES >= t // CHUNK
    return pl.pallas_call(
        _hybrid_kernel,
        grid=(batch, nt),
        in_specs=[pl.BlockSpec((t, C_SMALL), lambda b, i: (b * nt + i, 0)),
                  pl.BlockSpec((t, LANES), lambda b, i: (b * nt + i, 0)),
                  pl.BlockSpec((1, 1, ML_HEADS * (t // CHUNK), LANES), lambda b, i: (b, i, 0, 0)),
                  _const_spec((LANES, GLA_QK)),
                  _const_spec((1, GLA_QK)),
                  _const_spec((1, GLA_V)),
                  _const_spec((ML_CONV, 2 * ML_QK)),
                  _const_spec((1, LANES)),
                  _const_spec((ML_HEADS * (t // CHUNK), LANES)),
                  _const_spec((1, ML_V))],
        out_specs=pl.BlockSpec((t, GLA_V + ML_V), lambda b, i: (b * nt + i, 0)),
        out_shape=jax.ShapeDtypeStruct((m, GLA_V + ML_V), BF16),
        scratch_shapes=[pltpu.VMEM((GLA_HEADS // 2, GLA_DV, LANES), F32),
                        pltpu.VMEM((ML_HEADS // 2, ML_DV, LANES), F32),
                        pltpu.VMEM((ML_HEADS // 2, 1, LANES), F32),
                        pltpu.VMEM((ML_HEADS // 2, 1, LANES), F32),
                        pltpu.VMEM((SUBLANES, 2 * ML_QK), F32)],
        compiler_params=_params(2),
        name="hybrid_core",
    )(proj, small, gates_t, w_gate2, b_gate, gla_norm, conv_w, b_small, b_rows, ml_norm)


def permute_hybrid_weight(w_in):
    segs = np.cumsum([0, GLA_QK, GLA_QK, GLA_V, GLA_V, GLA_RANK, ML_QK, ML_QK, ML_V, ML_V, ML_HEADS, ML_HEADS])
    seg = lambda i: w_in[:, :, int(segs[i]):int(segs[i + 1])]
    pad = jnp.zeros(w_in.shape[:2] + (LANES - GLA_RANK - 2 * ML_HEADS,), w_in.dtype)
    return jnp.concatenate([seg(0), seg(1), seg(2), seg(3), seg(5), seg(6), seg(7), seg(8),
                            seg(4), seg(9), seg(10), pad], axis=2).astype(BF16)


def hybrid_layer(x, gain, w_perm, layer, w_gate2, b_gate, gla_norm, conv_w, b_if, ml_norm, batch):
    m, d = x.shape
    seq = m // batch
    proj, small = norm_proj(x, gain, w_perm, layer, LANES)
    gates = small[:, S_MI:S_MI + 2 * ML_HEADS]
    gates_t = gates.reshape(batch, seq // CHUNK, CHUNK, 2 * ML_HEADS).transpose(0, 1, 3, 2)
    t = min(MIX_TILE, seq)
    gates_t = gates_t.reshape(batch, seq // t, ML_HEADS * (t // CHUNK), LANES)
    b_small = jnp.zeros((1, LANES), F32).at[0, S_MI:S_MI + 2 * ML_HEADS].set(b_if)
    b_rows = jnp.tile(jnp.repeat(b_if, CHUNK).reshape(ML_HEADS, LANES), (t // CHUNK, 1))
    w_gate2_rows = jnp.zeros((LANES, GLA_QK), F32).at[S_GLR:S_GLR + GLA_RANK].set(w_gate2)
    return hybrid_core(proj, small, gates_t, w_gate2_rows, b_gate.reshape(1, -1), gla_norm.reshape(1, -1), conv_w,
                       b_small, b_rows, ml_norm.reshape(1, -1), batch)


LOG2E = 1.4426950408889634
AUG = 2 * MOBA_HD
LANES_PER_BLOCK = 4
MASK_BIG = 2.0 ** 100
PROJ_HEADS = 4
VT_ROWS = MOBA_HD + 16


def _split3(x):
    x = np.asarray(x, np.float32)
    hi = x.astype(BF16).astype(np.float32)
    mid = (x - hi).astype(BF16).astype(np.float32)
    lo = (x - hi - mid).astype(BF16).astype(np.float32)
    return np.stack([hi, mid, lo], axis=-1)


def _aug_tables(nb):
    c = np.asarray([LOG2E * 2.0 ** (-8.0 * (h + 1) / MOBA_HEADS) for h in range(MOBA_HEADS)], np.float32)
    pos = np.arange(MOBA_BLOCK, dtype=np.float32)
    base = LANES_PER_BLOCK * nb
    q_rows = np.zeros((MOBA_HEADS, MOBA_BLOCK, LANES), np.float32)
    k_rows = np.zeros((MOBA_HEADS, MOBA_BLOCK, LANES), np.float32)
    q_rows[:, :, base:base + 3] = _split3(-c[:, None] * pos[None, :])
    q_rows[:, :, base + 3:base + 6] = 1.0
    k_rows[:, :, base:base + 3] = 1.0
    k_rows[:, :, base + 3:base + 6] = _split3(c[:, None] * pos[None, :])
    i_idx = np.arange(nb)[:, None]
    j_idx = np.arange(nb)[None, :]
    steps = np.zeros((MOBA_HEADS, nb, nb, LANES_PER_BLOCK), np.float32)
    steps[..., 0] = np.where(j_idx == i_idx, 0.0, -MASK_BIG)
    gap = _split3(-c[:, None, None] * (MOBA_BLOCK * (i_idx - j_idx)).astype(np.float32)[None])
    steps[..., 1:] = np.where((j_idx <= i_idx)[None, :, :, None], gap, 0.0)
    q_steps = np.zeros((MOBA_HEADS, nb, 1, LANES), np.float32)
    q_steps[:, :, 0, :base] = steps.reshape(MOBA_HEADS, nb, base)
    k_steps = np.zeros((nb, 1, LANES), np.float32)
    k_steps[:, 0, :base] = np.repeat(np.eye(nb, dtype=np.float32), LANES_PER_BLOCK, axis=1)
    expand = np.zeros((nb, LANES), np.float32)
    expand[np.arange(nb), LANES_PER_BLOCK * np.arange(nb)] = MASK_BIG
    return (jnp.asarray(q_rows), jnp.asarray(q_steps), jnp.asarray(k_rows), jnp.asarray(k_steps),
            jnp.asarray(expand, BF16))


def _moba_prep_kernel(x_ref, g_ref, w_ref, qg_ref, kg_ref, qrow_ref, qstep_ref, krow_ref, kstep_ref, exp_ref,
                      qa_ref, ka_ref, vt_ref, kmean_ref):
    i = pl.program_id(1)
    nb = kmean_ref.shape[1]
    d = MOBA_HEADS * MOBA_HD

    @pl.when(i == 0)
    def _():
        kmean_ref[...] = jnp.zeros_like(kmean_ref)

    xn = _rms(x_ref[...], g_ref[...]).astype(BF16)
    blk = lax.broadcasted_iota(jnp.int32, (nb, MOBA_BLOCK), 0)
    past = blk < i
    ones_rows = (lax.broadcasted_iota(jnp.int32, (VT_ROWS - MOBA_HD, MOBA_BLOCK), 0) == 0).astype(F32)
    k_step = kstep_ref[i]

    for h in range(MOBA_HEADS):
        hg = h % PROJ_HEADS
        if hg == 0:
            c0 = h * MOBA_HD
            q_grp, k_grp, v_grp = (jnp.dot(xn, w_ref[:, off + c0:off + c0 + PROJ_HEADS * MOBA_HD],
                                           preferred_element_type=F32) for off in (0, d, 2 * d))
        part = slice(hg * MOBA_HD, (hg + 1) * MOBA_HD)
        q = _rms(q_grp[:, part], qg_ref[...]) * (MOBA_HD ** -0.5)
        k = _rms(k_grp[:, part], kg_ref[...])
        v = v_grp[:, part]
        vt_ref[0, h] = jnp.concatenate([v.T, ones_rows], axis=0).astype(vt_ref.dtype)

        gate = _dot_nt(kmean_ref[h], q)
        gate = jnp.where(past, gate, -jnp.inf)
        rank = jnp.zeros((nb, MOBA_BLOCK), F32)
        for j in range(nb):
            other = gate[j:j + 1]
            ahead = (other > gate) | ((other == gate) & (j < blk))
            rank = rank + ahead.astype(F32)
        sel = (past & (rank < MOBA_TOPK)).astype(BF16)
        q_extra = (qrow_ref[h] + qstep_ref[h, i]) + _dot_tn(sel, exp_ref[...])
        qa_ref[:, h * AUG:h * AUG + MOBA_HD] = (q * LOG2E).astype(qa_ref.dtype)
        qa_ref[:, h * AUG + MOBA_HD:(h + 1) * AUG] = q_extra.astype(qa_ref.dtype)
        ka_ref[:, h * AUG:h * AUG + MOBA_HD] = k.astype(ka_ref.dtype)
        ka_ref[:, h * AUG + MOBA_HD:(h + 1) * AUG] = (krow_ref[h] + k_step).astype(ka_ref.dtype)

        kmean_ref[h, pl.ds(i, 1), :] = jnp.mean(k, axis=0, keepdims=True)


def moba_prep(x, gain, w_qkv, layer, q_gain, k_gain, batch):
    m, dm = x.shape
    seq = m // batch
    nb = seq // MOBA_BLOCK
    assert LANES_PER_BLOCK * nb + 6 <= LANES
    d = MOBA_HEADS * MOBA_HD
    q_rows, q_steps, k_rows, k_steps, expand = _aug_tables(nb)
    return pl.pallas_call(
        _moba_prep_kernel,
        grid=(batch, nb),
        in_specs=[pl.BlockSpec((MOBA_BLOCK, dm), lambda b, i: (b * nb + i, 0)),
                  _const_spec((1, dm)),
                  _layer_spec((dm, 3 * d), layer),
                  _const_spec((1, MOBA_HD)),
                  _const_spec((1, MOBA_HD)),
                  _const_spec((MOBA_HEADS, MOBA_BLOCK, LANES)),
                  _const_spec((MOBA_HEADS, nb, 1, LANES)),
                  _const_spec((MOBA_HEADS, MOBA_BLOCK, LANES)),
                  _const_spec((nb, 1, LANES)),
                  _const_spec((nb, LANES))],
        out_specs=[pl.BlockSpec((MOBA_BLOCK, MOBA_HEADS * AUG), lambda b, i: (b * nb + i, 0)),
                   pl.BlockSpec((MOBA_BLOCK, MOBA_HEADS * AUG), lambda b, i: (b * nb + i, 0)),
                   pl.BlockSpec((1, MOBA_HEADS, VT_ROWS, MOBA_BLOCK), lambda b, i: (b, 0, 0, i))],
        out_shape=[jax.ShapeDtypeStruct((m, MOBA_HEADS * AUG), BF16),
                   jax.ShapeDtypeStruct((m, MOBA_HEADS * AUG), BF16),
                   jax.ShapeDtypeStruct((batch, MOBA_HEADS, VT_ROWS, seq), BF16)],
        scratch_shapes=[pltpu.VMEM((MOBA_HEADS, nb, MOBA_HD), F32)],
        compiler_params=_params(2),
        name="moba_prep",
    )(x, gain.reshape(1, dm), w_qkv, q_gain.reshape(1, -1), k_gain.reshape(1, -1),
      q_rows, q_steps, k_rows, k_steps, expand)


ATTN_HEADS_PER_STEP = 2
ATTN_CHUNK_BLOCKS = 2
PV_KEYS = 256


def _moba_attn_kernel(q_ref, k_ref, vt_ref, o_ref, s_even, s_odd, m_even, m_odd, *, heads, chunk_blocks, nb):
    t = pl.program_id(2)
    i = jnp.minimum(t, nb - 1)
    chunk_keys = chunk_blocks * MOBA_BLOCK
    n_chunks = i // chunk_blocks + 1
    n_valid2 = jnp.where(t > 0, (t - 1) // chunk_blocks + 1, 0)

    @pl.when((pl.program_id(0) == 0) & (pl.program_id(1) == 0) & (t == 0))
    def _():
        s_even[...] = jnp.zeros_like(s_even)
        s_odd[...] = jnp.zeros_like(s_odd)

    key_r = lax.broadcasted_iota(jnp.int32, (MOBA_BLOCK, MOBA_BLOCK), 0)
    qry_c = lax.broadcasted_iota(jnp.int32, (MOBA_BLOCK, MOBA_BLOCK), 1)
    causal = qry_c >= key_r
    q_aug = [q_ref[:, hh * AUG:(hh + 1) * AUG] for hh in range(heads)]

    def both(c, carry, last, s_w, s_r, m_row2, stage1, stage2):
        m_part, acc = carry
        k0 = c * chunk_keys
        live2 = jnp.full((1, MOBA_BLOCK), c, jnp.int32) < n_valid2
        new_m, new_acc = [], []
        for hh in range(heads):
            m_new, a = m_part[hh], acc[hh]
            if stage1:
                s = lax.dot_general(k_ref[pl.ds(k0, chunk_keys), hh * AUG:(hh + 1) * AUG], q_aug[hh],
                                    (((1,), (1,)), ((), ())), preferred_element_type=F32)
                if last:
                    parts = []
                    for u in range(chunk_blocks):
                        other_block = jnp.full((1, MOBA_BLOCK), c * chunk_blocks + u, jnp.int32) != i
                        parts.append(jnp.where(causal | other_block, s[u * MOBA_BLOCK:(u + 1) * MOBA_BLOCK], NEG))
                    s = jnp.concatenate(parts, axis=0)
                s_w[hh, pl.ds(k0, chunk_keys), :] = s
                m_new = jnp.maximum(m_new, jnp.max(s.reshape(chunk_keys // SUBLANES, SUBLANES, MOBA_BLOCK), axis=0))
            if stage2:
                for u in range(0, chunk_keys, PV_KEYS):
                    ku = k0 + u
                    p = jnp.exp2(s_r[hh, pl.ds(ku, PV_KEYS), :] - m_row2[hh])
                    p = jnp.where(live2, p, 0.0).astype(BF16)
                    a = a + jnp.dot(vt_ref[0, hh, :, pl.ds(ku, PV_KEYS)], p, preferred_element_type=F32)
            new_m.append(m_new)
            new_acc.append(a)
        return tuple(new_m), tuple(new_acc)

    def step(n, s_w, s_r, m_w, m_r, stage1=True, stage2=True):
        m_row2 = [m_r[hh] for hh in range(heads)]
        carry = (tuple(jnp.full((SUBLANES, MOBA_BLOCK), NEG, F32) for _ in range(heads)),
                 tuple(jnp.zeros((VT_ROWS, MOBA_BLOCK), F32) for _ in range(heads)))
        for c in range(n):
            carry = both(c, carry, c == n - 1, s_w, s_r, m_row2, stage1, stage2)
        m_part, acc = carry
        for hh in range(heads):
            if stage1:
                m_w[hh] = jnp.max(m_part[hh], axis=0, keepdims=True)
            if stage2:
                o = acc[hh][:MOBA_HD] * (1.0 / acc[hh][MOBA_HD:MOBA_HD + 1])
                o_ref[:, hh * MOBA_HD:(hh + 1) * MOBA_HD] = o.T.astype(o_ref.dtype)

    even = (s_even, s_odd, m_even, m_odd)
    odd = (s_odd, s_even, m_odd, m_even)
    n_max = nb // chunk_blocks
    pl.when(t == 0)(functools.partial(step, 1, *even, stage2=False))
    pl.when(t == nb)(functools.partial(step, n_max, *(odd if nb % 2 else even), stage1=False))
    inner = (t > 0) & (t < nb)
    for n in range(1, n_max + 1):
        pl.when(inner & (n_chunks == n) & (t % 2 == 0))(functools.partial(step, n, *even))
        pl.when(inner & (n_chunks == n) & (t % 2 == 1))(functools.partial(step, n, *odd))


def moba_attention(q_aug, k_aug, vt, batch):
    m = q_aug.shape[0]
    seq = m // batch
    nb = seq // MOBA_BLOCK
    hps = ATTN_HEADS_PER_STEP
    assert nb % ATTN_CHUNK_BLOCKS == 0
    return pl.pallas_call(
        functools.partial(_moba_attn_kernel, heads=hps, chunk_blocks=ATTN_CHUNK_BLOCKS, nb=nb),
        grid=(batch, MOBA_HEADS // hps, nb + 1),
        in_specs=[pl.BlockSpec((MOBA_BLOCK, hps * AUG), lambda b, g, t: (b * nb + jnp.minimum(t, nb - 1), g)),
                  pl.BlockSpec((seq, hps * AUG), lambda b, g, t: (b, g)),
                  pl.BlockSpec((1, hps, VT_ROWS, seq), lambda b, g, t: (b, g, 0, 0))],
        out_specs=pl.BlockSpec((MOBA_BLOCK, hps * MOBA_HD), lambda b, g, t: (b * nb + jnp.maximum(t - 1, 0), g)),
        out_shape=jax.ShapeDtypeStruct((m, MOBA_HEADS * MOBA_HD), BF16),
        scratch_shapes=[pltpu.VMEM((hps, seq, MOBA_BLOCK), F32), pltpu.VMEM((hps, seq, MOBA_BLOCK), F32),
                        pltpu.VMEM((hps, 1, MOBA_BLOCK), F32), pltpu.VMEM((hps, 1, MOBA_BLOCK), F32)],
        compiler_params=_params(3),
        name="moba_attn",
    )(q_aug, k_aug, vt)


def moba_layer(x, gain, w_qkv, layer, q_gain, k_gain, batch):
    q_aug, k_aug, vt = moba_prep(x, gain, w_qkv, layer, q_gain, k_gain, batch)
    return moba_attention(q_aug, k_aug, vt, batch)


def kernel(x, norm_mix, norm_ffn, hyb_w_in, gla_w_gate2, gla_b_gate, gla_norm, ml_conv, ml_b_if, ml_norm,
           hyb_w_out, moba_w_qkv, moba_q_norm, moba_k_norm, moba_w_o, ffn_w_gate, ffn_w_up, ffn_w_down):
    batch, seq, d = x.shape
    depth = norm_mix.shape[0]
    xf = x.reshape(batch * seq, d)
    w_hyb_in = permute_hybrid_weight(hyb_w_in)
    w_hyb_out, w_moba_qkv, w_moba_o = hyb_w_out.astype(BF16), moba_w_qkv.astype(BF16), moba_w_o.astype(BF16)
    w_gate, w_up, w_down = ffn_w_gate.astype(BF16), ffn_w_up.astype(BF16), ffn_w_down.astype(BF16)
    for l in range(depth):
        j = l // 2
        if l % 2 == 0:
            y = hybrid_layer(xf, norm_mix[l], w_hyb_in, j, gla_w_gate2[j], gla_b_gate[j], gla_norm[j],
                             ml_conv[j], ml_b_if[j], ml_norm[j], batch)
            w_o = w_hyb_out
        else:
            y = moba_layer(xf, norm_mix[l], w_moba_qkv, j, moba_q_norm[j], moba_k_norm[j], batch)
            w_o = w_moba_o
        xf = out_ffn(xf, y, w_o, j, norm_ffn[l], w_gate, w_up, w_down, l)
    return xf.reshape(batch, seq, d)
```

```python
import functools

import jax
import jax.numpy as jnp
import numpy as np
from jax import lax
from jax.experimental import pallas as pl
from jax.experimental.pallas import tpu as pltpu

F32 = jnp.float32
BF16 = jnp.bfloat16

LANES = 128
SUBLANES = 8
VMEM_LIMIT_BYTES = 56 * 1024 * 1024

EPS = 1e-6
NEG = -1e30

GLA_HEADS = 4
GLA_DK = 64
GLA_DV = 128
GLA_RANK = 16
GLA_TAU = 16.0
ML_HEADS = 4
ML_DK = 64
ML_DV = 128
ML_CONV = 4
CHUNK = 64
SUB = 16
MOBA_HEADS = 8
MOBA_HD = 128
MOBA_BLOCK = 256
MOBA_TOPK = 3

GLA_QK = GLA_HEADS * GLA_DK
GLA_V = GLA_HEADS * GLA_DV
ML_QK = ML_HEADS * ML_DK
ML_V = ML_HEADS * ML_DV

C_GQ = 0
C_GK = C_GQ + GLA_QK
C_GV = C_GK + GLA_QK
C_GR = C_GV + GLA_V
C_MQ = C_GR + GLA_V
C_MK = C_MQ + ML_QK
C_MV = C_MK + ML_QK
C_MO = C_MV + ML_V
C_SMALL = C_MO + ML_V
HYB_COLS = C_SMALL + LANES
S_GLR = 0
S_MI = GLA_RANK
S_MF = GLA_RANK + ML_HEADS

ROW_TILE = 512
MIX_TILE = 256
FF_TILE = 256
PROJ_COL_TILE = 512


def _rms(x, gain):
    return x * lax.rsqrt(jnp.mean(x * x, axis=-1, keepdims=True) + EPS) * gain


def _log_sigmoid(z):
    return jnp.minimum(z, 0.0) - jnp.log1p(jnp.exp(-jnp.abs(z)))


def _sigmoid(z):
    return 1.0 / (1.0 + jnp.exp(-z))


def _dot(a, b):
    return jnp.dot(a.astype(BF16), b.astype(BF16), preferred_element_type=F32)


def _dot_nt(a, b):
    return lax.dot_general(a.astype(BF16), b.astype(BF16), (((1,), (1,)), ((), ())),
                           preferred_element_type=F32)


def _dot_tn(a, b):
    return lax.dot_general(a.astype(BF16), b.astype(BF16), (((0,), (0,)), ((), ())),
                           preferred_element_type=F32)


def _const_spec(shape):
    return pl.BlockSpec(shape, lambda *_: (0,) * len(shape), pipeline_mode=pl.Buffered(1))


def _layer_spec(shape, layer):
    return pl.BlockSpec((None,) + tuple(shape), lambda *_: (layer,) + (0,) * len(shape),
                        pipeline_mode=pl.Buffered(1))


def _params(n_grid):
    return pltpu.CompilerParams(dimension_semantics=("arbitrary",) * n_grid,
                                vmem_limit_bytes=VMEM_LIMIT_BYTES)


def _norm_proj_kernel(x_ref, g_ref, w_ref, main_ref, tail_ref, *, col_tile):
    xn = _rms(x_ref[...], g_ref[...]).astype(BF16)
    n_main = main_ref.shape[1]
    for c0 in range(0, n_main, col_tile):
        main_ref[:, c0:c0 + col_tile] = jnp.dot(xn, w_ref[:, c0:c0 + col_tile],
                                                 preferred_element_type=F32).astype(main_ref.dtype)
    tail_ref[...] = jnp.dot(xn, w_ref[:, n_main:], preferred_element_type=F32)


def norm_proj(x, gain, w, layer, n_tail):
    m, d = x.shape
    n = w.shape[2]
    n_main = n - n_tail
    tm = min(ROW_TILE, m)
    assert n_main % PROJ_COL_TILE == 0
    return pl.pallas_call(
        functools.partial(_norm_proj_kernel, col_tile=PROJ_COL_TILE),
        grid=(m // tm,),
        in_specs=[pl.BlockSpec((tm, d), lambda i: (i, 0)),
                  _const_spec((1, d)),
                  _layer_spec((d, n), layer)],
        out_specs=[pl.BlockSpec((tm, n_main), lambda i: (i, 0)),
                   pl.BlockSpec((tm, n_tail), lambda i: (i, 0))],
        out_shape=[jax.ShapeDtypeStruct((m, n_main), BF16),
                   jax.ShapeDtypeStruct((m, n_tail), F32)],
        compiler_params=_params(1),
        name="norm_proj",
    )(x, gain.reshape(1, d), w)


def _out_ffn_kernel(x_ref, y_ref, wo_ref, g_ref, wg_ref, wu_ref, wd_ref, o_ref, h_ref, *, ff_tile):
    x1 = x_ref[...] + jnp.dot(y_ref[...], wo_ref[...], preferred_element_type=F32)
    xn = _rms(x1, g_ref[...]).astype(BF16)
    d_ff = wg_ref.shape[1]
    for c0 in range(0, d_ff, ff_tile):
        g = jnp.dot(xn, wg_ref[:, c0:c0 + ff_tile], preferred_element_type=F32)
        u = jnp.dot(xn, wu_ref[:, c0:c0 + ff_tile], preferred_element_type=F32)
        h_ref[:, c0:c0 + ff_tile] = (g * _sigmoid(g) * u).astype(BF16)
    o_ref[...] = x1 + jnp.dot(h_ref[...], wd_ref[...], preferred_element_type=F32)


def out_ffn(x, y, w_o, o_layer, gain, w_gate, w_up, w_down, layer):
    m, d = x.shape
    dy = y.shape[1]
    d_ff = w_gate.shape[2]
    tm = min(ROW_TILE, m)
    return pl.pallas_call(
        functools.partial(_out_ffn_kernel, ff_tile=FF_TILE),
        grid=(m // tm,),
        in_specs=[pl.BlockSpec((tm, d), lambda i: (i, 0)),
                  pl.BlockSpec((tm, dy), lambda i: (i, 0)),
                  _layer_spec((dy, d), o_layer),
                  _const_spec((1, d)),
                  _layer_spec((d, d_ff), layer),
                  _layer_spec((d, d_ff), layer),
                  _layer_spec((d_ff, d), layer)],
        out_specs=pl.BlockSpec((tm, d), lambda i: (i, 0)),
        out_shape=jax.ShapeDtypeStruct((m, d), F32),
        scratch_shapes=[pltpu.VMEM((tm, d_ff), BF16)],
        compiler_params=_params(1),
        name="out_ffn",
    )(x, y, w_o, gain.reshape(1, d), w_gate, w_up, w_down)


def _split3_rows(x):
    hi = x.astype(BF16)
    r1 = x - hi.astype(F32)
    mid = r1.astype(BF16)
    lo = (r1 - mid.astype(F32)).astype(BF16)
    return jnp.concatenate([hi, mid, lo], axis=0)


def _split3_lanes(x):
    hi = x.astype(BF16)
    r1 = x - hi.astype(F32)
    mid = r1.astype(BF16)
    lo = (r1 - mid.astype(F32)).astype(BF16)
    return jnp.concatenate([hi, mid, lo], axis=1)


def _blocks_diag(tiles, zero):
    n = len(tiles)
    return jnp.concatenate(
        [jnp.concatenate([tiles[a] if a == b else zero for b in range(n)], axis=1) for a in range(n)], axis=0)


def _hybrid_kernel(proj_ref, small_ref, gates_ref, wg2_ref, bg_ref, gn_ref, conv_ref, bsmall_ref, brow_ref, mn_ref,
                   y_ref, gla_st, ml_cs, ml_ns, ml_m, conv_carry):
    t_tile = proj_ref.shape[0]
    nc = t_tile // CHUNK
    n_sub = CHUNK // SUB

    @pl.when(pl.program_id(1) == 0)
    def _():
        gla_st[...] = jnp.zeros_like(gla_st)
        ml_cs[...] = jnp.zeros_like(ml_cs)
        ml_ns[...] = jnp.zeros_like(ml_ns)
        ml_m[...] = jnp.zeros_like(ml_m)
        conv_carry[...] = jnp.zeros_like(conv_carry)

    xcat = proj_ref[:, C_MQ:C_MQ + 2 * ML_QK].astype(F32)
    ext = jnp.concatenate([conv_carry[...], xcat], axis=0)
    cw = conv_ref[...]
    acc = ext * cw[ML_CONV - 1:ML_CONV]
    for j in range(1, ML_CONV):
        acc = acc + pltpu.roll(ext, j, axis=0) * cw[ML_CONV - 1 - j:ML_CONV - j]
    conv = acc[SUBLANES:]
    qk = conv * _sigmoid(conv)
    conv_carry[...] = xcat[t_tile - SUBLANES:]

    lane = lax.broadcasted_iota(jnp.int32, (1, LANES), 1)
    head_mask = (lane < GLA_DK, lane >= GLA_DK)
    hm0 = head_mask[0]
    r_t = lax.broadcasted_iota(jnp.int32, (t_tile, t_tile), 0)
    c_t = lax.broadcasted_iota(jnp.int32, (t_tile, t_tile), 1)
    tril_bd = ((r_t >= c_t) & (r_t // CHUNK == c_t // CHUNK)).astype(BF16)
    tril3 = jnp.concatenate([tril_bd, tril_bd, tril_bd], axis=1)
    r_l = lax.broadcasted_iota(jnp.int32, (LANES, LANES), 0)
    c_l = lax.broadcasted_iota(jnp.int32, (LANES, LANES), 1)
    triu_bd = ((r_l <= c_l) & (r_l // CHUNK == c_l // CHUNK)).astype(BF16)
    triu3 = jnp.concatenate([triu_bd, triu_bd, triu_bd], axis=0)
    row_c = lax.broadcasted_iota(jnp.int32, (CHUNK, LANES), 0)
    causal2 = (lax.broadcasted_iota(jnp.int32, (CHUNK, LANES), 1) % CHUNK) <= row_c
    sub_id = (lax.broadcasted_iota(jnp.int32, (t_tile, LANES), 0) % CHUNK) // SUB
    row_sub = lax.broadcasted_iota(jnp.int32, (2 * SUBLANES, LANES), 0)
    zero_tile = jnp.zeros((CHUNK, LANES), F32)
    zero_v = jnp.zeros((CHUNK, LANES), BF16)

    def chunk_rows(x, c):
        return x[c * CHUNK:(c + 1) * CHUNK]

    def diag_blocks(r):
        return [r[c * CHUNK:(c + 1) * CHUNK, c * LANES:(c + 1) * LANES] for c in range(nc)]

    small = small_ref[...]
    smallb = small + bsmall_ref[...]

    z = _dot(small, wg2_ref[...]) + bg_ref[...]
    log_a = _log_sigmoid(z) * (1.0 / GLA_TAU)
    cum = jnp.dot(tril3, _split3_rows(jnp.concatenate([log_a, _log_sigmoid(smallb)], axis=1)),
                  preferred_element_type=F32)
    g_all = gates_ref[0, 0] + brow_ref[...]
    bc_rows = jnp.dot(_split3_lanes(_log_sigmoid(g_all)), triu3, preferred_element_type=F32)

    for p in range(GLA_HEADS // 2):
        q2 = proj_ref[:, C_GQ + p * LANES:C_GQ + (p + 1) * LANES].astype(F32) * (GLA_DK ** -0.5)
        k2 = proj_ref[:, C_GK + p * LANES:C_GK + (p + 1) * LANES].astype(F32)
        bc = cum[:, p * LANES:(p + 1) * LANES]
        v = [proj_ref[:, C_GV + (2 * p + hl) * GLA_DV:C_GV + (2 * p + hl + 1) * GLA_DV] for hl in range(2)]

        ref_pt = [[chunk_rows(bc, c)[i * SUB - 1:i * SUB] if i else jnp.zeros((1, LANES), F32)
                   for i in range(n_sub)] for c in range(nc)]
        r_own = jnp.concatenate([jnp.broadcast_to(ref_pt[c][i], (SUB, LANES))
                                 for c in range(nc) for i in range(n_sub)], axis=0)
        qs = q2 * jnp.exp(bc - r_own)
        q_hat = jnp.concatenate([jnp.where(sub_id == i, qs, 0.0) for i in range(n_sub)], axis=1)
        k_hat, k_dec, last = [], [], []
        for c in range(nc):
            kc, bcc = chunk_rows(k2, c), chunk_rows(bc, c)
            ke = [kc * jnp.exp(ref_pt[c][i] - bcc) for i in range(n_sub)]
            for hl in range(2):
                k_hat.append(jnp.concatenate([jnp.where(head_mask[hl], ke[i], 0.0) for i in range(n_sub)], axis=1))
            last.append(bcc[CHUNK - 1:CHUNK])
            k_dec.append(kc * jnp.exp(last[c] - bcc))
        k_hat = jnp.concatenate(k_hat, axis=0)
        sc = diag_blocks(_dot_nt(q_hat, k_hat))
        sc = [jnp.where(causal2, s, 0.0) for s in sc]
        v_bd = jnp.concatenate([jnp.concatenate([chunk_rows(v[hl], c) if hl == b else zero_v for b in range(2)],
                                                axis=1) for c in range(nc) for hl in range(2)], axis=0)
        o_intra = _dot(_blocks_diag(sc, zero_tile), v_bd)
        u_all = _dot_tn(jnp.concatenate(v, axis=1), _blocks_diag(k_dec, zero_tile))

        st = gla_st[p]
        before = []
        for c in range(nc):
            before.append(st)
            u = jnp.where(hm0, u_all[:GLA_DV, c * LANES:(c + 1) * LANES], u_all[GLA_DV:, c * LANES:(c + 1) * LANES])
            st = st * jnp.exp(last[c]) + u
        gla_st[p] = st
        st_stack = jnp.concatenate(before, axis=0)
        q_in = q2 * jnp.exp(bc)
        for hl in range(2):
            h = 2 * p + hl
            o_inter = jnp.concatenate(diag_blocks(_dot_nt(jnp.where(head_mask[hl], q_in, 0.0), st_stack)), axis=0)
            o = o_intra[:, hl * GLA_DV:(hl + 1) * GLA_DV] + o_inter
            gate = proj_ref[:, C_GR + h * GLA_DV:C_GR + (h + 1) * GLA_DV].astype(F32)
            yg = _rms(o, gn_ref[:, h * GLA_DV:(h + 1) * GLA_DV]) * (gate * _sigmoid(gate))
            y_ref[:, h * GLA_DV:(h + 1) * GLA_DV] = yg.astype(y_ref.dtype)

    ones_tile = jnp.ones((t_tile, LANES), BF16)
    for p in range(ML_HEADS // 2):
        mq2 = qk[:, p * LANES:(p + 1) * LANES]
        mk2 = qk[:, ML_QK + p * LANES:ML_QK + (p + 1) * LANES] * (ML_DK ** -0.5)
        v = [proj_ref[:, C_MV + (2 * p + hl) * ML_DV:C_MV + (2 * p + hl + 1) * ML_DV] for hl in range(2)]
        k_stack = jnp.concatenate([jnp.where(head_mask[hl], chunk_rows(mk2, c), 0.0)
                                   for c in range(nc) for hl in range(2)], axis=0)
        qk_blocks = diag_blocks(_dot_nt(mq2, k_stack))

        w_blocks, wk_blocks, cm_cols, b_max, bc_last = [], [], [], [], []
        for c in range(nc):
            bcr = bc_rows[ML_HEADS * c + 2 + p:ML_HEADS * c + 3 + p]
            b_pair = g_all[ML_HEADS * c + p:ML_HEADS * c + p + 1] - bcr
            bm = jnp.where(causal2, b_pair, -jnp.inf)
            cm = [jnp.max(jnp.where(head_mask[hl], bm, -jnp.inf), axis=1, keepdims=True) for hl in range(2)]
            w_blocks.append(qk_blocks[c] * jnp.exp(bm - jnp.where(hm0, cm[0], cm[1])))
            cm_cols.append(cm)
            bmx = [jnp.max(jnp.where(head_mask[hl], b_pair, -jnp.inf), axis=1, keepdims=True) for hl in range(2)]
            b_max.append(jnp.where(hm0, bmx[0], bmx[1]))
            bc_last.append(jnp.where(hm0, bcr[:, CHUNK - 1:CHUNK], bcr[:, LANES - 1:LANES]))
            e_col = []
            for hl in range(2):
                h = 2 * p + hl
                b_col = (chunk_rows(smallb, c)[:, S_MI + h:S_MI + h + 1]
                         - chunk_rows(cum, c)[:, GLA_QK + S_MF + h:GLA_QK + S_MF + h + 1])
                e_col.append(jnp.exp(b_col - bmx[hl]))
            wk_blocks.append(chunk_rows(mk2, c) * jnp.where(hm0, e_col[0], e_col[1]))

        ones_col = [jnp.broadcast_to((lane == hl).astype(BF16), (CHUNK, LANES)) for hl in range(2)]
        v_bd = jnp.concatenate([jnp.concatenate([chunk_rows(v[hl], c) if hl == b else zero_v for b in range(2)]
                                                + [ones_col[hl]], axis=1)
                                for c in range(nc) for hl in range(2)], axis=0)
        nv = _dot(_blocks_diag(w_blocks, zero_tile), v_bd)
        u_all = _dot_tn(jnp.concatenate(v + [ones_tile], axis=1), _blocks_diag(wk_blocks, zero_tile))

        cs, ns, m_run = ml_cs[p], ml_ns[p], ml_m[p]
        before = []
        for c in range(nc):
            before.append((cs, ns, m_run))
            m_top = jnp.maximum(m_run, b_max[c])
            keep = jnp.exp(m_run - m_top)
            gain = jnp.exp(b_max[c] - m_top)
            blk = slice(c * LANES, (c + 1) * LANES)
            cs = keep * cs + gain * jnp.where(hm0, u_all[:ML_DV, blk], u_all[ML_DV:2 * ML_DV, blk])
            ns = keep * ns + gain * u_all[2 * ML_DV:2 * ML_DV + 1, blk]
            m_run = bc_last[c] + m_top
        ml_cs[p], ml_ns[p], ml_m[p] = cs, ns, m_run
        ns_rows = jnp.zeros((2 * SUBLANES, LANES), F32)
        for c in range(nc):
            ns_rows = jnp.where(row_sub == c, before[c][1], ns_rows)
        state_stack = jnp.concatenate([b[0] for b in before] + [ns_rows], axis=0)

        for hl in range(2):
            h = 2 * p + hl
            rr = _dot_nt(jnp.where(head_mask[hl], mq2, 0.0), state_stack)
            q_cs = diag_blocks(rr)
            h_chunks = []
            for c in range(nc):
                rows = slice(c * CHUNK, (c + 1) * CHUNK)
                q_ns = rr[rows, nc * ML_DV + c:nc * ML_DV + c + 1]
                m_prev = before[c][2][:, hl * ML_DK:hl * ML_DK + 1]
                cm = cm_cols[c][hl]
                m_top = jnp.maximum(m_prev, cm)
                local = jnp.exp(cm - m_top)
                carried = jnp.exp(m_prev - m_top)
                num = local * nv[rows, hl * ML_DV:(hl + 1) * ML_DV] + carried * q_cs[c]
                den = local * nv[rows, 2 * ML_DV + hl:2 * ML_DV + hl + 1] + carried * q_ns
                bc_col = chunk_rows(cum, c)[:, GLA_QK + S_MF + h:GLA_QK + S_MF + h + 1]
                h_chunks.append(num / jnp.maximum(jnp.abs(den), jnp.exp(-(bc_col + m_top))))
            og = proj_ref[:, C_MO + h * ML_DV:C_MO + (h + 1) * ML_DV].astype(F32)
            ym = _rms(jnp.concatenate(h_chunks, axis=0) * _sigmoid(og), mn_ref[:, h * ML_DV:(h + 1) * ML_DV])
            y_ref[:, GLA_V + h * ML_DV:GLA_V + (h + 1) * ML_DV] = ym.astype(y_ref.dtype)


def hybrid_core(proj, small, gates_t, w_gate2, b_gate, gla_norm, conv_w, b_small, b_rows, ml_norm, batch):
    m = proj.shape[0]
    seq = m // batch
    t = min(MIX_TILE, seq)
    nt = seq // t
    assert 2 * SUBLANES >= t // CHUNK
    return pl.pallas_call(
        _hybrid_kernel,
        grid=(batch, nt),
        in_specs=[pl.BlockSpec((t, C_SMALL), lambda b, i: (b * nt + i, 0)),
                  pl.BlockSpec((t, LANES), lambda b, i: (b * nt + i, 0)),
                  pl.BlockSpec((1, 1, ML_HEADS * (t // CHUNK), LANES), lambda b, i: (b, i, 0, 0)),
                  _const_spec((LANES, GLA_QK)),
                  _const_spec((1, GLA_QK)),
                  _const_spec((1, GLA_V)),
                  _const_spec((ML_CONV, 2 * ML_QK)),
                  _const_spec((1, LANES)),
                  _const_spec((ML_HEADS * (t // CHUNK), LANES)),
                  _const_spec((1, ML_V))],
        out_specs=pl.BlockSpec((t, GLA_V + ML_V), lambda b, i: (b * nt + i, 0)),
        out_shape=jax.ShapeDtypeStruct((m, GLA_V + ML_V), BF16),
        scratch_shapes=[pltpu.VMEM((GLA_HEADS // 2, GLA_DV, LANES), F32),
                        pltpu.VMEM((ML_HEADS // 2, ML_DV, LANES), F32),
                        pltpu.VMEM((ML_HEADS // 2, 1, LANES), F32),
                        pltpu.VMEM((ML_HEADS // 2, 1, LANES), F32),
                        pltpu.VMEM((SUBLANES, 2 * ML_QK), F32)],
        compiler_params=_params(2),
        name="hybrid_core",
    )(proj, small, gates_t, w_gate2, b_gate, gla_norm, conv_w, b_small, b_rows, ml_norm)


def permute_hybrid_weight(w_in):
    segs = np.cumsum([0, GLA_QK, GLA_QK, GLA_V, GLA_V, GLA_RANK, ML_QK, ML_QK, ML_V, ML_V, ML_HEADS, ML_HEADS])
    seg = lambda i: w_in[:, :, int(segs[i]):int(segs[i + 1])]
    pad = jnp.zeros(w_in.shape[:2] + (LANES - GLA_RANK - 2 * ML_HEADS,), w_in.dtype)
    return jnp.concatenate([seg(0), seg(1), seg(2), seg(3), seg(5), seg(6), seg(7), seg(8),
                            seg(4), seg(9), seg(10), pad], axis=2).astype(BF16)


def hybrid_layer(x, gain, w_perm, layer, w_gate2, b_gate, gla_norm, conv_w, b_if, ml_norm, batch):
    m, d = x.shape
    seq = m // batch
    proj, small = norm_proj(x, gain, w_perm, layer, LANES)
    gates = small[:, S_MI:S_MI + 2 * ML_HEADS]
    gates_t = gates.reshape(batch, seq // CHUNK, CHUNK, 2 * ML_HEADS).transpose(0, 1, 3, 2)
    t = min(MIX_TILE, seq)
    gates_t = gates_t.reshape(batch, seq // t, ML_HEADS * (t // CHUNK), LANES)
    b_small = jnp.zeros((1, LANES), F32).at[0, S_MI:S_MI + 2 * ML_HEADS].set(b_if)
    b_rows = jnp.tile(jnp.repeat(b_if, CHUNK).reshape(ML_HEADS, LANES), (t // CHUNK, 1))
    w_gate2_rows = jnp.zeros((LANES, GLA_QK), F32).at[S_GLR:S_GLR + GLA_RANK].set(w_gate2)
    return hybrid_core(proj, small, gates_t, w_gate2_rows, b_gate.reshape(1, -1), gla_norm.reshape(1, -1), conv_w,
                       b_small, b_rows, ml_norm.reshape(1, -1), batch)


LOG2E = 1.4426950408889634
AUG = 2 * MOBA_HD
LANES_PER_BLOCK = 4
MASK_BIG = 2.0 ** 100
PROJ_HEADS = 4
VT_ROWS = MOBA_HD + 16


def _split3(x):
    x = np.asarray(x, np.float32)
    hi = x.astype(BF16).astype(np.float32)
    mid = (x - hi).astype(BF16).astype(np.float32)
    lo = (x - hi - mid).astype(BF16).astype(np.float32)
    return np.stack([hi, mid, lo], axis=-1)


def _aug_tables(nb):
    c = np.asarray([LOG2E * 2.0 ** (-8.0 * (h + 1) / MOBA_HEADS) for h in range(MOBA_HEADS)], np.float32)
    pos = np.arange(MOBA_BLOCK, dtype=np.float32)
    base = LANES_PER_BLOCK * nb
    q_rows = np.zeros((MOBA_HEADS, MOBA_BLOCK, LANES), np.float32)
    k_rows = np.zeros((MOBA_HEADS, MOBA_BLOCK, LANES), np.float32)
    q_rows[:, :, base:base + 3] = _split3(-c[:, None] * pos[None, :])
    q_rows[:, :, base + 3:base + 6] = 1.0
    k_rows[:, :, base:base + 3] = 1.0
    k_rows[:, :, base + 3:base + 6] = _split3(c[:, None] * pos[None, :])
    i_idx = np.arange(nb)[:, None]
    j_idx = np.arange(nb)[None, :]
    steps = np.zeros((MOBA_HEADS, nb, nb, LANES_PER_BLOCK), np.float32)
    steps[..., 0] = np.where(j_idx == i_idx, 0.0, -MASK_BIG)
    gap = _split3(-c[:, None, None] * (MOBA_BLOCK * (i_idx - j_idx)).astype(np.float32)[None])
    steps[..., 1:] = np.where((j_idx <= i_idx)[None, :, :, None], gap, 0.0)
    q_steps = np.zeros((MOBA_HEADS, nb, 1, LANES), np.float32)
    q_steps[:, :, 0, :base] = steps.reshape(MOBA_HEADS, nb, base)
    k_steps = np.zeros((nb, 1, LANES), np.float32)
    k_steps[:, 0, :base] = np.repeat(np.eye(nb, dtype=np.float32), LANES_PER_BLOCK, axis=1)
    expand = np.zeros((nb, LANES), np.float32)
    expand[np.arange(nb), LANES_PER_BLOCK * np.arange(nb)] = MASK_BIG
    return (jnp.asarray(q_rows), jnp.asarray(q_steps), jnp.asarray(k_rows), jnp.asarray(k_steps),
            jnp.asarray(expand, BF16))


def _moba_prep_kernel(x_ref, g_ref, w_ref, qg_ref, kg_ref, qrow_ref, qstep_ref, krow_ref, kstep_ref, exp_ref,
                      qa_ref, ka_ref, vt_ref, kmean_ref):
    i = pl.program_id(1)
    nb = kmean_ref.shape[1]
    d = MOBA_HEADS * MOBA_HD

    @pl.when(i == 0)
    def _():
        kmean_ref[...] = jnp.zeros_like(kmean_ref)

    xn = _rms(x_ref[...], g_ref[...]).astype(BF16)
    blk = lax.broadcasted_iota(jnp.int32, (nb, MOBA_BLOCK), 0)
    past = blk < i
    ones_rows = (lax.broadcasted_iota(jnp.int32, (VT_ROWS - MOBA_HD, MOBA_BLOCK), 0) == 0).astype(F32)
    k_step = kstep_ref[i]

    for h in range(MOBA_HEADS):
        hg = h % PROJ_HEADS
        if hg == 0:
            c0 = h * MOBA_HD
            q_grp, k_grp, v_grp = (jnp.dot(xn, w_ref[:, off + c0:off + c0 + PROJ_HEADS * MOBA_HD],
                                           preferred_element_type=F32) for off in (0, d, 2 * d))
        part = slice(hg * MOBA_HD, (hg + 1) * MOBA_HD)
        q = _rms(q_grp[:, part], qg_ref[...]) * (MOBA_HD ** -0.5)
        k = _rms(k_grp[:, part], kg_ref[...])
        v = v_grp[:, part]
        vt_ref[0, h] = jnp.concatenate([v.T, ones_rows], axis=0).astype(vt_ref.dtype)

        gate = _dot_nt(kmean_ref[h], q)
        gate = jnp.where(past, gate, -jnp.inf)
        rank = jnp.zeros((nb, MOBA_BLOCK), F32)
        for j in range(nb):
            other = gate[j:j + 1]
            ahead = (other > gate) | ((other == gate) & (j < blk))
            rank = rank + ahead.astype(F32)
        sel = (past & (rank < MOBA_TOPK)).astype(BF16)
        q_extra = (qrow_ref[h] + qstep_ref[h, i]) + _dot_tn(sel, exp_ref[...])
        qa_ref[:, h * AUG:h * AUG + MOBA_HD] = (q * LOG2E).astype(qa_ref.dtype)
        qa_ref[:, h * AUG + MOBA_HD:(h + 1) * AUG] = q_extra.astype(qa_ref.dtype)
        ka_ref[:, h * AUG:h * AUG + MOBA_HD] = k.astype(ka_ref.dtype)
        ka_ref[:, h * AUG + MOBA_HD:(h + 1) * AUG] = (krow_ref[h] + k_step).astype(ka_ref.dtype)

        kmean_ref[h, pl.ds(i, 1), :] = jnp.mean(k, axis=0, keepdims=True)


def moba_prep(x, gain, w_qkv, layer, q_gain, k_gain, batch):
    m, dm = x.shape
    seq = m // batch
    nb = seq // MOBA_BLOCK
    assert LANES_PER_BLOCK * nb + 6 <= LANES
    d = MOBA_HEADS * MOBA_HD
    q_rows, q_steps, k_rows, k_steps, expand = _aug_tables(nb)
    return pl.pallas_call(
        _moba_prep_kernel,
        grid=(batch, nb),
        in_specs=[pl.BlockSpec((MOBA_BLOCK, dm), lambda b, i: (b * nb + i, 0)),
                  _const_spec((1, dm)),
                  _layer_spec((dm, 3 * d), layer),
                  _const_spec((1, MOBA_HD)),
                  _const_spec((1, MOBA_HD)),
                  _const_spec((MOBA_HEADS, MOBA_BLOCK, LANES)),
                  _const_spec((MOBA_HEADS, nb, 1, LANES)),
                  _const_spec((MOBA_HEADS, MOBA_BLOCK, LANES)),
                  _const_spec((nb, 1, LANES)),
                  _const_spec((nb, LANES))],
        out_specs=[pl.BlockSpec((MOBA_BLOCK, MOBA_HEADS * AUG), lambda b, i: (b * nb + i, 0)),
                   pl.BlockSpec((MOBA_BLOCK, MOBA_HEADS * AUG), lambda b, i: (b * nb + i, 0)),
                   pl.BlockSpec((1, MOBA_HEADS, VT_ROWS, MOBA_BLOCK), lambda b, i: (b, 0, 0, i))],
        out_shape=[jax.ShapeDtypeStruct((m, MOBA_HEADS * AUG), BF16),
                   jax.ShapeDtypeStruct((m, MOBA_HEADS * AUG), BF16),
                   jax.ShapeDtypeStruct((batch, MOBA_HEADS, VT_ROWS, seq), BF16)],
        scratch_shapes=[pltpu.VMEM((MOBA_HEADS, nb, MOBA_HD), F32)],
        compiler_params=_params(2),
        name="moba_prep",
    )(x, gain.reshape(1, dm), w_qkv, q_gain.reshape(1, -1), k_gain.reshape(1, -1),
      q_rows, q_steps, k_rows, k_steps, expand)


ATTN_HEADS_PER_STEP = 2
ATTN_CHUNK_BLOCKS = 4
PV_KEYS = 256


def _moba_attn_kernel(q_ref, k_ref, vt_ref, o_ref, s_even, s_odd, m_even, m_odd, *, heads, chunk_blocks, nb):
    t = pl.program_id(2)
    i = jnp.minimum(t, nb - 1)
    chunk_keys = chunk_blocks * MOBA_BLOCK
    n_chunks = i // chunk_blocks + 1
    n_valid2 = jnp.where(t > 0, (t - 1) // chunk_blocks + 1, 0)

    @pl.when((pl.program_id(0) == 0) & (pl.program_id(1) == 0) & (t == 0))
    def _():
        s_even[...] = jnp.zeros_like(s_even)
        s_odd[...] = jnp.zeros_like(s_odd)

    key_r = lax.broadcasted_iota(jnp.int32, (MOBA_BLOCK, MOBA_BLOCK), 0)
    qry_c = lax.broadcasted_iota(jnp.int32, (MOBA_BLOCK, MOBA_BLOCK), 1)
    causal = qry_c >= key_r
    q_aug = [q_ref[:, hh * AUG:(hh + 1) * AUG] for hh in range(heads)]

    def both(c, carry, last, s_w, s_r, m_row2, stage1, stage2):
        m_part, acc = carry
        k0 = c * chunk_keys
        live2 = jnp.full((1, MOBA_BLOCK), c, jnp.int32) < n_valid2
        new_m, new_acc = [], []
        for hh in range(heads):
            m_new, a = m_part[hh], acc[hh]
            if stage1:
                s = lax.dot_general(k_ref[pl.ds(k0, chunk_keys), hh * AUG:(hh + 1) * AUG], q_aug[hh],
                                    (((1,), (1,)), ((), ())), preferred_element_type=F32)
                if last:
                    parts = []
                    for u in range(chunk_blocks):
                        other_block = jnp.full((1, MOBA_BLOCK), c * chunk_blocks + u, jnp.int32) != i
                        parts.append(jnp.where(causal | other_block, s[u * MOBA_BLOCK:(u + 1) * MOBA_BLOCK], NEG))
                    s = jnp.concatenate(parts, axis=0)
                s_w[hh, pl.ds(k0, chunk_keys), :] = s
                m_new = jnp.maximum(m_new, jnp.max(s.reshape(chunk_keys // SUBLANES, SUBLANES, MOBA_BLOCK), axis=0))
            if stage2:
                for u in range(0, chunk_keys, PV_KEYS):
                    ku = k0 + u
                    p = jnp.exp2(s_r[hh, pl.ds(ku, PV_KEYS), :] - m_row2[hh])
                    p = jnp.where(live2, p, 0.0).astype(BF16)
                    a = a + jnp.dot(vt_ref[0, hh, :, pl.ds(ku, PV_KEYS)], p, preferred_element_type=F32)
            new_m.append(m_new)
            new_acc.append(a)
        return tuple(new_m), tuple(new_acc)

    def step(n, s_w, s_r, m_w, m_r, stage1=True, stage2=True):
        m_row2 = [m_r[hh] for hh in range(heads)]
        carry = (tuple(jnp.full((SUBLANES, MOBA_BLOCK), NEG, F32) for _ in range(heads)),
                 tuple(jnp.zeros((VT_ROWS, MOBA_BLOCK), F32) for _ in range(heads)))
        for c in range(n):
            carry = both(c, carry, c == n - 1, s_w, s_r, m_row2, stage1, stage2)
        m_part, acc = carry
        for hh in range(heads):
            if stage1:
                m_w[hh] = jnp.max(m_part[hh], axis=0, keepdims=True)
            if stage2:
                o = acc[hh][:MOBA_HD] * (1.0 / acc[hh][MOBA_HD:MOBA_HD + 1])
                o_ref[:, hh * MOBA_HD:(hh + 1) * MOBA_HD] = o.T.astype(o_ref.dtype)

    even = (s_even, s_odd, m_even, m_odd)
    odd = (s_odd, s_even, m_odd, m_even)
    n_max = nb // chunk_blocks
    pl.when(t == 0)(functools.partial(step, 1, *even, stage2=False))
    pl.when(t == nb)(functools.partial(step, n_max, *(odd if nb % 2 else even), stage1=False))
    inner = (t > 0) & (t < nb)
    for n in range(1, n_max + 1):
        pl.when(inner & (n_chunks == n) & (t % 2 == 0))(functools.partial(step, n, *even))
        pl.when(inner & (n_chunks == n) & (t % 2 == 1))(functools.partial(step, n, *odd))


def moba_attention(q_aug, k_aug, vt, batch):
    m = q_aug.shape[0]
    seq = m // batch
    nb = seq // MOBA_BLOCK
    hps = ATTN_HEADS_PER_STEP
    assert nb % ATTN_CHUNK_BLOCKS == 0
    return pl.pallas_call(
        functools.partial(_moba_attn_kernel, heads=hps, chunk_blocks=ATTN_CHUNK_BLOCKS, nb=nb),
        grid=(batch, MOBA_HEADS // hps, nb + 1),
        in_specs=[pl.BlockSpec((MOBA_BLOCK, hps * AUG), lambda b, g, t: (b * nb + jnp.minimum(t, nb - 1), g)),
                  pl.BlockSpec((seq, hps * AUG), lambda b, g, t: (b, g)),
                  pl.BlockSpec((1, hps, VT_ROWS, seq), lambda b, g, t: (b, g, 0, 0))],
        out_specs=pl.BlockSpec((MOBA_BLOCK, hps * MOBA_HD), lambda b, g, t: (b * nb + jnp.maximum(t - 1, 0), g)),
        out_shape=jax.ShapeDtypeStruct((m, MOBA_HEADS * MOBA_HD), BF16),
        scratch_shapes=[pltpu.VMEM((hps, seq, MOBA_BLOCK), F32), pltpu.VMEM((hps, seq, MOBA_BLOCK), F32),
                        pltpu.VMEM((hps, 1, MOBA_BLOCK), F32), pltpu.VMEM((hps, 1, MOBA_BLOCK), F32)],
        compiler_params=_params(3),
        name="moba_attn",
    )(q_aug, k_aug, vt)


def moba_layer(x, gain, w_qkv, layer, q_gain, k_gain, batch):
    q_aug, k_aug, vt = moba_prep(x, gain, w_qkv, layer, q_gain, k_gain, batch)
    return moba_attention(q_aug, k_aug, vt, batch)


def kernel(x, norm_mix, norm_ffn, hyb_w_in, gla_w_gate2, gla_b_gate, gla_norm, ml_conv, ml_b_if, ml_norm,
           hyb_w_out, moba_w_qkv, moba_q_norm, moba_k_norm, moba_w_o, ffn_w_gate, ffn_w_up, ffn_w_down):
    batch, seq, d = x.shape
    depth = norm_mix.shape[0]
    xf = x.reshape(batch * seq, d)
    w_hyb_in = permute_hybrid_weight(hyb_w_in)
    w_hyb_out, w_moba_qkv, w_moba_o = hyb_w_out.astype(BF16), moba_w_qkv.astype(BF16), moba_w_o.astype(BF16)
    w_gate, w_up, w_down = ffn_w_gate.astype(BF16), ffn_w_up.astype(BF16), ffn_w_down.astype(BF16)
    for l in range(depth):
        j = l // 2
        if l % 2 == 0:
            y = hybrid_layer(xf, norm_mix[l], w_hyb_in, j, gla_w_gate2[j], gla_b_gate[j], gla_norm[j],
                             ml_conv[j], ml_b_if[j], ml_norm[j], batch)
            w_o = w_hyb_out
        else:
            y = moba_layer(xf, norm_mix[l], w_moba_qkv, j, moba_q_norm[j], moba_k_norm[j], batch)
            w_o = w_moba_o
        xf = out_ffn(xf, y, w_o, j, norm_ffn[l], w_gate, w_up, w_down, l)
    return xf.reshape(batch, seq, d)
```

```python
import functools

import jax
import jax.numpy as jnp
import numpy as np
from jax import lax
from jax.experimental import pallas as pl
from jax.experimental.pallas import tpu as pltpu

F32 = jnp.float32
BF16 = jnp.bfloat16

LANES = 128
SUBLANES = 8
VMEM_LIMIT_BYTES = 56 * 1024 * 1024

EPS = 1e-6
NEG = -1e30

GLA_HEADS = 4
GLA_DK = 64
GLA_DV = 128
GLA_RANK = 16
GLA_TAU = 16.0
ML_HEADS = 4
ML_DK = 64
ML_DV = 128
ML_CONV = 4
CHUNK = 64
SUB = 16
MOBA_HEADS = 8
MOBA_HD = 128
MOBA_BLOCK = 256
MOBA_TOPK = 3

GLA_QK = GLA_HEADS * GLA_DK
GLA_V = GLA_HEADS * GLA_DV
ML_QK = ML_HEADS * ML_DK
ML_V = ML_HEADS * ML_DV

C_GQ = 0
C_GK = C_GQ + GLA_QK
C_GV = C_GK + GLA_QK
C_GR = C_GV + GLA_V
C_MQ = C_GR + GLA_V
C_MK = C_MQ + ML_QK
C_MV = C_MK + ML_QK
C_MO = C_MV + ML_V
C_SMALL = C_MO + ML_V
HYB_COLS = C_SMALL + LANES
S_GLR = 0
S_MI = GLA_RANK
S_MF = GLA_RANK + ML_HEADS

ROW_TILE = 512
MIX_TILE = 256
FF_TILE = 256
PROJ_COL_TILE = 512


def _rms(x, gain):
    return x * lax.rsqrt(jnp.mean(x * x, axis=-1, keepdims=True) + EPS) * gain


def _log_sigmoid(z):
    return jnp.minimum(z, 0.0) - jnp.log1p(jnp.exp(-jnp.abs(z)))


def _sigmoid(z):
    return 1.0 / (1.0 + jnp.exp(-z))


def _dot(a, b):
    return jnp.dot(a.astype(BF16), b.astype(BF16), preferred_element_type=F32)


def _dot_nt(a, b):
    return lax.dot_general(a.astype(BF16), b.astype(BF16), (((1,), (1,)), ((), ())),
                           preferred_element_type=F32)


def _dot_tn(a, b):
    return lax.dot_general(a.astype(BF16), b.astype(BF16), (((0,), (0,)), ((), ())),
                           preferred_element_type=F32)


def _const_spec(shape):
    return pl.BlockSpec(shape, lambda *_: (0,) * len(shape), pipeline_mode=pl.Buffered(1))


def _layer_spec(shape, layer):
    return pl.BlockSpec((None,) + tuple(shape), lambda *_: (layer,) + (0,) * len(shape),
                        pipeline_mode=pl.Buffered(1))


def _params(n_grid):
    return pltpu.CompilerParams(dimension_semantics=("arbitrary",) * n_grid,
                                vmem_limit_bytes=VMEM_LIMIT_BYTES)


def _out_ffn_kernel(x_ref, y_ref, wo_ref, g_ref, wg_ref, wu_ref, wd_ref, o_ref, h_ref, *, ff_tile):
    x1 = x_ref[...] + jnp.dot(y_ref[...], wo_ref[...], preferred_element_type=F32)
    xn = _rms(x1, g_ref[...]).astype(BF16)
    d_ff = wg_ref.shape[1]
    for c0 in range(0, d_ff, ff_tile):
        g = jnp.dot(xn, wg_ref[:, c0:c0 + ff_tile], preferred_element_type=F32)
        u = jnp.dot(xn, wu_ref[:, c0:c0 + ff_tile], preferred_element_type=F32)
        h_ref[:, c0:c0 + ff_tile] = (g * _sigmoid(g) * u).astype(BF16)
    o_ref[...] = x1 + jnp.dot(h_ref[...], wd_ref[...], preferred_element_type=F32)


def out_ffn(x, y, w_o, o_layer, gain, w_gate, w_up, w_down, layer):
    m, d = x.shape
    dy = y.shape[1]
    d_ff = w_gate.shape[2]
    tm = min(ROW_TILE, m)
    return pl.pallas_call(
        functools.partial(_out_ffn_kernel, ff_tile=FF_TILE),
        grid=(m // tm,),
        in_specs=[pl.BlockSpec((tm, d), lambda i: (i, 0)),
                  pl.BlockSpec((tm, dy), lambda i: (i, 0)),
                  _layer_spec((dy, d), o_layer),
                  _const_spec((1, d)),
                  _layer_spec((d, d_ff), layer),
                  _layer_spec((d, d_ff), layer),
                  _layer_spec((d_ff, d), layer)],
        out_specs=pl.BlockSpec((tm, d), lambda i: (i, 0)),
        out_shape=jax.ShapeDtypeStruct((m, d), F32),
        scratch_shapes=[pltpu.VMEM((tm, d_ff), BF16)],
        compiler_params=_params(1),
        name="out_ffn",
    )(x, y, w_o, gain.reshape(1, d), w_gate, w_up, w_down)


def _split3_rows(x):
    hi = x.astype(BF16)
    r1 = x - hi.astype(F32)
    mid = r1.astype(BF16)
    lo = (r1 - mid.astype(F32)).astype(BF16)
    return jnp.concatenate([hi, mid, lo], axis=0)


def _blocks_diag(tiles, zero):
    n = len(tiles)
    return jnp.concatenate(
        [jnp.concatenate([tiles[a] if a == b else zero for b in range(n)], axis=1) for a in range(n)], axis=0)


def _hybrid_kernel(x_ref, g_ref, w_ref, wg2_ref, bg_ref, gn_ref, conv_ref, bsmall_ref, mn_ref,
                   y_ref, proj_ref, gla_st, ml_cs, ml_ns, ml_m, conv_carry):
    t_tile = proj_ref.shape[0]
    nc = t_tile // CHUNK
    n_sub = CHUNK // SUB

    @pl.when(pl.program_id(1) == 0)
    def _():
        gla_st[...] = jnp.zeros_like(gla_st)
        ml_cs[...] = jnp.zeros_like(ml_cs)
        ml_ns[...] = jnp.zeros_like(ml_ns)
        ml_m[...] = jnp.zeros_like(ml_m)
        conv_carry[...] = jnp.zeros_like(conv_carry)

    xn = _rms(x_ref[...], g_ref[...]).astype(BF16)
    for c0 in range(0, C_SMALL, PROJ_COL_TILE):
        proj_ref[:, c0:c0 + PROJ_COL_TILE] = jnp.dot(xn, w_ref[:, c0:c0 + PROJ_COL_TILE],
                                                      preferred_element_type=F32).astype(proj_ref.dtype)
    small = jnp.dot(xn, w_ref[:, C_SMALL:], preferred_element_type=F32)

    xcat = proj_ref[:, C_MQ:C_MQ + 2 * ML_QK].astype(F32)
    ext = jnp.concatenate([conv_carry[...], xcat], axis=0)
    cw = conv_ref[...]
    acc = ext * cw[ML_CONV - 1:ML_CONV]
    for j in range(1, ML_CONV):
        acc = acc + pltpu.roll(ext, j, axis=0) * cw[ML_CONV - 1 - j:ML_CONV - j]
    conv = acc[SUBLANES:]
    qk = conv * _sigmoid(conv)
    conv_carry[...] = xcat[t_tile - SUBLANES:]

    lane = lax.broadcasted_iota(jnp.int32, (1, LANES), 1)
    head_mask = (lane < GLA_DK, lane >= GLA_DK)
    hm0 = head_mask[0]
    r_t = lax.broadcasted_iota(jnp.int32, (t_tile, t_tile), 0)
    c_t = lax.broadcasted_iota(jnp.int32, (t_tile, t_tile), 1)
    tril_bd = ((r_t >= c_t) & (r_t // CHUNK == c_t // CHUNK)).astype(BF16)
    tril3 = jnp.concatenate([tril_bd, tril_bd, tril_bd], axis=1)
    col_w = lax.broadcasted_iota(jnp.int32, (t_tile, 2 * nc * LANES), 1)
    row_w = lax.broadcasted_iota(jnp.int32, (t_tile, 2 * nc * LANES), 0)
    to_rows = ((col_w % CHUNK == row_w % CHUNK) & ((col_w % (nc * LANES)) // LANES == row_w // CHUNK)
               & ((col_w % LANES) // CHUNK == col_w // (nc * LANES))).astype(BF16)
    to_rows3 = jnp.concatenate([to_rows, to_rows, to_rows], axis=0)
    row_c = lax.broadcasted_iota(jnp.int32, (CHUNK, LANES), 0)
    causal2 = (lax.broadcasted_iota(jnp.int32, (CHUNK, LANES), 1) % CHUNK) <= row_c
    sub_id = (lax.broadcasted_iota(jnp.int32, (t_tile, LANES), 0) % CHUNK) // SUB
    row_sub = lax.broadcasted_iota(jnp.int32, (2 * SUBLANES, LANES), 0)
    zero_tile = jnp.zeros((CHUNK, LANES), F32)
    zero_v = jnp.zeros((CHUNK, LANES), BF16)

    def chunk_rows(x, c):
        return x[c * CHUNK:(c + 1) * CHUNK]

    def diag_blocks(r):
        return [r[c * CHUNK:(c + 1) * CHUNK, c * LANES:(c + 1) * LANES] for c in range(nc)]

    smallb = small + bsmall_ref[...]

    z = _dot(small, wg2_ref[...]) + bg_ref[...]
    log_a = _log_sigmoid(z) * (1.0 / GLA_TAU)
    cum = jnp.dot(tril3, _split3_rows(jnp.concatenate([log_a, _log_sigmoid(smallb)], axis=1)),
                  preferred_element_type=F32)
    b_cols = smallb - pltpu.roll(cum[:, GLA_QK:], LANES - ML_HEADS, axis=1)
    b_rows = lax.dot_general(_split3_rows(b_cols), to_rows3, (((0,), (0,)), ((), ())), preferred_element_type=F32)

    for p in range(GLA_HEADS // 2):
        q2 = proj_ref[:, C_GQ + p * LANES:C_GQ + (p + 1) * LANES].astype(F32) * (GLA_DK ** -0.5)
        k2 = proj_ref[:, C_GK + p * LANES:C_GK + (p + 1) * LANES].astype(F32)
        bc = cum[:, p * LANES:(p + 1) * LANES]
        v = [proj_ref[:, C_GV + (2 * p + hl) * GLA_DV:C_GV + (2 * p + hl + 1) * GLA_DV] for hl in range(2)]

        ref_pt = [[chunk_rows(bc, c)[i * SUB - 1:i * SUB] if i else jnp.zeros((1, LANES), F32)
                   for i in range(n_sub)] for c in range(nc)]
        r_own = jnp.concatenate([jnp.broadcast_to(ref_pt[c][i], (SUB, LANES))
                                 for c in range(nc) for i in range(n_sub)], axis=0)
        qs = q2 * jnp.exp(bc - r_own)
        q_hat = jnp.concatenate([jnp.where(sub_id == i, qs, 0.0) for i in range(n_sub)], axis=1)
        k_hat, k_dec, last = [], [], []
        for c in range(nc):
            kc, bcc = chunk_rows(k2, c), chunk_rows(bc, c)
            ke = [kc * jnp.exp(ref_pt[c][i] - bcc) for i in range(n_sub)]
            for hl in range(2):
                k_hat.append(jnp.concatenate([jnp.where(head_mask[hl], ke[i], 0.0) for i in range(n_sub)], axis=1))
            last.append(bcc[CHUNK - 1:CHUNK])
            k_dec.append(kc * jnp.exp(last[c] - bcc))
        k_hat = jnp.concatenate(k_hat, axis=0)
        sc = diag_blocks(_dot_nt(q_hat, k_hat))
        sc = [jnp.where(causal2, s, 0.0) for s in sc]
        v_bd = jnp.concatenate([jnp.concatenate([chunk_rows(v[hl], c) if hl == b else zero_v for b in range(2)],
                                                axis=1) for c in range(nc) for hl in range(2)], axis=0)
        o_intra = _dot(_blocks_diag(sc, zero_tile), v_bd)
        u_all = _dot_tn(jnp.concatenate(v, axis=1), _blocks_diag(k_dec, zero_tile))

        st = gla_st[p]
        before = []
        for c in range(nc):
            before.append(st)
            u = jnp.where(hm0, u_all[:GLA_DV, c * LANES:(c + 1) * LANES], u_all[GLA_DV:, c * LANES:(c + 1) * LANES])
            st = st * jnp.exp(last[c]) + u
        gla_st[p] = st
        st_stack = jnp.concatenate(before, axis=0)
        q_in = q2 * jnp.exp(bc)
        for hl in range(2):
            h = 2 * p + hl
            o_inter = jnp.concatenate(diag_blocks(_dot_nt(jnp.where(head_mask[hl], q_in, 0.0), st_stack)), axis=0)
            o = o_intra[:, hl * GLA_DV:(hl + 1) * GLA_DV] + o_inter
            gate = proj_ref[:, C_GR + h * GLA_DV:C_GR + (h + 1) * GLA_DV].astype(F32)
            yg = _rms(o, gn_ref[:, h * GLA_DV:(h + 1) * GLA_DV]) * (gate * _sigmoid(gate))
            y_ref[:, h * GLA_DV:(h + 1) * GLA_DV] = yg.astype(y_ref.dtype)

    ones_tile = jnp.ones((t_tile, LANES), BF16)
    for p in range(ML_HEADS // 2):
        mq2 = qk[:, p * LANES:(p + 1) * LANES]
        mk2 = qk[:, ML_QK + p * LANES:ML_QK + (p + 1) * LANES] * (ML_DK ** -0.5)
        v = [proj_ref[:, C_MV + (2 * p + hl) * ML_DV:C_MV + (2 * p + hl + 1) * ML_DV] for hl in range(2)]
        k_stack = jnp.concatenate([jnp.where(head_mask[hl], chunk_rows(mk2, c), 0.0)
                                   for c in range(nc) for hl in range(2)], axis=0)
        qk_blocks = diag_blocks(_dot_nt(mq2, k_stack))

        w_blocks, wk_blocks, cm_cols, b_max, bc_last = [], [], [], [], []
        for c in range(nc):
            blk = slice(c * LANES, (c + 1) * LANES)
            b_pair = (b_rows[S_MI + 2 * p:S_MI + 2 * p + 1, blk]
                      + b_rows[S_MI + 2 * p + 1:S_MI + 2 * p + 2, nc * LANES + c * LANES:nc * LANES + (c + 1) * LANES])
            f_last = [chunk_rows(cum, c)[CHUNK - 1:CHUNK, GLA_QK + S_MF + 2 * p + hl:GLA_QK + S_MF + 2 * p + hl + 1]
                      for hl in range(2)]
            bm = jnp.where(causal2, b_pair, -jnp.inf)
            cm = [jnp.max(jnp.where(head_mask[hl], bm, -jnp.inf), axis=1, keepdims=True) for hl in range(2)]
            w_blocks.append(qk_blocks[c] * jnp.exp(bm - jnp.where(hm0, cm[0], cm[1])))
            cm_cols.append(cm)
            bmx = [jnp.max(jnp.where(head_mask[hl], b_pair, -jnp.inf), axis=1, keepdims=True) for hl in range(2)]
            b_max.append(jnp.where(hm0, bmx[0], bmx[1]))
            bc_last.append(jnp.where(hm0, f_last[0], f_last[1]))
            e_col = []
            for hl in range(2):
                h = 2 * p + hl
                b_col = (chunk_rows(smallb, c)[:, S_MI + h:S_MI + h + 1]
                         - chunk_rows(cum, c)[:, GLA_QK + S_MF + h:GLA_QK + S_MF + h + 1])
                e_col.append(jnp.exp(b_col - bmx[hl]))
            wk_blocks.append(chunk_rows(mk2, c) * jnp.where(hm0, e_col[0], e_col[1]))

        ones_col = [jnp.broadcast_to((lane == hl).astype(BF16), (CHUNK, LANES)) for hl in range(2)]
        v_bd = jnp.concatenate([jnp.concatenate([chunk_rows(v[hl], c) if hl == b else zero_v for b in range(2)]
                                                + [ones_col[hl]], axis=1)
                                for c in range(nc) for hl in range(2)], axis=0)
        nv = _dot(_blocks_diag(w_blocks, zero_tile), v_bd)
        u_all = _dot_tn(jnp.concatenate(v + [ones_tile], axis=1), _blocks_diag(wk_blocks, zero_tile))

        cs, ns, m_run = ml_cs[p], ml_ns[p], ml_m[p]
        before = []
        for c in range(nc):
            before.append((cs, ns, m_run))
            m_top = jnp.maximum(m_run, b_max[c])
            keep = jnp.exp(m_run - m_top)
            gain = jnp.exp(b_max[c] - m_top)
            blk = slice(c * LANES, (c + 1) * LANES)
            cs = keep * cs + gain * jnp.where(hm0, u_all[:ML_DV, blk], u_all[ML_DV:2 * ML_DV, blk])
            ns = keep * ns + gain * u_all[2 * ML_DV:2 * ML_DV + 1, blk]
            m_run = bc_last[c] + m_top
        ml_cs[p], ml_ns[p], ml_m[p] = cs, ns, m_run
        ns_rows = jnp.zeros((2 * SUBLANES, LANES), F32)
        for c in range(nc):
            ns_rows = jnp.where(row_sub == c, before[c][1], ns_rows)
        state_stack = jnp.concatenate([b[0] for b in before] + [ns_rows], axis=0)

        for hl in range(2):
            h = 2 * p + hl
            rr = _dot_nt(jnp.where(head_mask[hl], mq2, 0.0), state_stack)
            q_cs = diag_blocks(rr)
            h_chunks = []
            for c in range(nc):
                rows = slice(c * CHUNK, (c + 1) * CHUNK)
                q_ns = rr[rows, nc * ML_DV + c:nc * ML_DV + c + 1]
                m_prev = before[c][2][:, hl * ML_DK:hl * ML_DK + 1]
                cm = cm_cols[c][hl]
                m_top = jnp.maximum(m_prev, cm)
                local = jnp.exp(cm - m_top)
                carried = jnp.exp(m_prev - m_top)
                num = local * nv[rows, hl * ML_DV:(hl + 1) * ML_DV] + carried * q_cs[c]
                den = local * nv[rows, 2 * ML_DV + hl:2 * ML_DV + hl + 1] + carried * q_ns
                bc_col = chunk_rows(cum, c)[:, GLA_QK + S_MF + h:GLA_QK + S_MF + h + 1]
                h_chunks.append(num / jnp.maximum(jnp.abs(den), jnp.exp(-(bc_col + m_top))))
            og = proj_ref[:, C_MO + h * ML_DV:C_MO + (h + 1) * ML_DV].astype(F32)
            ym = _rms(jnp.concatenate(h_chunks, axis=0) * _sigmoid(og), mn_ref[:, h * ML_DV:(h + 1) * ML_DV])
            y_ref[:, GLA_V + h * ML_DV:GLA_V + (h + 1) * ML_DV] = ym.astype(y_ref.dtype)


def permute_hybrid_weight(w_in):
    segs = np.cumsum([0, GLA_QK, GLA_QK, GLA_V, GLA_V, GLA_RANK, ML_QK, ML_QK, ML_V, ML_V, ML_HEADS, ML_HEADS])
    seg = lambda i: w_in[:, :, int(segs[i]):int(segs[i + 1])]
    pad = jnp.zeros(w_in.shape[:2] + (LANES - GLA_RANK - 2 * ML_HEADS,), w_in.dtype)
    return jnp.concatenate([seg(0), seg(1), seg(2), seg(3), seg(5), seg(6), seg(7), seg(8),
                            seg(4), seg(9), seg(10), pad], axis=2).astype(BF16)


def hybrid_layer(x, gain, w_perm, layer, w_gate2, b_gate, gla_norm, conv_w, b_if, ml_norm, batch):
    m, d = x.shape
    seq = m // batch
    t = min(MIX_TILE, seq)
    nt = seq // t
    assert 2 * SUBLANES >= t // CHUNK
    b_small = jnp.zeros((1, LANES), F32).at[0, S_MI:S_MI + 2 * ML_HEADS].set(b_if)
    w_gate2_rows = jnp.zeros((LANES, GLA_QK), F32).at[S_GLR:S_GLR + GLA_RANK].set(w_gate2)
    return pl.pallas_call(
        _hybrid_kernel,
        grid=(batch, nt),
        in_specs=[pl.BlockSpec((t, d), lambda b, i: (b * nt + i, 0)),
                  _const_spec((1, d)),
                  _layer_spec((d, HYB_COLS), layer),
                  _const_spec((LANES, GLA_QK)),
                  _const_spec((1, GLA_QK)),
                  _const_spec((1, GLA_V)),
                  _const_spec((ML_CONV, 2 * ML_QK)),
                  _const_spec((1, LANES)),
                  _const_spec((1, ML_V))],
        out_specs=pl.BlockSpec((t, GLA_V + ML_V), lambda b, i: (b * nt + i, 0)),
        out_shape=jax.ShapeDtypeStruct((m, GLA_V + ML_V), BF16),
        scratch_shapes=[pltpu.VMEM((t, C_SMALL), BF16),
                        pltpu.VMEM((GLA_HEADS // 2, GLA_DV, LANES), F32),
                        pltpu.VMEM((ML_HEADS // 2, ML_DV, LANES), F32),
                        pltpu.VMEM((ML_HEADS // 2, 1, LANES), F32),
                        pltpu.VMEM((ML_HEADS // 2, 1, LANES), F32),
                        pltpu.VMEM((SUBLANES, 2 * ML_QK), F32)],
        compiler_params=_params(2),
        name="hybrid_core",
    )(x, gain.reshape(1, d), w_perm, w_gate2_rows, b_gate.reshape(1, -1), gla_norm.reshape(1, -1), conv_w,
      b_small, ml_norm.reshape(1, -1))


LOG2E = 1.4426950408889634
AUG = 2 * MOBA_HD
LANES_PER_BLOCK = 4
MASK_BIG = 2.0 ** 100
PROJ_HEADS = 4
VT_ROWS = MOBA_HD + 16


def _split3(x):
    x = np.asarray(x, np.float32)
    hi = x.astype(BF16).astype(np.float32)
    mid = (x - hi).astype(BF16).astype(np.float32)
    lo = (x - hi - mid).astype(BF16).astype(np.float32)
    return np.stack([hi, mid, lo], axis=-1)


def _aug_tables(nb):
    c = np.asarray([LOG2E * 2.0 ** (-8.0 * (h + 1) / MOBA_HEADS) for h in range(MOBA_HEADS)], np.float32)
    pos = np.arange(MOBA_BLOCK, dtype=np.float32)
    base = LANES_PER_BLOCK * nb
    q_rows = np.zeros((MOBA_HEADS, MOBA_BLOCK, LANES), np.float32)
    k_rows = np.zeros((MOBA_HEADS, MOBA_BLOCK, LANES), np.float32)
    q_rows[:, :, base:base + 3] = _split3(-c[:, None] * pos[None, :])
    q_rows[:, :, base + 3:base + 6] = 1.0
    k_rows[:, :, base:base + 3] = 1.0
    k_rows[:, :, base + 3:base + 6] = _split3(c[:, None] * pos[None, :])
    i_idx = np.arange(nb)[:, None]
    j_idx = np.arange(nb)[None, :]
    steps = np.zeros((MOBA_HEADS, nb, nb, LANES_PER_BLOCK), np.float32)
    steps[..., 0] = np.where(j_idx == i_idx, 0.0, -MASK_BIG)
    gap = _split3(-c[:, None, None] * (MOBA_BLOCK * (i_idx - j_idx)).astype(np.float32)[None])
    steps[..., 1:] = np.where((j_idx <= i_idx)[None, :, :, None], gap, 0.0)
    q_steps = np.zeros((MOBA_HEADS, nb, 1, LANES), np.float32)
    q_steps[:, :, 0, :base] = steps.reshape(MOBA_HEADS, nb, base)
    k_steps = np.zeros((nb, 1, LANES), np.float32)
    k_steps[:, 0, :base] = np.repeat(np.eye(nb, dtype=np.float32), LANES_PER_BLOCK, axis=1)
    expand = np.zeros((nb, LANES), np.float32)
    expand[np.arange(nb), LANES_PER_BLOCK * np.arange(nb)] = MASK_BIG
    return (jnp.asarray(q_rows), jnp.asarray(q_steps), jnp.asarray(k_rows), jnp.asarray(k_steps),
            jnp.asarray(expand, BF16))


def _moba_prep_kernel(x_ref, g_ref, w_ref, qg_ref, kg_ref, qrow_ref, qstep_ref, krow_ref, kstep_ref, exp_ref,
                      qa_ref, ka_ref, vt_ref, kmean_ref):
    i = pl.program_id(1)
    nb = kmean_ref.shape[1]
    d = MOBA_HEADS * MOBA_HD

    @pl.when(i == 0)
    def _():
        kmean_ref[...] = jnp.zeros_like(kmean_ref)

    xn = _rms(x_ref[...], g_ref[...]).astype(BF16)
    blk = lax.broadcasted_iota(jnp.int32, (nb, MOBA_BLOCK), 0)
    past = blk < i
    ones_rows = (lax.broadcasted_iota(jnp.int32, (VT_ROWS - MOBA_HD, MOBA_BLOCK), 0) == 0).astype(F32)
    k_step = kstep_ref[i]

    for h in range(MOBA_HEADS):
        hg = h % PROJ_HEADS
        if hg == 0:
            c0 = h * MOBA_HD
            q_grp, k_grp, v_grp = (jnp.dot(xn, w_ref[:, off + c0:off + c0 + PROJ_HEADS * MOBA_HD],
                                           preferred_element_type=F32) for off in (0, d, 2 * d))
        part = slice(hg * MOBA_HD, (hg + 1) * MOBA_HD)
        q = _rms(q_grp[:, part], qg_ref[...]) * (MOBA_HD ** -0.5)
        k = _rms(k_grp[:, part], kg_ref[...])
        v = v_grp[:, part]
        vt_ref[0, h] = jnp.concatenate([v.T, ones_rows], axis=0).astype(vt_ref.dtype)

        gate = _dot_nt(kmean_ref[h], q)
        gate = jnp.where(past, gate, -jnp.inf)
        rank = jnp.zeros((nb, MOBA_BLOCK), F32)
        for j in range(nb):
            other = gate[j:j + 1]
            ahead = (other > gate) | ((other == gate) & (j < blk))
            rank = rank + ahead.astype(F32)
        sel = (past & (rank < MOBA_TOPK)).astype(BF16)
        q_extra = (qrow_ref[h] + qstep_ref[h, i]) + _dot_tn(sel, exp_ref[...])
        qa_ref[:, h * AUG:h * AUG + MOBA_HD] = (q * LOG2E).astype(qa_ref.dtype)
        qa_ref[:, h * AUG + MOBA_HD:(h + 1) * AUG] = q_extra.astype(qa_ref.dtype)
        ka_ref[:, h * AUG:h * AUG + MOBA_HD] = k.astype(ka_ref.dtype)
        ka_ref[:, h * AUG + MOBA_HD:(h + 1) * AUG] = (krow_ref[h] + k_step).astype(ka_ref.dtype)

        kmean_ref[h, pl.ds(i, 1), :] = jnp.mean(k, axis=0, keepdims=True)


def moba_prep(x, gain, w_qkv, layer, q_gain, k_gain, batch):
    m, dm = x.shape
    seq = m // batch
    nb = seq // MOBA_BLOCK
    assert LANES_PER_BLOCK * nb + 6 <= LANES
    d = MOBA_HEADS * MOBA_HD
    q_rows, q_steps, k_rows, k_steps, expand = _aug_tables(nb)
    return pl.pallas_call(
        _moba_prep_kernel,
        grid=(batch, nb),
        in_specs=[pl.BlockSpec((MOBA_BLOCK, dm), lambda b, i: (b * nb + i, 0)),
                  _const_spec((1, dm)),
                  _layer_spec((dm, 3 * d), layer),
                  _const_spec((1, MOBA_HD)),
                  _const_spec((1, MOBA_HD)),
                  _const_spec((MOBA_HEADS, MOBA_BLOCK, LANES)),
                  _const_spec((MOBA_HEADS, nb, 1, LANES)),
                  _const_spec((MOBA_HEADS, MOBA_BLOCK, LANES)),
                  _const_spec((nb, 1, LANES)),
                  _const_spec((nb, LANES))],
        out_specs=[pl.BlockSpec((MOBA_BLOCK, MOBA_HEADS * AUG), lambda b, i: (b * nb + i, 0)),
                   pl.BlockSpec((MOBA_BLOCK, MOBA_HEADS * AUG), lambda b, i: (b * nb + i, 0)),
                   pl.BlockSpec((1, MOBA_HEADS, VT_ROWS, MOBA_BLOCK), lambda b, i: (b, 0, 0, i))],
        out_shape=[jax.ShapeDtypeStruct((m, MOBA_HEADS * AUG), BF16),
                   jax.ShapeDtypeStruct((m, MOBA_HEADS * AUG), BF16),
                   jax.ShapeDtypeStruct((batch, MOBA_HEADS, VT_ROWS, seq), BF16)],
        scratch_shapes=[pltpu.VMEM((MOBA_HEADS, nb, MOBA_HD), F32)],
        compiler_params=_params(2),
        name="moba_prep",
    )(x, gain.reshape(1, dm), w_qkv, q_gain.reshape(1, -1), k_gain.reshape(1, -1),
      q_rows, q_steps, k_rows, k_steps, expand)


ATTN_HEADS_PER_STEP = 2
ATTN_CHUNK_BLOCKS = 4
PV_KEYS = 256


def _moba_attn_kernel(q_ref, k_ref, vt_ref, o_ref, s_even, s_odd, m_even, m_odd, *, heads, chunk_blocks, nb):
    t = pl.program_id(2)
    i = jnp.minimum(t, nb - 1)
    chunk_keys = chunk_blocks * MOBA_BLOCK
    n_chunks = i // chunk_blocks + 1
    n_valid2 = jnp.where(t > 0, (t - 1) // chunk_blocks + 1, 0)

    @pl.when((pl.program_id(0) == 0) & (pl.program_id(1) == 0) & (t == 0))
    def _():
        s_even[...] = jnp.zeros_like(s_even)
        s_odd[...] = jnp.zeros_like(s_odd)

    key_r = lax.broadcasted_iota(jnp.int32, (MOBA_BLOCK, MOBA_BLOCK), 0)
    qry_c = lax.broadcasted_iota(jnp.int32, (MOBA_BLOCK, MOBA_BLOCK), 1)
    causal = qry_c >= key_r
    q_aug = [q_ref[:, hh * AUG:(hh + 1) * AUG] for hh in range(heads)]

    def both(c, carry, last, s_w, s_r, m_row2, stage1, stage2):
        m_part, acc = carry
        k0 = c * chunk_keys
        live2 = jnp.full((1, MOBA_BLOCK), c, jnp.int32) < n_valid2
        new_m, new_acc = [], []
        for hh in range(heads):
            m_new, a = m_part[hh], acc[hh]
            if stage1:
                s = lax.dot_general(k_ref[pl.ds(k0, chunk_keys), hh * AUG:(hh + 1) * AUG], q_aug[hh],
                                    (((1,), (1,)), ((), ())), preferred_element_type=F32)
                if last:
                    parts = []
                    for u in range(chunk_blocks):
                        other_block = jnp.full((1, MOBA_BLOCK), c * chunk_blocks + u, jnp.int32) != i
                        parts.append(jnp.where(causal | other_block, s[u * MOBA_BLOCK:(u + 1) * MOBA_BLOCK], NEG))
                    s = jnp.concatenate(parts, axis=0)
                s_w[hh, pl.ds(k0, chunk_keys), :] = s
                m_new = jnp.maximum(m_new, jnp.max(s.reshape(chunk_keys // SUBLANES, SUBLANES, MOBA_BLOCK), axis=0))
            if stage2:
                for u in range(0, chunk_keys, PV_KEYS):
                    ku = k0 + u
                    p = jnp.exp2(s_r[hh, pl.ds(ku, PV_KEYS), :] - m_row2[hh])
                    p = jnp.where(live2, p, 0.0).astype(BF16)
                    a = a + jnp.dot(vt_ref[0, hh, :, pl.ds(ku, PV_KEYS)], p, preferred_element_type=F32)
            new_m.append(m_new)
            new_acc.append(a)
        return tuple(new_m), tuple(new_acc)

    def step(n, s_w, s_r, m_w, m_r, stage1=True, stage2=True):
        m_row2 = [m_r[hh] for hh in range(heads)]
        carry = (tuple(jnp.full((SUBLANES, MOBA_BLOCK), NEG, F32) for _ in range(heads)),
                 tuple(jnp.zeros((VT_ROWS, MOBA_BLOCK), F32) for _ in range(heads)))
        for c in range(n):
            carry = both(c, carry, c == n - 1, s_w, s_r, m_row2, stage1, stage2)
        m_part, acc = carry
        for hh in range(heads):
            if stage1:
                m_w[hh] = jnp.max(m_part[hh], axis=0, keepdims=True)
            if stage2:
                o = acc[hh][:MOBA_HD] * (1.0 / acc[hh][MOBA_HD:MOBA_HD + 1])
                o_ref[:, hh * MOBA_HD:(hh + 1) * MOBA_HD] = o.T.astype(o_ref.dtype)

    even = (s_even, s_odd, m_even, m_odd)
    odd = (s_odd, s_even, m_odd, m_even)
    n_max = nb // chunk_blocks
    pl.when(t == 0)(functools.partial(step, 1, *even, stage2=False))
    pl.when(t == nb)(functools.partial(step, n_max, *(odd if nb % 2 else even), stage1=False))
    inner = (t > 0) & (t < nb)
    for n in range(1, n_max + 1):
        pl.when(inner & (n_chunks == n) & (t % 2 == 0))(functools.partial(step, n, *even))
        pl.when(inner & (n_chunks == n) & (t % 2 == 1))(functools.partial(step, n, *odd))


def moba_attention(q_aug, k_aug, vt, batch):
    m = q_aug.shape[0]
    seq = m // batch
    nb = seq // MOBA_BLOCK
    hps = ATTN_HEADS_PER_STEP
    assert nb % ATTN_CHUNK_BLOCKS == 0
    return pl.pallas_call(
        functools.partial(_moba_attn_kernel, heads=hps, chunk_blocks=ATTN_CHUNK_BLOCKS, nb=nb),
        grid=(batch, MOBA_HEADS // hps, nb + 1),
        in_specs=[pl.BlockSpec((MOBA_BLOCK, hps * AUG), lambda b, g, t: (b * nb + jnp.minimum(t, nb - 1), g)),
                  pl.BlockSpec((seq, hps * AUG), lambda b, g, t: (b, g)),
                  pl.BlockSpec((1, hps, VT_ROWS, seq), lambda b, g, t: (b, g, 0, 0))],
        out_specs=pl.BlockSpec((MOBA_BLOCK, hps * MOBA_HD), lambda b, g, t: (b * nb + jnp.maximum(t - 1, 0), g)),
        out_shape=jax.ShapeDtypeStruct((m, MOBA_HEADS * MOBA_HD), BF16),
        scratch_shapes=[pltpu.VMEM((hps, seq, MOBA_BLOCK), F32), pltpu.VMEM((hps, seq, MOBA_BLOCK), F32),
                        pltpu.VMEM((hps, 1, MOBA_BLOCK), F32), pltpu.VMEM((hps, 1, MOBA_BLOCK), F32)],
        compiler_params=_params(3),
        name="moba_attn",
    )(q_aug, k_aug, vt)


def moba_layer(x, gain, w_qkv, layer, q_gain, k_gain, batch):
    q_aug, k_aug, vt = moba_prep(x, gain, w_qkv, layer, q_gain, k_gain, batch)
    return moba_attention(q_aug, k_aug, vt, batch)


def kernel(x, norm_mix, norm_ffn, hyb_w_in, gla_w_gate2, gla_b_gate, gla_norm, ml_conv, ml_b_if, ml_norm,
           hyb_w_out, moba_w_qkv, moba_q_norm, moba_k_norm, moba_w_o, ffn_w_gate, ffn_w_up, ffn_w_down):
    batch, seq, d = x.shape
    depth = norm_mix.shape[0]
    xf = x.reshape(batch * seq, d)
    w_hyb_in = permute_hybrid_weight(hyb_w_in)
    w_hyb_out, w_moba_qkv, w_moba_o = hyb_w_out.astype(BF16), moba_w_qkv.astype(BF16), moba_w_o.astype(BF16)
    w_gate, w_up, w_down = ffn_w_gate.astype(BF16), ffn_w_up.astype(BF16), ffn_w_down.astype(BF16)
    for l in range(depth):
        j = l // 2
        if l % 2 == 0:
            y = hybrid_layer(xf, norm_mix[l], w_hyb_in, j, gla_w_gate2[j], gla_b_gate[j], gla_norm[j],
                             ml_conv[j], ml_b_if[j], ml_norm[j], batch)
            w_o = w_hyb_out
        else:
            y = moba_layer(xf, norm_mix[l], w_moba_qkv, j, moba_q_norm[j], moba_k_norm[j], batch)
            w_o = w_moba_o
        xf = out_ffn(xf, y, w_o, j, norm_ffn[l], w_gate, w_up, w_down, l)
    return xf.reshape(batch, seq, d)
```

```python
import functools

import jax
import jax.numpy as jnp
import numpy as np
from jax import lax
from jax.experimental import pallas as pl
from jax.experimental.pallas import tpu as pltpu

F32 = jnp.float32
BF16 = jnp.bfloat16

LANES = 128
SUBLANES = 8
VMEM_LIMIT_BYTES = 56 * 1024 * 1024

EPS = 1e-6
NEG = -1e30

GLA_HEADS = 4
GLA_DK = 64
GLA_DV = 128
GLA_RANK = 16
GLA_TAU = 16.0
ML_HEADS = 4
ML_DK = 64
ML_DV = 128
ML_CONV = 4
CHUNK = 64
SUB = 16
MOBA_HEADS = 8
MOBA_HD = 128
MOBA_BLOCK = 256
MOBA_TOPK = 3

GLA_QK = GLA_HEADS * GLA_DK
GLA_V = GLA_HEADS * GLA_DV
ML_QK = ML_HEADS * ML_DK
ML_V = ML_HEADS * ML_DV

C_GQ = 0
C_GK = C_GQ + GLA_QK
C_GV = C_GK + GLA_QK
C_GR = C_GV + GLA_V
C_MQ = C_GR + GLA_V
C_MK = C_MQ + ML_QK
C_MV = C_MK + ML_QK
C_MO = C_MV + ML_V
C_SMALL = C_MO + ML_V
HYB_COLS = C_SMALL + LANES
S_GLR = 0
S_MI = GLA_RANK
S_MF = GLA_RANK + ML_HEADS

ROW_TILE = 1024
MIX_TILE = 256
FF_TILE = 256
PROJ_COL_TILE = 512


def _rms(x, gain):
    return x * lax.rsqrt(jnp.mean(x * x, axis=-1, keepdims=True) + EPS) * gain


def _log_sigmoid(z):
    return jnp.minimum(z, 0.0) - jnp.log1p(jnp.exp(-jnp.abs(z)))


def _sigmoid(z):
    return 1.0 / (1.0 + jnp.exp(-z))


def _dot(a, b):
    return jnp.dot(a.astype(BF16), b.astype(BF16), preferred_element_type=F32)


def _dot_nt(a, b):
    return lax.dot_general(a.astype(BF16), b.astype(BF16), (((1,), (1,)), ((), ())),
                           preferred_element_type=F32)


def _dot_tn(a, b):
    return lax.dot_general(a.astype(BF16), b.astype(BF16), (((0,), (0,)), ((), ())),
                           preferred_element_type=F32)


def _const_spec(shape):
    return pl.BlockSpec(shape, lambda *_: (0,) * len(shape), pipeline_mode=pl.Buffered(1))


def _layer_spec(shape, layer):
    return pl.BlockSpec((None,) + tuple(shape), lambda *_: (layer,) + (0,) * len(shape),
                        pipeline_mode=pl.Buffered(1))


def _params(n_grid):
    return pltpu.CompilerParams(dimension_semantics=("arbitrary",) * n_grid,
                                vmem_limit_bytes=VMEM_LIMIT_BYTES)


def _out_ffn_kernel(x_ref, y_ref, wo_ref, g_ref, wg_ref, wu_ref, wd_ref, o_ref, h_ref, *, ff_tile):
    x1 = x_ref[...] + jnp.dot(y_ref[...], wo_ref[...], preferred_element_type=F32)
    xn = _rms(x1, g_ref[...]).astype(BF16)
    d_ff = wg_ref.shape[1]
    for c0 in range(0, d_ff, ff_tile):
        g = jnp.dot(xn, wg_ref[:, c0:c0 + ff_tile], preferred_element_type=F32)
        u = jnp.dot(xn, wu_ref[:, c0:c0 + ff_tile], preferred_element_type=F32)
        h_ref[:, c0:c0 + ff_tile] = (g * _sigmoid(g) * u).astype(BF16)
    o_ref[...] = x1 + jnp.dot(h_ref[...], wd_ref[...], preferred_element_type=F32)


def out_ffn(x, y, w_o, o_layer, gain, w_gate, w_up, w_down, layer):
    m, d = x.shape
    dy = y.shape[1]
    d_ff = w_gate.shape[2]
    tm = min(ROW_TILE, m)
    return pl.pallas_call(
        functools.partial(_out_ffn_kernel, ff_tile=FF_TILE),
        grid=(m // tm,),
        in_specs=[pl.BlockSpec((tm, d), lambda i: (i, 0)),
                  pl.BlockSpec((tm, dy), lambda i: (i, 0)),
                  _layer_spec((dy, d), o_layer),
                  _const_spec((1, d)),
                  _layer_spec((d, d_ff), layer),
                  _layer_spec((d, d_ff), layer),
                  _layer_spec((d_ff, d), layer)],
        out_specs=pl.BlockSpec((tm, d), lambda i: (i, 0)),
        out_shape=jax.ShapeDtypeStruct((m, d), F32),
        scratch_shapes=[pltpu.VMEM((tm, d_ff), BF16)],
        compiler_params=_params(1),
        name="out_ffn",
    )(x, y, w_o, gain.reshape(1, d), w_gate, w_up, w_down)


def _split3_rows(x):
    hi = x.astype(BF16)
    r1 = x - hi.astype(F32)
    mid = r1.astype(BF16)
    lo = (r1 - mid.astype(F32)).astype(BF16)
    return jnp.concatenate([hi, mid, lo], axis=0)


def _blocks_diag(tiles, zero):
    n = len(tiles)
    return jnp.concatenate(
        [jnp.concatenate([tiles[a] if a == b else zero for b in range(n)], axis=1) for a in range(n)], axis=0)


def _hybrid_kernel(x_ref, g_ref, w_ref, wg2_ref, bg_ref, gn_ref, conv_ref, bsmall_ref, mn_ref,
                   y_ref, proj_ref, gla_st, ml_cs, ml_ns, ml_m, conv_carry):
    t_tile = proj_ref.shape[0]
    nc = t_tile // CHUNK
    n_sub = CHUNK // SUB

    @pl.when(pl.program_id(1) == 0)
    def _():
        gla_st[...] = jnp.zeros_like(gla_st)
        ml_cs[...] = jnp.zeros_like(ml_cs)
        ml_ns[...] = jnp.zeros_like(ml_ns)
        ml_m[...] = jnp.zeros_like(ml_m)
        conv_carry[...] = jnp.zeros_like(conv_carry)

    xn = _rms(x_ref[...], g_ref[...]).astype(BF16)
    for c0 in range(0, C_SMALL, PROJ_COL_TILE):
        proj_ref[:, c0:c0 + PROJ_COL_TILE] = jnp.dot(xn, w_ref[:, c0:c0 + PROJ_COL_TILE],
                                                      preferred_element_type=F32).astype(proj_ref.dtype)
    small = jnp.dot(xn, w_ref[:, C_SMALL:], preferred_element_type=F32)

    xcat = proj_ref[:, C_MQ:C_MQ + 2 * ML_QK].astype(F32)
    ext = jnp.concatenate([conv_carry[...], xcat], axis=0)
    cw = conv_ref[...]
    acc = ext * cw[ML_CONV - 1:ML_CONV]
    for j in range(1, ML_CONV):
        acc = acc + pltpu.roll(ext, j, axis=0) * cw[ML_CONV - 1 - j:ML_CONV - j]
    conv = acc[SUBLANES:]
    qk = conv * _sigmoid(conv)
    conv_carry[...] = xcat[t_tile - SUBLANES:]

    lane = lax.broadcasted_iota(jnp.int32, (1, LANES), 1)
    head_mask = (lane < GLA_DK, lane >= GLA_DK)
    hm0 = head_mask[0]
    r_t = lax.broadcasted_iota(jnp.int32, (t_tile, t_tile), 0)
    c_t = lax.broadcasted_iota(jnp.int32, (t_tile, t_tile), 1)
    tril_bd = ((r_t >= c_t) & (r_t // CHUNK == c_t // CHUNK)).astype(BF16)
    tril3 = jnp.concatenate([tril_bd, tril_bd, tril_bd], axis=1)
    col_w = lax.broadcasted_iota(jnp.int32, (t_tile, 2 * nc * LANES), 1)
    row_w = lax.broadcasted_iota(jnp.int32, (t_tile, 2 * nc * LANES), 0)
    to_rows = ((col_w % CHUNK == row_w % CHUNK) & ((col_w % (nc * LANES)) // LANES == row_w // CHUNK)
               & ((col_w % LANES) // CHUNK == col_w // (nc * LANES))).astype(BF16)
    to_rows3 = jnp.concatenate([to_rows, to_rows, to_rows], axis=0)
    row_c = lax.broadcasted_iota(jnp.int32, (CHUNK, LANES), 0)
    causal2 = (lax.broadcasted_iota(jnp.int32, (CHUNK, LANES), 1) % CHUNK) <= row_c
    sub_id = (lax.broadcasted_iota(jnp.int32, (t_tile, LANES), 0) % CHUNK) // SUB
    row_sub = lax.broadcasted_iota(jnp.int32, (2 * SUBLANES, LANES), 0)
    zero_tile = jnp.zeros((CHUNK, LANES), F32)
    zero_v = jnp.zeros((CHUNK, LANES), BF16)

    def chunk_rows(x, c):
        return x[c * CHUNK:(c + 1) * CHUNK]

    def diag_blocks(r):
        return [r[c * CHUNK:(c + 1) * CHUNK, c * LANES:(c + 1) * LANES] for c in range(nc)]

    smallb = small + bsmall_ref[...]

    z = _dot(small, wg2_ref[...]) + bg_ref[...]
    log_a = _log_sigmoid(z) * (1.0 / GLA_TAU)
    cum = jnp.dot(tril3, _split3_rows(jnp.concatenate([log_a, _log_sigmoid(smallb)], axis=1)),
                  preferred_element_type=F32)
    b_cols = smallb - pltpu.roll(cum[:, GLA_QK:], LANES - ML_HEADS, axis=1)
    b_rows = lax.dot_general(_split3_rows(b_cols), to_rows3, (((0,), (0,)), ((), ())), preferred_element_type=F32)

    for p in range(GLA_HEADS // 2):
        q2 = proj_ref[:, C_GQ + p * LANES:C_GQ + (p + 1) * LANES].astype(F32) * (GLA_DK ** -0.5)
        k2 = proj_ref[:, C_GK + p * LANES:C_GK + (p + 1) * LANES].astype(F32)
        bc = cum[:, p * LANES:(p + 1) * LANES]
        v = [proj_ref[:, C_GV + (2 * p + hl) * GLA_DV:C_GV + (2 * p + hl + 1) * GLA_DV] for hl in range(2)]

        ref_pt = [[chunk_rows(bc, c)[i * SUB - 1:i * SUB] if i else jnp.zeros((1, LANES), F32)
                   for i in range(n_sub)] for c in range(nc)]
        r_own = jnp.concatenate([jnp.broadcast_to(ref_pt[c][i], (SUB, LANES))
                                 for c in range(nc) for i in range(n_sub)], axis=0)
        qs = q2 * jnp.exp(bc - r_own)
        q_hat = jnp.concatenate([jnp.where(sub_id == i, qs, 0.0) for i in range(n_sub)], axis=1)
        k_hat, k_dec, last = [], [], []
        for c in range(nc):
            kc, bcc = chunk_rows(k2, c), chunk_rows(bc, c)
            ke = [kc * jnp.exp(ref_pt[c][i] - bcc) for i in range(n_sub)]
            for hl in range(2):
                k_hat.append(jnp.concatenate([jnp.where(head_mask[hl], ke[i], 0.0) for i in range(n_sub)], axis=1))
            last.append(bcc[CHUNK - 1:CHUNK])
            k_dec.append(kc * jnp.exp(last[c] - bcc))
        k_hat = jnp.concatenate(k_hat, axis=0)
        sc = diag_blocks(_dot_nt(q_hat, k_hat))
        sc = [jnp.where(causal2, s, 0.0) for s in sc]
        v_bd = jnp.concatenate([jnp.concatenate([chunk_rows(v[hl], c) if hl == b else zero_v for b in range(2)],
                                                axis=1) for c in range(nc) for hl in range(2)], axis=0)
        o_intra = _dot(_blocks_diag(sc, zero_tile), v_bd)
        u_all = _dot_tn(jnp.concatenate(v, axis=1), _blocks_diag(k_dec, zero_tile))

        st = gla_st[p]
        before = []
        for c in range(nc):
            before.append(st)
            u = jnp.where(hm0, u_all[:GLA_DV, c * LANES:(c + 1) * LANES], u_all[GLA_DV:, c * LANES:(c + 1) * LANES])
            st = st * jnp.exp(last[c]) + u
        gla_st[p] = st
        st_stack = jnp.concatenate(before, axis=0)
        q_in = q2 * jnp.exp(bc)
        for hl in range(2):
            h = 2 * p + hl
            o_inter = jnp.concatenate(diag_blocks(_dot_nt(jnp.where(head_mask[hl], q_in, 0.0), st_stack)), axis=0)
            o = o_intra[:, hl * GLA_DV:(hl + 1) * GLA_DV] + o_inter
            gate = proj_ref[:, C_GR + h * GLA_DV:C_GR + (h + 1) * GLA_DV].astype(F32)
            yg = _rms(o, gn_ref[:, h * GLA_DV:(h + 1) * GLA_DV]) * (gate * _sigmoid(gate))
            y_ref[:, h * GLA_DV:(h + 1) * GLA_DV] = yg.astype(y_ref.dtype)

    ones_tile = jnp.ones((t_tile, LANES), BF16)
    for p in range(ML_HEADS // 2):
        mq2 = qk[:, p * LANES:(p + 1) * LANES]
        mk2 = qk[:, ML_QK + p * LANES:ML_QK + (p + 1) * LANES] * (ML_DK ** -0.5)
        v = [proj_ref[:, C_MV + (2 * p + hl) * ML_DV:C_MV + (2 * p + hl + 1) * ML_DV] for hl in range(2)]
        k_stack = jnp.concatenate([jnp.where(head_mask[hl], chunk_rows(mk2, c), 0.0)
                                   for c in range(nc) for hl in range(2)], axis=0)
        qk_blocks = diag_blocks(_dot_nt(mq2, k_stack))

        w_blocks, wk_blocks, cm_cols, b_max, bc_last = [], [], [], [], []
        for c in range(nc):
            blk = slice(c * LANES, (c + 1) * LANES)
            b_pair = (b_rows[S_MI + 2 * p:S_MI + 2 * p + 1, blk]
                      + b_rows[S_MI + 2 * p + 1:S_MI + 2 * p + 2, nc * LANES + c * LANES:nc * LANES + (c + 1) * LANES])
            f_last = [chunk_rows(cum, c)[CHUNK - 1:CHUNK, GLA_QK + S_MF + 2 * p + hl:GLA_QK + S_MF + 2 * p + hl + 1]
                      for hl in range(2)]
            bm = jnp.where(causal2, b_pair, -jnp.inf)
            cm = [jnp.max(jnp.where(head_mask[hl], bm, -jnp.inf), axis=1, keepdims=True) for hl in range(2)]
            w_blocks.append(qk_blocks[c] * jnp.exp(bm - jnp.where(hm0, cm[0], cm[1])))
            cm_cols.append(cm)
            bmx = [jnp.max(jnp.where(head_mask[hl], b_pair, -jnp.inf), axis=1, keepdims=True) for hl in range(2)]
            b_max.append(jnp.where(hm0, bmx[0], bmx[1]))
            bc_last.append(jnp.where(hm0, f_last[0], f_last[1]))
            e_col = []
            for hl in range(2):
                h = 2 * p + hl
                b_col = (chunk_rows(smallb, c)[:, S_MI + h:S_MI + h + 1]
                         - chunk_rows(cum, c)[:, GLA_QK + S_MF + h:GLA_QK + S_MF + h + 1])
                e_col.append(jnp.exp(b_col - bmx[hl]))
            wk_blocks.append(chunk_rows(mk2, c) * jnp.where(hm0, e_col[0], e_col[1]))

        ones_col = [jnp.broadcast_to((lane == hl).astype(BF16), (CHUNK, LANES)) for hl in range(2)]
        v_bd = jnp.concatenate([jnp.concatenate([chunk_rows(v[hl], c) if hl == b else zero_v for b in range(2)]
                                                + [ones_col[hl]], axis=1)
                                for c in range(nc) for hl in range(2)], axis=0)
        nv = _dot(_blocks_diag(w_blocks, zero_tile), v_bd)
        u_all = _dot_tn(jnp.concatenate(v + [ones_tile], axis=1), _blocks_diag(wk_blocks, zero_tile))

        cs, ns, m_run = ml_cs[p], ml_ns[p], ml_m[p]
        before = []
        for c in range(nc):
            before.append((cs, ns, m_run))
            m_top = jnp.maximum(m_run, b_max[c])
            keep = jnp.exp(m_run - m_top)
            gain = jnp.exp(b_max[c] - m_top)
            blk = slice(c * LANES, (c + 1) * LANES)
            cs = keep * cs + gain * jnp.where(hm0, u_all[:ML_DV, blk], u_all[ML_DV:2 * ML_DV, blk])
            ns = keep * ns + gain * u_all[2 * ML_DV:2 * ML_DV + 1, blk]
            m_run = bc_last[c] + m_top
        ml_cs[p], ml_ns[p], ml_m[p] = cs, ns, m_run
        ns_rows = jnp.zeros((2 * SUBLANES, LANES), F32)
        for c in range(nc):
            ns_rows = jnp.where(row_sub == c, before[c][1], ns_rows)
        state_stack = jnp.concatenate([b[0] for b in before] + [ns_rows], axis=0)

        for hl in range(2):
            h = 2 * p + hl
            rr = _dot_nt(jnp.where(head_mask[hl], mq2, 0.0), state_stack)
            q_cs = diag_blocks(rr)
            h_chunks = []
            for c in range(nc):
                rows = slice(c * CHUNK, (c + 1) * CHUNK)
                q_ns = rr[rows, nc * ML_DV + c:nc * ML_DV + c + 1]
                m_prev = before[c][2][:, hl * ML_DK:hl * ML_DK + 1]
                cm = cm_cols[c][hl]
                m_top = jnp.maximum(m_prev, cm)
                local = jnp.exp(cm - m_top)
                carried = jnp.exp(m_prev - m_top)
                num = local * nv[rows, hl * ML_DV:(hl + 1) * ML_DV] + carried * q_cs[c]
                den = local * nv[rows, 2 * ML_DV + hl:2 * ML_DV + hl + 1] + carried * q_ns
                bc_col = chunk_rows(cum, c)[:, GLA_QK + S_MF + h:GLA_QK + S_MF + h + 1]
                h_chunks.append(num / jnp.maximum(jnp.abs(den), jnp.exp(-(bc_col + m_top))))
            og = proj_ref[:, C_MO + h * ML_DV:C_MO + (h + 1) * ML_DV].astype(F32)
            ym = _rms(jnp.concatenate(h_chunks, axis=0) * _sigmoid(og), mn_ref[:, h * ML_DV:(h + 1) * ML_DV])
            y_ref[:, GLA_V + h * ML_DV:GLA_V + (h + 1) * ML_DV] = ym.astype(y_ref.dtype)


def permute_hybrid_weight(w_in):
    segs = np.cumsum([0, GLA_QK, GLA_QK, GLA_V, GLA_V, GLA_RANK, ML_QK, ML_QK, ML_V, ML_V, ML_HEADS, ML_HEADS])
    seg = lambda i: w_in[:, :, int(segs[i]):int(segs[i + 1])]
    pad = jnp.zeros(w_in.shape[:2] + (LANES - GLA_RANK - 2 * ML_HEADS,), w_in.dtype)
    return jnp.concatenate([seg(0), seg(1), seg(2), seg(3), seg(5), seg(6), seg(7), seg(8),
                            seg(4), seg(9), seg(10), pad], axis=2).astype(BF16)


def hybrid_layer(x, gain, w_perm, layer, w_gate2, b_gate, gla_norm, conv_w, b_if, ml_norm, batch):
    m, d = x.shape
    seq = m // batch
    t = min(MIX_TILE, seq)
    nt = seq // t
    assert 2 * SUBLANES >= t // CHUNK
    b_small = jnp.zeros((1, LANES), F32).at[0, S_MI:S_MI + 2 * ML_HEADS].set(b_if)
    w_gate2_rows = jnp.zeros((LANES, GLA_QK), F32).at[S_GLR:S_GLR + GLA_RANK].set(w_gate2)
    return pl.pallas_call(
        _hybrid_kernel,
        grid=(batch, nt),
        in_specs=[pl.BlockSpec((t, d), lambda b, i: (b * nt + i, 0)),
                  _const_spec((1, d)),
                  _layer_spec((d, HYB_COLS), layer),
                  _const_spec((LANES, GLA_QK)),
                  _const_spec((1, GLA_QK)),
                  _const_spec((1, GLA_V)),
                  _const_spec((ML_CONV, 2 * ML_QK)),
                  _const_spec((1, LANES)),
                  _const_spec((1, ML_V))],
        out_specs=pl.BlockSpec((t, GLA_V + ML_V), lambda b, i: (b * nt + i, 0)),
        out_shape=jax.ShapeDtypeStruct((m, GLA_V + ML_V), BF16),
        scratch_shapes=[pltpu.VMEM((t, C_SMALL), BF16),
                        pltpu.VMEM((GLA_HEADS // 2, GLA_DV, LANES), F32),
                        pltpu.VMEM((ML_HEADS // 2, ML_DV, LANES), F32),
                        pltpu.VMEM((ML_HEADS // 2, 1, LANES), F32),
                        pltpu.VMEM((ML_HEADS // 2, 1, LANES), F32),
                        pltpu.VMEM((SUBLANES, 2 * ML_QK), F32)],
        compiler_params=_params(2),
        name="hybrid_core",
    )(x, gain.reshape(1, d), w_perm, w_gate2_rows, b_gate.reshape(1, -1), gla_norm.reshape(1, -1), conv_w,
      b_small, ml_norm.reshape(1, -1))


LOG2E = 1.4426950408889634
AUG = 2 * MOBA_HD
LANES_PER_BLOCK = 4
MASK_BIG = 2.0 ** 100
PROJ_HEADS = 4
VT_ROWS = MOBA_HD + 16


def _split3(x):
    x = np.asarray(x, np.float32)
    hi = x.astype(BF16).astype(np.float32)
    mid = (x - hi).astype(BF16).astype(np.float32)
    lo = (x - hi - mid).astype(BF16).astype(np.float32)
    return np.stack([hi, mid, lo], axis=-1)


def _aug_tables(nb):
    c = np.asarray([LOG2E * 2.0 ** (-8.0 * (h + 1) / MOBA_HEADS) for h in range(MOBA_HEADS)], np.float32)
    pos = np.arange(MOBA_BLOCK, dtype=np.float32)
    base = LANES_PER_BLOCK * nb
    q_rows = np.zeros((MOBA_HEADS, MOBA_BLOCK, LANES), np.float32)
    k_rows = np.zeros((MOBA_HEADS, MOBA_BLOCK, LANES), np.float32)
    q_rows[:, :, base:base + 3] = _split3(-c[:, None] * pos[None, :])
    q_rows[:, :, base + 3:base + 6] = 1.0
    k_rows[:, :, base:base + 3] = 1.0
    k_rows[:, :, base + 3:base + 6] = _split3(c[:, None] * pos[None, :])
    i_idx = np.arange(nb)[:, None]
    j_idx = np.arange(nb)[None, :]
    steps = np.zeros((MOBA_HEADS, nb, nb, LANES_PER_BLOCK), np.float32)
    steps[..., 0] = np.where(j_idx == i_idx, 0.0, -MASK_BIG)
    gap = _split3(-c[:, None, None] * (MOBA_BLOCK * (i_idx - j_idx)).astype(np.float32)[None])
    steps[..., 1:] = np.where((j_idx <= i_idx)[None, :, :, None], gap, 0.0)
    q_steps = np.zeros((MOBA_HEADS, nb, 1, LANES), np.float32)
    q_steps[:, :, 0, :base] = steps.reshape(MOBA_HEADS, nb, base)
    k_steps = np.zeros((nb, 1, LANES), np.float32)
    k_steps[:, 0, :base] = np.repeat(np.eye(nb, dtype=np.float32), LANES_PER_BLOCK, axis=1)
    expand = np.zeros((nb, LANES), np.float32)
    expand[np.arange(nb), LANES_PER_BLOCK * np.arange(nb)] = MASK_BIG
    return (jnp.asarray(q_rows), jnp.asarray(q_steps), jnp.asarray(k_rows), jnp.asarray(k_steps),
            jnp.asarray(expand, BF16))


def _moba_prep_kernel(x_ref, g_ref, w_ref, qg_ref, kg_ref, qrow_ref, qstep_ref, krow_ref, kstep_ref, exp_ref,
                      qa_ref, ka_ref, vt_ref, kmean_ref):
    i = pl.program_id(1)
    nb = kmean_ref.shape[1]
    d = MOBA_HEADS * MOBA_HD

    @pl.when(i == 0)
    def _():
        kmean_ref[...] = jnp.zeros_like(kmean_ref)

    xn = _rms(x_ref[...], g_ref[...]).astype(BF16)
    blk = lax.broadcasted_iota(jnp.int32, (nb, MOBA_BLOCK), 0)
    past = blk < i
    ones_rows = (lax.broadcasted_iota(jnp.int32, (VT_ROWS - MOBA_HD, MOBA_BLOCK), 0) == 0).astype(F32)
    k_step = kstep_ref[i]

    for h in range(MOBA_HEADS):
        hg = h % PROJ_HEADS
        if hg == 0:
            c0 = h * MOBA_HD
            q_grp, k_grp, v_grp = (jnp.dot(xn, w_ref[:, off + c0:off + c0 + PROJ_HEADS * MOBA_HD],
                                           preferred_element_type=F32) for off in (0, d, 2 * d))
        part = slice(hg * MOBA_HD, (hg + 1) * MOBA_HD)
        q = _rms(q_grp[:, part], qg_ref[...]) * (MOBA_HD ** -0.5)
        k = _rms(k_grp[:, part], kg_ref[...])
        v = v_grp[:, part]
        vt_ref[0, h] = jnp.concatenate([v.T, ones_rows], axis=0).astype(vt_ref.dtype)

        gate = _dot_nt(kmean_ref[h], q)
        gate = jnp.where(past, gate, -jnp.inf)
        rank = jnp.zeros((nb, MOBA_BLOCK), F32)
        for j in range(nb):
            other = gate[j:j + 1]
            ahead = (other > gate) | ((other == gate) & (j < blk))
            rank = rank + ahead.astype(F32)
        sel = (past & (rank < MOBA_TOPK)).astype(BF16)
        q_extra = (qrow_ref[h] + qstep_ref[h, i]) + _dot_tn(sel, exp_ref[...])
        qa_ref[:, h * AUG:h * AUG + MOBA_HD] = (q * LOG2E).astype(qa_ref.dtype)
        qa_ref[:, h * AUG + MOBA_HD:(h + 1) * AUG] = q_extra.astype(qa_ref.dtype)
        ka_ref[:, h * AUG:h * AUG + MOBA_HD] = k.astype(ka_ref.dtype)
        ka_ref[:, h * AUG + MOBA_HD:(h + 1) * AUG] = (krow_ref[h] + k_step).astype(ka_ref.dtype)

        kmean_ref[h, pl.ds(i, 1), :] = jnp.mean(k, axis=0, keepdims=True)


def moba_prep(x, gain, w_qkv, layer, q_gain, k_gain, batch):
    m, dm = x.shape
    seq = m // batch
    nb = seq // MOBA_BLOCK
    assert LANES_PER_BLOCK * nb + 6 <= LANES
    d = MOBA_HEADS * MOBA_HD
    q_rows, q_steps, k_rows, k_steps, expand = _aug_tables(nb)
    return pl.pallas_call(
        _moba_prep_kernel,
        grid=(batch, nb),
        in_specs=[pl.BlockSpec((MOBA_BLOCK, dm), lambda b, i: (b * nb + i, 0)),
                  _const_spec((1, dm)),
                  _layer_spec((dm, 3 * d), layer),
                  _const_spec((1, MOBA_HD)),
                  _const_spec((1, MOBA_HD)),
                  _const_spec((MOBA_HEADS, MOBA_BLOCK, LANES)),
                  _const_spec((MOBA_HEADS, nb, 1, LANES)),
                  _const_spec((MOBA_HEADS, MOBA_BLOCK, LANES)),
                  _const_spec((nb, 1, LANES)),
                  _const_spec((nb, LANES))],
        out_specs=[pl.BlockSpec((MOBA_BLOCK, MOBA_HEADS * AUG), lambda b, i: (b * nb + i, 0)),
                   pl.BlockSpec((MOBA_BLOCK, MOBA_HEADS * AUG), lambda b, i: (b * nb + i, 0)),
                   pl.BlockSpec((1, MOBA_HEADS, VT_ROWS, MOBA_BLOCK), lambda b, i: (b, 0, 0, i))],
        out_shape=[jax.ShapeDtypeStruct((m, MOBA_HEADS * AUG), BF16),
                   jax.ShapeDtypeStruct((m, MOBA_HEADS * AUG), BF16),
                   jax.ShapeDtypeStruct((batch, MOBA_HEADS, VT_ROWS, seq), BF16)],
        scratch_shapes=[pltpu.VMEM((MOBA_HEADS, nb, MOBA_HD), F32)],
        compiler_params=_params(2),
        name="moba_prep",
    )(x, gain.reshape(1, dm), w_qkv, q_gain.reshape(1, -1), k_gain.reshape(1, -1),
      q_rows, q_steps, k_rows, k_steps, expand)


ATTN_HEADS_PER_STEP = 4
ATTN_CHUNK_BLOCKS = 4
PV_KEYS = 256


def _moba_attn_kernel(q_ref, k_ref, vt_ref, o_ref, s_even, s_odd, m_even, m_odd, *, heads, chunk_blocks, nb):
    t = pl.program_id(2)
    i = jnp.minimum(t, nb - 1)
    chunk_keys = chunk_blocks * MOBA_BLOCK
    n_chunks = i // chunk_blocks + 1
    n_valid2 = jnp.where(t > 0, (t - 1) // chunk_blocks + 1, 0)

    @pl.when((pl.program_id(0) == 0) & (pl.program_id(1) == 0) & (t == 0))
    def _():
        s_even[...] = jnp.zeros_like(s_even)
        s_odd[...] = jnp.zeros_like(s_odd)

    key_r = lax.broadcasted_iota(jnp.int32, (MOBA_BLOCK, MOBA_BLOCK), 0)
    qry_c = lax.broadcasted_iota(jnp.int32, (MOBA_BLOCK, MOBA_BLOCK), 1)
    causal = qry_c >= key_r
    q_aug = [q_ref[:, hh * AUG:(hh + 1) * AUG] for hh in range(heads)]

    def both(c, carry, last, s_w, s_r, m_row2, stage1, stage2):
        m_part, acc = carry
        k0 = c * chunk_keys
        live2 = jnp.full((1, MOBA_BLOCK), c, jnp.int32) < n_valid2
        new_m, new_acc = [], []
        for hh in range(heads):
            m_new, a = m_part[hh], acc[hh]
            if stage1:
                s = lax.dot_general(k_ref[pl.ds(k0, chunk_keys), hh * AUG:(hh + 1) * AUG], q_aug[hh],
                                    (((1,), (1,)), ((), ())), preferred_element_type=F32)
                if last:
                    parts = []
                    for u in range(chunk_blocks):
                        other_block = jnp.full((1, MOBA_BLOCK), c * chunk_blocks + u, jnp.int32) != i
                        parts.append(jnp.where(causal | other_block, s[u * MOBA_BLOCK:(u + 1) * MOBA_BLOCK], NEG))
                    s = jnp.concatenate(parts, axis=0)
                s_w[hh, pl.ds(k0, chunk_keys), :] = s
                m_new = jnp.maximum(m_new, jnp.max(s.reshape(chunk_keys // SUBLANES, SUBLANES, MOBA_BLOCK), axis=0))
            if stage2:
                for u in range(0, chunk_keys, PV_KEYS):
                    ku = k0 + u
                    p = jnp.exp2(s_r[hh, pl.ds(ku, PV_KEYS), :] - m_row2[hh])
                    p = jnp.where(live2, p, 0.0).astype(BF16)
                    a = a + jnp.dot(vt_ref[0, hh, :, pl.ds(ku, PV_KEYS)], p, preferred_element_type=F32)
            new_m.append(m_new)
            new_acc.append(a)
        return tuple(new_m), tuple(new_acc)

    def step(n, s_w, s_r, m_w, m_r, stage1=True, stage2=True):
        m_row2 = [m_r[hh] for hh in range(heads)]
        carry = (tuple(jnp.full((SUBLANES, MOBA_BLOCK), NEG, F32) for _ in range(heads)),
                 tuple(jnp.zeros((VT_ROWS, MOBA_BLOCK), F32) for _ in range(heads)))
        for c in range(n):
            carry = both(c, carry, c == n - 1, s_w, s_r, m_row2, stage1, stage2)
        m_part, acc = carry
        for hh in range(heads):
            if stage1:
                m_w[hh] = jnp.max(m_part[hh], axis=0, keepdims=True)
            if stage2:
                o = acc[hh][:MOBA_HD] * (1.0 / acc[hh][MOBA_HD:MOBA_HD + 1])
                o_ref[:, hh * MOBA_HD:(hh + 1) * MOBA_HD] = o.T.astype(o_ref.dtype)

    even = (s_even, s_odd, m_even, m_odd)
    odd = (s_odd, s_even, m_odd, m_even)
    n_max = nb // chunk_blocks
    pl.when(t == 0)(functools.partial(step, 1, *even, stage2=False))
    pl.when(t == nb)(functools.partial(step, n_max, *(odd if nb % 2 else even), stage1=False))
    inner = (t > 0) & (t < nb)
    for n in range(1, n_max + 1):
        pl.when(inner & (n_chunks == n) & (t % 2 == 0))(functools.partial(step, n, *even))
        pl.when(inner & (n_chunks == n) & (t % 2 == 1))(functools.partial(step, n, *odd))


def moba_attention(q_aug, k_aug, vt, batch):
    m = q_aug.shape[0]
    seq = m // batch
    nb = seq // MOBA_BLOCK
    hps = ATTN_HEADS_PER_STEP
    assert nb % ATTN_CHUNK_BLOCKS == 0
    return pl.pallas_call(
        functools.partial(_moba_attn_kernel, heads=hps, chunk_blocks=ATTN_CHUNK_BLOCKS, nb=nb),
        grid=(batch, MOBA_HEADS // hps, nb + 1),
        in_specs=[pl.BlockSpec((MOBA_BLOCK, hps * AUG), lambda b, g, t: (b * nb + jnp.minimum(t, nb - 1), g)),
                  pl.BlockSpec((seq, hps * AUG), lambda b, g, t: (b, g), pipeline_mode=pl.Buffered(1)),
                  pl.BlockSpec((1, hps, VT_ROWS, seq), lambda b, g, t: (b, g, 0, 0),
                               pipeline_mode=pl.Buffered(1))],
        out_specs=pl.BlockSpec((MOBA_BLOCK, hps * MOBA_HD), lambda b, g, t: (b * nb + jnp.maximum(t - 1, 0), g)),
        out_shape=jax.ShapeDtypeStruct((m, MOBA_HEADS * MOBA_HD), BF16),
        scratch_shapes=[pltpu.VMEM((hps, seq, MOBA_BLOCK), F32), pltpu.VMEM((hps, seq, MOBA_BLOCK), F32),
                        pltpu.VMEM((hps, 1, MOBA_BLOCK), F32), pltpu.VMEM((hps, 1, MOBA_BLOCK), F32)],
        compiler_params=_params(3),
        name="moba_attn",
    )(q_aug, k_aug, vt)


def moba_layer(x, gain, w_qkv, layer, q_gain, k_gain, batch):
    q_aug, k_aug, vt = moba_prep(x, gain, w_qkv, layer, q_gain, k_gain, batch)
    return moba_attention(q_aug, k_aug, vt, batch)


def kernel(x, norm_mix, norm_ffn, hyb_w_in, gla_w_gate2, gla_b_gate, gla_norm, ml_conv, ml_b_if, ml_norm,
           hyb_w_out, moba_w_qkv, moba_q_norm, moba_k_norm, moba_w_o, ffn_w_gate, ffn_w_up, ffn_w_down):
    batch, seq, d = x.shape
    depth = norm_mix.shape[0]
    xf = x.reshape(batch * seq, d)
    w_hyb_in = permute_hybrid_weight(hyb_w_in)
    w_hyb_out, w_moba_qkv, w_moba_o = hyb_w_out.astype(BF16), moba_w_qkv.astype(BF16), moba_w_o.astype(BF16)
    w_gate, w_up, w_down = ffn_w_gate.astype(BF16), ffn_w_up.astype(BF16), ffn_w_down.astype(BF16)
    for l in range(depth):
        j = l // 2
        if l % 2 == 0:
            y = hybrid_layer(xf, norm_mix[l], w_hyb_in, j, gla_w_gate2[j], gla_b_gate[j], gla_norm[j],
                             ml_conv[j], ml_b_if[j], ml_norm[j], batch)
            w_o = w_hyb_out
        else:
            y = moba_layer(xf, norm_mix[l], w_moba_qkv, j, moba_q_norm[j], moba_k_norm[j], batch)
            w_o = w_moba_o
        xf = out_ffn(xf, y, w_o, j, norm_ffn[l], w_gate, w_up, w_down, l)
    return xf.reshape(batch, seq, d)
```

```python
import functools

import jax
import jax.numpy as jnp
import numpy as np
from jax import lax
from jax.experimental import pallas as pl
from jax.experimental.pallas import tpu as pltpu

F32 = jnp.float32
BF16 = jnp.bfloat16

LANES = 128
SUBLANES = 8
VMEM_LIMIT_BYTES = 56 * 1024 * 1024

EPS = 1e-6
NEG = -1e30

GLA_HEADS = 4
GLA_DK = 64
GLA_DV = 128
GLA_RANK = 16
GLA_TAU = 16.0
ML_HEADS = 4
ML_DK = 64
ML_DV = 128
ML_CONV = 4
CHUNK = 64
SUB = 16
MOBA_HEADS = 8
MOBA_HD = 128
MOBA_BLOCK = 256
MOBA_TOPK = 3

GLA_QK = GLA_HEADS * GLA_DK
GLA_V = GLA_HEADS * GLA_DV
ML_QK = ML_HEADS * ML_DK
ML_V = ML_HEADS * ML_DV

C_GQ = 0
C_GK = C_GQ + GLA_QK
C_GV = C_GK + GLA_QK
C_GR = C_GV + GLA_V
C_MQ = C_GR + GLA_V
C_MK = C_MQ + ML_QK
C_MV = C_MK + ML_QK
C_MO = C_MV + ML_V
C_SMALL = C_MO + ML_V
HYB_COLS = C_SMALL + LANES
S_GLR = 0
S_MI = GLA_RANK
S_MF = GLA_RANK + ML_HEADS

ROW_TILE = 1024
MIX_TILE = 256
FF_TILE = 256
PROJ_COL_TILE = 512


def _rms(x, gain):
    return x * lax.rsqrt(jnp.mean(x * x, axis=-1, keepdims=True) + EPS) * gain


def _log_sigmoid(z):
    return jnp.minimum(z, 0.0) - jnp.log1p(jnp.exp(-jnp.abs(z)))


def _sigmoid(z):
    return 1.0 / (1.0 + jnp.exp(-z))


def _dot(a, b):
    return jnp.dot(a.astype(BF16), b.astype(BF16), preferred_element_type=F32)


def _dot_nt(a, b):
    return lax.dot_general(a.astype(BF16), b.astype(BF16), (((1,), (1,)), ((), ())),
                           preferred_element_type=F32)


def _dot_tn(a, b):
    return lax.dot_general(a.astype(BF16), b.astype(BF16), (((0,), (0,)), ((), ())),
                           preferred_element_type=F32)


def _const_spec(shape):
    return pl.BlockSpec(shape, lambda *_: (0,) * len(shape), pipeline_mode=pl.Buffered(1))


def _layer_spec(shape, layer):
    return pl.BlockSpec((None,) + tuple(shape), lambda *_: (layer,) + (0,) * len(shape),
                        pipeline_mode=pl.Buffered(1))


def _params(n_grid):
    return pltpu.CompilerParams(dimension_semantics=("arbitrary",) * n_grid,
                                vmem_limit_bytes=VMEM_LIMIT_BYTES)


def _out_ffn_kernel(x_ref, y_ref, wo_ref, g_ref, wg_ref, wu_ref, wd_ref, o_ref, h_ref, *, ff_tile):
    x1 = x_ref[...] + jnp.dot(y_ref[...], wo_ref[...], preferred_element_type=F32)
    xn = _rms(x1, g_ref[...]).astype(BF16)
    d_ff = wg_ref.shape[1]
    for c0 in range(0, d_ff, ff_tile):
        g = jnp.dot(xn, wg_ref[:, c0:c0 + ff_tile], preferred_element_type=F32)
        u = jnp.dot(xn, wu_ref[:, c0:c0 + ff_tile], preferred_element_type=F32)
        h_ref[:, c0:c0 + ff_tile] = (g * _sigmoid(g) * u).astype(BF16)
    o_ref[...] = x1 + jnp.dot(h_ref[...], wd_ref[...], preferred_element_type=F32)


def out_ffn(x, y, w_o, o_layer, gain, w_gate, w_up, w_down, layer):
    m, d = x.shape
    dy = y.shape[1]
    d_ff = w_gate.shape[2]
    tm = min(ROW_TILE, m)
    return pl.pallas_call(
        functools.partial(_out_ffn_kernel, ff_tile=FF_TILE),
        grid=(m // tm,),
        in_specs=[pl.BlockSpec((tm, d), lambda i: (i, 0)),
                  pl.BlockSpec((tm, dy), lambda i: (i, 0)),
                  _layer_spec((dy, d), o_layer),
                  _const_spec((1, d)),
                  _layer_spec((d, d_ff), layer),
                  _layer_spec((d, d_ff), layer),
                  _layer_spec((d_ff, d), layer)],
        out_specs=pl.BlockSpec((tm, d), lambda i: (i, 0)),
        out_shape=jax.ShapeDtypeStruct((m, d), F32),
        scratch_shapes=[pltpu.VMEM((tm, d_ff), BF16)],
        compiler_params=_params(1),
        name="out_ffn",
    )(x, y, w_o, gain.reshape(1, d), w_gate, w_up, w_down)


def _split3_rows(x):
    hi = x.astype(BF16)
    r1 = x - hi.astype(F32)
    mid = r1.astype(BF16)
    lo = (r1 - mid.astype(F32)).astype(BF16)
    return jnp.concatenate([hi, mid, lo], axis=0)


def _blocks_diag(tiles, zero):
    n = len(tiles)
    return jnp.concatenate(
        [jnp.concatenate([tiles[a] if a == b else zero for b in range(n)], axis=1) for a in range(n)], axis=0)


def _hybrid_kernel(x_ref, g_ref, w_ref, wg2_ref, bg_ref, gn_ref, conv_ref, bsmall_ref, mn_ref,
                   y_ref, proj_ref, gla_st, ml_cs, ml_ns, ml_m, conv_carry):
    t_tile = proj_ref.shape[0]
    nc = t_tile // CHUNK
    n_sub = CHUNK // SUB

    @pl.when(pl.program_id(1) == 0)
    def _():
        gla_st[...] = jnp.zeros_like(gla_st)
        ml_cs[...] = jnp.zeros_like(ml_cs)
        ml_ns[...] = jnp.zeros_like(ml_ns)
        ml_m[...] = jnp.zeros_like(ml_m)
        conv_carry[...] = jnp.zeros_like(conv_carry)

    xn = _rms(x_ref[...], g_ref[...]).astype(BF16)
    for c0 in range(0, C_SMALL, PROJ_COL_TILE):
        proj_ref[:, c0:c0 + PROJ_COL_TILE] = jnp.dot(xn, w_ref[:, c0:c0 + PROJ_COL_TILE],
                                                      preferred_element_type=F32).astype(proj_ref.dtype)
    small = jnp.dot(xn, w_ref[:, C_SMALL:], preferred_element_type=F32)

    xcat = proj_ref[:, C_MQ:C_MQ + 2 * ML_QK].astype(F32)
    ext = jnp.concatenate([conv_carry[...], xcat], axis=0)
    cw = conv_ref[...]
    acc = ext * cw[ML_CONV - 1:ML_CONV]
    for j in range(1, ML_CONV):
        acc = acc + pltpu.roll(ext, j, axis=0) * cw[ML_CONV - 1 - j:ML_CONV - j]
    conv = acc[SUBLANES:]
    qk = conv * _sigmoid(conv)
    conv_carry[...] = xcat[t_tile - SUBLANES:]

    lane = lax.broadcasted_iota(jnp.int32, (1, LANES), 1)
    head_mask = (lane < GLA_DK, lane >= GLA_DK)
    hm0 = head_mask[0]
    r_t = lax.broadcasted_iota(jnp.int32, (t_tile, t_tile), 0)
    c_t = lax.broadcasted_iota(jnp.int32, (t_tile, t_tile), 1)
    tril_bd = ((r_t >= c_t) & (r_t // CHUNK == c_t // CHUNK)).astype(BF16)
    tril3 = jnp.concatenate([tril_bd, tril_bd, tril_bd], axis=1)
    col_w = lax.broadcasted_iota(jnp.int32, (t_tile, 2 * nc * LANES), 1)
    row_w = lax.broadcasted_iota(jnp.int32, (t_tile, 2 * nc * LANES), 0)
    to_rows = ((col_w % CHUNK == row_w % CHUNK) & ((col_w % (nc * LANES)) // LANES == row_w // CHUNK)
               & ((col_w % LANES) // CHUNK == col_w // (nc * LANES))).astype(BF16)
    to_rows3 = jnp.concatenate([to_rows, to_rows, to_rows], axis=0)
    row_c = lax.broadcasted_iota(jnp.int32, (CHUNK, LANES), 0)
    causal2 = (lax.broadcasted_iota(jnp.int32, (CHUNK, LANES), 1) % CHUNK) <= row_c
    sub_id = (lax.broadcasted_iota(jnp.int32, (t_tile, LANES), 0) % CHUNK) // SUB
    row_sub = lax.broadcasted_iota(jnp.int32, (2 * SUBLANES, LANES), 0)
    zero_tile = jnp.zeros((CHUNK, LANES), F32)
    zero_v = jnp.zeros((CHUNK, LANES), BF16)

    def chunk_rows(x, c):
        return x[c * CHUNK:(c + 1) * CHUNK]

    def diag_blocks(r):
        return [r[c * CHUNK:(c + 1) * CHUNK, c * LANES:(c + 1) * LANES] for c in range(nc)]

    smallb = small + bsmall_ref[...]

    z = _dot(small, wg2_ref[...]) + bg_ref[...]
    log_a = _log_sigmoid(z) * (1.0 / GLA_TAU)
    cum = jnp.dot(tril3, _split3_rows(jnp.concatenate([log_a, _log_sigmoid(smallb)], axis=1)),
                  preferred_element_type=F32)
    b_cols = smallb - pltpu.roll(cum[:, GLA_QK:], LANES - ML_HEADS, axis=1)
    b_rows = lax.dot_general(_split3_rows(b_cols), to_rows3, (((0,), (0,)), ((), ())), preferred_element_type=F32)

    for p in range(GLA_HEADS // 2):
        q2 = proj_ref[:, C_GQ + p * LANES:C_GQ + (p + 1) * LANES].astype(F32) * (GLA_DK ** -0.5)
        k2 = proj_ref[:, C_GK + p * LANES:C_GK + (p + 1) * LANES].astype(F32)
        bc = cum[:, p * LANES:(p + 1) * LANES]
        v = [proj_ref[:, C_GV + (2 * p + hl) * GLA_DV:C_GV + (2 * p + hl + 1) * GLA_DV] for hl in range(2)]

        ref_pt = [[chunk_rows(bc, c)[i * SUB - 1:i * SUB] if i else jnp.zeros((1, LANES), F32)
                   for i in range(n_sub)] for c in range(nc)]
        r_own = jnp.concatenate([jnp.broadcast_to(ref_pt[c][i], (SUB, LANES))
                                 for c in range(nc) for i in range(n_sub)], axis=0)
        qs = q2 * jnp.exp(bc - r_own)
        q_hat = jnp.concatenate([jnp.where(sub_id == i, qs, 0.0) for i in range(n_sub)], axis=1)
        k_hat, k_dec, last = [], [], []
        for c in range(nc):
            kc, bcc = chunk_rows(k2, c), chunk_rows(bc, c)
            ke = [kc * jnp.exp(ref_pt[c][i] - bcc) for i in range(n_sub)]
            for hl in range(2):
                k_hat.append(jnp.concatenate([jnp.where(head_mask[hl], ke[i], 0.0) for i in range(n_sub)], axis=1))
            last.append(bcc[CHUNK - 1:CHUNK])
            k_dec.append(kc * jnp.exp(last[c] - bcc))
        k_hat = jnp.concatenate(k_hat, axis=0)
        sc = diag_blocks(_dot_nt(q_hat, k_hat))
        sc = [jnp.where(causal2, s, 0.0) for s in sc]
        v_bd = jnp.concatenate([jnp.concatenate([chunk_rows(v[hl], c) if hl == b else zero_v for b in range(2)],
                                                axis=1) for c in range(nc) for hl in range(2)], axis=0)
        o_intra = _dot(_blocks_diag(sc, zero_tile), v_bd)
        u_all = _dot_tn(jnp.concatenate(v, axis=1), _blocks_diag(k_dec, zero_tile))

        st = gla_st[p]
        before = []
        for c in range(nc):
            before.append(st)
            u = jnp.where(hm0, u_all[:GLA_DV, c * LANES:(c + 1) * LANES], u_all[GLA_DV:, c * LANES:(c + 1) * LANES])
            st = st * jnp.exp(last[c]) + u
        gla_st[p] = st
        st_stack = jnp.concatenate(before, axis=0)
        q_in = q2 * jnp.exp(bc)
        for hl in range(2):
            h = 2 * p + hl
            o_inter = jnp.concatenate(diag_blocks(_dot_nt(jnp.where(head_mask[hl], q_in, 0.0), st_stack)), axis=0)
            o = o_intra[:, hl * GLA_DV:(hl + 1) * GLA_DV] + o_inter
            gate = proj_ref[:, C_GR + h * GLA_DV:C_GR + (h + 1) * GLA_DV].astype(F32)
            yg = _rms(o, gn_ref[:, h * GLA_DV:(h + 1) * GLA_DV]) * (gate * _sigmoid(gate))
            y_ref[:, h * GLA_DV:(h + 1) * GLA_DV] = yg.astype(y_ref.dtype)

    ones_tile = jnp.ones((t_tile, LANES), BF16)
    for p in range(ML_HEADS // 2):
        mq2 = qk[:, p * LANES:(p + 1) * LANES]
        mk2 = qk[:, ML_QK + p * LANES:ML_QK + (p + 1) * LANES] * (ML_DK ** -0.5)
        v = [proj_ref[:, C_MV + (2 * p + hl) * ML_DV:C_MV + (2 * p + hl + 1) * ML_DV] for hl in range(2)]
        k_stack = jnp.concatenate([jnp.where(head_mask[hl], chunk_rows(mk2, c), 0.0)
                                   for c in range(nc) for hl in range(2)], axis=0)
        qk_blocks = diag_blocks(_dot_nt(mq2, k_stack))

        w_blocks, wk_blocks, cm_cols, b_max, bc_last = [], [], [], [], []
        for c in range(nc):
            blk = slice(c * LANES, (c + 1) * LANES)
            b_pair = (b_rows[S_MI + 2 * p:S_MI + 2 * p + 1, blk]
                      + b_rows[S_MI + 2 * p + 1:S_MI + 2 * p + 2, nc * LANES + c * LANES:nc * LANES + (c + 1) * LANES])
            f_last = [chunk_rows(cum, c)[CHUNK - 1:CHUNK, GLA_QK + S_MF + 2 * p + hl:GLA_QK + S_MF + 2 * p + hl + 1]
                      for hl in range(2)]
            bm = jnp.where(causal2, b_pair, -jnp.inf)
            cm = [jnp.max(jnp.where(head_mask[hl], bm, -jnp.inf), axis=1, keepdims=True) for hl in range(2)]
            w_blocks.append(qk_blocks[c] * jnp.exp(bm - jnp.where(hm0, cm[0], cm[1])))
            cm_cols.append(cm)
            bmx = [jnp.max(jnp.where(head_mask[hl], b_pair, -jnp.inf), axis=1, keepdims=True) for hl in range(2)]
            b_max.append(jnp.where(hm0, bmx[0], bmx[1]))
            bc_last.append(jnp.where(hm0, f_last[0], f_last[1]))
            e_col = []
            for hl in range(2):
                h = 2 * p + hl
                b_col = (chunk_rows(smallb, c)[:, S_MI + h:S_MI + h + 1]
                         - chunk_rows(cum, c)[:, GLA_QK + S_MF + h:GLA_QK + S_MF + h + 1])
                e_col.append(jnp.exp(b_col - bmx[hl]))
            wk_blocks.append(chunk_rows(mk2, c) * jnp.where(hm0, e_col[0], e_col[1]))

        ones_col = [jnp.broadcast_to((lane == hl).astype(BF16), (CHUNK, LANES)) for hl in range(2)]
        v_bd = jnp.concatenate([jnp.concatenate([chunk_rows(v[hl], c) if hl == b else zero_v for b in range(2)]
                                                + [ones_col[hl]], axis=1)
                                for c in range(nc) for hl in range(2)], axis=0)
        nv = _dot(_blocks_diag(w_blocks, zero_tile), v_bd)
        u_all = _dot_tn(jnp.concatenate(v + [ones_tile], axis=1), _blocks_diag(wk_blocks, zero_tile))

        cs, ns, m_run = ml_cs[p], ml_ns[p], ml_m[p]
        before = []
        for c in range(nc):
            before.append((cs, ns, m_run))
            m_top = jnp.maximum(m_run, b_max[c])
            keep = jnp.exp(m_run - m_top)
            gain = jnp.exp(b_max[c] - m_top)
            blk = slice(c * LANES, (c + 1) * LANES)
            cs = keep * cs + gain * jnp.where(hm0, u_all[:ML_DV, blk], u_all[ML_DV:2 * ML_DV, blk])
            ns = keep * ns + gain * u_all[2 * ML_DV:2 * ML_DV + 1, blk]
            m_run = bc_last[c] + m_top
        ml_cs[p], ml_ns[p], ml_m[p] = cs, ns, m_run
        ns_rows = jnp.zeros((2 * SUBLANES, LANES), F32)
        for c in range(nc):
            ns_rows = jnp.where(row_sub == c, before[c][1], ns_rows)
        state_stack = jnp.concatenate([b[0] for b in before] + [ns_rows], axis=0)

        for hl in range(2):
            h = 2 * p + hl
            rr = _dot_nt(jnp.where(head_mask[hl], mq2, 0.0), state_stack)
            q_cs = diag_blocks(rr)
            h_chunks = []
            for c in range(nc):
                rows = slice(c * CHUNK, (c + 1) * CHUNK)
                q_ns = rr[rows, nc * ML_DV + c:nc * ML_DV + c + 1]
                m_prev = before[c][2][:, hl * ML_DK:hl * ML_DK + 1]
                cm = cm_cols[c][hl]
                m_top = jnp.maximum(m_prev, cm)
                local = jnp.exp(cm - m_top)
                carried = jnp.exp(m_prev - m_top)
                num = local * nv[rows, hl * ML_DV:(hl + 1) * ML_DV] + carried * q_cs[c]
                den = local * nv[rows, 2 * ML_DV + hl:2 * ML_DV + hl + 1] + carried * q_ns
                bc_col = chunk_rows(cum, c)[:, GLA_QK + S_MF + h:GLA_QK + S_MF + h + 1]
                h_chunks.append(num / jnp.maximum(jnp.abs(den), jnp.exp(-(bc_col + m_top))))
            og = proj_ref[:, C_MO + h * ML_DV:C_MO + (h + 1) * ML_DV].astype(F32)
            ym = _rms(jnp.concatenate(h_chunks, axis=0) * _sigmoid(og), mn_ref[:, h * ML_DV:(h + 1) * ML_DV])
            y_ref[:, GLA_V + h * ML_DV:GLA_V + (h + 1) * ML_DV] = ym.astype(y_ref.dtype)


def permute_hybrid_weight(w_in):
    segs = np.cumsum([0, GLA_QK, GLA_QK, GLA_V, GLA_V, GLA_RANK, ML_QK, ML_QK, ML_V, ML_V, ML_HEADS, ML_HEADS])
    seg = lambda i: w_in[:, :, int(segs[i]):int(segs[i + 1])]
    pad = jnp.zeros(w_in.shape[:2] + (LANES - GLA_RANK - 2 * ML_HEADS,), w_in.dtype)
    return jnp.concatenate([seg(0), seg(1), seg(2), seg(3), seg(5), seg(6), seg(7), seg(8),
                            seg(4), seg(9), seg(10), pad], axis=2).astype(BF16)


def hybrid_layer(x, gain, w_perm, layer, w_gate2, b_gate, gla_norm, conv_w, b_if, ml_norm, batch):
    m, d = x.shape
    seq = m // batch
    t = min(MIX_TILE, seq)
    nt = seq // t
    assert 2 * SUBLANES >= t // CHUNK
    b_small = jnp.zeros((1, LANES), F32).at[0, S_MI:S_MI + 2 * ML_HEADS].set(b_if)
    w_gate2_rows = jnp.zeros((LANES, GLA_QK), F32).at[S_GLR:S_GLR + GLA_RANK].set(w_gate2)
    return pl.pallas_call(
        _hybrid_kernel,
        grid=(batch, nt),
        in_specs=[pl.BlockSpec((t, d), lambda b, i: (b * nt + i, 0)),
                  _const_spec((1, d)),
                  _layer_spec((d, HYB_COLS), layer),
                  _const_spec((LANES, GLA_QK)),
                  _const_spec((1, GLA_QK)),
                  _const_spec((1, GLA_V)),
                  _const_spec((ML_CONV, 2 * ML_QK)),
                  _const_spec((1, LANES)),
                  _const_spec((1, ML_V))],
        out_specs=pl.BlockSpec((t, GLA_V + ML_V), lambda b, i: (b * nt + i, 0)),
        out_shape=jax.ShapeDtypeStruct((m, GLA_V + ML_V), BF16),
        scratch_shapes=[pltpu.VMEM((t, C_SMALL), BF16),
                        pltpu.VMEM((GLA_HEADS // 2, GLA_DV, LANES), F32),
                        pltpu.VMEM((ML_HEADS // 2, ML_DV, LANES), F32),
                        pltpu.VMEM((ML_HEADS // 2, 1, LANES), F32),
                        pltpu.VMEM((ML_HEADS // 2, 1, LANES), F32),
                        pltpu.VMEM((SUBLANES, 2 * ML_QK), F32)],
        compiler_params=_params(2),
        name="hybrid_core",
    )(x, gain.reshape(1, d), w_perm, w_gate2_rows, b_gate.reshape(1, -1), gla_norm.reshape(1, -1), conv_w,
      b_small, ml_norm.reshape(1, -1))


LOG2E = 1.4426950408889634
AUG = 2 * MOBA_HD
LANES_PER_BLOCK = 4
MASK_BIG = 2.0 ** 100
PROJ_HEADS = 4
VT_ROWS = MOBA_HD + 16


def _split3(x):
    x = np.asarray(x, np.float32)
    hi = x.astype(BF16).astype(np.float32)
    mid = (x - hi).astype(BF16).astype(np.float32)
    lo = (x - hi - mid).astype(BF16).astype(np.float32)
    return np.stack([hi, mid, lo], axis=-1)


def _aug_tables(nb):
    c = np.asarray([LOG2E * 2.0 ** (-8.0 * (h + 1) / MOBA_HEADS) for h in range(MOBA_HEADS)], np.float32)
    pos = np.arange(MOBA_BLOCK, dtype=np.float32)
    base = LANES_PER_BLOCK * nb
    q_rows = np.zeros((MOBA_HEADS, MOBA_BLOCK, LANES), np.float32)
    k_rows = np.zeros((MOBA_HEADS, MOBA_BLOCK, LANES), np.float32)
    q_rows[:, :, base:base + 3] = _split3(-c[:, None] * pos[None, :])
    q_rows[:, :, base + 3:base + 6] = 1.0
    k_rows[:, :, base:base + 3] = 1.0
    k_rows[:, :, base + 3:base + 6] = _split3(c[:, None] * pos[None, :])
    i_idx = np.arange(nb)[:, None]
    j_idx = np.arange(nb)[None, :]
    steps = np.zeros((MOBA_HEADS, nb, nb, LANES_PER_BLOCK), np.float32)
    steps[..., 0] = np.where(j_idx == i_idx, 0.0, -MASK_BIG)
    gap = _split3(-c[:, None, None] * (MOBA_BLOCK * (i_idx - j_idx)).astype(np.float32)[None])
    steps[..., 1:] = np.where((j_idx <= i_idx)[None, :, :, None], gap, 0.0)
    q_steps = np.zeros((MOBA_HEADS, nb, 1, LANES), np.float32)
    q_steps[:, :, 0, :base] = steps.reshape(MOBA_HEADS, nb, base)
    k_steps = np.zeros((nb, 1, LANES), np.float32)
    k_steps[:, 0, :base] = np.repeat(np.eye(nb, dtype=np.float32), LANES_PER_BLOCK, axis=1)
    expand = np.zeros((nb, LANES), np.float32)
    expand[np.arange(nb), LANES_PER_BLOCK * np.arange(nb)] = MASK_BIG
    return (jnp.asarray(q_rows), jnp.asarray(q_steps), jnp.asarray(k_rows), jnp.asarray(k_steps),
            jnp.asarray(expand, BF16))


def _moba_prep_kernel(x_ref, g_ref, w_ref, qg_ref, kg_ref, qrow_ref, qstep_ref, krow_ref, kstep_ref, exp_ref,
                      qa_ref, ka_ref, vt_ref, kmean_ref, *, blocks):
    step = pl.program_id(1)
    nb = kmean_ref.shape[1]
    d = MOBA_HEADS * MOBA_HD

    @pl.when(step == 0)
    def _():
        kmean_ref[...] = jnp.zeros_like(kmean_ref)

    xn = _rms(x_ref[...], g_ref[...]).astype(BF16)
    blk = lax.broadcasted_iota(jnp.int32, (nb, MOBA_BLOCK), 0)
    ones_rows = (lax.broadcasted_iota(jnp.int32, (VT_ROWS - MOBA_HD, MOBA_BLOCK), 0) == 0).astype(F32)

    for h in range(MOBA_HEADS):
        hg = h % PROJ_HEADS
        if hg == 0:
            c0 = h * MOBA_HD
            q_grp, k_grp, v_grp = (jnp.dot(xn, w_ref[:, off + c0:off + c0 + PROJ_HEADS * MOBA_HD],
                                           preferred_element_type=F32) for off in (0, d, 2 * d))
        part = slice(hg * MOBA_HD, (hg + 1) * MOBA_HD)
        for u in range(blocks):
            i = step * blocks + u
            rows = slice(u * MOBA_BLOCK, (u + 1) * MOBA_BLOCK)
            past = blk < i
            q = _rms(q_grp[rows, part], qg_ref[...]) * (MOBA_HD ** -0.5)
            k = _rms(k_grp[rows, part], kg_ref[...])
            v = v_grp[rows, part]
            vt_ref[0, h, :, rows] = jnp.concatenate([v.T, ones_rows], axis=0).astype(vt_ref.dtype)

            gate = _dot_nt(kmean_ref[h], q)
            gate = jnp.where(past, gate, -jnp.inf)
            rank = jnp.zeros((nb, MOBA_BLOCK), F32)
            for j in range(nb):
                other = gate[j:j + 1]
                ahead = (other > gate) | ((other == gate) & (j < blk))
                rank = rank + ahead.astype(F32)
            sel = (past & (rank < MOBA_TOPK)).astype(BF16)
            q_extra = (qrow_ref[h] + qstep_ref[h, i]) + _dot_tn(sel, exp_ref[...])
            qa_ref[rows, h * AUG:h * AUG + MOBA_HD] = (q * LOG2E).astype(qa_ref.dtype)
            qa_ref[rows, h * AUG + MOBA_HD:(h + 1) * AUG] = q_extra.astype(qa_ref.dtype)
            ka_ref[rows, h * AUG:h * AUG + MOBA_HD] = k.astype(ka_ref.dtype)
            ka_ref[rows, h * AUG + MOBA_HD:(h + 1) * AUG] = (krow_ref[h] + kstep_ref[i]).astype(ka_ref.dtype)

            kmean_ref[h, pl.ds(i, 1), :] = jnp.mean(k, axis=0, keepdims=True)


PREP_BLOCKS = 4


def moba_prep(x, gain, w_qkv, layer, q_gain, k_gain, batch):
    m, dm = x.shape
    seq = m // batch
    nb = seq // MOBA_BLOCK
    assert LANES_PER_BLOCK * nb + 6 <= LANES and nb % PREP_BLOCKS == 0
    d = MOBA_HEADS * MOBA_HD
    rows = PREP_BLOCKS * MOBA_BLOCK
    ns = nb // PREP_BLOCKS
    q_rows, q_steps, k_rows, k_steps, expand = _aug_tables(nb)
    return pl.pallas_call(
        functools.partial(_moba_prep_kernel, blocks=PREP_BLOCKS),
        grid=(batch, ns),
        in_specs=[pl.BlockSpec((rows, dm), lambda b, i: (b * ns + i, 0)),
                  _const_spec((1, dm)),
                  _layer_spec((dm, 3 * d), layer),
                  _const_spec((1, MOBA_HD)),
                  _const_spec((1, MOBA_HD)),
                  _const_spec((MOBA_HEADS, MOBA_BLOCK, LANES)),
                  _const_spec((MOBA_HEADS, nb, 1, LANES)),
                  _const_spec((MOBA_HEADS, MOBA_BLOCK, LANES)),
                  _const_spec((nb, 1, LANES)),
                  _const_spec((nb, LANES))],
        out_specs=[pl.BlockSpec((rows, MOBA_HEADS * AUG), lambda b, i: (b * ns + i, 0)),
                   pl.BlockSpec((rows, MOBA_HEADS * AUG), lambda b, i: (b * ns + i, 0)),
                   pl.BlockSpec((1, MOBA_HEADS, VT_ROWS, rows), lambda b, i: (b, 0, 0, i))],
        out_shape=[jax.ShapeDtypeStruct((m, MOBA_HEADS * AUG), BF16),
                   jax.ShapeDtypeStruct((m, MOBA_HEADS * AUG), BF16),
                   jax.ShapeDtypeStruct((batch, MOBA_HEADS, VT_ROWS, seq), BF16)],
        scratch_shapes=[pltpu.VMEM((MOBA_HEADS, nb, MOBA_HD), F32)],
        compiler_params=_params(2),
        name="moba_prep",
    )(x, gain.reshape(1, dm), w_qkv, q_gain.reshape(1, -1), k_gain.reshape(1, -1),
      q_rows, q_steps, k_rows, k_steps, expand)


ATTN_HEADS_PER_STEP = 4
ATTN_CHUNK_BLOCKS = 4
PV_KEYS = 256


def _moba_attn_kernel(q_ref, k_ref, vt_ref, o_ref, s_even, s_odd, m_even, m_odd, *, heads, chunk_blocks, nb):
    t = pl.program_id(2)
    i = jnp.minimum(t, nb - 1)
    chunk_keys = chunk_blocks * MOBA_BLOCK
    n_chunks = i // chunk_blocks + 1
    n_valid2 = jnp.where(t > 0, (t - 1) // chunk_blocks + 1, 0)

    @pl.when((pl.program_id(0) == 0) & (pl.program_id(1) == 0) & (t == 0))
    def _():
        s_even[...] = jnp.zeros_like(s_even)
        s_odd[...] = jnp.zeros_like(s_odd)

    key_r = lax.broadcasted_iota(jnp.int32, (MOBA_BLOCK, MOBA_BLOCK), 0)
    qry_c = lax.broadcasted_iota(jnp.int32, (MOBA_BLOCK, MOBA_BLOCK), 1)
    causal = qry_c >= key_r
    q_aug = [q_ref[:, hh * AUG:(hh + 1) * AUG] for hh in range(heads)]

    def both(c, carry, last, s_w, s_r, m_row2, stage1, stage2):
        m_part, acc = carry
        k0 = c * chunk_keys
        live2 = jnp.full((1, MOBA_BLOCK), c, jnp.int32) < n_valid2
        new_m, new_acc = [], []
        for hh in range(heads):
            m_new, a = m_part[hh], acc[hh]
            if stage1:
                s = lax.dot_general(k_ref[pl.ds(k0, chunk_keys), hh * AUG:(hh + 1) * AUG], q_aug[hh],
                                    (((1,), (1,)), ((), ())), preferred_element_type=F32)
                if last:
                    parts = []
                    for u in range(chunk_blocks):
                        other_block = jnp.full((1, MOBA_BLOCK), c * chunk_blocks + u, jnp.int32) != i
                        parts.append(jnp.where(causal | other_block, s[u * MOBA_BLOCK:(u + 1) * MOBA_BLOCK], NEG))
                    s = jnp.concatenate(parts, axis=0)
                s_w[hh, pl.ds(k0, chunk_keys), :] = s
                m_new = jnp.maximum(m_new, jnp.max(s.reshape(chunk_keys // SUBLANES, SUBLANES, MOBA_BLOCK), axis=0))
            if stage2:
                for u in range(0, chunk_keys, PV_KEYS):
                    ku = k0 + u
                    p = jnp.exp2(s_r[hh, pl.ds(ku, PV_KEYS), :] - m_row2[hh])
                    p = jnp.where(live2, p, 0.0).astype(BF16)
                    a = a + jnp.dot(vt_ref[0, hh, :, pl.ds(ku, PV_KEYS)], p, preferred_element_type=F32)
            new_m.append(m_new)
            new_acc.append(a)
        return tuple(new_m), tuple(new_acc)

    def step(n, s_w, s_r, m_w, m_r, stage1=True, stage2=True):
        m_row2 = [m_r[hh] for hh in range(heads)]
        carry = (tuple(jnp.full((SUBLANES, MOBA_BLOCK), NEG, F32) for _ in range(heads)),
                 tuple(jnp.zeros((VT_ROWS, MOBA_BLOCK), F32) for _ in range(heads)))
        for c in range(n):
            carry = both(c, carry, c == n - 1, s_w, s_r, m_row2, stage1, stage2)
        m_part, acc = carry
        for hh in range(heads):
            if stage1:
                m_w[hh] = jnp.max(m_part[hh], axis=0, keepdims=True)
            if stage2:
                o = acc[hh][:MOBA_HD] * (1.0 / acc[hh][MOBA_HD:MOBA_HD + 1])
                o_ref[:, hh * MOBA_HD:(hh + 1) * MOBA_HD] = o.T.astype(o_ref.dtype)

    even = (s_even, s_odd, m_even, m_odd)
    odd = (s_odd, s_even, m_odd, m_even)
    n_max = nb // chunk_blocks
    pl.when(t == 0)(functools.partial(step, 1, *even, stage2=False))
    pl.when(t == nb)(functools.partial(step, n_max, *(odd if nb % 2 else even), stage1=False))
    inner = (t > 0) & (t < nb)
    for n in range(1, n_max + 1):
        pl.when(inner & (n_chunks == n) & (t % 2 == 0))(functools.partial(step, n, *even))
        pl.when(inner & (n_chunks == n) & (t % 2 == 1))(functools.partial(step, n, *odd))


def moba_attention(q_aug, k_aug, vt, batch):
    m = q_aug.shape[0]
    seq = m // batch
    nb = seq // MOBA_BLOCK
    hps = ATTN_HEADS_PER_STEP
    assert nb % ATTN_CHUNK_BLOCKS == 0
    return pl.pallas_call(
        functools.partial(_moba_attn_kernel, heads=hps, chunk_blocks=ATTN_CHUNK_BLOCKS, nb=nb),
        grid=(batch, MOBA_HEADS // hps, nb + 1),
        in_specs=[pl.BlockSpec((MOBA_BLOCK, hps * AUG), lambda b, g, t: (b * nb + jnp.minimum(t, nb - 1), g)),
                  pl.BlockSpec((seq, hps * AUG), lambda b, g, t: (b, g), pipeline_mode=pl.Buffered(1)),
                  pl.BlockSpec((1, hps, VT_ROWS, seq), lambda b, g, t: (b, g, 0, 0),
                               pipeline_mode=pl.Buffered(1))],
        out_specs=pl.BlockSpec((MOBA_BLOCK, hps * MOBA_HD), lambda b, g, t: (b * nb + jnp.maximum(t - 1, 0), g)),
        out_shape=jax.ShapeDtypeStruct((m, MOBA_HEADS * MOBA_HD), BF16),
        scratch_shapes=[pltpu.VMEM((hps, seq, MOBA_BLOCK), F32), pltpu.VMEM((hps, seq, MOBA_BLOCK), F32),
                        pltpu.VMEM((hps, 1, MOBA_BLOCK), F32), pltpu.VMEM((hps, 1, MOBA_BLOCK), F32)],
        compiler_params=_params(3),
        name="moba_attn",
    )(q_aug, k_aug, vt)


def moba_layer(x, gain, w_qkv, layer, q_gain, k_gain, batch):
    q_aug, k_aug, vt = moba_prep(x, gain, w_qkv, layer, q_gain, k_gain, batch)
    return moba_attention(q_aug, k_aug, vt, batch)


def kernel(x, norm_mix, norm_ffn, hyb_w_in, gla_w_gate2, gla_b_gate, gla_norm, ml_conv, ml_b_if, ml_norm,
           hyb_w_out, moba_w_qkv, moba_q_norm, moba_k_norm, moba_w_o, ffn_w_gate, ffn_w_up, ffn_w_down):
    batch, seq, d = x.shape
    depth = norm_mix.shape[0]
    xf = x.reshape(batch * seq, d)
    w_hyb_in = permute_hybrid_weight(hyb_w_in)
    w_hyb_out, w_moba_qkv, w_moba_o = hyb_w_out.astype(BF16), moba_w_qkv.astype(BF16), moba_w_o.astype(BF16)
    w_gate, w_up, w_down = ffn_w_gate.astype(BF16), ffn_w_up.astype(BF16), ffn_w_down.astype(BF16)
    for l in range(depth):
        j = l // 2
        if l % 2 == 0:
            y = hybrid_layer(xf, norm_mix[l], w_hyb_in, j, gla_w_gate2[j], gla_b_gate[j], gla_norm[j],
                             ml_conv[j], ml_b_if[j], ml_norm[j], batch)
            w_o = w_hyb_out
        else:
            y = moba_layer(xf, norm_mix[l], w_moba_qkv, j, moba_q_norm[j], moba_k_norm[j], batch)
            w_o = w_moba_o
        xf = out_ffn(xf, y, w_o, j, norm_ffn[l], w_gate, w_up, w_down, l)
    return xf.reshape(batch, seq, d)
```

```python
import functools

import jax
import jax.numpy as jnp
import numpy as np
from jax import lax
from jax.experimental import pallas as pl
from jax.experimental.pallas import tpu as pltpu

F32 = jnp.float32
BF16 = jnp.bfloat16

LANES = 128
SUBLANES = 8
BF16_ROWS = 2 * SUBLANES
VMEM_LIMIT_BYTES = 56 * 1024 * 1024

EPS = 1e-6
NEG = -1e30

GLA_HEADS = 4
GLA_DK = 64
GLA_DV = 128
GLA_RANK = 16
GLA_TAU = 16.0
ML_HEADS = 4
ML_DK = 64
ML_DV = 128
ML_CONV = 4
CHUNK = 64
SUB = 16
MOBA_HEADS = 8
MOBA_HD = 128
MOBA_BLOCK = 256
MOBA_TOPK = 3

GLA_QK = GLA_HEADS * GLA_DK
GLA_V = GLA_HEADS * GLA_DV
ML_QK = ML_HEADS * ML_DK
ML_V = ML_HEADS * ML_DV

C_GQ = 0
C_GK = C_GQ + GLA_QK
C_GV = C_GK + GLA_QK
C_GR = C_GV + GLA_V
C_MQ = C_GR + GLA_V
C_MK = C_MQ + ML_QK
C_MV = C_MK + ML_QK
C_MO = C_MV + ML_V
C_SMALL = C_MO + ML_V
HYB_COLS = C_SMALL + LANES
S_GLR = 0
S_MI = GLA_RANK
S_MF = GLA_RANK + ML_HEADS

ROW_TILE = 1024
MIX_TILE = 256
FF_TILE = 256
PROJ_COL_TILE = 512


def _rms(x, gain):
    return x * lax.rsqrt(jnp.mean(x * x, axis=-1, keepdims=True) + EPS) * gain


def _log_sigmoid(z):
    return jnp.minimum(z, 0.0) - jnp.log1p(jnp.exp(-jnp.abs(z)))


def _sigmoid(z):
    return 1.0 / (1.0 + jnp.exp(-z))


def _dot(a, b):
    return jnp.dot(a.astype(BF16), b.astype(BF16), preferred_element_type=F32)


def _dot_nt(a, b):
    return lax.dot_general(a.astype(BF16), b.astype(BF16), (((1,), (1,)), ((), ())),
                           preferred_element_type=F32)


def _dot_tn(a, b):
    return lax.dot_general(a.astype(BF16), b.astype(BF16), (((0,), (0,)), ((), ())),
                           preferred_element_type=F32)


def _const_spec(shape):
    return pl.BlockSpec(shape, lambda *_: (0,) * len(shape), pipeline_mode=pl.Buffered(1))


def _layer_spec(shape, layer):
    return pl.BlockSpec((None,) + tuple(shape), lambda *_: (layer,) + (0,) * len(shape),
                        pipeline_mode=pl.Buffered(1))


def _params(n_grid):
    return pltpu.CompilerParams(dimension_semantics=("arbitrary",) * n_grid,
                                vmem_limit_bytes=VMEM_LIMIT_BYTES)


def _out_ffn_kernel(x_ref, y_ref, wo_ref, g_ref, wg_ref, wu_ref, wd_ref, o_ref, h_ref, *, ff_tile):
    x1 = x_ref[...] + jnp.dot(y_ref[...], wo_ref[...], preferred_element_type=F32)
    xn = _rms(x1, g_ref[...]).astype(BF16)
    d_ff = wg_ref.shape[1]
    for c0 in range(0, d_ff, ff_tile):
        g = jnp.dot(xn, wg_ref[:, c0:c0 + ff_tile], preferred_element_type=F32)
        u = jnp.dot(xn, wu_ref[:, c0:c0 + ff_tile], preferred_element_type=F32)
        h_ref[:, c0:c0 + ff_tile] = (g * _sigmoid(g) * u).astype(BF16)
    o_ref[...] = x1 + jnp.dot(h_ref[...], wd_ref[...], preferred_element_type=F32)


def out_ffn(x, y, w_o, o_layer, gain, w_gate, w_up, w_down, layer):
    m, d = x.shape
    dy = y.shape[1]
    d_ff = w_gate.shape[2]
    tm = min(ROW_TILE, m)
    return pl.pallas_call(
        functools.partial(_out_ffn_kernel, ff_tile=FF_TILE),
        grid=(m // tm,),
        in_specs=[pl.BlockSpec((tm, d), lambda i: (i, 0)),
                  pl.BlockSpec((tm, dy), lambda i: (i, 0)),
                  _layer_spec((dy, d), o_layer),
                  _const_spec((1, d)),
                  _layer_spec((d, d_ff), layer),
                  _layer_spec((d, d_ff), layer),
                  _layer_spec((d_ff, d), layer)],
        out_specs=pl.BlockSpec((tm, d), lambda i: (i, 0)),
        out_shape=jax.ShapeDtypeStruct((m, d), F32),
        scratch_shapes=[pltpu.VMEM((tm, d_ff), BF16)],
        compiler_params=_params(1),
        name="out_ffn",
    )(x, y, w_o, gain.reshape(1, d), w_gate, w_up, w_down)


def _split3_rows(x):
    hi = x.astype(BF16)
    r1 = x - hi.astype(F32)
    mid = r1.astype(BF16)
    lo = (r1 - mid.astype(F32)).astype(BF16)
    return jnp.concatenate([hi, mid, lo], axis=0)


def _blocks_diag(tiles, zero):
    n = len(tiles)
    return jnp.concatenate(
        [jnp.concatenate([tiles[a] if a == b else zero for b in range(n)], axis=1) for a in range(n)], axis=0)


def _hybrid_kernel(x_ref, g_ref, w_ref, wg2_ref, bg_ref, gn_ref, conv_ref, bsmall_ref, mn_ref,
                   y_ref, proj_ref, gla_st, ml_cs, ml_ns, ml_m, conv_carry):
    t_tile = proj_ref.shape[0]
    nc = t_tile // CHUNK
    n_sub = CHUNK // SUB

    @pl.when(pl.program_id(1) == 0)
    def _():
        gla_st[...] = jnp.zeros_like(gla_st)
        ml_cs[...] = jnp.zeros_like(ml_cs)
        ml_ns[...] = jnp.zeros_like(ml_ns)
        ml_m[...] = jnp.zeros_like(ml_m)
        conv_carry[...] = jnp.zeros_like(conv_carry)

    xn = _rms(x_ref[...], g_ref[...]).astype(BF16)
    for c0 in range(0, C_SMALL, PROJ_COL_TILE):
        proj_ref[:, c0:c0 + PROJ_COL_TILE] = jnp.dot(xn, w_ref[:, c0:c0 + PROJ_COL_TILE],
                                                      preferred_element_type=F32).astype(proj_ref.dtype)
    small = jnp.dot(xn, w_ref[:, C_SMALL:], preferred_element_type=F32)

    xcat = proj_ref[:, C_MQ:C_MQ + 2 * ML_QK].astype(F32)
    ext = jnp.concatenate([conv_carry[...], xcat], axis=0)
    cw = conv_ref[...]
    acc = ext * cw[ML_CONV - 1:ML_CONV]
    for j in range(1, ML_CONV):
        acc = acc + pltpu.roll(ext, j, axis=0) * cw[ML_CONV - 1 - j:ML_CONV - j]
    conv = acc[SUBLANES:]
    qk = conv * _sigmoid(conv)
    conv_carry[...] = xcat[t_tile - SUBLANES:]

    lane = lax.broadcasted_iota(jnp.int32, (1, LANES), 1)
    head_mask = (lane < GLA_DK, lane >= GLA_DK)
    hm0 = head_mask[0]
    r_t = lax.broadcasted_iota(jnp.int32, (t_tile, t_tile), 0)
    c_t = lax.broadcasted_iota(jnp.int32, (t_tile, t_tile), 1)
    tril_bd = ((r_t >= c_t) & (r_t // CHUNK == c_t // CHUNK)).astype(BF16)
    tril3 = jnp.concatenate([tril_bd, tril_bd, tril_bd], axis=1)
    col_w = lax.broadcasted_iota(jnp.int32, (t_tile, 2 * nc * LANES), 1)
    row_w = lax.broadcasted_iota(jnp.int32, (t_tile, 2 * nc * LANES), 0)
    to_rows = ((col_w % CHUNK == row_w % CHUNK) & ((col_w % (nc * LANES)) // LANES == row_w // CHUNK)
               & ((col_w % LANES) // CHUNK == col_w // (nc * LANES))).astype(BF16)
    to_rows3 = jnp.concatenate([to_rows, to_rows, to_rows], axis=0)
    row_c = lax.broadcasted_iota(jnp.int32, (CHUNK, LANES), 0)
    causal2 = (lax.broadcasted_iota(jnp.int32, (CHUNK, LANES), 1) % CHUNK) <= row_c
    sub_id = (lax.broadcasted_iota(jnp.int32, (t_tile, LANES), 0) % CHUNK) // SUB
    row_sub = lax.broadcasted_iota(jnp.int32, (BF16_ROWS, LANES), 0)
    zero_tile = jnp.zeros((CHUNK, LANES), F32)
    zero_v = jnp.zeros((CHUNK, LANES), BF16)

    def chunk_rows(x, c):
        return x[c * CHUNK:(c + 1) * CHUNK]

    def diag_blocks(r):
        return [r[c * CHUNK:(c + 1) * CHUNK, c * LANES:(c + 1) * LANES] for c in range(nc)]

    smallb = small + bsmall_ref[...]

    z = _dot(small, wg2_ref[...]) + bg_ref[...]
    log_a = _log_sigmoid(z) * (1.0 / GLA_TAU)
    cum = jnp.dot(tril3, _split3_rows(jnp.concatenate([log_a, _log_sigmoid(smallb)], axis=1)),
                  preferred_element_type=F32)
    b_cols = smallb - pltpu.roll(cum[:, GLA_QK:], LANES - ML_HEADS, axis=1)
    b_rows = lax.dot_general(_split3_rows(b_cols), to_rows3, (((0,), (0,)), ((), ())), preferred_element_type=F32)

    for p in range(GLA_HEADS // 2):
        q2 = proj_ref[:, C_GQ + p * LANES:C_GQ + (p + 1) * LANES].astype(F32) * (GLA_DK ** -0.5)
        k2 = proj_ref[:, C_GK + p * LANES:C_GK + (p + 1) * LANES].astype(F32)
        bc = cum[:, p * LANES:(p + 1) * LANES]
        v = [proj_ref[:, C_GV + (2 * p + hl) * GLA_DV:C_GV + (2 * p + hl + 1) * GLA_DV] for hl in range(2)]

        ref_pt = [[chunk_rows(bc, c)[i * SUB - 1:i * SUB] if i else jnp.zeros((1, LANES), F32)
                   for i in range(n_sub)] for c in range(nc)]
        r_own = jnp.concatenate([jnp.broadcast_to(ref_pt[c][i], (SUB, LANES))
                                 for c in range(nc) for i in range(n_sub)], axis=0)
        qs = q2 * jnp.exp(bc - r_own)
        q_hat = jnp.concatenate([jnp.where(sub_id == i, qs, 0.0) for i in range(n_sub)], axis=1)
        k_hat, k_dec, last = [], [], []
        for c in range(nc):
            kc, bcc = chunk_rows(k2, c), chunk_rows(bc, c)
            ke = [kc * jnp.exp(ref_pt[c][i] - bcc) for i in range(n_sub)]
            for hl in range(2):
                k_hat.append(jnp.concatenate([jnp.where(head_mask[hl], ke[i], 0.0) for i in range(n_sub)], axis=1))
            last.append(bcc[CHUNK - 1:CHUNK])
            k_dec.append(kc * jnp.exp(last[c] - bcc))
        k_hat = jnp.concatenate(k_hat, axis=0)
        sc = diag_blocks(_dot_nt(q_hat, k_hat))
        sc = [jnp.where(causal2, s, 0.0) for s in sc]
        v_bd = jnp.concatenate([jnp.concatenate([chunk_rows(v[hl], c) if hl == b else zero_v for b in range(2)],
                                                axis=1) for c in range(nc) for hl in range(2)], axis=0)
        o_intra = _dot(_blocks_diag(sc, zero_tile), v_bd)
        u_all = _dot_tn(jnp.concatenate(v, axis=1), _blocks_diag(k_dec, zero_tile))

        st = gla_st[p]
        before = []
        for c in range(nc):
            before.append(st)
            u = jnp.where(hm0, u_all[:GLA_DV, c * LANES:(c + 1) * LANES], u_all[GLA_DV:, c * LANES:(c + 1) * LANES])
            st = st * jnp.exp(last[c]) + u
        gla_st[p] = st
        st_stack = jnp.concatenate(before, axis=0)
        q_in = q2 * jnp.exp(bc)
        for hl in range(2):
            h = 2 * p + hl
            o_inter = jnp.concatenate(diag_blocks(_dot_nt(jnp.where(head_mask[hl], q_in, 0.0), st_stack)), axis=0)
            o = o_intra[:, hl * GLA_DV:(hl + 1) * GLA_DV] + o_inter
            gate = proj_ref[:, C_GR + h * GLA_DV:C_GR + (h + 1) * GLA_DV].astype(F32)
            yg = _rms(o, gn_ref[:, h * GLA_DV:(h + 1) * GLA_DV]) * (gate * _sigmoid(gate))
            y_ref[:, h * GLA_DV:(h + 1) * GLA_DV] = yg.astype(y_ref.dtype)

    ones_tile = jnp.ones((t_tile, LANES), BF16)
    for p in range(ML_HEADS // 2):
        mq2 = qk[:, p * LANES:(p + 1) * LANES]
        mk2 = qk[:, ML_QK + p * LANES:ML_QK + (p + 1) * LANES] * (ML_DK ** -0.5)
        v = [proj_ref[:, C_MV + (2 * p + hl) * ML_DV:C_MV + (2 * p + hl + 1) * ML_DV] for hl in range(2)]
        k_stack = jnp.concatenate([jnp.where(head_mask[hl], chunk_rows(mk2, c), 0.0)
                                   for c in range(nc) for hl in range(2)], axis=0)
        qk_blocks = diag_blocks(_dot_nt(mq2, k_stack))

        w_blocks, wk_blocks, cm_cols, b_max, bc_last = [], [], [], [], []
        for c in range(nc):
            blk = slice(c * LANES, (c + 1) * LANES)
            b_pair = (b_rows[S_MI + 2 * p:S_MI + 2 * p + 1, blk]
                      + b_rows[S_MI + 2 * p + 1:S_MI + 2 * p + 2, nc * LANES + c * LANES:nc * LANES + (c + 1) * LANES])
            f_last = [chunk_rows(cum, c)[CHUNK - 1:CHUNK, GLA_QK + S_MF + 2 * p + hl:GLA_QK + S_MF + 2 * p + hl + 1]
                      for hl in range(2)]
            bm = jnp.where(causal2, b_pair, -jnp.inf)
            cm = [jnp.max(jnp.where(head_mask[hl], bm, -jnp.inf), axis=1, keepdims=True) for hl in range(2)]
            w_blocks.append(qk_blocks[c] * jnp.exp(bm - jnp.where(hm0, cm[0], cm[1])))
            cm_cols.append(cm)
            bmx = [jnp.max(jnp.where(head_mask[hl], b_pair, -jnp.inf), axis=1, keepdims=True) for hl in range(2)]
            b_max.append(jnp.where(hm0, bmx[0], bmx[1]))
            bc_last.append(jnp.where(hm0, f_last[0], f_last[1]))
            e_col = []
            for hl in range(2):
                h = 2 * p + hl
                b_col = (chunk_rows(smallb, c)[:, S_MI + h:S_MI + h + 1]
                         - chunk_rows(cum, c)[:, GLA_QK + S_MF + h:GLA_QK + S_MF + h + 1])
                e_col.append(jnp.exp(b_col - bmx[hl]))
            wk_blocks.append(chunk_rows(mk2, c) * jnp.where(hm0, e_col[0], e_col[1]))

        ones_col = [jnp.broadcast_to((lane == hl).astype(BF16), (CHUNK, LANES)) for hl in range(2)]
        v_bd = jnp.concatenate([jnp.concatenate([chunk_rows(v[hl], c) if hl == b else zero_v for b in range(2)]
                                                + [ones_col[hl]], axis=1)
                                for c in range(nc) for hl in range(2)], axis=0)
        nv = _dot(_blocks_diag(w_blocks, zero_tile), v_bd)
        u_all = _dot_tn(jnp.concatenate(v + [ones_tile], axis=1), _blocks_diag(wk_blocks, zero_tile))

        cs, ns, m_run = ml_cs[p], ml_ns[p], ml_m[p]
        before = []
        for c in range(nc):
            before.append((cs, ns, m_run))
            m_top = jnp.maximum(m_run, b_max[c])
            keep = jnp.exp(m_run - m_top)
            gain = jnp.exp(b_max[c] - m_top)
            blk = slice(c * LANES, (c + 1) * LANES)
            cs = keep * cs + gain * jnp.where(hm0, u_all[:ML_DV, blk], u_all[ML_DV:2 * ML_DV, blk])
            ns = keep * ns + gain * u_all[2 * ML_DV:2 * ML_DV + 1, blk]
            m_run = bc_last[c] + m_top
        ml_cs[p], ml_ns[p], ml_m[p] = cs, ns, m_run
        ns_rows = jnp.zeros((BF16_ROWS, LANES), F32)
        for c in range(nc):
            ns_rows = jnp.where(row_sub == c, before[c][1], ns_rows)
        state_stack = jnp.concatenate([b[0] for b in before] + [ns_rows], axis=0)

        for hl in range(2):
            h = 2 * p + hl
            rr = _dot_nt(jnp.where(head_mask[hl], mq2, 0.0), state_stack)
            q_cs = diag_blocks(rr)
            h_chunks = []
            for c in range(nc):
                rows = slice(c * CHUNK, (c + 1) * CHUNK)
                q_ns = rr[rows, nc * ML_DV + c:nc * ML_DV + c + 1]
                m_prev = before[c][2][:, hl * ML_DK:hl * ML_DK + 1]
                cm = cm_cols[c][hl]
                m_top = jnp.maximum(m_prev, cm)
                local = jnp.exp(cm - m_top)
                carried = jnp.exp(m_prev - m_top)
                num = local * nv[rows, hl * ML_DV:(hl + 1) * ML_DV] + carried * q_cs[c]
                den = local * nv[rows, 2 * ML_DV + hl:2 * ML_DV + hl + 1] + carried * q_ns
                bc_col = chunk_rows(cum, c)[:, GLA_QK + S_MF + h:GLA_QK + S_MF + h + 1]
                h_chunks.append(num / jnp.maximum(jnp.abs(den), jnp.exp(-(bc_col + m_top))))
            og = proj_ref[:, C_MO + h * ML_DV:C_MO + (h + 1) * ML_DV].astype(F32)
            ym = _rms(jnp.concatenate(h_chunks, axis=0) * _sigmoid(og), mn_ref[:, h * ML_DV:(h + 1) * ML_DV])
            y_ref[:, GLA_V + h * ML_DV:GLA_V + (h + 1) * ML_DV] = ym.astype(y_ref.dtype)


def permute_hybrid_weight(w_in):
    w = w_in.astype(BF16)
    n_gla = 2 * GLA_QK + 2 * GLA_V
    n_ml = 2 * ML_QK + 2 * ML_V
    ml0 = n_gla + GLA_RANK
    pad = jnp.zeros(w.shape[:2] + (LANES - GLA_RANK - 2 * ML_HEADS,), BF16)
    return jnp.concatenate([w[:, :, :n_gla], w[:, :, ml0:ml0 + n_ml], w[:, :, n_gla:ml0], w[:, :, ml0 + n_ml:], pad],
                           axis=2)


def hybrid_layer(x, gain, w_perm, layer, w_gate2, b_gate, gla_norm, conv_w, b_if, ml_norm, batch):
    m, d = x.shape
    seq = m // batch
    t = min(MIX_TILE, seq)
    nt = seq // t
    assert BF16_ROWS >= t // CHUNK
    b_small = jnp.zeros((1, LANES), F32).at[0, S_MI:S_MI + 2 * ML_HEADS].set(b_if)
    w_gate2_rows = jnp.zeros((LANES, GLA_QK), F32).at[S_GLR:S_GLR + GLA_RANK].set(w_gate2)
    return pl.pallas_call(
        _hybrid_kernel,
        grid=(batch, nt),
        in_specs=[pl.BlockSpec((t, d), lambda b, i: (b * nt + i, 0)),
                  _const_spec((1, d)),
                  _layer_spec((d, HYB_COLS), layer),
                  _const_spec((LANES, GLA_QK)),
                  _const_spec((1, GLA_QK)),
                  _const_spec((1, GLA_V)),
                  _const_spec((ML_CONV, 2 * ML_QK)),
                  _const_spec((1, LANES)),
                  _const_spec((1, ML_V))],
        out_specs=pl.BlockSpec((t, GLA_V + ML_V), lambda b, i: (b * nt + i, 0)),
        out_shape=jax.ShapeDtypeStruct((m, GLA_V + ML_V), BF16),
        scratch_shapes=[pltpu.VMEM((t, C_SMALL), BF16),
                        pltpu.VMEM((GLA_HEADS // 2, GLA_DV, LANES), F32),
                        pltpu.VMEM((ML_HEADS // 2, ML_DV, LANES), F32),
                        pltpu.VMEM((ML_HEADS // 2, 1, LANES), F32),
                        pltpu.VMEM((ML_HEADS // 2, 1, LANES), F32),
                        pltpu.VMEM((SUBLANES, 2 * ML_QK), F32)],
        compiler_params=_params(2),
        name="hybrid_core",
    )(x, gain.reshape(1, d), w_perm, w_gate2_rows, b_gate.reshape(1, -1), gla_norm.reshape(1, -1), conv_w,
      b_small, ml_norm.reshape(1, -1))


LOG2E = 1.4426950408889634
AUG = 2 * MOBA_HD
LANES_PER_BLOCK = 4
MASK_BIG = 2.0 ** 100
PROJ_HEADS = 4
VT_ROWS = MOBA_HD + BF16_ROWS


def _split3(x):
    x = np.asarray(x, np.float32)
    hi = x.astype(BF16).astype(np.float32)
    mid = (x - hi).astype(BF16).astype(np.float32)
    lo = (x - hi - mid).astype(BF16).astype(np.float32)
    return np.stack([hi, mid, lo], axis=-1)


def _aug_tables(nb):
    c = np.asarray([LOG2E * 2.0 ** (-8.0 * (h + 1) / MOBA_HEADS) for h in range(MOBA_HEADS)], np.float32)
    pos = np.arange(MOBA_BLOCK, dtype=np.float32)
    base = LANES_PER_BLOCK * nb
    q_rows = np.zeros((MOBA_HEADS, MOBA_BLOCK, LANES), np.float32)
    k_rows = np.zeros((MOBA_HEADS, MOBA_BLOCK, LANES), np.float32)
    q_rows[:, :, base:base + 3] = _split3(-c[:, None] * pos[None, :])
    q_rows[:, :, base + 3:base + 6] = 1.0
    k_rows[:, :, base:base + 3] = 1.0
    k_rows[:, :, base + 3:base + 6] = _split3(c[:, None] * pos[None, :])
    i_idx = np.arange(nb)[:, None]
    j_idx = np.arange(nb)[None, :]
    steps = np.zeros((MOBA_HEADS, nb, nb, LANES_PER_BLOCK), np.float32)
    steps[..., 0] = np.where(j_idx == i_idx, 0.0, -MASK_BIG)
    gap = _split3(-c[:, None, None] * (MOBA_BLOCK * (i_idx - j_idx)).astype(np.float32)[None])
    steps[..., 1:] = np.where((j_idx <= i_idx)[None, :, :, None], gap, 0.0)
    q_steps = np.zeros((MOBA_HEADS, nb, 1, LANES), np.float32)
    q_steps[:, :, 0, :base] = steps.reshape(MOBA_HEADS, nb, base)
    k_steps = np.zeros((nb, 1, LANES), np.float32)
    k_steps[:, 0, :base] = np.repeat(np.eye(nb, dtype=np.float32), LANES_PER_BLOCK, axis=1)
    expand = np.zeros((nb, LANES), np.float32)
    expand[np.arange(nb), LANES_PER_BLOCK * np.arange(nb)] = MASK_BIG
    return (jnp.asarray(q_rows), jnp.asarray(q_steps), jnp.asarray(k_rows), jnp.asarray(k_steps),
            jnp.asarray(expand, BF16))


def _moba_prep_kernel(x_ref, g_ref, w_ref, qg_ref, kg_ref, qrow_ref, qstep_ref, krow_ref, kstep_ref, exp_ref,
                      qa_ref, ka_ref, vt_ref, kmean_ref, *, blocks):
    step = pl.program_id(1)
    nb = kmean_ref.shape[1]
    d = MOBA_HEADS * MOBA_HD

    @pl.when(step == 0)
    def _():
        kmean_ref[...] = jnp.zeros_like(kmean_ref)

    xn = _rms(x_ref[...], g_ref[...]).astype(BF16)
    blk = lax.broadcasted_iota(jnp.int32, (nb, MOBA_BLOCK), 0)
    ones_rows = (lax.broadcasted_iota(jnp.int32, (VT_ROWS - MOBA_HD, MOBA_BLOCK), 0) == 0).astype(F32)

    for h in range(MOBA_HEADS):
        hg = h % PROJ_HEADS
        if hg == 0:
            c0 = h * MOBA_HD
            q_grp, k_grp, v_grp = (jnp.dot(xn, w_ref[:, off + c0:off + c0 + PROJ_HEADS * MOBA_HD],
                                           preferred_element_type=F32) for off in (0, d, 2 * d))
        part = slice(hg * MOBA_HD, (hg + 1) * MOBA_HD)
        for u in range(blocks):
            i = step * blocks + u
            rows = slice(u * MOBA_BLOCK, (u + 1) * MOBA_BLOCK)
            past = blk < i
            q = _rms(q_grp[rows, part], qg_ref[...]) * (MOBA_HD ** -0.5)
            k = _rms(k_grp[rows, part], kg_ref[...])
            v = v_grp[rows, part]
            vt_ref[0, h, :, rows] = jnp.concatenate([v.T, ones_rows], axis=0).astype(vt_ref.dtype)

            gate = _dot_nt(kmean_ref[h], q)
            gate = jnp.where(past, gate, -jnp.inf)
            rank = jnp.zeros((nb, MOBA_BLOCK), F32)
            for j in range(nb):
                other = gate[j:j + 1]
                ahead = (other > gate) | ((other == gate) & (j < blk))
                rank = rank + ahead.astype(F32)
            sel = (past & (rank < MOBA_TOPK)).astype(BF16)
            q_extra = (qrow_ref[h] + qstep_ref[h, i]) + _dot_tn(sel, exp_ref[...])
            qa_ref[rows, h * AUG:h * AUG + MOBA_HD] = (q * LOG2E).astype(qa_ref.dtype)
            qa_ref[rows, h * AUG + MOBA_HD:(h + 1) * AUG] = q_extra.astype(qa_ref.dtype)
            ka_ref[rows, h * AUG:h * AUG + MOBA_HD] = k.astype(ka_ref.dtype)
            ka_ref[rows, h * AUG + MOBA_HD:(h + 1) * AUG] = (krow_ref[h] + kstep_ref[i]).astype(ka_ref.dtype)

            kmean_ref[h, pl.ds(i, 1), :] = jnp.mean(k, axis=0, keepdims=True)


PREP_BLOCKS = 4


def moba_prep(x, gain, w_qkv, layer, q_gain, k_gain, batch):
    m, dm = x.shape
    seq = m // batch
    nb = seq // MOBA_BLOCK
    assert LANES_PER_BLOCK * nb + 6 <= LANES and nb % PREP_BLOCKS == 0
    d = MOBA_HEADS * MOBA_HD
    rows = PREP_BLOCKS * MOBA_BLOCK
    ns = nb // PREP_BLOCKS
    q_rows, q_steps, k_rows, k_steps, expand = _aug_tables(nb)
    return pl.pallas_call(
        functools.partial(_moba_prep_kernel, blocks=PREP_BLOCKS),
        grid=(batch, ns),
        in_specs=[pl.BlockSpec((rows, dm), lambda b, i: (b * ns + i, 0)),
                  _const_spec((1, dm)),
                  _layer_spec((dm, 3 * d), layer),
                  _const_spec((1, MOBA_HD)),
                  _const_spec((1, MOBA_HD)),
                  _const_spec((MOBA_HEADS, MOBA_BLOCK, LANES)),
                  _const_spec((MOBA_HEADS, nb, 1, LANES)),
                  _const_spec((MOBA_HEADS, MOBA_BLOCK, LANES)),
                  _const_spec((nb, 1, LANES)),
                  _const_spec((nb, LANES))],
        out_specs=[pl.BlockSpec((rows, MOBA_HEADS * AUG), lambda b, i: (b * ns + i, 0)),
                   pl.BlockSpec((rows, MOBA_HEADS * AUG), lambda b, i: (b * ns + i, 0)),
                   pl.BlockSpec((1, MOBA_HEADS, VT_ROWS, rows), lambda b, i: (b, 0, 0, i))],
        out_shape=[jax.ShapeDtypeStruct((m, MOBA_HEADS * AUG), BF16),
                   jax.ShapeDtypeStruct((m, MOBA_HEADS * AUG), BF16),
                   jax.ShapeDtypeStruct((batch, MOBA_HEADS, VT_ROWS, seq), BF16)],
        scratch_shapes=[pltpu.VMEM((MOBA_HEADS, nb, MOBA_HD), F32)],
        compiler_params=_params(2),
        name="moba_prep",
    )(x, gain.reshape(1, dm), w_qkv, q_gain.reshape(1, -1), k_gain.reshape(1, -1),
      q_rows, q_steps, k_rows, k_steps, expand)


ATTN_HEADS_PER_STEP = 4
ATTN_CHUNK_BLOCKS = 4
PV_KEYS = 256


def _moba_attn_kernel(q_ref, k_ref, vt_ref, o_ref, s_even, s_odd, m_even, m_odd, *, heads, chunk_blocks, nb):
    t = pl.program_id(2)
    i = jnp.minimum(t, nb - 1)
    chunk_keys = chunk_blocks * MOBA_BLOCK
    n_chunks = i // chunk_blocks + 1
    n_valid2 = jnp.where(t > 0, (t - 1) // chunk_blocks + 1, 0)

    @pl.when((pl.program_id(0) == 0) & (pl.program_id(1) == 0) & (t == 0))
    def _():
        s_even[...] = jnp.zeros_like(s_even)
        s_odd[...] = jnp.zeros_like(s_odd)

    key_r = lax.broadcasted_iota(jnp.int32, (MOBA_BLOCK, MOBA_BLOCK), 0)
    qry_c = lax.broadcasted_iota(jnp.int32, (MOBA_BLOCK, MOBA_BLOCK), 1)
    causal = qry_c >= key_r
    q_aug = [q_ref[:, hh * AUG:(hh + 1) * AUG] for hh in range(heads)]

    def both(c, carry, last, s_w, s_r, m_row2, stage1, stage2):
        m_part, acc = carry
        k0 = c * chunk_keys
        live2 = jnp.full((1, MOBA_BLOCK), c, jnp.int32) < n_valid2
        new_m, new_acc = [], []
        for hh in range(heads):
            m_new, a = m_part[hh], acc[hh]
            if stage1:
                s = lax.dot_general(k_ref[pl.ds(k0, chunk_keys), hh * AUG:(hh + 1) * AUG], q_aug[hh],
                                    (((1,), (1,)), ((), ())), preferred_element_type=F32)
                if last:
                    parts = []
                    for u in range(chunk_blocks):
                        other_block = jnp.full((1, MOBA_BLOCK), c * chunk_blocks + u, jnp.int32) != i
                        parts.append(jnp.where(causal | other_block, s[u * MOBA_BLOCK:(u + 1) * MOBA_BLOCK], NEG))
                    s = jnp.concatenate(parts, axis=0)
                s_w[hh, pl.ds(k0, chunk_keys), :] = s
                m_new = jnp.maximum(m_new, jnp.max(s.reshape(chunk_keys // SUBLANES, SUBLANES, MOBA_BLOCK), axis=0))
            if stage2:
                for u in range(0, chunk_keys, PV_KEYS):
                    ku = k0 + u
                    p = jnp.exp2(s_r[hh, pl.ds(ku, PV_KEYS), :] - m_row2[hh])
                    p = jnp.where(live2, p, 0.0).astype(BF16)
                    a = a + jnp.dot(vt_ref[0, hh, :, pl.ds(ku, PV_KEYS)], p, preferred_element_type=F32)
            new_m.append(m_new)
            new_acc.append(a)
        return tuple(new_m), tuple(new_acc)

    def step(n, s_w, s_r, m_w, m_r, stage1=True, stage2=True):
        m_row2 = [m_r[hh] for hh in range(heads)]
        carry = (tuple(jnp.full((SUBLANES, MOBA_BLOCK), NEG, F32) for _ in range(heads)),
                 tuple(jnp.zeros((VT_ROWS, MOBA_BLOCK), F32) for _ in range(heads)))
        for c in range(n):
            carry = both(c, carry, c == n - 1, s_w, s_r, m_row2, stage1, stage2)
        m_part, acc = carry
        for hh in range(heads):
            if stage1:
                m_w[hh] = jnp.max(m_part[hh], axis=0, keepdims=True)
            if stage2:
                o = acc[hh][:MOBA_HD] * (1.0 / acc[hh][MOBA_HD:MOBA_HD + 1])
                o_ref[:, hh * MOBA_HD:(hh + 1) * MOBA_HD] = o.T.astype(o_ref.dtype)

    even = (s_even, s_odd, m_even, m_odd)
    odd = (s_odd, s_even, m_odd, m_even)
    n_max = nb // chunk_blocks
    pl.when(t == 0)(functools.partial(step, 1, *even, stage2=False))
    pl.when(t == nb)(functools.partial(step, n_max, *(odd if nb % 2 else even), stage1=False))
    inner = (t > 0) & (t < nb)
    for n in range(1, n_max + 1):
        pl.when(inner & (n_chunks == n) & (t % 2 == 0))(functools.partial(step, n, *even))
        pl.when(inner & (n_chunks == n) & (t % 2 == 1))(functools.partial(step, n, *odd))


def moba_attention(q_aug, k_aug, vt, batch):
    m = q_aug.shape[0]
    seq = m // batch
    nb = seq // MOBA_BLOCK
    hps = ATTN_HEADS_PER_STEP
    assert nb % ATTN_CHUNK_BLOCKS == 0
    return pl.pallas_call(
        functools.partial(_moba_attn_kernel, heads=hps, chunk_blocks=ATTN_CHUNK_BLOCKS, nb=nb),
        grid=(batch, MOBA_HEADS // hps, nb + 1),
        in_specs=[pl.BlockSpec((MOBA_BLOCK, hps * AUG), lambda b, g, t: (b * nb + jnp.minimum(t, nb - 1), g)),
                  pl.BlockSpec((seq, hps * AUG), lambda b, g, t: (b, g), pipeline_mode=pl.Buffered(1)),
                  pl.BlockSpec((1, hps, VT_ROWS, seq), lambda b, g, t: (b, g, 0, 0),
                               pipeline_mode=pl.Buffered(1))],
        out_specs=pl.BlockSpec((MOBA_BLOCK, hps * MOBA_HD), lambda b, g, t: (b * nb + jnp.maximum(t - 1, 0), g)),
        out_shape=jax.ShapeDtypeStruct((m, MOBA_HEADS * MOBA_HD), BF16),
        scratch_shapes=[pltpu.VMEM((hps, seq, MOBA_BLOCK), F32), pltpu.VMEM((hps, seq, MOBA_BLOCK), F32),
                        pltpu.VMEM((hps, 1, MOBA_BLOCK), F32), pltpu.VMEM((hps, 1, MOBA_BLOCK), F32)],
        compiler_params=_params(3),
        name="moba_attn",
    )(q_aug, k_aug, vt)


def moba_layer(x, gain, w_qkv, layer, q_gain, k_gain, batch):
    q_aug, k_aug, vt = moba_prep(x, gain, w_qkv, layer, q_gain, k_gain, batch)
    return moba_attention(q_aug, k_aug, vt, batch)


def kernel(x, norm_mix, norm_ffn, hyb_w_in, gla_w_gate2, gla_b_gate, gla_norm, ml_conv, ml_b_if, ml_norm,
           hyb_w_out, moba_w_qkv, moba_q_norm, moba_k_norm, moba_w_o, ffn_w_gate, ffn_w_up, ffn_w_down):
    batch, seq, d = x.shape
    depth = norm_mix.shape[0]
    xf = x.reshape(batch * seq, d)
    w_hyb_in = permute_hybrid_weight(hyb_w_in)
    w_hyb_out, w_moba_qkv, w_moba_o = hyb_w_out.astype(BF16), moba_w_qkv.astype(BF16), moba_w_o.astype(BF16)
    w_gate, w_up, w_down = ffn_w_gate.astype(BF16), ffn_w_up.astype(BF16), ffn_w_down.astype(BF16)
    for l in range(depth):
        j = l // 2
        if l % 2 == 0:
            y = hybrid_layer(xf, norm_mix[l], w_hyb_in, j, gla_w_gate2[j], gla_b_gate[j], gla_norm[j],
                             ml_conv[j], ml_b_if[j], ml_norm[j], batch)
            w_o = w_hyb_out
        else:
            y = moba_layer(xf, norm_mix[l], w_moba_qkv, j, moba_q_norm[j], moba_k_norm[j], batch)
            w_o = w_moba_o
        xf = out_ffn(xf, y, w_o, j, norm_ffn[l], w_gate, w_up, w_down, l)
    return xf.reshape(batch, seq, d)
```

```python
import functools

import jax
import jax.numpy as jnp
import numpy as np
from jax import lax
from jax.experimental import pallas as pl
from jax.experimental.pallas import tpu as pltpu

F32 = jnp.float32
BF16 = jnp.bfloat16

LANES = 128
SUBLANES = 8
BF16_ROWS = 2 * SUBLANES
VMEM_LIMIT_BYTES = 56 * 1024 * 1024

EPS = 1e-6
NEG = -1e30

GLA_HEADS = 4
GLA_DK = 64
GLA_DV = 128
GLA_RANK = 16
GLA_TAU = 16.0
ML_HEADS = 4
ML_DK = 64
ML_DV = 128
ML_CONV = 4
CHUNK = 64
SUB = 16
MOBA_HEADS = 8
MOBA_HD = 128
MOBA_BLOCK = 256
MOBA_TOPK = 3

GLA_QK = GLA_HEADS * GLA_DK
GLA_V = GLA_HEADS * GLA_DV
ML_QK = ML_HEADS * ML_DK
ML_V = ML_HEADS * ML_DV

C_GQ = 0
C_GK = C_GQ + GLA_QK
C_GV = C_GK + GLA_QK
C_GR = C_GV + GLA_V
C_MQ = C_GR + GLA_V
C_MK = C_MQ + ML_QK
C_MV = C_MK + ML_QK
C_MO = C_MV + ML_V
C_SMALL = C_MO + ML_V
S_GLR = 0
S_MI = GLA_RANK
S_MF = GLA_RANK + ML_HEADS

ROW_TILE = 1024
MIX_TILE = 256
FF_TILE = 256
PROJ_COL_TILE = 512


def _rms(x, gain):
    return x * lax.rsqrt(jnp.mean(x * x, axis=-1, keepdims=True) + EPS) * gain


def _log_sigmoid(z):
    return jnp.minimum(z, 0.0) - jnp.log1p(jnp.exp(-jnp.abs(z)))


def _sigmoid(z):
    return 1.0 / (1.0 + jnp.exp(-z))


def _dot(a, b):
    return jnp.dot(a.astype(BF16), b.astype(BF16), preferred_element_type=F32)


def _dot_nt(a, b):
    return lax.dot_general(a.astype(BF16), b.astype(BF16), (((1,), (1,)), ((), ())),
                           preferred_element_type=F32)


def _dot_tn(a, b):
    return lax.dot_general(a.astype(BF16), b.astype(BF16), (((0,), (0,)), ((), ())),
                           preferred_element_type=F32)


def _const_spec(shape):
    return pl.BlockSpec(shape, lambda *_: (0,) * len(shape), pipeline_mode=pl.Buffered(1))


def _layer_spec(shape, layer):
    return pl.BlockSpec((None,) + tuple(shape), lambda *_: (layer,) + (0,) * len(shape),
                        pipeline_mode=pl.Buffered(1))


def _params(n_grid):
    return pltpu.CompilerParams(dimension_semantics=("arbitrary",) * n_grid,
                                vmem_limit_bytes=VMEM_LIMIT_BYTES)


def _out_ffn_kernel(x_ref, y_ref, wo_ref, g_ref, wg_ref, wu_ref, wd_ref, o_ref, h_ref, *, ff_tile):
    x1 = x_ref[...] + jnp.dot(y_ref[...], wo_ref[...], preferred_element_type=F32)
    xn = _rms(x1, g_ref[...]).astype(BF16)
    d_ff = wg_ref.shape[1]
    for c0 in range(0, d_ff, ff_tile):
        g = jnp.dot(xn, wg_ref[:, c0:c0 + ff_tile], preferred_element_type=F32)
        u = jnp.dot(xn, wu_ref[:, c0:c0 + ff_tile], preferred_element_type=F32)
        h_ref[:, c0:c0 + ff_tile] = (g * _sigmoid(g) * u).astype(BF16)
    o_ref[...] = x1 + jnp.dot(h_ref[...], wd_ref[...], preferred_element_type=F32)


def out_ffn(x, y, w_o, o_layer, gain, w_gate, w_up, w_down, layer):
    m, d = x.shape
    dy = y.shape[1]
    d_ff = w_gate.shape[2]
    tm = min(ROW_TILE, m)
    return pl.pallas_call(
        functools.partial(_out_ffn_kernel, ff_tile=FF_TILE),
        grid=(m // tm,),
        in_specs=[pl.BlockSpec((tm, d), lambda i: (i, 0)),
                  pl.BlockSpec((tm, dy), lambda i: (i, 0)),
                  _layer_spec((dy, d), o_layer),
                  _const_spec((1, d)),
                  _layer_spec((d, d_ff), layer),
                  _layer_spec((d, d_ff), layer),
                  _layer_spec((d_ff, d), layer)],
        out_specs=pl.BlockSpec((tm, d), lambda i: (i, 0)),
        out_shape=jax.ShapeDtypeStruct((m, d), F32),
        scratch_shapes=[pltpu.VMEM((tm, d_ff), BF16)],
        compiler_params=_params(1),
        name="out_ffn",
    )(x, y, w_o, gain.reshape(1, d), w_gate, w_up, w_down)


def _split3_rows(x):
    hi = x.astype(BF16)
    r1 = x - hi.astype(F32)
    mid = r1.astype(BF16)
    lo = (r1 - mid.astype(F32)).astype(BF16)
    return jnp.concatenate([hi, mid, lo], axis=0)


def _blocks_diag(tiles, zero):
    n = len(tiles)
    return jnp.concatenate(
        [jnp.concatenate([tiles[a] if a == b else zero for b in range(n)], axis=1) for a in range(n)], axis=0)


def _hybrid_kernel(x_ref, g_ref, wa_ref, wb_ref, ws_ref, wg2_ref, bg_ref, gn_ref, conv_ref, bsmall_ref, mn_ref,
                   y_ref, proj_ref, gla_st, ml_cs, ml_ns, ml_m, conv_carry):
    t_tile = proj_ref.shape[0]
    nc = t_tile // CHUNK
    n_sub = CHUNK // SUB

    @pl.when(pl.program_id(1) == 0)
    def _():
        gla_st[...] = jnp.zeros_like(gla_st)
        ml_cs[...] = jnp.zeros_like(ml_cs)
        ml_ns[...] = jnp.zeros_like(ml_ns)
        ml_m[...] = jnp.zeros_like(ml_m)
        conv_carry[...] = jnp.zeros_like(conv_carry)

    xn = _rms(x_ref[...], g_ref[...]).astype(BF16)
    for w_part, base in ((wa_ref, 0), (wb_ref, C_MQ)):
        for c0 in range(0, C_MQ, PROJ_COL_TILE):
            proj_ref[:, base + c0:base + c0 + PROJ_COL_TILE] = jnp.dot(
                xn, w_part[:, c0:c0 + PROJ_COL_TILE], preferred_element_type=F32).astype(proj_ref.dtype)
    small = jnp.dot(xn, ws_ref[...], preferred_element_type=F32)

    xcat = proj_ref[:, C_MQ:C_MQ + 2 * ML_QK].astype(F32)
    ext = jnp.concatenate([conv_carry[...], xcat], axis=0)
    cw = conv_ref[...]
    acc = ext * cw[ML_CONV - 1:ML_CONV]
    for j in range(1, ML_CONV):
        acc = acc + pltpu.roll(ext, j, axis=0) * cw[ML_CONV - 1 - j:ML_CONV - j]
    conv = acc[SUBLANES:]
    qk = conv * _sigmoid(conv)
    conv_carry[...] = xcat[t_tile - SUBLANES:]

    lane = lax.broadcasted_iota(jnp.int32, (1, LANES), 1)
    head_mask = (lane < GLA_DK, lane >= GLA_DK)
    hm0 = head_mask[0]
    r_t = lax.broadcasted_iota(jnp.int32, (t_tile, t_tile), 0)
    c_t = lax.broadcasted_iota(jnp.int32, (t_tile, t_tile), 1)
    tril_bd = ((r_t >= c_t) & (r_t // CHUNK == c_t // CHUNK)).astype(BF16)
    tril3 = jnp.concatenate([tril_bd, tril_bd, tril_bd], axis=1)
    col_w = lax.broadcasted_iota(jnp.int32, (t_tile, 2 * nc * LANES), 1)
    row_w = lax.broadcasted_iota(jnp.int32, (t_tile, 2 * nc * LANES), 0)
    to_rows = ((col_w % CHUNK == row_w % CHUNK) & ((col_w % (nc * LANES)) // LANES == row_w // CHUNK)
               & ((col_w % LANES) // CHUNK == col_w // (nc * LANES))).astype(BF16)
    to_rows3 = jnp.concatenate([to_rows, to_rows, to_rows], axis=0)
    row_c = lax.broadcasted_iota(jnp.int32, (CHUNK, LANES), 0)
    causal2 = (lax.broadcasted_iota(jnp.int32, (CHUNK, LANES), 1) % CHUNK) <= row_c
    sub_id = (lax.broadcasted_iota(jnp.int32, (t_tile, LANES), 0) % CHUNK) // SUB
    row_sub = lax.broadcasted_iota(jnp.int32, (BF16_ROWS, LANES), 0)
    zero_tile = jnp.zeros((CHUNK, LANES), F32)
    zero_v = jnp.zeros((CHUNK, LANES), BF16)

    def chunk_rows(x, c):
        return x[c * CHUNK:(c + 1) * CHUNK]

    def diag_blocks(r):
        return [r[c * CHUNK:(c + 1) * CHUNK, c * LANES:(c + 1) * LANES] for c in range(nc)]

    smallb = small + bsmall_ref[...]

    z = _dot(small, wg2_ref[...]) + bg_ref[...]
    log_a = _log_sigmoid(z) * (1.0 / GLA_TAU)
    cum = jnp.dot(tril3, _split3_rows(jnp.concatenate([log_a, _log_sigmoid(smallb)], axis=1)),
                  preferred_element_type=F32)
    b_cols = smallb - pltpu.roll(cum[:, GLA_QK:], LANES - ML_HEADS, axis=1)
    b_rows = lax.dot_general(_split3_rows(b_cols), to_rows3, (((0,), (0,)), ((), ())), preferred_element_type=F32)

    for p in range(GLA_HEADS // 2):
        q2 = proj_ref[:, C_GQ + p * LANES:C_GQ + (p + 1) * LANES].astype(F32) * (GLA_DK ** -0.5)
        k2 = proj_ref[:, C_GK + p * LANES:C_GK + (p + 1) * LANES].astype(F32)
        bc = cum[:, p * LANES:(p + 1) * LANES]
        v = [proj_ref[:, C_GV + (2 * p + hl) * GLA_DV:C_GV + (2 * p + hl + 1) * GLA_DV] for hl in range(2)]

        ref_pt = [[chunk_rows(bc, c)[i * SUB - 1:i * SUB] if i else jnp.zeros((1, LANES), F32)
                   for i in range(n_sub)] for c in range(nc)]
        r_own = jnp.concatenate([jnp.broadcast_to(ref_pt[c][i], (SUB, LANES))
                                 for c in range(nc) for i in range(n_sub)], axis=0)
        qs = q2 * jnp.exp(bc - r_own)
        q_hat = jnp.concatenate([jnp.where(sub_id == i, qs, 0.0) for i in range(n_sub)], axis=1)
        k_hat, k_dec, last = [], [], []
        for c in range(nc):
            kc, bcc = chunk_rows(k2, c), chunk_rows(bc, c)
            ke = [kc * jnp.exp(ref_pt[c][i] - bcc) for i in range(n_sub)]
            for hl in range(2):
                k_hat.append(jnp.concatenate([jnp.where(head_mask[hl], ke[i], 0.0) for i in range(n_sub)], axis=1))
            last.append(bcc[CHUNK - 1:CHUNK])
            k_dec.append(kc * jnp.exp(last[c] - bcc))
        k_hat = jnp.concatenate(k_hat, axis=0)
        sc = diag_blocks(_dot_nt(q_hat, k_hat))
        sc = [jnp.where(causal2, s, 0.0) for s in sc]
        v_bd = jnp.concatenate([jnp.concatenate([chunk_rows(v[hl], c) if hl == b else zero_v for b in range(2)],
                                                axis=1) for c in range(nc) for hl in range(2)], axis=0)
        o_intra = _dot(_blocks_diag(sc, zero_tile), v_bd)
        u_all = _dot_tn(jnp.concatenate(v, axis=1), _blocks_diag(k_dec, zero_tile))

        st = gla_st[p]
        before = []
        for c in range(nc):
            before.append(st)
            u = jnp.where(hm0, u_all[:GLA_DV, c * LANES:(c + 1) * LANES], u_all[GLA_DV:, c * LANES:(c + 1) * LANES])
            st = st * jnp.exp(last[c]) + u
        gla_st[p] = st
        st_stack = jnp.concatenate(before, axis=0)
        q_in = q2 * jnp.exp(bc)
        for hl in range(2):
            h = 2 * p + hl
            o_inter = jnp.concatenate(diag_blocks(_dot_nt(jnp.where(head_mask[hl], q_in, 0.0), st_stack)), axis=0)
            o = o_intra[:, hl * GLA_DV:(hl + 1) * GLA_DV] + o_inter
            gate = proj_ref[:, C_GR + h * GLA_DV:C_GR + (h + 1) * GLA_DV].astype(F32)
            yg = _rms(o, gn_ref[:, h * GLA_DV:(h + 1) * GLA_DV]) * (gate * _sigmoid(gate))
            y_ref[:, h * GLA_DV:(h + 1) * GLA_DV] = yg.astype(y_ref.dtype)

    ones_tile = jnp.ones((t_tile, LANES), BF16)
    for p in range(ML_HEADS // 2):
        mq2 = qk[:, p * LANES:(p + 1) * LANES]
        mk2 = qk[:, ML_QK + p * LANES:ML_QK + (p + 1) * LANES] * (ML_DK ** -0.5)
        v = [proj_ref[:, C_MV + (2 * p + hl) * ML_DV:C_MV + (2 * p + hl + 1) * ML_DV] for hl in range(2)]
        k_stack = jnp.concatenate([jnp.where(head_mask[hl], chunk_rows(mk2, c), 0.0)
                                   for c in range(nc) for hl in range(2)], axis=0)
        qk_blocks = diag_blocks(_dot_nt(mq2, k_stack))

        w_blocks, wk_blocks, cm_cols, b_max, bc_last = [], [], [], [], []
        for c in range(nc):
            blk = slice(c * LANES, (c + 1) * LANES)
            b_pair = (b_rows[S_MI + 2 * p:S_MI + 2 * p + 1, blk]
                      + b_rows[S_MI + 2 * p + 1:S_MI + 2 * p + 2, nc * LANES + c * LANES:nc * LANES + (c + 1) * LANES])
            f_last = [chunk_rows(cum, c)[CHUNK - 1:CHUNK, GLA_QK + S_MF + 2 * p + hl:GLA_QK + S_MF + 2 * p + hl + 1]
                      for hl in range(2)]
            bm = jnp.where(causal2, b_pair, -jnp.inf)
            cm = [jnp.max(jnp.where(head_mask[hl], bm, -jnp.inf), axis=1, keepdims=True) for hl in range(2)]
            w_blocks.append(qk_blocks[c] * jnp.exp(bm - jnp.where(hm0, cm[0], cm[1])))
            cm_cols.append(cm)
            bmx = [jnp.max(jnp.where(head_mask[hl], b_pair, -jnp.inf), axis=1, keepdims=True) for hl in range(2)]
            b_max.append(jnp.where(hm0, bmx[0], bmx[1]))
            bc_last.append(jnp.where(hm0, f_last[0], f_last[1]))
            e_col = []
            for hl in range(2):
                h = 2 * p + hl
                b_col = (chunk_rows(smallb, c)[:, S_MI + h:S_MI + h + 1]
                         - chunk_rows(cum, c)[:, GLA_QK + S_MF + h:GLA_QK + S_MF + h + 1])
                e_col.append(jnp.exp(b_col - bmx[hl]))
            wk_blocks.append(chunk_rows(mk2, c) * jnp.where(hm0, e_col[0], e_col[1]))

        ones_col = [jnp.broadcast_to((lane == hl).astype(BF16), (CHUNK, LANES)) for hl in range(2)]
        v_bd = jnp.concatenate([jnp.concatenate([chunk_rows(v[hl], c) if hl == b else zero_v for b in range(2)]
                                                + [ones_col[hl]], axis=1)
                                for c in range(nc) for hl in range(2)], axis=0)
        nv = _dot(_blocks_diag(w_blocks, zero_tile), v_bd)
        u_all = _dot_tn(jnp.concatenate(v + [ones_tile], axis=1), _blocks_diag(wk_blocks, zero_tile))

        cs, ns, m_run = ml_cs[p], ml_ns[p], ml_m[p]
        before = []
        for c in range(nc):
            before.append((cs, ns, m_run))
            m_top = jnp.maximum(m_run, b_max[c])
            keep = jnp.exp(m_run - m_top)
            gain = jnp.exp(b_max[c] - m_top)
            blk = slice(c * LANES, (c + 1) * LANES)
            cs = keep * cs + gain * jnp.where(hm0, u_all[:ML_DV, blk], u_all[ML_DV:2 * ML_DV, blk])
            ns = keep * ns + gain * u_all[2 * ML_DV:2 * ML_DV + 1, blk]
            m_run = bc_last[c] + m_top
        ml_cs[p], ml_ns[p], ml_m[p] = cs, ns, m_run
        ns_rows = jnp.zeros((BF16_ROWS, LANES), F32)
        for c in range(nc):
            ns_rows = jnp.where(row_sub == c, before[c][1], ns_rows)
        state_stack = jnp.concatenate([b[0] for b in before] + [ns_rows], axis=0)

        for hl in range(2):
            h = 2 * p + hl
            rr = _dot_nt(jnp.where(head_mask[hl], mq2, 0.0), state_stack)
            q_cs = diag_blocks(rr)
            h_chunks = []
            for c in range(nc):
                rows = slice(c * CHUNK, (c + 1) * CHUNK)
                q_ns = rr[rows, nc * ML_DV + c:nc * ML_DV + c + 1]
                m_prev = before[c][2][:, hl * ML_DK:hl * ML_DK + 1]
                cm = cm_cols[c][hl]
                m_top = jnp.maximum(m_prev, cm)
                local = jnp.exp(cm - m_top)
                carried = jnp.exp(m_prev - m_top)
                num = local * nv[rows, hl * ML_DV:(hl + 1) * ML_DV] + carried * q_cs[c]
                den = local * nv[rows, 2 * ML_DV + hl:2 * ML_DV + hl + 1] + carried * q_ns
                bc_col = chunk_rows(cum, c)[:, GLA_QK + S_MF + h:GLA_QK + S_MF + h + 1]
                h_chunks.append(num / jnp.maximum(jnp.abs(den), jnp.exp(-(bc_col + m_top))))
            og = proj_ref[:, C_MO + h * ML_DV:C_MO + (h + 1) * ML_DV].astype(F32)
            ym = _rms(jnp.concatenate(h_chunks, axis=0) * _sigmoid(og), mn_ref[:, h * ML_DV:(h + 1) * ML_DV])
            y_ref[:, GLA_V + h * ML_DV:GLA_V + (h + 1) * ML_DV] = ym.astype(y_ref.dtype)


def split_hybrid_weight(w_in):
    assert C_MQ == 2 * ML_QK + 2 * ML_V
    ml0 = C_MQ + GLA_RANK
    pad = jnp.zeros(w_in.shape[:2] + (LANES - GLA_RANK - 2 * ML_HEADS,), w_in.dtype)
    small = jnp.concatenate([w_in[:, :, C_MQ:ml0], w_in[:, :, ml0 + C_MQ:], pad], axis=2)
    return w_in[:, :, :C_MQ].astype(BF16), w_in[:, :, ml0:ml0 + C_MQ].astype(BF16), small.astype(BF16)


def hybrid_layer(x, gain, w_parts, layer, w_gate2, b_gate, gla_norm, conv_w, b_if, ml_norm, batch):
    m, d = x.shape
    seq = m // batch
    t = min(MIX_TILE, seq)
    nt = seq // t
    assert BF16_ROWS >= t // CHUNK
    b_small = jnp.zeros((1, LANES), F32).at[0, S_MI:S_MI + 2 * ML_HEADS].set(b_if)
    w_gate2_rows = jnp.zeros((LANES, GLA_QK), F32).at[S_GLR:S_GLR + GLA_RANK].set(w_gate2)
    return pl.pallas_call(
        _hybrid_kernel,
        grid=(batch, nt),
        in_specs=[pl.BlockSpec((t, d), lambda b, i: (b * nt + i, 0)),
                  _const_spec((1, d)),
                  _layer_spec((d, C_MQ), layer),
                  _layer_spec((d, C_MQ), layer),
                  _layer_spec((d, LANES), layer),
                  _const_spec((LANES, GLA_QK)),
                  _const_spec((1, GLA_QK)),
                  _const_spec((1, GLA_V)),
                  _const_spec((ML_CONV, 2 * ML_QK)),
                  _const_spec((1, LANES)),
                  _const_spec((1, ML_V))],
        out_specs=pl.BlockSpec((t, GLA_V + ML_V), lambda b, i: (b * nt + i, 0)),
        out_shape=jax.ShapeDtypeStruct((m, GLA_V + ML_V), BF16),
        scratch_shapes=[pltpu.VMEM((t, C_SMALL), BF16),
                        pltpu.VMEM((GLA_HEADS // 2, GLA_DV, LANES), F32),
                        pltpu.VMEM((ML_HEADS // 2, ML_DV, LANES), F32),
                        pltpu.VMEM((ML_HEADS // 2, 1, LANES), F32),
                        pltpu.VMEM((ML_HEADS // 2, 1, LANES), F32),
                        pltpu.VMEM((SUBLANES, 2 * ML_QK), F32)],
        compiler_params=_params(2),
        name="hybrid_core",
    )(x, gain.reshape(1, d), *w_parts, w_gate2_rows, b_gate.reshape(1, -1), gla_norm.reshape(1, -1), conv_w,
      b_small, ml_norm.reshape(1, -1))


LOG2E = 1.4426950408889634
AUG = 2 * MOBA_HD
LANES_PER_BLOCK = 4
MASK_BIG = 2.0 ** 100
PROJ_HEADS = 4
VT_ROWS = MOBA_HD + BF16_ROWS


def _split3(x):
    x = np.asarray(x, np.float32)
    hi = x.astype(BF16).astype(np.float32)
    mid = (x - hi).astype(BF16).astype(np.float32)
    lo = (x - hi - mid).astype(BF16).astype(np.float32)
    return np.stack([hi, mid, lo], axis=-1)


def _aug_tables(nb):
    c = np.asarray([LOG2E * 2.0 ** (-8.0 * (h + 1) / MOBA_HEADS) for h in range(MOBA_HEADS)], np.float32)
    pos = np.arange(MOBA_BLOCK, dtype=np.float32)
    base = LANES_PER_BLOCK * nb
    q_rows = np.zeros((MOBA_HEADS, MOBA_BLOCK, LANES), np.float32)
    k_rows = np.zeros((MOBA_HEADS, MOBA_BLOCK, LANES), np.float32)
    q_rows[:, :, base:base + 3] = _split3(-c[:, None] * pos[None, :])
    q_rows[:, :, base + 3:base + 6] = 1.0
    k_rows[:, :, base:base + 3] = 1.0
    k_rows[:, :, base + 3:base + 6] = _split3(c[:, None] * pos[None, :])
    i_idx = np.arange(nb)[:, None]
    j_idx = np.arange(nb)[None, :]
    steps = np.zeros((MOBA_HEADS, nb, nb, LANES_PER_BLOCK), np.float32)
    steps[..., 0] = np.where(j_idx == i_idx, 0.0, -MASK_BIG)
    gap = _split3(-c[:, None, None] * (MOBA_BLOCK * (i_idx - j_idx)).astype(np.float32)[None])
    steps[..., 1:] = np.where((j_idx <= i_idx)[None, :, :, None], gap, 0.0)
    q_steps = np.zeros((MOBA_HEADS, nb, 1, LANES), np.float32)
    q_steps[:, :, 0, :base] = steps.reshape(MOBA_HEADS, nb, base)
    k_steps = np.zeros((nb, 1, LANES), np.float32)
    k_steps[:, 0, :base] = np.repeat(np.eye(nb, dtype=np.float32), LANES_PER_BLOCK, axis=1)
    expand = np.zeros((nb, LANES), np.float32)
    expand[np.arange(nb), LANES_PER_BLOCK * np.arange(nb)] = MASK_BIG
    return (jnp.asarray(q_rows), jnp.asarray(q_steps), jnp.asarray(k_rows), jnp.asarray(k_steps),
            jnp.asarray(expand, BF16))


def _moba_prep_kernel(x_ref, g_ref, w_ref, qg_ref, kg_ref, qrow_ref, qstep_ref, krow_ref, kstep_ref, exp_ref,
                      qa_ref, ka_ref, vt_ref, kmean_ref, *, blocks):
    step = pl.program_id(1)
    nb = kmean_ref.shape[1]
    d = MOBA_HEADS * MOBA_HD

    @pl.when(step == 0)
    def _():
        kmean_ref[...] = jnp.zeros_like(kmean_ref)

    xn = _rms(x_ref[...], g_ref[...]).astype(BF16)
    blk = lax.broadcasted_iota(jnp.int32, (nb, MOBA_BLOCK), 0)
    ones_rows = (lax.broadcasted_iota(jnp.int32, (VT_ROWS - MOBA_HD, MOBA_BLOCK), 0) == 0).astype(F32)

    for h in range(MOBA_HEADS):
        hg = h % PROJ_HEADS
        if hg == 0:
            c0 = h * MOBA_HD
            q_grp, k_grp, v_grp = (jnp.dot(xn, w_ref[:, off + c0:off + c0 + PROJ_HEADS * MOBA_HD],
                                           preferred_element_type=F32) for off in (0, d, 2 * d))
        part = slice(hg * MOBA_HD, (hg + 1) * MOBA_HD)
        for u in range(blocks):
            i = step * blocks + u
            rows = slice(u * MOBA_BLOCK, (u + 1) * MOBA_BLOCK)
            past = blk < i
            q = _rms(q_grp[rows, part], qg_ref[...]) * (MOBA_HD ** -0.5)
            k = _rms(k_grp[rows, part], kg_ref[...])
            v = v_grp[rows, part]
            vt_ref[0, h, :, rows] = jnp.concatenate([v.T, ones_rows], axis=0).astype(vt_ref.dtype)

            gate = _dot_nt(kmean_ref[h], q)
            gate = jnp.where(past, gate, -jnp.inf)
            rank = jnp.zeros((nb, MOBA_BLOCK), F32)
            for j in range(nb):
                other = gate[j:j + 1]
                ahead = (other > gate) | ((other == gate) & (j < blk))
                rank = rank + ahead.astype(F32)
            sel = (past & (rank < MOBA_TOPK)).astype(BF16)
            q_extra = (qrow_ref[h] + qstep_ref[h, i]) + _dot_tn(sel, exp_ref[...])
            qa_ref[rows, h * AUG:h * AUG + MOBA_HD] = (q * LOG2E).astype(qa_ref.dtype)
            qa_ref[rows, h * AUG + MOBA_HD:(h + 1) * AUG] = q_extra.astype(qa_ref.dtype)
            ka_ref[rows, h * AUG:h * AUG + MOBA_HD] = k.astype(ka_ref.dtype)
            ka_ref[rows, h * AUG + MOBA_HD:(h + 1) * AUG] = (krow_ref[h] + kstep_ref[i]).astype(ka_ref.dtype)

            kmean_ref[h, pl.ds(i, 1), :] = jnp.mean(k, axis=0, keepdims=True)


PREP_BLOCKS = 4


def moba_prep(x, gain, w_qkv, layer, q_gain, k_gain, batch):
    m, dm = x.shape
    seq = m // batch
    nb = seq // MOBA_BLOCK
    assert LANES_PER_BLOCK * nb + 6 <= LANES and nb % PREP_BLOCKS == 0
    d = MOBA_HEADS * MOBA_HD
    rows = PREP_BLOCKS * MOBA_BLOCK
    ns = nb // PREP_BLOCKS
    q_rows, q_steps, k_rows, k_steps, expand = _aug_tables(nb)
    return pl.pallas_call(
        functools.partial(_moba_prep_kernel, blocks=PREP_BLOCKS),
        grid=(batch, ns),
        in_specs=[pl.BlockSpec((rows, dm), lambda b, i: (b * ns + i, 0)),
                  _const_spec((1, dm)),
                  _layer_spec((dm, 3 * d), layer),
                  _const_spec((1, MOBA_HD)),
                  _const_spec((1, MOBA_HD)),
                  _const_spec((MOBA_HEADS, MOBA_BLOCK, LANES)),
                  _const_spec((MOBA_HEADS, nb, 1, LANES)),
                  _const_spec((MOBA_HEADS, MOBA_BLOCK, LANES)),
                  _const_spec((nb, 1, LANES)),
                  _const_spec((nb, LANES))],
        out_specs=[pl.BlockSpec((rows, MOBA_HEADS * AUG), lambda b, i: (b * ns + i, 0)),
                   pl.BlockSpec((rows, MOBA_HEADS * AUG), lambda b, i: (b * ns + i, 0)),
                   pl.BlockSpec((1, MOBA_HEADS, VT_ROWS, rows), lambda b, i: (b, 0, 0, i))],
        out_shape=[jax.ShapeDtypeStruct((m, MOBA_HEADS * AUG), BF16),
                   jax.ShapeDtypeStruct((m, MOBA_HEADS * AUG), BF16),
                   jax.ShapeDtypeStruct((batch, MOBA_HEADS, VT_ROWS, seq), BF16)],
        scratch_shapes=[pltpu.VMEM((MOBA_HEADS, nb, MOBA_HD), F32)],
        compiler_params=_params(2),
        name="moba_prep",
    )(x, gain.reshape(1, dm), w_qkv, q_gain.reshape(1, -1), k_gain.reshape(1, -1),
      q_rows, q_steps, k_rows, k_steps, expand)


ATTN_HEADS_PER_STEP = 4
ATTN_CHUNK_BLOCKS = 4
PV_KEYS = 256


def _moba_attn_kernel(q_ref, k_ref, vt_ref, o_ref, s_even, s_odd, m_even, m_odd, *, heads, chunk_blocks, nb):
    t = pl.program_id(2)
    i = jnp.minimum(t, nb - 1)
    chunk_keys = chunk_blocks * MOBA_BLOCK
    n_chunks = i // chunk_blocks + 1
    n_valid2 = jnp.where(t > 0, (t - 1) // chunk_blocks + 1, 0)

    @pl.when((pl.program_id(0) == 0) & (pl.program_id(1) == 0) & (t == 0))
    def _():
        s_even[...] = jnp.zeros_like(s_even)
        s_odd[...] = jnp.zeros_like(s_odd)

    key_r = lax.broadcasted_iota(jnp.int32, (MOBA_BLOCK, MOBA_BLOCK), 0)
    qry_c = lax.broadcasted_iota(jnp.int32, (MOBA_BLOCK, MOBA_BLOCK), 1)
    causal = qry_c >= key_r
    q_aug = [q_ref[:, hh * AUG:(hh + 1) * AUG] for hh in range(heads)]

    def both(c, carry, last, s_w, s_r, m_row2, stage1, stage2):
        m_part, acc = carry
        k0 = c * chunk_keys
        live2 = jnp.full((1, MOBA_BLOCK), c, jnp.int32) < n_valid2
        new_m, new_acc = [], []
        for hh in range(heads):
            m_new, a = m_part[hh], acc[hh]
            if stage1:
                s = lax.dot_general(k_ref[pl.ds(k0, chunk_keys), hh * AUG:(hh + 1) * AUG], q_aug[hh],
                                    (((1,), (1,)), ((), ())), preferred_element_type=F32)
                if last:
                    parts = []
                    for u in range(chunk_blocks):
                        other_block = jnp.full((1, MOBA_BLOCK), c * chunk_blocks + u, jnp.int32) != i
                        parts.append(jnp.where(causal | other_block, s[u * MOBA_BLOCK:(u + 1) * MOBA_BLOCK], NEG))
                    s = jnp.concatenate(parts, axis=0)
                s_w[hh, pl.ds(k0, chunk_keys), :] = s
                m_new = jnp.maximum(m_new, jnp.max(s.reshape(chunk_keys // SUBLANES, SUBLANES, MOBA_BLOCK), axis=0))
            if stage2:
                for u in range(0, chunk_keys, PV_KEYS):
                    ku = k0 + u
                    p = jnp.exp2(s_r[hh, pl.ds(ku, PV_KEYS), :] - m_row2[hh])
                    p = jnp.where(live2, p, 0.0).astype(BF16)
                    a = a + jnp.dot(vt_ref[0, hh, :, pl.ds(ku, PV_KEYS)], p, preferred_element_type=F32)
            new_m.append(m_new)
            new_acc.append(a)
        return tuple(new_m), tuple(new_acc)

    def step(n, s_w, s_r, m_w, m_r, stage1=True, stage2=True):
        m_row2 = [m_r[hh] for hh in range(heads)]
        carry = (tuple(jnp.full((SUBLANES, MOBA_BLOCK), NEG, F32) for _ in range(heads)),
                 tuple(jnp.zeros((VT_ROWS, MOBA_BLOCK), F32) for _ in range(heads)))
        for c in range(n):
            carry = both(c, carry, c == n - 1, s_w, s_r, m_row2, stage1, stage2)
        m_part, acc = carry
        for hh in range(heads):
            if stage1:
                m_w[hh] = jnp.max(m_part[hh], axis=0, keepdims=True)
            if stage2:
                o = acc[hh][:MOBA_HD] * (1.0 / acc[hh][MOBA_HD:MOBA_HD + 1])
                o_ref[:, hh * MOBA_HD:(hh + 1) * MOBA_HD] = o.T.astype(o_ref.dtype)

    even = (s_even, s_odd, m_even, m_odd)
    odd = (s_odd, s_even, m_odd, m_even)
    n_max = nb // chunk_blocks
    pl.when(t == 0)(functools.partial(step, 1, *even, stage2=False))
    pl.when(t == nb)(functools.partial(step, n_max, *(odd if nb % 2 else even), stage1=False))
    inner = (t > 0) & (t < nb)
    for n in range(1, n_max + 1):
        pl.when(inner & (n_chunks == n) & (t % 2 == 0))(functools.partial(step, n, *even))
        pl.when(inner & (n_chunks == n) & (t % 2 == 1))(functools.partial(step, n, *odd))


def moba_attention(q_aug, k_aug, vt, batch):
    m = q_aug.shape[0]
    seq = m // batch
    nb = seq // MOBA_BLOCK
    hps = ATTN_HEADS_PER_STEP
    assert nb % ATTN_CHUNK_BLOCKS == 0
    return pl.pallas_call(
        functools.partial(_moba_attn_kernel, heads=hps, chunk_blocks=ATTN_CHUNK_BLOCKS, nb=nb),
        grid=(batch, MOBA_HEADS // hps, nb + 1),
        in_specs=[pl.BlockSpec((MOBA_BLOCK, hps * AUG), lambda b, g, t: (b * nb + jnp.minimum(t, nb - 1), g)),
                  pl.BlockSpec((seq, hps * AUG), lambda b, g, t: (b, g), pipeline_mode=pl.Buffered(1)),
                  pl.BlockSpec((1, hps, VT_ROWS, seq), lambda b, g, t: (b, g, 0, 0),
                               pipeline_mode=pl.Buffered(1))],
        out_specs=pl.BlockSpec((MOBA_BLOCK, hps * MOBA_HD), lambda b, g, t: (b * nb + jnp.maximum(t - 1, 0), g)),
        out_shape=jax.ShapeDtypeStruct((m, MOBA_HEADS * MOBA_HD), BF16),
        scratch_shapes=[pltpu.VMEM((hps, seq, MOBA_BLOCK), F32), pltpu.VMEM((hps, seq, MOBA_BLOCK), F32),
                        pltpu.VMEM((hps, 1, MOBA_BLOCK), F32), pltpu.VMEM((hps, 1, MOBA_BLOCK), F32)],
        compiler_params=_params(3),
        name="moba_attn",
    )(q_aug, k_aug, vt)


def moba_layer(x, gain, w_qkv, layer, q_gain, k_gain, batch):
    q_aug, k_aug, vt = moba_prep(x, gain, w_qkv, layer, q_gain, k_gain, batch)
    return moba_attention(q_aug, k_aug, vt, batch)


def kernel(x, norm_mix, norm_ffn, hyb_w_in, gla_w_gate2, gla_b_gate, gla_norm, ml_conv, ml_b_if, ml_norm,
           hyb_w_out, moba_w_qkv, moba_q_norm, moba_k_norm, moba_w_o, ffn_w_gate, ffn_w_up, ffn_w_down):
    batch, seq, d = x.shape
    depth = norm_mix.shape[0]
    xf = x.reshape(batch * seq, d)
    w_hyb_in = split_hybrid_weight(hyb_w_in)
    w_hyb_out, w_moba_qkv, w_moba_o = hyb_w_out.astype(BF16), moba_w_qkv.astype(BF16), moba_w_o.astype(BF16)
    w_gate, w_up, w_down = ffn_w_gate.astype(BF16), ffn_w_up.astype(BF16), ffn_w_down.astype(BF16)
    for l in range(depth):
        j = l // 2
        if l % 2 == 0:
            y = hybrid_layer(xf, norm_mix[l], w_hyb_in, j, gla_w_gate2[j], gla_b_gate[j], gla_norm[j],
                             ml_conv[j], ml_b_if[j], ml_norm[j], batch)
            w_o = w_hyb_out
        else:
            y = moba_layer(xf, norm_mix[l], w_moba_qkv, j, moba_q_norm[j], moba_k_norm[j], batch)
            w_o = w_moba_o
        xf = out_ffn(xf, y, w_o, j, norm_ffn[l], w_gate, w_up, w_down, l)
    return xf.reshape(batch, seq, d)
```

```python
import functools

import jax
import jax.numpy as jnp
import numpy as np
from jax import lax
from jax.experimental import pallas as pl
from jax.experimental.pallas import tpu as pltpu

F32 = jnp.float32
BF16 = jnp.bfloat16

LANES = 128
SUBLANES = 8
BF16_ROWS = 2 * SUBLANES
VMEM_LIMIT_BYTES = 60 * 1024 * 1024

EPS = 1e-6
NEG = -1e30

GLA_HEADS = 4
GLA_DK = 64
GLA_DV = 128
GLA_RANK = 16
GLA_TAU = 16.0
ML_HEADS = 4
ML_DK = 64
ML_DV = 128
ML_CONV = 4
CHUNK = 64
SUB = 16
MOBA_HEADS = 8
MOBA_HD = 128
MOBA_BLOCK = 256
MOBA_TOPK = 3

GLA_QK = GLA_HEADS * GLA_DK
GLA_V = GLA_HEADS * GLA_DV
ML_QK = ML_HEADS * ML_DK
ML_V = ML_HEADS * ML_DV

C_GQ = 0
C_GK = C_GQ + GLA_QK
C_GV = C_GK + GLA_QK
C_GR = C_GV + GLA_V
C_MQ = C_GR + GLA_V
C_MK = C_MQ + ML_QK
C_MV = C_MK + ML_QK
C_MO = C_MV + ML_V
C_SMALL = C_MO + ML_V
S_GLR = 0
S_MI = GLA_RANK
S_MF = GLA_RANK + ML_HEADS

ROW_TILE = 1024
MIX_TILE = 256
FF_TILE = 256
PROJ_COL_TILE = 512


def _rms(x, gain):
    return x * lax.rsqrt(jnp.mean(x * x, axis=-1, keepdims=True) + EPS) * gain


def _log_sigmoid(z):
    return jnp.minimum(z, 0.0) - jnp.log1p(jnp.exp(-jnp.abs(z)))


def _sigmoid(z):
    return 1.0 / (1.0 + jnp.exp(-z))


def _dot(a, b):
    return jnp.dot(a.astype(BF16), b.astype(BF16), preferred_element_type=F32)


def _dot_nt(a, b):
    return lax.dot_general(a.astype(BF16), b.astype(BF16), (((1,), (1,)), ((), ())),
                           preferred_element_type=F32)


def _dot_tn(a, b):
    return lax.dot_general(a.astype(BF16), b.astype(BF16), (((0,), (0,)), ((), ())),
                           preferred_element_type=F32)


def _const_spec(shape):
    return pl.BlockSpec(shape, lambda *_: (0,) * len(shape), pipeline_mode=pl.Buffered(1))


def _layer_spec(shape, layer):
    return pl.BlockSpec((None,) + tuple(shape), lambda *_: (layer,) + (0,) * len(shape),
                        pipeline_mode=pl.Buffered(1))


def _params(n_grid):
    return pltpu.CompilerParams(dimension_semantics=("arbitrary",) * n_grid,
                                vmem_limit_bytes=VMEM_LIMIT_BYTES)


def _out_ffn_kernel(x_ref, y_ref, wo_ref, g_ref, wg_ref, wu_ref, wd_ref, o_ref, h_ref, *, ff_tile):
    x1 = x_ref[...] + jnp.dot(y_ref[...], wo_ref[...], preferred_element_type=F32)
    xn = _rms(x1, g_ref[...]).astype(BF16)
    d_ff = wg_ref.shape[1]
    for c0 in range(0, d_ff, ff_tile):
        g = jnp.dot(xn, wg_ref[:, c0:c0 + ff_tile], preferred_element_type=F32)
        u = jnp.dot(xn, wu_ref[:, c0:c0 + ff_tile], preferred_element_type=F32)
        h_ref[:, c0:c0 + ff_tile] = (g * _sigmoid(g) * u).astype(BF16)
    o_ref[...] = x1 + jnp.dot(h_ref[...], wd_ref[...], preferred_element_type=F32)


def out_ffn(x, y, w_o, o_layer, gain, w_gate, w_up, w_down, layer):
    m, d = x.shape
    dy = y.shape[1]
    d_ff = w_gate.shape[2]
    tm = min(ROW_TILE, m)
    return pl.pallas_call(
        functools.partial(_out_ffn_kernel, ff_tile=FF_TILE),
        grid=(m // tm,),
        in_specs=[pl.BlockSpec((tm, d), lambda i: (i, 0)),
                  pl.BlockSpec((tm, dy), lambda i: (i, 0)),
                  _layer_spec((dy, d), o_layer),
                  _const_spec((1, d)),
                  _layer_spec((d, d_ff), layer),
                  _layer_spec((d, d_ff), layer),
                  _layer_spec((d_ff, d), layer)],
        out_specs=pl.BlockSpec((tm, d), lambda i: (i, 0)),
        out_shape=jax.ShapeDtypeStruct((m, d), F32),
        scratch_shapes=[pltpu.VMEM((tm, d_ff), BF16)],
        compiler_params=_params(1),
        name="out_ffn",
    )(x, y, w_o, gain.reshape(1, d), w_gate, w_up, w_down)


def _split3_rows(x):
    hi = x.astype(BF16)
    r1 = x - hi.astype(F32)
    mid = r1.astype(BF16)
    lo = (r1 - mid.astype(F32)).astype(BF16)
    return jnp.concatenate([hi, mid, lo], axis=0)


def _blocks_diag(tiles, zero):
    n = len(tiles)
    return jnp.concatenate(
        [jnp.concatenate([tiles[a] if a == b else zero for b in range(n)], axis=1) for a in range(n)], axis=0)


def _hybrid_kernel(x_ref, g_ref, wa_ref, wb_ref, ws_ref, wg2_ref, bg_ref, gn_ref, conv_ref, bsmall_ref, mn_ref,
                   y_ref, proj_ref, gla_st, ml_cs, ml_ns, ml_m, conv_carry):
    t_tile = proj_ref.shape[0]
    nc = t_tile // CHUNK
    n_sub = CHUNK // SUB

    @pl.when(pl.program_id(1) == 0)
    def _():
        gla_st[...] = jnp.zeros_like(gla_st)
        ml_cs[...] = jnp.zeros_like(ml_cs)
        ml_ns[...] = jnp.zeros_like(ml_ns)
        ml_m[...] = jnp.zeros_like(ml_m)
        conv_carry[...] = jnp.zeros_like(conv_carry)

    xn = _rms(x_ref[...], g_ref[...]).astype(BF16)
    for w_part, base in ((wa_ref, 0), (wb_ref, C_MQ)):
        for c0 in range(0, C_MQ, PROJ_COL_TILE):
            proj_ref[:, base + c0:base + c0 + PROJ_COL_TILE] = jnp.dot(
                xn, w_part[:, c0:c0 + PROJ_COL_TILE], preferred_element_type=F32).astype(proj_ref.dtype)
    small = jnp.dot(xn, ws_ref[...], preferred_element_type=F32)

    xcat = proj_ref[:, C_MQ:C_MQ + 2 * ML_QK].astype(F32)
    ext = jnp.concatenate([conv_carry[...], xcat], axis=0)
    cw = conv_ref[...]
    acc = ext * cw[ML_CONV - 1:ML_CONV]
    for j in range(1, ML_CONV):
        acc = acc + pltpu.roll(ext, j, axis=0) * cw[ML_CONV - 1 - j:ML_CONV - j]
    conv = acc[SUBLANES:]
    qk = conv * _sigmoid(conv)
    conv_carry[...] = xcat[t_tile - SUBLANES:]

    lane = lax.broadcasted_iota(jnp.int32, (1, LANES), 1)
    head_mask = (lane < GLA_DK, lane >= GLA_DK)
    hm0 = head_mask[0]
    r_t = lax.broadcasted_iota(jnp.int32, (t_tile, t_tile), 0)
    c_t = lax.broadcasted_iota(jnp.int32, (t_tile, t_tile), 1)
    tril_bd = ((r_t >= c_t) & (r_t // CHUNK == c_t // CHUNK)).astype(BF16)
    tril3 = jnp.concatenate([tril_bd, tril_bd, tril_bd], axis=1)
    col_w = lax.broadcasted_iota(jnp.int32, (t_tile, 2 * nc * LANES), 1)
    row_w = lax.broadcasted_iota(jnp.int32, (t_tile, 2 * nc * LANES), 0)
    to_rows = ((col_w % CHUNK == row_w % CHUNK) & ((col_w % (nc * LANES)) // LANES == row_w // CHUNK)
               & ((col_w % LANES) // CHUNK == col_w // (nc * LANES))).astype(BF16)
    to_rows3 = jnp.concatenate([to_rows, to_rows, to_rows], axis=0)
    row_c = lax.broadcasted_iota(jnp.int32, (CHUNK, LANES), 0)
    causal2 = (lax.broadcasted_iota(jnp.int32, (CHUNK, LANES), 1) % CHUNK) <= row_c
    sub_id = (lax.broadcasted_iota(jnp.int32, (t_tile, LANES), 0) % CHUNK) // SUB
    row_sub = lax.broadcasted_iota(jnp.int32, (BF16_ROWS, LANES), 0)
    zero_tile = jnp.zeros((CHUNK, LANES), F32)
    zero_v = jnp.zeros((CHUNK, LANES), BF16)

    def chunk_rows(x, c):
        return x[c * CHUNK:(c + 1) * CHUNK]

    def diag_blocks(r):
        return [r[c * CHUNK:(c + 1) * CHUNK, c * LANES:(c + 1) * LANES] for c in range(nc)]

    smallb = small + bsmall_ref[...]

    z = _dot(small, wg2_ref[...]) + bg_ref[...]
    log_a = _log_sigmoid(z) * (1.0 / GLA_TAU)
    cum = jnp.dot(tril3, _split3_rows(jnp.concatenate([log_a, _log_sigmoid(smallb)], axis=1)),
                  preferred_element_type=F32)
    b_cols = smallb - pltpu.roll(cum[:, GLA_QK:], LANES - ML_HEADS, axis=1)
    b_rows = lax.dot_general(_split3_rows(b_cols), to_rows3, (((0,), (0,)), ((), ())), preferred_element_type=F32)

    for p in range(GLA_HEADS // 2):
        q2 = proj_ref[:, C_GQ + p * LANES:C_GQ + (p + 1) * LANES].astype(F32) * (GLA_DK ** -0.5)
        k2 = proj_ref[:, C_GK + p * LANES:C_GK + (p + 1) * LANES].astype(F32)
        bc = cum[:, p * LANES:(p + 1) * LANES]
        v = [proj_ref[:, C_GV + (2 * p + hl) * GLA_DV:C_GV + (2 * p + hl + 1) * GLA_DV] for hl in range(2)]

        ref_pt = [[chunk_rows(bc, c)[i * SUB - 1:i * SUB] if i else jnp.zeros((1, LANES), F32)
                   for i in range(n_sub)] for c in range(nc)]
        r_own = jnp.concatenate([jnp.broadcast_to(ref_pt[c][i], (SUB, LANES))
                                 for c in range(nc) for i in range(n_sub)], axis=0)
        qs = q2 * jnp.exp(bc - r_own)
        q_hat = jnp.concatenate([jnp.where(sub_id == i, qs, 0.0) for i in range(n_sub)], axis=1)
        k_hat, k_dec, last = [], [], []
        for c in range(nc):
            kc, bcc = chunk_rows(k2, c), chunk_rows(bc, c)
            ke = [kc * jnp.exp(ref_pt[c][i] - bcc) for i in range(n_sub)]
            for hl in range(2):
                k_hat.append(jnp.concatenate([jnp.where(head_mask[hl], ke[i], 0.0) for i in range(n_sub)], axis=1))
            last.append(bcc[CHUNK - 1:CHUNK])
            k_dec.append(kc * jnp.exp(last[c] - bcc))
        k_hat = jnp.concatenate(k_hat, axis=0)
        sc = diag_blocks(_dot_nt(q_hat, k_hat))
        sc = [jnp.where(causal2, s, 0.0) for s in sc]
        v_bd = jnp.concatenate([jnp.concatenate([chunk_rows(v[hl], c) if hl == b else zero_v for b in range(2)],
                                                axis=1) for c in range(nc) for hl in range(2)], axis=0)
        o_intra = _dot(_blocks_diag(sc, zero_tile), v_bd)
        u_all = _dot_tn(jnp.concatenate(v, axis=1), _blocks_diag(k_dec, zero_tile))

        st = gla_st[p]
        before = []
        for c in range(nc):
            before.append(st)
            u = jnp.where(hm0, u_all[:GLA_DV, c * LANES:(c + 1) * LANES], u_all[GLA_DV:, c * LANES:(c + 1) * LANES])
            st = st * jnp.exp(last[c]) + u
        gla_st[p] = st
        st_stack = jnp.concatenate(before, axis=0)
        q_in = q2 * jnp.exp(bc)
        for hl in range(2):
            h = 2 * p + hl
            o_inter = jnp.concatenate(diag_blocks(_dot_nt(jnp.where(head_mask[hl], q_in, 0.0), st_stack)), axis=0)
            o = o_intra[:, hl * GLA_DV:(hl + 1) * GLA_DV] + o_inter
            gate = proj_ref[:, C_GR + h * GLA_DV:C_GR + (h + 1) * GLA_DV].astype(F32)
            yg = _rms(o, gn_ref[:, h * GLA_DV:(h + 1) * GLA_DV]) * (gate * _sigmoid(gate))
            y_ref[:, h * GLA_DV:(h + 1) * GLA_DV] = yg.astype(y_ref.dtype)

    ones_tile = jnp.ones((t_tile, LANES), BF16)
    for p in range(ML_HEADS // 2):
        mq2 = qk[:, p * LANES:(p + 1) * LANES]
        mk2 = qk[:, ML_QK + p * LANES:ML_QK + (p + 1) * LANES] * (ML_DK ** -0.5)
        v = [proj_ref[:, C_MV + (2 * p + hl) * ML_DV:C_MV + (2 * p + hl + 1) * ML_DV] for hl in range(2)]
        k_stack = jnp.concatenate([jnp.where(head_mask[hl], chunk_rows(mk2, c), 0.0)
                                   for c in range(nc) for hl in range(2)], axis=0)
        qk_blocks = diag_blocks(_dot_nt(mq2, k_stack))

        w_blocks, wk_blocks, cm_cols, b_max, bc_last = [], [], [], [], []
        for c in range(nc):
            blk = slice(c * LANES, (c + 1) * LANES)
            b_pair = (b_rows[S_MI + 2 * p:S_MI + 2 * p + 1, blk]
                      + b_rows[S_MI + 2 * p + 1:S_MI + 2 * p + 2, nc * LANES + c * LANES:nc * LANES + (c + 1) * LANES])
            f_last = [chunk_rows(cum, c)[CHUNK - 1:CHUNK, GLA_QK + S_MF + 2 * p + hl:GLA_QK + S_MF + 2 * p + hl + 1]
                      for hl in range(2)]
            bm = jnp.where(causal2, b_pair, -jnp.inf)
            cm = [jnp.max(jnp.where(head_mask[hl], bm, -jnp.inf), axis=1, keepdims=True) for hl in range(2)]
            w_blocks.append(qk_blocks[c] * jnp.exp(bm - jnp.where(hm0, cm[0], cm[1])))
            cm_cols.append(cm)
            bmx = [jnp.max(jnp.where(head_mask[hl], b_pair, -jnp.inf), axis=1, keepdims=True) for hl in range(2)]
            b_max.append(jnp.where(hm0, bmx[0], bmx[1]))
            bc_last.append(jnp.where(hm0, f_last[0], f_last[1]))
            e_col = []
            for hl in range(2):
                h = 2 * p + hl
                b_col = (chunk_rows(smallb, c)[:, S_MI + h:S_MI + h + 1]
                         - chunk_rows(cum, c)[:, GLA_QK + S_MF + h:GLA_QK + S_MF + h + 1])
                e_col.append(jnp.exp(b_col - bmx[hl]))
            wk_blocks.append(chunk_rows(mk2, c) * jnp.where(hm0, e_col[0], e_col[1]))

        ones_col = [jnp.broadcast_to((lane == hl).astype(BF16), (CHUNK, LANES)) for hl in range(2)]
        v_bd = jnp.concatenate([jnp.concatenate([chunk_rows(v[hl], c) if hl == b else zero_v for b in range(2)]
                                                + [ones_col[hl]], axis=1)
                                for c in range(nc) for hl in range(2)], axis=0)
        nv = _dot(_blocks_diag(w_blocks, zero_tile), v_bd)
        u_all = _dot_tn(jnp.concatenate(v + [ones_tile], axis=1), _blocks_diag(wk_blocks, zero_tile))

        cs, ns, m_run = ml_cs[p], ml_ns[p], ml_m[p]
        before = []
        for c in range(nc):
            before.append((cs, ns, m_run))
            m_top = jnp.maximum(m_run, b_max[c])
            keep = jnp.exp(m_run - m_top)
            gain = jnp.exp(b_max[c] - m_top)
            blk = slice(c * LANES, (c + 1) * LANES)
            cs = keep * cs + gain * jnp.where(hm0, u_all[:ML_DV, blk], u_all[ML_DV:2 * ML_DV, blk])
            ns = keep * ns + gain * u_all[2 * ML_DV:2 * ML_DV + 1, blk]
            m_run = bc_last[c] + m_top
        ml_cs[p], ml_ns[p], ml_m[p] = cs, ns, m_run
        ns_rows = jnp.zeros((BF16_ROWS, LANES), F32)
        for c in range(nc):
            ns_rows = jnp.where(row_sub == c, before[c][1], ns_rows)
        state_stack = jnp.concatenate([b[0] for b in before] + [ns_rows], axis=0)

        for hl in range(2):
            h = 2 * p + hl
            rr = _dot_nt(jnp.where(head_mask[hl], mq2, 0.0), state_stack)
            q_cs = diag_blocks(rr)
            h_chunks = []
            for c in range(nc):
                rows = slice(c * CHUNK, (c + 1) * CHUNK)
                q_ns = rr[rows, nc * ML_DV + c:nc * ML_DV + c + 1]
                m_prev = before[c][2][:, hl * ML_DK:hl * ML_DK + 1]
                cm = cm_cols[c][hl]
                m_top = jnp.maximum(m_prev, cm)
                local = jnp.exp(cm - m_top)
                carried = jnp.exp(m_prev - m_top)
                num = local * nv[rows, hl * ML_DV:(hl + 1) * ML_DV] + carried * q_cs[c]
                den = local * nv[rows, 2 * ML_DV + hl:2 * ML_DV + hl + 1] + carried * q_ns
                bc_col = chunk_rows(cum, c)[:, GLA_QK + S_MF + h:GLA_QK + S_MF + h + 1]
                h_chunks.append(num / jnp.maximum(jnp.abs(den), jnp.exp(-(bc_col + m_top))))
            og = proj_ref[:, C_MO + h * ML_DV:C_MO + (h + 1) * ML_DV].astype(F32)
            ym = _rms(jnp.concatenate(h_chunks, axis=0) * _sigmoid(og), mn_ref[:, h * ML_DV:(h + 1) * ML_DV])
            y_ref[:, GLA_V + h * ML_DV:GLA_V + (h + 1) * ML_DV] = ym.astype(y_ref.dtype)


def split_hybrid_weight(w_in):
    assert C_MQ == 2 * ML_QK + 2 * ML_V
    ml0 = C_MQ + GLA_RANK
    pad = jnp.zeros(w_in.shape[:2] + (LANES - GLA_RANK - 2 * ML_HEADS,), w_in.dtype)
    small = jnp.concatenate([w_in[:, :, C_MQ:ml0], w_in[:, :, ml0 + C_MQ:], pad], axis=2)
    return w_in[:, :, :C_MQ].astype(BF16), w_in[:, :, ml0:ml0 + C_MQ].astype(BF16), small.astype(BF16)


def hybrid_layer(x, gain, w_parts, layer, w_gate2, b_gate, gla_norm, conv_w, b_if, ml_norm, batch):
    m, d = x.shape
    seq = m // batch
    t = min(MIX_TILE, seq)
    nt = seq // t
    assert BF16_ROWS >= t // CHUNK
    b_small = jnp.zeros((1, LANES), F32).at[0, S_MI:S_MI + 2 * ML_HEADS].set(b_if)
    w_gate2_rows = jnp.zeros((LANES, GLA_QK), F32).at[S_GLR:S_GLR + GLA_RANK].set(w_gate2)
    return pl.pallas_call(
        _hybrid_kernel,
        grid=(batch, nt),
        in_specs=[pl.BlockSpec((t, d), lambda b, i: (b * nt + i, 0)),
                  _const_spec((1, d)),
                  _layer_spec((d, C_MQ), layer),
                  _layer_spec((d, C_MQ), layer),
                  _layer_spec((d, LANES), layer),
                  _const_spec((LANES, GLA_QK)),
                  _const_spec((1, GLA_QK)),
                  _const_spec((1, GLA_V)),
                  _const_spec((ML_CONV, 2 * ML_QK)),
                  _const_spec((1, LANES)),
                  _const_spec((1, ML_V))],
        out_specs=pl.BlockSpec((t, GLA_V + ML_V), lambda b, i: (b * nt + i, 0)),
        out_shape=jax.ShapeDtypeStruct((m, GLA_V + ML_V), BF16),
        scratch_shapes=[pltpu.VMEM((t, C_SMALL), BF16),
                        pltpu.VMEM((GLA_HEADS // 2, GLA_DV, LANES), F32),
                        pltpu.VMEM((ML_HEADS // 2, ML_DV, LANES), F32),
                        pltpu.VMEM((ML_HEADS // 2, 1, LANES), F32),
                        pltpu.VMEM((ML_HEADS // 2, 1, LANES), F32),
                        pltpu.VMEM((SUBLANES, 2 * ML_QK), F32)],
        compiler_params=_params(2),
        name="hybrid_core",
    )(x, gain.reshape(1, d), *w_parts, w_gate2_rows, b_gate.reshape(1, -1), gla_norm.reshape(1, -1), conv_w,
      b_small, ml_norm.reshape(1, -1))


LOG2E = 1.4426950408889634
AUG = 2 * MOBA_HD
LANES_PER_BLOCK = 4
MASK_BIG = 2.0 ** 100
PROJ_HEADS = 4
VT_ROWS = MOBA_HD + BF16_ROWS


def _split3(x):
    x = np.asarray(x, np.float32)
    hi = x.astype(BF16).astype(np.float32)
    mid = (x - hi).astype(BF16).astype(np.float32)
    lo = (x - hi - mid).astype(BF16).astype(np.float32)
    return np.stack([hi, mid, lo], axis=-1)


def _aug_tables(nb):
    c = np.asarray([LOG2E * 2.0 ** (-8.0 * (h + 1) / MOBA_HEADS) for h in range(MOBA_HEADS)], np.float32)
    pos = np.arange(MOBA_BLOCK, dtype=np.float32)
    base = LANES_PER_BLOCK * nb
    q_rows = np.zeros((MOBA_HEADS, MOBA_BLOCK, LANES), np.float32)
    k_rows = np.zeros((MOBA_HEADS, MOBA_BLOCK, LANES), np.float32)
    q_rows[:, :, base:base + 3] = _split3(-c[:, None] * pos[None, :])
    q_rows[:, :, base + 3:base + 6] = 1.0
    k_rows[:, :, base:base + 3] = 1.0
    k_rows[:, :, base + 3:base + 6] = _split3(c[:, None] * pos[None, :])
    i_idx = np.arange(nb)[:, None]
    j_idx = np.arange(nb)[None, :]
    steps = np.zeros((MOBA_HEADS, nb, nb, LANES_PER_BLOCK), np.float32)
    steps[..., 0] = np.where(j_idx == i_idx, 0.0, -MASK_BIG)
    gap = _split3(-c[:, None, None] * (MOBA_BLOCK * (i_idx - j_idx)).astype(np.float32)[None])
    steps[..., 1:] = np.where((j_idx <= i_idx)[None, :, :, None], gap, 0.0)
    q_steps = np.zeros((MOBA_HEADS, nb, 1, LANES), np.float32)
    q_steps[:, :, 0, :base] = steps.reshape(MOBA_HEADS, nb, base)
    k_steps = np.zeros((nb, 1, LANES), np.float32)
    k_steps[:, 0, :base] = np.repeat(np.eye(nb, dtype=np.float32), LANES_PER_BLOCK, axis=1)
    expand = np.zeros((nb, LANES), np.float32)
    expand[np.arange(nb), LANES_PER_BLOCK * np.arange(nb)] = MASK_BIG
    return (jnp.asarray(q_rows), jnp.asarray(q_steps), jnp.asarray(k_rows), jnp.asarray(k_steps),
            jnp.asarray(expand, BF16))


def _moba_prep_kernel(x_ref, g_ref, w_ref, qg_ref, kg_ref, qrow_ref, qstep_ref, krow_ref, kstep_ref, exp_ref,
                      qa_ref, ka_ref, vt_ref, kmean_ref, *, blocks):
    step = pl.program_id(1)
    nb = kmean_ref.shape[1]
    d = MOBA_HEADS * MOBA_HD

    @pl.when(step == 0)
    def _():
        kmean_ref[...] = jnp.zeros_like(kmean_ref)

    xn = _rms(x_ref[...], g_ref[...]).astype(BF16)
    blk = lax.broadcasted_iota(jnp.int32, (nb, MOBA_BLOCK), 0)
    ones_rows = (lax.broadcasted_iota(jnp.int32, (VT_ROWS - MOBA_HD, MOBA_BLOCK), 0) == 0).astype(F32)

    for h in range(MOBA_HEADS):
        hg = h % PROJ_HEADS
        if hg == 0:
            c0 = h * MOBA_HD
            q_grp, k_grp, v_grp = (jnp.dot(xn, w_ref[:, off + c0:off + c0 + PROJ_HEADS * MOBA_HD],
                                           preferred_element_type=F32) for off in (0, d, 2 * d))
        part = slice(hg * MOBA_HD, (hg + 1) * MOBA_HD)
        for u in range(blocks):
            i = step * blocks + u
            rows = slice(u * MOBA_BLOCK, (u + 1) * MOBA_BLOCK)
            past = blk < i
            q = _rms(q_grp[rows, part], qg_ref[...]) * (MOBA_HD ** -0.5)
            k = _rms(k_grp[rows, part], kg_ref[...])
            v = v_grp[rows, part]
            vt_ref[0, h, :, rows] = jnp.concatenate([v.T, ones_rows], axis=0).astype(vt_ref.dtype)

            gate = _dot_nt(kmean_ref[h], q)
            gate = jnp.where(past, gate, -jnp.inf)
            rank = jnp.zeros((nb, MOBA_BLOCK), F32)
            for j in range(nb):
                other = gate[j:j + 1]
                ahead = (other > gate) | ((other == gate) & (j < blk))
                rank = rank + ahead.astype(F32)
            sel = (past & (rank < MOBA_TOPK)).astype(BF16)
            q_extra = (qrow_ref[h] + qstep_ref[h, i]) + _dot_tn(sel, exp_ref[...])
            qa_ref[rows, h * AUG:h * AUG + MOBA_HD] = (q * LOG2E).astype(qa_ref.dtype)
            qa_ref[rows, h * AUG + MOBA_HD:(h + 1) * AUG] = q_extra.astype(qa_ref.dtype)
            ka_ref[rows, h * AUG:h * AUG + MOBA_HD] = k.astype(ka_ref.dtype)
            ka_ref[rows, h * AUG + MOBA_HD:(h + 1) * AUG] = (krow_ref[h] + kstep_ref[i]).astype(ka_ref.dtype)

            kmean_ref[h, pl.ds(i, 1), :] = jnp.mean(k, axis=0, keepdims=True)


PREP_BLOCKS = 4


def moba_prep(x, gain, w_qkv, layer, q_gain, k_gain, batch):
    m, dm = x.shape
    seq = m // batch
    nb = seq // MOBA_BLOCK
    assert LANES_PER_BLOCK * nb + 6 <= LANES and nb % PREP_BLOCKS == 0
    d = MOBA_HEADS * MOBA_HD
    rows = PREP_BLOCKS * MOBA_BLOCK
    ns = nb // PREP_BLOCKS
    q_rows, q_steps, k_rows, k_steps, expand = _aug_tables(nb)
    return pl.pallas_call(
        functools.partial(_moba_prep_kernel, blocks=PREP_BLOCKS),
        grid=(batch, ns),
        in_specs=[pl.BlockSpec((rows, dm), lambda b, i: (b * ns + i, 0)),
                  _const_spec((1, dm)),
                  _layer_spec((dm, 3 * d), layer),
                  _const_spec((1, MOBA_HD)),
                  _const_spec((1, MOBA_HD)),
                  _const_spec((MOBA_HEADS, MOBA_BLOCK, LANES)),
                  _const_spec((MOBA_HEADS, nb, 1, LANES)),
                  _const_spec((MOBA_HEADS, MOBA_BLOCK, LANES)),
                  _const_spec((nb, 1, LANES)),
                  _const_spec((nb, LANES))],
        out_specs=[pl.BlockSpec((rows, MOBA_HEADS * AUG), lambda b, i: (b * ns + i, 0)),
                   pl.BlockSpec((rows, MOBA_HEADS * AUG), lambda b, i: (b * ns + i, 0)),
                   pl.BlockSpec((1, MOBA_HEADS, VT_ROWS, rows), lambda b, i: (b, 0, 0, i))],
        out_shape=[jax.ShapeDtypeStruct((m, MOBA_HEADS * AUG), BF16),
                   jax.ShapeDtypeStruct((m, MOBA_HEADS * AUG), BF16),
                   jax.ShapeDtypeStruct((batch, MOBA_HEADS, VT_ROWS, seq), BF16)],
        scratch_shapes=[pltpu.VMEM((MOBA_HEADS, nb, MOBA_HD), F32)],
        compiler_params=_params(2),
        name="moba_prep",
    )(x, gain.reshape(1, dm), w_qkv, q_gain.reshape(1, -1), k_gain.reshape(1, -1),
      q_rows, q_steps, k_rows, k_steps, expand)


ATTN_HEADS_PER_STEP = 4
ATTN_CHUNK_BLOCKS = 4
PV_KEYS = 256


def _moba_attn_kernel(q_ref, k_ref, vt_ref, o_ref, s_even, s_odd, m_even, m_odd, *, heads, chunk_blocks, nb):
    t = pl.program_id(2)
    i = jnp.minimum(t, nb - 1)
    chunk_keys = chunk_blocks * MOBA_BLOCK
    n_chunks = i // chunk_blocks + 1
    n_valid2 = jnp.where(t > 0, (t - 1) // chunk_blocks + 1, 0)

    @pl.when((pl.program_id(0) == 0) & (pl.program_id(1) == 0) & (t == 0))
    def _():
        s_even[...] = jnp.zeros_like(s_even)
        s_odd[...] = jnp.zeros_like(s_odd)

    key_r = lax.broadcasted_iota(jnp.int32, (MOBA_BLOCK, MOBA_BLOCK), 0)
    qry_c = lax.broadcasted_iota(jnp.int32, (MOBA_BLOCK, MOBA_BLOCK), 1)
    causal = qry_c >= key_r
    q_aug = [q_ref[:, hh * AUG:(hh + 1) * AUG] for hh in range(heads)]

    def both(c, carry, last, s_w, s_r, m_row2, stage1, stage2):
        m_part, acc = carry
        k0 = c * chunk_keys
        live2 = jnp.full((1, MOBA_BLOCK), c, jnp.int32) < n_valid2
        new_m, new_acc = [], []
        for hh in range(heads):
            m_new, a = m_part[hh], acc[hh]
            if stage1:
                s = lax.dot_general(k_ref[pl.ds(k0, chunk_keys), hh * AUG:(hh + 1) * AUG], q_aug[hh],
                                    (((1,), (1,)), ((), ())), preferred_element_type=F32)
                if last:
                    parts = []
                    for u in range(chunk_blocks):
                        other_block = jnp.full((1, MOBA_BLOCK), c * chunk_blocks + u, jnp.int32) != i
                        parts.append(jnp.where(causal | other_block, s[u * MOBA_BLOCK:(u + 1) * MOBA_BLOCK], NEG))
                    s = jnp.concatenate(parts, axis=0)
                s_w[hh, pl.ds(k0, chunk_keys), :] = s
                m_new = jnp.maximum(m_new, jnp.max(s.reshape(chunk_keys // SUBLANES, SUBLANES, MOBA_BLOCK), axis=0))
            if stage2:
                for u in range(0, chunk_keys, PV_KEYS):
                    ku = k0 + u
                    p = jnp.exp2(s_r[hh, pl.ds(ku, PV_KEYS), :] - m_row2[hh])
                    p = jnp.where(live2, p, 0.0).astype(BF16)
                    a = a + jnp.dot(vt_ref[0, hh, :, pl.ds(ku, PV_KEYS)], p, preferred_element_type=F32)
            new_m.append(m_new)
            new_acc.append(a)
        return tuple(new_m), tuple(new_acc)

    def step(n, s_w, s_r, m_w, m_r, stage1=True, stage2=True):
        m_row2 = [m_r[hh] for hh in range(heads)]
        carry = (tuple(jnp.full((SUBLANES, MOBA_BLOCK), NEG, F32) for _ in range(heads)),
                 tuple(jnp.zeros((VT_ROWS, MOBA_BLOCK), F32) for _ in range(heads)))
        for c in range(n):
            carry = both(c, carry, c == n - 1, s_w, s_r, m_row2, stage1, stage2)
        m_part, acc = carry
        for hh in range(heads):
            if stage1:
                m_w[hh] = jnp.max(m_part[hh], axis=0, keepdims=True)
            if stage2:
                o = acc[hh][:MOBA_HD] * (1.0 / acc[hh][MOBA_HD:MOBA_HD + 1])
                o_ref[:, hh * MOBA_HD:(hh + 1) * MOBA_HD] = o.T.astype(o_ref.dtype)

    even = (s_even, s_odd, m_even, m_odd)
    odd = (s_odd, s_even, m_odd, m_even)
    n_max = nb // chunk_blocks
    pl.when(t == 0)(functools.partial(step, 1, *even, stage2=False))
    pl.when(t == nb)(functools.partial(step, n_max, *(odd if nb % 2 else even), stage1=False))
    inner = (t > 0) & (t < nb)
    for n in range(1, n_max + 1):
        pl.when(inner & (n_chunks == n) & (t % 2 == 0))(functools.partial(step, n, *even))
        pl.when(inner & (n_chunks == n) & (t % 2 == 1))(functools.partial(step, n, *odd))


def moba_attention(q_aug, k_aug, vt, batch):
    m = q_aug.shape[0]
    seq = m // batch
    nb = seq // MOBA_BLOCK
    hps = ATTN_HEADS_PER_STEP
    assert nb % ATTN_CHUNK_BLOCKS == 0
    return pl.pallas_call(
        functools.partial(_moba_attn_kernel, heads=hps, chunk_blocks=ATTN_CHUNK_BLOCKS, nb=nb),
        grid=(batch, MOBA_HEADS // hps, nb + 1),
        in_specs=[pl.BlockSpec((MOBA_BLOCK, hps * AUG), lambda b, g, t: (b * nb + jnp.minimum(t, nb - 1), g)),
                  pl.BlockSpec((seq, hps * AUG), lambda b, g, t: (b, g)),
                  pl.BlockSpec((1, hps, VT_ROWS, seq), lambda b, g, t: (b, g, 0, 0),
                               pipeline_mode=pl.Buffered(1))],
        out_specs=pl.BlockSpec((MOBA_BLOCK, hps * MOBA_HD), lambda b, g, t: (b * nb + jnp.maximum(t - 1, 0), g)),
        out_shape=jax.ShapeDtypeStruct((m, MOBA_HEADS * MOBA_HD), BF16),
        scratch_shapes=[pltpu.VMEM((hps, seq, MOBA_BLOCK), F32), pltpu.VMEM((hps, seq, MOBA_BLOCK), F32),
                        pltpu.VMEM((hps, 1, MOBA_BLOCK), F32), pltpu.VMEM((hps, 1, MOBA_BLOCK), F32)],
        compiler_params=_params(3),
        name="moba_attn",
    )(q_aug, k_aug, vt)


def moba_layer(x, gain, w_qkv, layer, q_gain, k_gain, batch):
    q_aug, k_aug, vt = moba_prep(x, gain, w_qkv, layer, q_gain, k_gain, batch)
    return moba_attention(q_aug, k_aug, vt, batch)


def kernel(x, norm_mix, norm_ffn, hyb_w_in, gla_w_gate2, gla_b_gate, gla_norm, ml_conv, ml_b_if, ml_norm,
           hyb_w_out, moba_w_qkv, moba_q_norm, moba_k_norm, moba_w_o, ffn_w_gate, ffn_w_up, ffn_w_down):
    batch, seq, d = x.shape
    depth = norm_mix.shape[0]
    xf = x.reshape(batch * seq, d)
    w_hyb_in = split_hybrid_weight(hyb_w_in)
    w_hyb_out, w_moba_qkv, w_moba_o = hyb_w_out.astype(BF16), moba_w_qkv.astype(BF16), moba_w_o.astype(BF16)
    w_gate, w_up, w_down = ffn_w_gate.astype(BF16), ffn_w_up.astype(BF16), ffn_w_down.astype(BF16)
    for l in range(depth):
        j = l // 2
        if l % 2 == 0:
            y = hybrid_layer(xf, norm_mix[l], w_hyb_in, j, gla_w_gate2[j], gla_b_gate[j], gla_norm[j],
                             ml_conv[j], ml_b_if[j], ml_norm[j], batch)
            w_o = w_hyb_out
        else:
            y = moba_layer(xf, norm_mix[l], w_moba_qkv, j, moba_q_norm[j], moba_k_norm[j], batch)
            w_o = w_moba_o
        xf = out_ffn(xf, y, w_o, j, norm_ffn[l], w_gate, w_up, w_down, l)
    return xf.reshape(batch, seq, d)
```

```python
import functools

import jax
import jax.numpy as jnp
import numpy as np
from jax import lax
from jax.experimental import pallas as pl
from jax.experimental.pallas import tpu as pltpu

F32 = jnp.float32
BF16 = jnp.bfloat16

LANES = 128
SUBLANES = 8
BF16_ROWS = 2 * SUBLANES
VMEM_LIMIT_BYTES = 62 * 1024 * 1024

EPS = 1e-6
NEG = -1e30

GLA_HEADS = 4
GLA_DK = 64
GLA_DV = 128
GLA_RANK = 16
GLA_TAU = 16.0
ML_HEADS = 4
ML_DK = 64
ML_DV = 128
ML_CONV = 4
CHUNK = 64
SUB = 16
MOBA_HEADS = 8
MOBA_HD = 128
MOBA_BLOCK = 256
MOBA_TOPK = 3

GLA_QK = GLA_HEADS * GLA_DK
GLA_V = GLA_HEADS * GLA_DV
ML_QK = ML_HEADS * ML_DK
ML_V = ML_HEADS * ML_DV

C_GQ = 0
C_GK = C_GQ + GLA_QK
C_GV = C_GK + GLA_QK
C_GR = C_GV + GLA_V
C_MQ = C_GR + GLA_V
C_MK = C_MQ + ML_QK
C_MV = C_MK + ML_QK
C_MO = C_MV + ML_V
C_SMALL = C_MO + ML_V
S_GLR = 0
S_MI = GLA_RANK
S_MF = GLA_RANK + ML_HEADS

ROW_TILE = 1024
MIX_TILE = 256
FF_TILE = 256
PROJ_COL_TILE = 512


def _rms(x, gain):
    return x * lax.rsqrt(jnp.mean(x * x, axis=-1, keepdims=True) + EPS) * gain


def _log_sigmoid(z):
    return jnp.minimum(z, 0.0) - jnp.log1p(jnp.exp(-jnp.abs(z)))


def _sigmoid(z):
    return 1.0 / (1.0 + jnp.exp(-z))


def _dot(a, b):
    return jnp.dot(a.astype(BF16), b.astype(BF16), preferred_element_type=F32)


def _dot_nt(a, b):
    return lax.dot_general(a.astype(BF16), b.astype(BF16), (((1,), (1,)), ((), ())),
                           preferred_element_type=F32)


def _dot_tn(a, b):
    return lax.dot_general(a.astype(BF16), b.astype(BF16), (((0,), (0,)), ((), ())),
                           preferred_element_type=F32)


def _const_spec(shape):
    return pl.BlockSpec(shape, lambda *_: (0,) * len(shape), pipeline_mode=pl.Buffered(1))


def _layer_spec(shape, layer):
    return pl.BlockSpec((None,) + tuple(shape), lambda *_: (layer,) + (0,) * len(shape),
                        pipeline_mode=pl.Buffered(1))


def _params(n_grid):
    return pltpu.CompilerParams(dimension_semantics=("arbitrary",) * n_grid,
                                vmem_limit_bytes=VMEM_LIMIT_BYTES)


def _out_ffn_kernel(x_ref, y_ref, wo_ref, g_ref, wg_ref, wu_ref, wd_ref, o_ref, h_ref, *, ff_tile):
    x1 = x_ref[...] + jnp.dot(y_ref[...], wo_ref[...], preferred_element_type=F32)
    xn = _rms(x1, g_ref[...]).astype(BF16)
    d_ff = wg_ref.shape[1]
    for c0 in range(0, d_ff, ff_tile):
        g = jnp.dot(xn, wg_ref[:, c0:c0 + ff_tile], preferred_element_type=F32)
        u = jnp.dot(xn, wu_ref[:, c0:c0 + ff_tile], preferred_element_type=F32)
        h_ref[:, c0:c0 + ff_tile] = (g * _sigmoid(g) * u).astype(BF16)
    o_ref[...] = x1 + jnp.dot(h_ref[...], wd_ref[...], preferred_element_type=F32)


def out_ffn(x, y, w_o, o_layer, gain, w_gate, w_up, w_down, layer):
    m, d = x.shape
    dy = y.shape[1]
    d_ff = w_gate.shape[2]
    tm = min(ROW_TILE, m)
    return pl.pallas_call(
        functools.partial(_out_ffn_kernel, ff_tile=FF_TILE),
        grid=(m // tm,),
        in_specs=[pl.BlockSpec((tm, d), lambda i: (i, 0)),
                  pl.BlockSpec((tm, dy), lambda i: (i, 0)),
                  _layer_spec((dy, d), o_layer),
                  _const_spec((1, d)),
                  _layer_spec((d, d_ff), layer),
                  _layer_spec((d, d_ff), layer),
                  _layer_spec((d_ff, d), layer)],
        out_specs=pl.BlockSpec((tm, d), lambda i: (i, 0)),
        out_shape=jax.ShapeDtypeStruct((m, d), F32),
        scratch_shapes=[pltpu.VMEM((tm, d_ff), BF16)],
        compiler_params=_params(1),
        name="out_ffn",
    )(x, y, w_o, gain.reshape(1, d), w_gate, w_up, w_down)


def _split3_rows(x):
    hi = x.astype(BF16)
    r1 = x - hi.astype(F32)
    mid = r1.astype(BF16)
    lo = (r1 - mid.astype(F32)).astype(BF16)
    return jnp.concatenate([hi, mid, lo], axis=0)


def _blocks_diag(tiles, zero):
    n = len(tiles)
    return jnp.concatenate(
        [jnp.concatenate([tiles[a] if a == b else zero for b in range(n)], axis=1) for a in range(n)], axis=0)


def _hybrid_kernel(x_ref, g_ref, wa_ref, wb_ref, ws_ref, wg2_ref, bg_ref, gn_ref, conv_ref, bsmall_ref, mn_ref,
                   y_ref, proj_ref, gla_st, ml_cs, ml_ns, ml_m, conv_carry):
    t_tile = proj_ref.shape[0]
    nc = t_tile // CHUNK
    n_sub = CHUNK // SUB

    @pl.when(pl.program_id(1) == 0)
    def _():
        gla_st[...] = jnp.zeros_like(gla_st)
        ml_cs[...] = jnp.zeros_like(ml_cs)
        ml_ns[...] = jnp.zeros_like(ml_ns)
        ml_m[...] = jnp.zeros_like(ml_m)
        conv_carry[...] = jnp.zeros_like(conv_carry)

    xn = _rms(x_ref[...], g_ref[...]).astype(BF16)
    for w_part, base in ((wa_ref, 0), (wb_ref, C_MQ)):
        for c0 in range(0, C_MQ, PROJ_COL_TILE):
            proj_ref[:, base + c0:base + c0 + PROJ_COL_TILE] = jnp.dot(
                xn, w_part[:, c0:c0 + PROJ_COL_TILE], preferred_element_type=F32).astype(proj_ref.dtype)
    small = jnp.dot(xn, ws_ref[...], preferred_element_type=F32)

    xcat = proj_ref[:, C_MQ:C_MQ + 2 * ML_QK].astype(F32)
    ext = jnp.concatenate([conv_carry[...], xcat], axis=0)
    cw = conv_ref[...]
    acc = ext * cw[ML_CONV - 1:ML_CONV]
    for j in range(1, ML_CONV):
        acc = acc + pltpu.roll(ext, j, axis=0) * cw[ML_CONV - 1 - j:ML_CONV - j]
    conv = acc[SUBLANES:]
    qk = conv * _sigmoid(conv)
    conv_carry[...] = xcat[t_tile - SUBLANES:]

    lane = lax.broadcasted_iota(jnp.int32, (1, LANES), 1)
    head_mask = (lane < GLA_DK, lane >= GLA_DK)
    hm0 = head_mask[0]
    r_t = lax.broadcasted_iota(jnp.int32, (t_tile, t_tile), 0)
    c_t = lax.broadcasted_iota(jnp.int32, (t_tile, t_tile), 1)
    tril_bd = ((r_t >= c_t) & (r_t // CHUNK == c_t // CHUNK)).astype(BF16)
    tril3 = jnp.concatenate([tril_bd, tril_bd, tril_bd], axis=1)
    col_w = lax.broadcasted_iota(jnp.int32, (t_tile, 2 * nc * LANES), 1)
    row_w = lax.broadcasted_iota(jnp.int32, (t_tile, 2 * nc * LANES), 0)
    to_rows = ((col_w % CHUNK == row_w % CHUNK) & ((col_w % (nc * LANES)) // LANES == row_w // CHUNK)
               & ((col_w % LANES) // CHUNK == col_w // (nc * LANES))).astype(BF16)
    to_rows3 = jnp.concatenate([to_rows, to_rows, to_rows], axis=0)
    row_c = lax.broadcasted_iota(jnp.int32, (CHUNK, LANES), 0)
    causal2 = (lax.broadcasted_iota(jnp.int32, (CHUNK, LANES), 1) % CHUNK) <= row_c
    sub_id = (lax.broadcasted_iota(jnp.int32, (t_tile, LANES), 0) % CHUNK) // SUB
    row_sub = lax.broadcasted_iota(jnp.int32, (BF16_ROWS, LANES), 0)
    zero_tile = jnp.zeros((CHUNK, LANES), F32)
    zero_v = jnp.zeros((CHUNK, LANES), BF16)

    def chunk_rows(x, c):
        return x[c * CHUNK:(c + 1) * CHUNK]

    def diag_blocks(r):
        return [r[c * CHUNK:(c + 1) * CHUNK, c * LANES:(c + 1) * LANES] for c in range(nc)]

    smallb = small + bsmall_ref[...]

    z = _dot(small, wg2_ref[...]) + bg_ref[...]
    log_a = _log_sigmoid(z) * (1.0 / GLA_TAU)
    cum = jnp.dot(tril3, _split3_rows(jnp.concatenate([log_a, _log_sigmoid(smallb)], axis=1)),
                  preferred_element_type=F32)
    b_cols = smallb - pltpu.roll(cum[:, GLA_QK:], LANES - ML_HEADS, axis=1)
    b_rows = lax.dot_general(_split3_rows(b_cols), to_rows3, (((0,), (0,)), ((), ())), preferred_element_type=F32)

    for p in range(GLA_HEADS // 2):
        q2 = proj_ref[:, C_GQ + p * LANES:C_GQ + (p + 1) * LANES].astype(F32) * (GLA_DK ** -0.5)
        k2 = proj_ref[:, C_GK + p * LANES:C_GK + (p + 1) * LANES].astype(F32)
        bc = cum[:, p * LANES:(p + 1) * LANES]
        v = [proj_ref[:, C_GV + (2 * p + hl) * GLA_DV:C_GV + (2 * p + hl + 1) * GLA_DV] for hl in range(2)]

        ref_pt = [[chunk_rows(bc, c)[i * SUB - 1:i * SUB] if i else jnp.zeros((1, LANES), F32)
                   for i in range(n_sub)] for c in range(nc)]
        r_own = jnp.concatenate([jnp.broadcast_to(ref_pt[c][i], (SUB, LANES))
                                 for c in range(nc) for i in range(n_sub)], axis=0)
        qs = q2 * jnp.exp(bc - r_own)
        q_hat = jnp.concatenate([jnp.where(sub_id == i, qs, 0.0) for i in range(n_sub)], axis=1)
        k_hat, k_dec, last = [], [], []
        for c in range(nc):
            kc, bcc = chunk_rows(k2, c), chunk_rows(bc, c)
            ke = [kc * jnp.exp(ref_pt[c][i] - bcc) for i in range(n_sub)]
            for hl in range(2):
                k_hat.append(jnp.concatenate([jnp.where(head_mask[hl], ke[i], 0.0) for i in range(n_sub)], axis=1))
            last.append(bcc[CHUNK - 1:CHUNK])
            k_dec.append(kc * jnp.exp(last[c] - bcc))
        k_hat = jnp.concatenate(k_hat, axis=0)
        sc = diag_blocks(_dot_nt(q_hat, k_hat))
        sc = [jnp.where(causal2, s, 0.0) for s in sc]
        v_bd = jnp.concatenate([jnp.concatenate([chunk_rows(v[hl], c) if hl == b else zero_v for b in range(2)],
                                                axis=1) for c in range(nc) for hl in range(2)], axis=0)
        o_intra = _dot(_blocks_diag(sc, zero_tile), v_bd)
        u_all = _dot_tn(jnp.concatenate(v, axis=1), _blocks_diag(k_dec, zero_tile))

        st = gla_st[p]
        before = []
        for c in range(nc):
            before.append(st)
            u = jnp.where(hm0, u_all[:GLA_DV, c * LANES:(c + 1) * LANES], u_all[GLA_DV:, c * LANES:(c + 1) * LANES])
            st = st * jnp.exp(last[c]) + u
        gla_st[p] = st
        st_stack = jnp.concatenate(before, axis=0)
        q_in = q2 * jnp.exp(bc)
        for hl in range(2):
            h = 2 * p + hl
            o_inter = jnp.concatenate(diag_blocks(_dot_nt(jnp.where(head_mask[hl], q_in, 0.0), st_stack)), axis=0)
            o = o_intra[:, hl * GLA_DV:(hl + 1) * GLA_DV] + o_inter
            gate = proj_ref[:, C_GR + h * GLA_DV:C_GR + (h + 1) * GLA_DV].astype(F32)
            yg = _rms(o, gn_ref[:, h * GLA_DV:(h + 1) * GLA_DV]) * (gate * _sigmoid(gate))
            y_ref[:, h * GLA_DV:(h + 1) * GLA_DV] = yg.astype(y_ref.dtype)

    ones_tile = jnp.ones((t_tile, LANES), BF16)
    for p in range(ML_HEADS // 2):
        mq2 = qk[:, p * LANES:(p + 1) * LANES]
        mk2 = qk[:, ML_QK + p * LANES:ML_QK + (p + 1) * LANES] * (ML_DK ** -0.5)
        v = [proj_ref[:, C_MV + (2 * p + hl) * ML_DV:C_MV + (2 * p + hl + 1) * ML_DV] for hl in range(2)]
        k_stack = jnp.concatenate([jnp.where(head_mask[hl], chunk_rows(mk2, c), 0.0)
                                   for c in range(nc) for hl in range(2)], axis=0)
        qk_blocks = diag_blocks(_dot_nt(mq2, k_stack))

        w_blocks, wk_blocks, cm_cols, b_max, bc_last = [], [], [], [], []
        for c in range(nc):
            blk = slice(c * LANES, (c + 1) * LANES)
            b_pair = (b_rows[S_MI + 2 * p:S_MI + 2 * p + 1, blk]
                      + b_rows[S_MI + 2 * p + 1:S_MI + 2 * p + 2, nc * LANES + c * LANES:nc * LANES + (c + 1) * LANES])
            f_last = [chunk_rows(cum, c)[CHUNK - 1:CHUNK, GLA_QK + S_MF + 2 * p + hl:GLA_QK + S_MF + 2 * p + hl + 1]
                      for hl in range(2)]
            bm = jnp.where(causal2, b_pair, -jnp.inf)
            cm = [jnp.max(jnp.where(head_mask[hl], bm, -jnp.inf), axis=1, keepdims=True) for hl in range(2)]
            w_blocks.append(qk_blocks[c] * jnp.exp(bm - jnp.where(hm0, cm[0], cm[1])))
            cm_cols.append(cm)
            bmx = [jnp.max(jnp.where(head_mask[hl], b_pair, -jnp.inf), axis=1, keepdims=True) for hl in range(2)]
            b_max.append(jnp.where(hm0, bmx[0], bmx[1]))
            bc_last.append(jnp.where(hm0, f_last[0], f_last[1]))
            e_col = []
            for hl in range(2):
                h = 2 * p + hl
                b_col = (chunk_rows(smallb, c)[:, S_MI + h:S_MI + h + 1]
                         - chunk_rows(cum, c)[:, GLA_QK + S_MF + h:GLA_QK + S_MF + h + 1])
                e_col.append(jnp.exp(b_col - bmx[hl]))
            wk_blocks.append(chunk_rows(mk2, c) * jnp.where(hm0, e_col[0], e_col[1]))

        ones_col = [jnp.broadcast_to((lane == hl).astype(BF16), (CHUNK, LANES)) for hl in range(2)]
        v_bd = jnp.concatenate([jnp.concatenate([chunk_rows(v[hl], c) if hl == b else zero_v for b in range(2)]
                                                + [ones_col[hl]], axis=1)
                                for c in range(nc) for hl in range(2)], axis=0)
        nv = _dot(_blocks_diag(w_blocks, zero_tile), v_bd)
        u_all = _dot_tn(jnp.concatenate(v + [ones_tile], axis=1), _blocks_diag(wk_blocks, zero_tile))

        cs, ns, m_run = ml_cs[p], ml_ns[p], ml_m[p]
        before = []
        for c in range(nc):
            before.append((cs, ns, m_run))
            m_top = jnp.maximum(m_run, b_max[c])
            keep = jnp.exp(m_run - m_top)
            gain = jnp.exp(b_max[c] - m_top)
            blk = slice(c * LANES, (c + 1) * LANES)
            cs = keep * cs + gain * jnp.where(hm0, u_all[:ML_DV, blk], u_all[ML_DV:2 * ML_DV, blk])
            ns = keep * ns + gain * u_all[2 * ML_DV:2 * ML_DV + 1, blk]
            m_run = bc_last[c] + m_top
        ml_cs[p], ml_ns[p], ml_m[p] = cs, ns, m_run
        ns_rows = jnp.zeros((BF16_ROWS, LANES), F32)
        for c in range(nc):
            ns_rows = jnp.where(row_sub == c, before[c][1], ns_rows)
        state_stack = jnp.concatenate([b[0] for b in before] + [ns_rows], axis=0)

        for hl in range(2):
            h = 2 * p + hl
            rr = _dot_nt(jnp.where(head_mask[hl], mq2, 0.0), state_stack)
            q_cs = diag_blocks(rr)
            h_chunks = []
            for c in range(nc):
                rows = slice(c * CHUNK, (c + 1) * CHUNK)
                q_ns = rr[rows, nc * ML_DV + c:nc * ML_DV + c + 1]
                m_prev = before[c][2][:, hl * ML_DK:hl * ML_DK + 1]
                cm = cm_cols[c][hl]
                m_top = jnp.maximum(m_prev, cm)
                local = jnp.exp(cm - m_top)
                carried = jnp.exp(m_prev - m_top)
                num = local * nv[rows, hl * ML_DV:(hl + 1) * ML_DV] + carried * q_cs[c]
                den = local * nv[rows, 2 * ML_DV + hl:2 * ML_DV + hl + 1] + carried * q_ns
                bc_col = chunk_rows(cum, c)[:, GLA_QK + S_MF + h:GLA_QK + S_MF + h + 1]
                h_chunks.append(num / jnp.maximum(jnp.abs(den), jnp.exp(-(bc_col + m_top))))
            og = proj_ref[:, C_MO + h * ML_DV:C_MO + (h + 1) * ML_DV].astype(F32)
            ym = _rms(jnp.concatenate(h_chunks, axis=0) * _sigmoid(og), mn_ref[:, h * ML_DV:(h + 1) * ML_DV])
            y_ref[:, GLA_V + h * ML_DV:GLA_V + (h + 1) * ML_DV] = ym.astype(y_ref.dtype)


def split_hybrid_weight(w_in):
    assert C_MQ == 2 * ML_QK + 2 * ML_V
    ml0 = C_MQ + GLA_RANK
    pad = jnp.zeros(w_in.shape[:2] + (LANES - GLA_RANK - 2 * ML_HEADS,), w_in.dtype)
    small = jnp.concatenate([w_in[:, :, C_MQ:ml0], w_in[:, :, ml0 + C_MQ:], pad], axis=2)
    return w_in[:, :, :C_MQ].astype(BF16), w_in[:, :, ml0:ml0 + C_MQ].astype(BF16), small.astype(BF16)


def hybrid_layer(x, gain, w_parts, layer, w_gate2, b_gate, gla_norm, conv_w, b_if, ml_norm, batch):
    m, d = x.shape
    seq = m // batch
    t = min(MIX_TILE, seq)
    nt = seq // t
    assert BF16_ROWS >= t // CHUNK
    b_small = jnp.zeros((1, LANES), F32).at[0, S_MI:S_MI + 2 * ML_HEADS].set(b_if)
    w_gate2_rows = jnp.zeros((LANES, GLA_QK), F32).at[S_GLR:S_GLR + GLA_RANK].set(w_gate2)
    return pl.pallas_call(
        _hybrid_kernel,
        grid=(batch, nt),
        in_specs=[pl.BlockSpec((t, d), lambda b, i: (b * nt + i, 0)),
                  _const_spec((1, d)),
                  _layer_spec((d, C_MQ), layer),
                  _layer_spec((d, C_MQ), layer),
                  _layer_spec((d, LANES), layer),
                  _const_spec((LANES, GLA_QK)),
                  _const_spec((1, GLA_QK)),
                  _const_spec((1, GLA_V)),
                  _const_spec((ML_CONV, 2 * ML_QK)),
                  _const_spec((1, LANES)),
                  _const_spec((1, ML_V))],
        out_specs=pl.BlockSpec((t, GLA_V + ML_V), lambda b, i: (b * nt + i, 0)),
        out_shape=jax.ShapeDtypeStruct((m, GLA_V + ML_V), BF16),
        scratch_shapes=[pltpu.VMEM((t, C_SMALL), BF16),
                        pltpu.VMEM((GLA_HEADS // 2, GLA_DV, LANES), F32),
                        pltpu.VMEM((ML_HEADS // 2, ML_DV, LANES), F32),
                        pltpu.VMEM((ML_HEADS // 2, 1, LANES), F32),
                        pltpu.VMEM((ML_HEADS // 2, 1, LANES), F32),
                        pltpu.VMEM((SUBLANES, 2 * ML_QK), F32)],
        compiler_params=_params(2),
        name="hybrid_core",
    )(x, gain.reshape(1, d), *w_parts, w_gate2_rows, b_gate.reshape(1, -1), gla_norm.reshape(1, -1), conv_w,
      b_small, ml_norm.reshape(1, -1))


LOG2E = 1.4426950408889634
AUG = 2 * MOBA_HD
LANES_PER_BLOCK = 4
MASK_BIG = 2.0 ** 100
PROJ_HEADS = 4
VT_ROWS = MOBA_HD + BF16_ROWS


def _split3(x):
    x = np.asarray(x, np.float32)
    hi = x.astype(BF16).astype(np.float32)
    mid = (x - hi).astype(BF16).astype(np.float32)
    lo = (x - hi - mid).astype(BF16).astype(np.float32)
    return np.stack([hi, mid, lo], axis=-1)


def _aug_tables(nb):
    c = np.asarray([LOG2E * 2.0 ** (-8.0 * (h + 1) / MOBA_HEADS) for h in range(MOBA_HEADS)], np.float32)
    pos = np.arange(MOBA_BLOCK, dtype=np.float32)
    base = LANES_PER_BLOCK * nb
    q_rows = np.zeros((MOBA_HEADS, MOBA_BLOCK, LANES), np.float32)
    k_rows = np.zeros((MOBA_HEADS, MOBA_BLOCK, LANES), np.float32)
    q_rows[:, :, base:base + 3] = _split3(-c[:, None] * pos[None, :])
    q_rows[:, :, base + 3:base + 6] = 1.0
    k_rows[:, :, base:base + 3] = 1.0
    k_rows[:, :, base + 3:base + 6] = _split3(c[:, None] * pos[None, :])
    i_idx = np.arange(nb)[:, None]
    j_idx = np.arange(nb)[None, :]
    steps = np.zeros((MOBA_HEADS, nb, nb, LANES_PER_BLOCK), np.float32)
    steps[..., 0] = np.where(j_idx == i_idx, 0.0, -MASK_BIG)
    gap = _split3(-c[:, None, None] * (MOBA_BLOCK * (i_idx - j_idx)).astype(np.float32)[None])
    steps[..., 1:] = np.where((j_idx <= i_idx)[None, :, :, None], gap, 0.0)
    q_steps = np.zeros((MOBA_HEADS, nb, 1, LANES), np.float32)
    q_steps[:, :, 0, :base] = steps.reshape(MOBA_HEADS, nb, base)
    k_steps = np.zeros((nb, 1, LANES), np.float32)
    k_steps[:, 0, :base] = np.repeat(np.eye(nb, dtype=np.float32), LANES_PER_BLOCK, axis=1)
    expand = np.zeros((nb, LANES), np.float32)
    expand[np.arange(nb), LANES_PER_BLOCK * np.arange(nb)] = MASK_BIG
    return (jnp.asarray(q_rows), jnp.asarray(q_steps), jnp.asarray(k_rows), jnp.asarray(k_steps),
            jnp.asarray(expand, BF16))


def _moba_prep_kernel(x_ref, g_ref, w_ref, qg_ref, kg_ref, qrow_ref, qstep_ref, krow_ref, kstep_ref, exp_ref,
                      qa_ref, ka_ref, vt_ref, kmean_ref, *, blocks):
    step = pl.program_id(1)
    nb = kmean_ref.shape[1]
    d = MOBA_HEADS * MOBA_HD

    @pl.when(step == 0)
    def _():
        kmean_ref[...] = jnp.zeros_like(kmean_ref)

    xn = _rms(x_ref[...], g_ref[...]).astype(BF16)
    blk = lax.broadcasted_iota(jnp.int32, (nb, MOBA_BLOCK), 0)
    ones_rows = (lax.broadcasted_iota(jnp.int32, (VT_ROWS - MOBA_HD, MOBA_BLOCK), 0) == 0).astype(F32)

    for h in range(MOBA_HEADS):
        hg = h % PROJ_HEADS
        if hg == 0:
            c0 = h * MOBA_HD
            q_grp, k_grp, v_grp = (jnp.dot(xn, w_ref[:, off + c0:off + c0 + PROJ_HEADS * MOBA_HD],
                                           preferred_element_type=F32) for off in (0, d, 2 * d))
        part = slice(hg * MOBA_HD, (hg + 1) * MOBA_HD)
        for u in range(blocks):
            i = step * blocks + u
            rows = slice(u * MOBA_BLOCK, (u + 1) * MOBA_BLOCK)
            past = blk < i
            q = _rms(q_grp[rows, part], qg_ref[...]) * (MOBA_HD ** -0.5)
            k = _rms(k_grp[rows, part], kg_ref[...])
            v = v_grp[rows, part]
            vt_ref[0, h, :, rows] = jnp.concatenate([v.T, ones_rows], axis=0).astype(vt_ref.dtype)

            gate = _dot_nt(kmean_ref[h], q)
            gate = jnp.where(past, gate, -jnp.inf)
            rank = jnp.zeros((nb, MOBA_BLOCK), F32)
            for j in range(nb):
                other = gate[j:j + 1]
                ahead = (other > gate) | ((other == gate) & (j < blk))
                rank = rank + ahead.astype(F32)
            sel = (past & (rank < MOBA_TOPK)).astype(BF16)
            q_extra = (qrow_ref[h] + qstep_ref[h, i]) + _dot_tn(sel, exp_ref[...])
            qa_ref[rows, h * AUG:h * AUG + MOBA_HD] = (q * LOG2E).astype(qa_ref.dtype)
            qa_ref[rows, h * AUG + MOBA_HD:(h + 1) * AUG] = q_extra.astype(qa_ref.dtype)
            ka_ref[rows, h * AUG:h * AUG + MOBA_HD] = k.astype(ka_ref.dtype)
            ka_ref[rows, h * AUG + MOBA_HD:(h + 1) * AUG] = (krow_ref[h] + kstep_ref[i]).astype(ka_ref.dtype)

            kmean_ref[h, pl.ds(i, 1), :] = jnp.mean(k, axis=0, keepdims=True)


PREP_BLOCKS = 4


def moba_prep(x, gain, w_qkv, layer, q_gain, k_gain, batch):
    m, dm = x.shape
    seq = m // batch
    nb = seq // MOBA_BLOCK
    assert LANES_PER_BLOCK * nb + 6 <= LANES and nb % PREP_BLOCKS == 0
    d = MOBA_HEADS * MOBA_HD
    rows = PREP_BLOCKS * MOBA_BLOCK
    ns = nb // PREP_BLOCKS
    q_rows, q_steps, k_rows, k_steps, expand = _aug_tables(nb)
    return pl.pallas_call(
        functools.partial(_moba_prep_kernel, blocks=PREP_BLOCKS),
        grid=(batch, ns),
        in_specs=[pl.BlockSpec((rows, dm), lambda b, i: (b * ns + i, 0)),
                  _const_spec((1, dm)),
                  _layer_spec((dm, 3 * d), layer),
                  _const_spec((1, MOBA_HD)),
                  _const_spec((1, MOBA_HD)),
                  _const_spec((MOBA_HEADS, MOBA_BLOCK, LANES)),
                  _const_spec((MOBA_HEADS, nb, 1, LANES)),
                  _const_spec((MOBA_HEADS, MOBA_BLOCK, LANES)),
                  _const_spec((nb, 1, LANES)),
                  _const_spec((nb, LANES))],
        out_specs=[pl.BlockSpec((rows, MOBA_HEADS * AUG), lambda b, i: (b * ns + i, 0)),
                   pl.BlockSpec((rows, MOBA_HEADS * AUG), lambda b, i: (b * ns + i, 0)),
                   pl.BlockSpec((1, MOBA_HEADS, VT_ROWS, rows), lambda b, i: (b, 0, 0, i))],
        out_shape=[jax.ShapeDtypeStruct((m, MOBA_HEADS * AUG), BF16),
                   jax.ShapeDtypeStruct((m, MOBA_HEADS * AUG), BF16),
                   jax.ShapeDtypeStruct((batch, MOBA_HEADS, VT_ROWS, seq), BF16)],
        scratch_shapes=[pltpu.VMEM((MOBA_HEADS, nb, MOBA_HD), F32)],
        compiler_params=_params(2),
        name="moba_prep",
    )(x, gain.reshape(1, dm), w_qkv, q_gain.reshape(1, -1), k_gain.reshape(1, -1),
      q_rows, q_steps, k_rows, k_steps, expand)


ATTN_HEADS_PER_STEP = 4
ATTN_CHUNK_BLOCKS = 4
PV_KEYS = 256


def _moba_attn_kernel(q_ref, k_ref, vt_ref, o_ref, s_even, s_odd, m_even, m_odd, *, heads, chunk_blocks, nb):
    t = pl.program_id(2)
    i = jnp.minimum(t, nb - 1)
    chunk_keys = chunk_blocks * MOBA_BLOCK
    n_chunks = i // chunk_blocks + 1
    n_valid2 = jnp.where(t > 0, (t - 1) // chunk_blocks + 1, 0)

    @pl.when((pl.program_id(0) == 0) & (pl.program_id(1) == 0) & (t == 0))
    def _():
        s_even[...] = jnp.zeros_like(s_even)
        s_odd[...] = jnp.zeros_like(s_odd)

    key_r = lax.broadcasted_iota(jnp.int32, (MOBA_BLOCK, MOBA_BLOCK), 0)
    qry_c = lax.broadcasted_iota(jnp.int32, (MOBA_BLOCK, MOBA_BLOCK), 1)
    causal = qry_c >= key_r
    q_aug = [q_ref[:, hh * AUG:(hh + 1) * AUG] for hh in range(heads)]

    def both(c, carry, last, s_w, s_r, m_row2, stage1, stage2):
        m_part, acc = carry
        k0 = c * chunk_keys
        live2 = jnp.full((1, MOBA_BLOCK), c, jnp.int32) < n_valid2
        new_m, new_acc = [], []
        for hh in range(heads):
            m_new, a = m_part[hh], acc[hh]
            if stage1:
                s = lax.dot_general(k_ref[pl.ds(k0, chunk_keys), hh * AUG:(hh + 1) * AUG], q_aug[hh],
                                    (((1,), (1,)), ((), ())), preferred_element_type=F32)
                if last:
                    parts = []
                    for u in range(chunk_blocks):
                        other_block = jnp.full((1, MOBA_BLOCK), c * chunk_blocks + u, jnp.int32) != i
                        parts.append(jnp.where(causal | other_block, s[u * MOBA_BLOCK:(u + 1) * MOBA_BLOCK], NEG))
                    s = jnp.concatenate(parts, axis=0)
                s_w[hh, pl.ds(k0, chunk_keys), :] = s
                m_new = jnp.maximum(m_new, jnp.max(s.reshape(chunk_keys // SUBLANES, SUBLANES, MOBA_BLOCK), axis=0))
            if stage2:
                for u in range(0, chunk_keys, PV_KEYS):
                    ku = k0 + u
                    p = jnp.exp2(s_r[hh, pl.ds(ku, PV_KEYS), :] - m_row2[hh])
                    p = jnp.where(live2, p, 0.0).astype(BF16)
                    a = a + jnp.dot(vt_ref[0, hh, :, pl.ds(ku, PV_KEYS)], p, preferred_element_type=F32)
            new_m.append(m_new)
            new_acc.append(a)
        return tuple(new_m), tuple(new_acc)

    def step(n, s_w, s_r, m_w, m_r, stage1=True, stage2=True):
        m_row2 = [m_r[hh] for hh in range(heads)]
        carry = (tuple(jnp.full((SUBLANES, MOBA_BLOCK), NEG, F32) for _ in range(heads)),
                 tuple(jnp.zeros((VT_ROWS, MOBA_BLOCK), F32) for _ in range(heads)))
        for c in range(n):
            carry = both(c, carry, c == n - 1, s_w, s_r, m_row2, stage1, stage2)
        m_part, acc = carry
        for hh in range(heads):
            if stage1:
                m_w[hh] = jnp.max(m_part[hh], axis=0, keepdims=True)
            if stage2:
                o = acc[hh][:MOBA_HD] * (1.0 / acc[hh][MOBA_HD:MOBA_HD + 1])
                o_ref[:, hh * MOBA_HD:(hh + 1) * MOBA_HD] = o.T.astype(o_ref.dtype)

    even = (s_even, s_odd, m_even, m_odd)
    odd = (s_odd, s_even, m_odd, m_even)
    n_max = nb // chunk_blocks
    pl.when(t == 0)(functools.partial(step, 1, *even, stage2=False))
    pl.when(t == nb)(functools.partial(step, n_max, *(odd if nb % 2 else even), stage1=False))
    inner = (t > 0) & (t < nb)
    for n in range(1, n_max + 1):
        pl.when(inner & (n_chunks == n) & (t % 2 == 0))(functools.partial(step, n, *even))
        pl.when(inner & (n_chunks == n) & (t % 2 == 1))(functools.partial(step, n, *odd))


def moba_attention(q_aug, k_aug, vt, batch):
    m = q_aug.shape[0]
    seq = m // batch
    nb = seq // MOBA_BLOCK
    hps = ATTN_HEADS_PER_STEP
    assert nb % ATTN_CHUNK_BLOCKS == 0
    return pl.pallas_call(
        functools.partial(_moba_attn_kernel, heads=hps, chunk_blocks=ATTN_CHUNK_BLOCKS, nb=nb),
        grid=(batch, MOBA_HEADS // hps, nb + 1),
        in_specs=[pl.BlockSpec((MOBA_BLOCK, hps * AUG), lambda b, g, t: (b * nb + jnp.minimum(t, nb - 1), g)),
                  pl.BlockSpec((seq, hps * AUG), lambda b, g, t: (b, g)),
                  pl.BlockSpec((1, hps, VT_ROWS, seq), lambda b, g, t: (b, g, 0, 0))],
        out_specs=pl.BlockSpec((MOBA_BLOCK, hps * MOBA_HD), lambda b, g, t: (b * nb + jnp.maximum(t - 1, 0), g)),
        out_shape=jax.ShapeDtypeStruct((m, MOBA_HEADS * MOBA_HD), BF16),
        scratch_shapes=[pltpu.VMEM((hps, seq, MOBA_BLOCK), F32), pltpu.VMEM((hps, seq, MOBA_BLOCK), F32),
                        pltpu.VMEM((hps, 1, MOBA_BLOCK), F32), pltpu.VMEM((hps, 1, MOBA_BLOCK), F32)],
        compiler_params=_params(3),
        name="moba_attn",
    )(q_aug, k_aug, vt)


def moba_layer(x, gain, w_qkv, layer, q_gain, k_gain, batch):
    q_aug, k_aug, vt = moba_prep(x, gain, w_qkv, layer, q_gain, k_gain, batch)
    return moba_attention(q_aug, k_aug, vt, batch)


def kernel(x, norm_mix, norm_ffn, hyb_w_in, gla_w_gate2, gla_b_gate, gla_norm, ml_conv, ml_b_if, ml_norm,
           hyb_w_out, moba_w_qkv, moba_q_norm, moba_k_norm, moba_w_o, ffn_w_gate, ffn_w_up, ffn_w_down):
    batch, seq, d = x.shape
    depth = norm_mix.shape[0]
    xf = x.reshape(batch * seq, d)
    w_hyb_in = split_hybrid_weight(hyb_w_in)
    w_hyb_out, w_moba_qkv, w_moba_o = hyb_w_out.astype(BF16), moba_w_qkv.astype(BF16), moba_w_o.astype(BF16)
    w_gate, w_up, w_down = ffn_w_gate.astype(BF16), ffn_w_up.astype(BF16), ffn_w_down.astype(BF16)
    for l in range(depth):
        j = l // 2
        if l % 2 == 0:
            y = hybrid_layer(xf, norm_mix[l], w_hyb_in, j, gla_w_gate2[j], gla_b_gate[j], gla_norm[j],
                             ml_conv[j], ml_b_if[j], ml_norm[j], batch)
            w_o = w_hyb_out
        else:
            y = moba_layer(xf, norm_mix[l], w_moba_qkv, j, moba_q_norm[j], moba_k_norm[j], batch)
            w_o = w_moba_o
        xf = out_ffn(xf, y, w_o, j, norm_ffn[l], w_gate, w_up, w_down, l)
    return xf.reshape(batch, seq, d)
```
